```python
import math
import jax, jax.numpy as jnp
from jax import lax
import numpy as np

D_MODEL = 1024
BATCH = 8
SEQ = 4096
DEPTH = 4

N_MIXERS = 4
D_FF = 4 * D_MODEL
LN_EPS = 1e-5
RMS_EPS = 1e-6
GN_EPS = 1e-5
DN_ALPHA = (2.0 * DEPTH) ** 0.25
DN_BETA = (8.0 * DEPTH) ** -0.25
BLOCK = 128
NEG = -1e30

RET_HEADS = 4
RET_QK_DIM = D_MODEL // RET_HEADS
RET_V_DIM = 2 * RET_QK_DIM
RET_CHUNK = 128
RET_THETA = 10000.0

DIL_PAIRS = ((128, 1), (512, 4), (2048, 16))
DIL_HEADS = 8
DIL_HEAD_DIM = 128
DIL_ROT = DIL_HEAD_DIM // 4
ROPE_THETA = 500000.0

MLA_HEADS = 16
MLA_NOPE = 128
MLA_ROPE = 64
MLA_V = 128
MLA_Q_RANK = 256
MLA_KV_RANK = 128
MLA_THETA = 10000.0

RWKV_HEAD = 64
RWKV_HEADS = D_MODEL // RWKV_HEAD
RWKV_LORA = max(32, int(round(1.8 * D_MODEL ** 0.5 / 32)) * 32)
RWKV_GATE_LORA = max(32, int(round(0.6 * D_MODEL ** 0.8 / 32)) * 32)
RWKV_GN_EPS = 64e-5

kernel_name = "hybrid_interleaved_ret_dilswa_mla_rwkv7"


def _n_occ(m):
    return (DEPTH - m + N_MIXERS - 1) // N_MIXERS


def _layer_norm(x, g, b):
    xf = x.astype(jnp.float32)
    mu = xf.mean(-1, keepdims=True)
    var = jnp.square(xf - mu).mean(-1, keepdims=True)
    return ((xf - mu) * lax.rsqrt(var + LN_EPS) * g + b).astype(x.dtype)


def _rms_norm(x, g):
    xf = x.astype(jnp.float32)
    return (xf * lax.rsqrt(jnp.mean(jnp.square(xf), -1, keepdims=True) + RMS_EPS) * g).astype(x.dtype)


def _head_norm(o, g, b, eps):
    mu = o.mean(-1, keepdims=True)
    var = jnp.square(o - mu).mean(-1, keepdims=True)
    on = (o - mu) * lax.rsqrt(var + eps)
    return on.reshape(o.shape[0], o.shape[1], -1) * g + b


def _rope(x, pos, rot_dim, theta):
    half = rot_dim // 2
    inv_freq = theta ** (-jnp.arange(half, dtype=jnp.float32) / half)
    ang = pos.astype(jnp.float32)[:, None] * inv_freq[None, :]
    cos = jnp.cos(ang)[:, None, :]
    sin = jnp.sin(ang)[:, None, :]
    x1 = x[..., :half].astype(jnp.float32)
    x2 = x[..., half:rot_dim].astype(jnp.float32)
    rot = jnp.concatenate([x1 * cos - x2 * sin, x2 * cos + x1 * sin], axis=-1).astype(x.dtype)
    return jnp.concatenate([rot, x[..., rot_dim:]], axis=-1)


def _retention(x, w_in, gn, w_out):
    B, S, _ = x.shape
    H, dk, dv, C = RET_HEADS, RET_QK_DIM, RET_V_DIM, RET_CHUNK
    n_chunks = S // C
    pos = jnp.arange(S)
    q, k, v, g = jnp.split(x @ w_in, [H * dk, 2 * H * dk, 2 * H * dk + H * dv], axis=-1)
    q = _rope(q.reshape(B, S, H, dk), pos, dk, RET_THETA)
    k = _rope(k.reshape(B, S, H, dk), pos, dk, RET_THETA) * dk ** -0.5
    v = v.reshape(B, S, H, dv)

    def to_chunks(t):
        return t.astype(jnp.float32).reshape(B, n_chunks, C, H, -1).transpose(1, 0, 3, 2, 4)

    log_gamma = jnp.log(1.0 - 2.0 ** (-5.0 - jnp.arange(H, dtype=jnp.float32)))
    idx = jnp.arange(C, dtype=jnp.float32)
    diff = idx[:, None] - idx[None, :]
    intra = jnp.where(diff >= 0, jnp.exp(log_gamma[:, None, None] * jnp.maximum(diff, 0.0)), 0.0)
    q_dec = jnp.exp(log_gamma[:, None] * (idx + 1.0))[:, :, None]
    k_dec = jnp.exp(log_gamma[:, None] * (C - 1.0 - idx))[:, :, None]
    chunk_dec = jnp.exp(log_gamma * C)[:, None, None]

    def step(state, qkv):
        qc, kc, vc = qkv
        scores = jnp.einsum('bhid,bhjd->bhij', qc, kc) * intra
        o = jnp.einsum('bhij,bhjv->bhiv', scores, vc) + jnp.einsum('bhid,bhdv->bhiv', qc, state) * q_dec
        state = state * chunk_dec + jnp.einsum('bhjd,bhjv->bhdv', kc * k_dec, vc)
        return state, o

    state0 = jnp.zeros((B, H, dk, dv), jnp.float32)
    _, o = lax.scan(step, state0, (to_chunks(q), to_chunks(k), to_chunks(v)))
    o = o.transpose(1, 0, 3, 2, 4).reshape(B, S, H, dv)
    o = _head_norm(o, gn[0], gn[1], GN_EPS)
    o = (jax.nn.silu(g.astype(jnp.float32)) * o).astype(x.dtype)
    return o @ w_out


def _strided_band(q, k, v, dil, reach):
    B, S, H, Dh = q.shape
    L = S // dil
    nb = -(-L // BLOCK)
    Lp = nb * BLOCK

    def gather(t):
        t = t.reshape(B, L, dil, H, Dh).transpose(0, 2, 1, 3, 4).reshape(B * dil, L, H, Dh)
        t = jnp.pad(t, ((0, 0), (0, Lp - L), (0, 0), (0, 0)))
        return t.reshape(B * dil, nb, BLOCK, H, Dh)

    def with_prev(t):
        prev = jnp.pad(t[:, :-1], ((0, 0), (1, 0), (0, 0), (0, 0), (0, 0)))
        return jnp.concatenate([prev, t], axis=2)

    qb = gather(q)
    kb = with_prev(gather(k))
    vb = with_prev(gather(v)).astype(jnp.float32)
    s = jnp.einsum('znqhd,znkhd->znhqk', qb, kb, preferred_element_type=jnp.float32)
    qi = jnp.arange(BLOCK)[:, None]
    ki = jnp.arange(2 * BLOCK)[None, :]
    dist = qi + BLOCK - ki
    blk = jnp.arange(nb)[:, None, None]
    valid = (dist >= 0) & (dist <= reach) & ((blk > 0) | (ki >= BLOCK))
    s = jnp.where(valid[None, :, None], s, NEG)
    m = s.max(-1, keepdims=True)
    p = jnp.exp(s - m)
    l = p.sum(-1)
    o = jnp.einsum('znhqk,znkhd->znqhd', p, vb)

    def scatter_back(t):
        t = t.reshape(B, dil, Lp, *t.shape[3:])[:, :, :L]
        return jnp.moveaxis(t, 1, 2).reshape(B, S, *t.shape[3:])

    return (scatter_back(o), scatter_back(jnp.swapaxes(m[..., 0], 2, 3)),
            scatter_back(jnp.swapaxes(l, 2, 3)))


def _dilated(x, w_in, w_out):
    B, S, _ = x.shape
    G, H, Dh = len(DIL_PAIRS), DIL_HEADS, DIL_HEAD_DIM
    proj = (x @ w_in).reshape(B, S, G, 3, H, Dh)
    pos = jnp.arange(S)
    outs, maxes, dens = [], [], []
    for gi, (window, dil) in enumerate(DIL_PAIRS):
        q = _rope(proj[:, :, gi, 0], pos, DIL_ROT, ROPE_THETA) * Dh ** -0.5
        k = _rope(proj[:, :, gi, 1], pos, DIL_ROT, ROPE_THETA)
        o, m, l = _strided_band(q, k, proj[:, :, gi, 2], dil, window // dil)
        outs.append(o); maxes.append(m); dens.append(l)
    o_all = jnp.stack(outs)
    m_all = jnp.stack(maxes)
    l_all = jnp.stack(dens)
    wts = jnp.exp(m_all - m_all.max(0, keepdims=True))
    o = jnp.einsum('gbsh,gbshd->bshd', wts, o_all) / jnp.einsum('gbsh,gbsh->bsh', wts, l_all)[..., None]
    return o.reshape(B, S, H * Dh).astype(x.dtype) @ w_out


def _mla(x, w_down, norm_q, norm_kv, w_uq, w_ukv, w_out):
    B, S, _ = x.shape
    H = MLA_HEADS
    pos = jnp.arange(S)
    c_q, c_kv, k_pe = jnp.split(x @ w_down, [MLA_Q_RANK, MLA_Q_RANK + MLA_KV_RANK], axis=-1)
    q = (_rms_norm(c_q, norm_q) @ w_uq).reshape(B, S, H, MLA_NOPE + MLA_ROPE)
    kv = (_rms_norm(c_kv, norm_kv) @ w_ukv).reshape(B, S, H, MLA_NOPE + MLA_V)
    q_nope = q[..., :MLA_NOPE]
    q_pe = _rope(q[..., MLA_NOPE:], pos, MLA_ROPE, MLA_THETA)
    k_nope, v = kv[..., :MLA_NOPE], kv[..., MLA_NOPE:]
    k_pe = _rope(k_pe[:, :, None, :], pos, MLA_ROPE, MLA_THETA)[:, :, 0]
    scale = (MLA_NOPE + MLA_ROPE) ** -0.5
    n_blocks = S // BLOCK
    key_pos = jnp.arange(S)

    def blocks(t):
        return jnp.moveaxis(t.reshape(B, n_blocks, BLOCK, *t.shape[2:]), 1, 0)

    def attend(args):
        qn, qp, q0 = args
        s = (jnp.einsum('bqhd,bkhd->bhqk', qn, k_nope, preferred_element_type=jnp.float32)
             + jnp.einsum('bqhd,bkd->bhqk', qp, k_pe, preferred_element_type=jnp.float32)) * scale
        causal = (q0 + jnp.arange(BLOCK))[:, None] >= key_pos[None, :]
        p = jax.nn.softmax(jnp.where(causal, s, NEG), axis=-1)
        return jnp.einsum('bhqk,bkhd->bqhd', p.astype(v.dtype), v)

    o = lax.map(attend, (blocks(q_nope), blocks(q_pe), jnp.arange(n_blocks) * BLOCK))
    o = jnp.moveaxis(o, 0, 1).reshape(B, S, H * MLA_V)
    return o @ w_out


def _rwkv7(x, mu, w_rkv, w_out, vec, lora_a, lora_b, gate_a, gate_b, ln_x):
    B, S, D = x.shape
    H, N = RWKV_HEADS, RWKV_HEAD
    xx = jnp.pad(x, ((0, 0), (1, 0), (0, 0)))[:, :-1] - x
    xr, xw, xk, xv, xa, xg = [x + xx * mu[i] for i in range(6)]
    w0, a0, k_k, k_a, r_k = vec[0], vec[1], vec[2], vec[3], vec[4]

    def heads(t):
        return t.astype(jnp.float32).reshape(B, S, H, N)

    r = heads(xr @ w_rkv[0])
    k_raw = xk @ w_rkv[1]
    v = heads(xv @ w_rkv[2])
    log_w = -jax.nn.softplus(-(w0 + jnp.tanh(xw @ lora_a[0]) @ lora_b[0]).astype(jnp.float32)) - 0.5
    decay = heads(jnp.exp(-jnp.exp(log_w)))
    a = heads(jax.nn.sigmoid((a0 + (xa @ lora_a[1]) @ lora_b[1]).astype(jnp.float32)))
    g = jax.nn.sigmoid(xg @ gate_a) @ gate_b
    kk = heads(k_raw * k_k)
    kk = kk / jnp.maximum(jnp.sqrt(jnp.sum(jnp.square(kk), -1, keepdims=True)), 1e-12)
    k = heads(k_raw) * (1.0 + (a - 1.0) * k_a.astype(jnp.float32).reshape(H, N))

    def step(state, inp):
        r_t, w_t, k_t, v_t, kk_t, b_t = inp
        sa = jnp.einsum('bhij,bhj->bhi', state, -kk_t)
        state = (state * w_t[:, :, None, :] + sa[..., None] * b_t[:, :, None, :]
                 + v_t[..., None] * k_t[:, :, None, :])
        return state, jnp.einsum('bhij,bhj->bhi', state, r_t)

    tm = lambda t: jnp.moveaxis(t, 1, 0)
    state0 = jnp.zeros((B, H, N, N), jnp.float32)
    _, y = lax.scan(step, state0, (tm(r), tm(decay), tm(k), tm(v), tm(kk), tm(kk * a)))
    y = jnp.moveaxis(y, 0, 1)
    y = _head_norm(y, ln_x[0], ln_x[1], RWKV_GN_EPS)
    bonus = (jnp.sum(r * k * r_k.astype(jnp.float32).reshape(H, N), -1, keepdims=True) * v).reshape(B, S, D)
    y = ((y + bonus) * g.astype(jnp.float32)).astype(x.dtype)
    return y @ w_out


def _sq_relu_mlp(x, w1, w2):
    return jnp.square(jax.nn.relu(x @ w1)) @ w2


def setup_inputs(seed: int = 0) -> dict:
    key = jax.random.key(seed)
    ks = iter(jax.random.split(key, 64))
    f32 = jnp.float32

    def nrm(shape, fan_in, scale=1.0):
        return jax.random.normal(next(ks), shape, f32) * (scale * fan_in ** -0.5)

    def gain(shape):
        return 1.0 + 0.05 * jax.random.normal(next(ks), shape, f32)

    def small(shape, s=0.02):
        return s * jax.random.normal(next(ks), shape, f32)

    nA, nB, nC, nD = _n_occ(0), _n_occ(1), _n_occ(2), _n_occ(3)
    D = D_MODEL
    rh, dk, dv = RET_HEADS, RET_QK_DIM, RET_V_DIM
    G, dh, dd = len(DIL_PAIRS), DIL_HEADS, DIL_HEAD_DIM
    mh = MLA_HEADS
    inp = {}
    inp['x'] = jax.random.normal(next(ks), (BATCH, SEQ, D), f32)
    inp['ret_w_in'] = nrm((nA, D, 2 * rh * dk + 2 * rh * dv), D)
    inp['ret_gn'] = jnp.stack([gain((nA, rh * dv)), small((nA, rh * dv))], axis=1)
    inp['ret_w_out'] = nrm((nA, rh * dv, D), rh * dv, DN_BETA)
    inp['dil_w_in'] = nrm((nB, D, G * 3 * dh * dd), D)
    inp['dil_w_out'] = nrm((nB, dh * dd, D), dh * dd, DN_BETA)
    inp['mla_w_down'] = nrm((nC, D, MLA_Q_RANK + MLA_KV_RANK + MLA_ROPE), D)
    inp['mla_norm_q'] = gain((nC, MLA_Q_RANK))
    inp['mla_norm_kv'] = gain((nC, MLA_KV_RANK))
    inp['mla_w_uq'] = nrm((nC, MLA_Q_RANK, mh * (MLA_NOPE + MLA_ROPE)), MLA_Q_RANK)
    inp['mla_w_ukv'] = nrm((nC, MLA_KV_RANK, mh * (MLA_NOPE + MLA_V)), MLA_KV_RANK)
    inp['mla_w_out'] = nrm((nC, mh * MLA_V, D), mh * MLA_V, DN_BETA)
    inp['rwkv_mu'] = jax.random.uniform(next(ks), (nD, 6, D), f32, 0.0, 1.0)
    inp['rwkv_w_rkv'] = nrm((nD, 3, D, D), D)
    inp['rwkv_w_out'] = nrm((nD, D, D), D, DN_BETA)
    inp['rwkv_vec'] = jnp.stack([
        jax.random.uniform(next(ks), (nD, D), f32, -5.0, 0.0),
        small((nD, D), 0.1),
        0.85 + small((nD, D), 0.05),
        1.0 + small((nD, D), 0.05),
        small((nD, D), 0.1),
    ], axis=1)
    inp['rwkv_lora_a'] = nrm((nD, 2, D, RWKV_LORA), D)
    inp['rwkv_lora_b'] = nrm((nD, 2, RWKV_LORA, D), RWKV_LORA, 0.1)
    inp['rwkv_gate_a'] = nrm((nD, D, RWKV_GATE_LORA), D)
    inp['rwkv_gate_b'] = nrm((nD, RWKV_GATE_LORA, D), RWKV_GATE_LORA)
    inp['rwkv_ln_x'] = jnp.stack([gain((nD, D)), small((nD, D))], axis=1)
    inp['mlp_w1'] = nrm((DEPTH, D, D_FF), D)
    inp['mlp_w2'] = nrm((DEPTH, D_FF, D), D_FF, DN_BETA)
    inp['ln_g'] = gain((DEPTH, 2, D))
    inp['ln_b'] = small((DEPTH, 2, D))
    return inp


def reference(x, ret_w_in, ret_gn, ret_w_out, dil_w_in, dil_w_out, mla_w_down, mla_norm_q,
              mla_norm_kv, mla_w_uq, mla_w_ukv, mla_w_out, rwkv_mu, rwkv_w_rkv, rwkv_w_out,
              rwkv_vec, rwkv_lora_a, rwkv_lora_b, rwkv_gate_a, rwkv_gate_b, rwkv_ln_x,
              mlp_w1, mlp_w2, ln_g, ln_b):
    for i in range(DEPTH):
        m, j = i % N_MIXERS, i // N_MIXERS
        if m == 0:
            h = _retention(x, ret_w_in[j], ret_gn[j], ret_w_out[j])
        elif m == 1:
            h = _dilated(x, dil_w_in[j], dil_w_out[j])
        elif m == 2:
            h = _mla(x, mla_w_down[j], mla_norm_q[j], mla_norm_kv[j], mla_w_uq[j], mla_w_ukv[j], mla_w_out[j])
        else:
            h = _rwkv7(x, rwkv_mu[j], rwkv_w_rkv[j], rwkv_w_out[j], rwkv_vec[j], rwkv_lora_a[j],
                       rwkv_lora_b[j], rwkv_gate_a[j], rwkv_gate_b[j], rwkv_ln_x[j])
        x = _layer_norm(DN_ALPHA * x + h, ln_g[i, 0], ln_b[i, 0])
        x = _layer_norm(DN_ALPHA * x + _sq_relu_mlp(x, mlp_w1[i], mlp_w2[i]), ln_g[i, 1], ln_b[i, 1])
    return x
```

```python
import functools
import math

import jax
import jax.numpy as jnp
from jax import lax
from jax.experimental import pallas as pl
from jax.experimental.pallas import tpu as pltpu

F32 = jnp.float32
BF16 = jnp.bfloat16

D_MODEL = 1024
DEPTH = 4
D_FF = 4 * D_MODEL
LN_EPS = 1e-5
RMS_EPS = 1e-6
GN_EPS = 1e-5
DN_ALPHA = (2.0 * DEPTH) ** 0.25
NEG = -1e30
LANES = 128

RET_HEADS = 4
RET_QK_DIM = 256
RET_V_DIM = 512
RET_CHUNK = 128
RET_THETA = 10000.0

DIL_PAIRS = ((128, 1), (512, 4), (2048, 16))
DIL_HEADS = 8
DIL_HEAD_DIM = 128
DIL_ROT = 32
DIL_BLOCK = 128
ROPE_THETA = 500000.0

MLA_HEADS = 16
MLA_NOPE = 128
MLA_ROPE = 64
MLA_V = 128
MLA_Q_RANK = 256
MLA_KV_RANK = 128
MLA_THETA = 10000.0
MLA_BLOCK = 256

RWKV_HEAD = 64
RWKV_HEADS = D_MODEL // RWKV_HEAD
RWKV_GN_EPS = 64e-5
RWKV_CHUNK = 64
RWKV_GATE_PAD = 256

VMEM_LIMIT = 56 * 1024 * 1024


def _cparams(sem):
    return pltpu.CompilerParams(dimension_semantics=sem, vmem_limit_bytes=VMEM_LIMIT)


def _resident(shape):
    nd = len(shape)
    return pl.BlockSpec(shape, lambda *_: (0,) * nd, pipeline_mode=pl.Buffered(1))


def _layer_norm(z, g, b):
    mu = jnp.mean(z, axis=-1, keepdims=True)
    d = z - mu
    var = jnp.mean(d * d, axis=-1, keepdims=True)
    return d * lax.rsqrt(var + LN_EPS) * g + b


def _dot(a, b):
    return jnp.dot(a, b, preferred_element_type=F32)


def _dot_nt(a, b):
    return lax.dot_general(a, b, (((1,), (1,)), ((), ())), preferred_element_type=F32)


def _dot_tn(a, b):
    return lax.dot_general(a, b, (((0,), (0,)), ((), ())), preferred_element_type=F32)


def _split2(z):
    hi = z.astype(BF16)
    lo = (z - hi.astype(F32)).astype(BF16)
    return hi, lo


def _mm_body(x_ref, w_ref, *rest, pattern, shift, rope_jmax, scale, has_tabs):
    if has_tabs:
        c_ref, s1_ref, s2_ref, o_ref, xb_ref = rest
    else:
        o_ref, xb_ref = rest
    j = pl.program_id(1)

    @pl.when(j == 0)
    def _():
        xb_ref[...] = x_ref[...].astype(BF16)

    acc = _dot(xb_ref[...], w_ref[...])

    def fancy():
        for t, mode in enumerate(pattern):
            a = acc[:, t * LANES:(t + 1) * LANES]
            if mode == "rope":
                a = (a * c_ref[0] + pltpu.roll(a, LANES - shift, 1) * s1_ref[0]
                     + pltpu.roll(a, shift, 1) * s2_ref[0])
            elif mode == "scale":
                a = a * scale
            o_ref[:, t * LANES:(t + 1) * LANES] = a.astype(o_ref.dtype)

    if pattern is None:
        o_ref[...] = acc.astype(o_ref.dtype)
    elif rope_jmax is None:
        fancy()
    else:
        pl.when(j < rope_jmax)(fancy)

        @pl.when(j >= rope_jmax)
        def _():
            o_ref[...] = acc.astype(o_ref.dtype)


def _mm(x, w, *, tm, tn, m_rows, out_dtype=BF16, x_map=None, tabs=None, tab_map=None,
        pattern=None, shift=0, rope_jmax=None, scale=1.0, name="mm"):
    k, n = w.shape
    grid = (m_rows // tm, n // tn)
    if x_map is None:
        x_map = lambda i, j: (i, 0)
    in_specs = [pl.BlockSpec((tm, k), x_map), pl.BlockSpec((k, tn), lambda i, j: (0, j))]
    args = [x, w]
    if tabs is not None:
        for t in tabs:
            in_specs.append(pl.BlockSpec((1, tm, LANES), tab_map))
            args.append(t)
    body = functools.partial(_mm_body, pattern=pattern, shift=shift, rope_jmax=rope_jmax,
                             scale=scale, has_tabs=tabs is not None)
    return pl.pallas_call(
        body,
        grid=grid,
        in_specs=in_specs,
        out_specs=pl.BlockSpec((tm, tn), lambda i, j: (i, j)),
        out_shape=jax.ShapeDtypeStruct((m_rows, n), out_dtype),
        scratch_shapes=[pltpu.VMEM((tm, k), BF16)],
        compiler_params=_cparams(("parallel", "arbitrary")),
        name=name,
    )(*args)


def _mm_res_ln_body(a_ref, w_ref, res_ref, g_ref, b_ref, o_ref):
    acc = _dot(a_ref[...], w_ref[...])
    o_ref[...] = _layer_norm(DN_ALPHA * res_ref[...] + acc, g_ref[...], b_ref[...])


def _mm_res_ln(a, w, res, g, b, *, tm=512, name="mm_res_ln"):
    m, k = a.shape
    d = w.shape[1]
    row = lambda i: (i, 0)
    return pl.pallas_call(
        _mm_res_ln_body,
        grid=(m // tm,),
        in_specs=[pl.BlockSpec((tm, k), row), _resident((k, d)), pl.BlockSpec((tm, d), row),
                  _resident((1, d)), _resident((1, d))],
        out_specs=pl.BlockSpec((tm, d), row),
        out_shape=jax.ShapeDtypeStruct((m, d), F32),
        compiler_params=_cparams(("parallel",)),
        name=name,
    )(a, w, res, g, b)


def _mlp_body(x_ref, w1_ref, w2_ref, g_ref, b_ref, o_ref, *, fchunk):
    x = x_ref[...]
    xb = x.astype(BF16)
    acc = jnp.zeros(x.shape, F32)
    for c in range(D_FF // fchunk):
        h = _dot(xb, w1_ref[:, c * fchunk:(c + 1) * fchunk])
        h = jnp.maximum(h, 0.0)
        h = (h * h).astype(BF16)
        acc = acc + _dot(h, w2_ref[c * fchunk:(c + 1) * fchunk, :])
    o_ref[...] = _layer_norm(DN_ALPHA * x + acc, g_ref[...], b_ref[...])


def _mlp(x, w1, w2, g, b, *, tm=512, fchunk=1024):
    m, d = x.shape
    row = lambda i: (i, 0)
    return pl.pallas_call(
        functools.partial(_mlp_body, fchunk=fchunk),
        grid=(m // tm,),
        in_specs=[pl.BlockSpec((tm, d), row), _resident((d, D_FF)), _resident((D_FF, d)),
                  _resident((1, d)), _resident((1, d))],
        out_specs=pl.BlockSpec((tm, d), row),
        out_shape=jax.ShapeDtypeStruct((m, d), F32),
        compiler_params=_cparams(("parallel",)),
        name="mlp",
    )(x, w1, w2, g, b)


def _ret_body(q_ref, k_ref, v_ref, g_ref, cos_ref, sin_ref, intra_ref, qdec_ref, kdec_ref,
              cdec_ref, gn_ref, o_ref, state_ref):
    half = RET_QK_DIM // 2

    @pl.when(pl.program_id(2) == 0)
    def _():
        state_ref[...] = jnp.zeros(state_ref.shape, F32)

    cos = cos_ref[...]
    sin = sin_ref[...]

    def rope(t):
        t1 = t[:, :half].astype(F32)
        t2 = t[:, half:].astype(F32)
        return jnp.concatenate([t1 * cos - t2 * sin, t2 * cos + t1 * sin], axis=-1)

    q = rope(q_ref[...])
    k = rope(k_ref[...]) * (RET_QK_DIM ** -0.5)
    v = v_ref[...]
    qb = q.astype(BF16)
    scores = _dot_nt(qb, k.astype(BF16)) * intra_ref[0]
    state = state_ref[...]
    o = _dot(scores.astype(BF16), v) + _dot(qb, state.astype(BF16)) * qdec_ref[0]
    state_ref[...] = state * cdec_ref[0, 0:1, :] + _dot_tn((k * kdec_ref[0]).astype(BF16), v)

    mu = jnp.mean(o, axis=-1, keepdims=True)
    d = o - mu
    var = jnp.mean(d * d, axis=-1, keepdims=True)
    on = d * lax.rsqrt(var + GN_EPS) * gn_ref[0:1, :] + gn_ref[1:2, :]
    gate = g_ref[...].astype(F32)
    gate = gate * (1.0 / (1.0 + jnp.exp(-gate)))
    o_ref[...] = (gate * on).astype(o_ref.dtype)


def _retention(proj, gn, batch, seq):
    h_, dk, dv, c = RET_HEADS, RET_QK_DIM, RET_V_DIM, RET_CHUNK
    n = seq // c
    half = dk // 2
    pos = jnp.arange(seq, dtype=F32)
    inv_freq = RET_THETA ** (-jnp.arange(half, dtype=F32) / half)
    ang = pos[:, None] * inv_freq[None, :]
    cos, sin = jnp.cos(ang), jnp.sin(ang)
    log_gamma = jnp.log(1.0 - 2.0 ** (-5.0 - jnp.arange(h_, dtype=F32)))
    idx = jnp.arange(c, dtype=F32)
    diff = idx[:, None] - idx[None, :]
    intra = jnp.where(diff >= 0, jnp.exp(log_gamma[:, None, None] * jnp.maximum(diff, 0.0)), 0.0)
    qdec = jnp.broadcast_to(jnp.exp(log_gamma[:, None] * (idx + 1.0))[:, :, None], (h_, c, dv))
    kdec = jnp.broadcast_to(jnp.exp(log_gamma[:, None] * (c - 1.0 - idx))[:, :, None], (h_, c, dk))
    cdec = jnp.broadcast_to(jnp.exp(log_gamma * c)[:, None, None], (h_, 8, dv))
    nq = h_ * dk // dk
    nk_v = 2 * h_ * dk // dv
    ng = (2 * h_ * dk + h_ * dv) // dv
    return pl.pallas_call(
        _ret_body,
        grid=(batch, h_, n),
        in_specs=[
            pl.BlockSpec((c, dk), lambda b, h, i: (b * n + i, h)),
            pl.BlockSpec((c, dk), lambda b, h, i: (b * n + i, nq + h)),
            pl.BlockSpec((c, dv), lambda b, h, i: (b * n + i, nk_v + h)),
            pl.BlockSpec((c, dv), lambda b, h, i: (b * n + i, ng + h)),
            pl.BlockSpec((c, half), lambda b, h, i: (i, 0)),
            pl.BlockSpec((c, half), lambda b, h, i: (i, 0)),
            pl.BlockSpec((1, c, c), lambda b, h, i: (h, 0, 0)),
            pl.BlockSpec((1, c, dv), lambda b, h, i: (h, 0, 0)),
            pl.BlockSpec((1, c, dk), lambda b, h, i: (h, 0, 0)),
            pl.BlockSpec((1, 8, dv), lambda b, h, i: (h, 0, 0)),
            pl.BlockSpec((2, dv), lambda b, h, i: (0, h)),
        ],
        out_specs=pl.BlockSpec((c, dv), lambda b, h, i: (b * n + i, h)),
        out_shape=jax.ShapeDtypeStruct((batch * seq, h_ * dv), BF16),
        scratch_shapes=[pltpu.VMEM((dk, dv), F32)],
        compiler_params=_cparams(("parallel", "parallel", "arbitrary")),
        name="retention",
    )(proj, proj, proj, proj, cos, sin, intra, qdec, kdec, cdec, gn)


def _dil_attn_body(q_ref, kp_ref, kc_ref, vp_ref, vc_ref, o_ref, lse_ref):
    blk = DIL_BLOCK
    has_prev = pl.program_id(1) > 0
    qi = lax.broadcasted_iota(jnp.int32, (blk, blk), 0)
    ki = lax.broadcasted_iota(jnp.int32, (blk, blk), 1)
    prev_ok = jnp.logical_and(ki >= qi, has_prev)
    cur_ok = ki <= qi
    lane = lax.broadcasted_iota(jnp.int32, (blk, LANES), 1)
    lse_tile = jnp.zeros((blk, LANES), F32)
    for h in range(DIL_HEADS):
        sl = slice(h * DIL_HEAD_DIM, (h + 1) * DIL_HEAD_DIM)
        q = q_ref[:, sl]
        sp = jnp.where(prev_ok, _dot_nt(q, kp_ref[:, sl]), NEG)
        sc = jnp.where(cur_ok, _dot_nt(q, kc_ref[:, sl]), NEG)
        m = jnp.maximum(jnp.max(sp, axis=-1, keepdims=True), jnp.max(sc, axis=-1, keepdims=True))
        pp = jnp.exp(sp - m)
        pc = jnp.exp(sc - m)
        l = jnp.sum(pp, axis=-1, keepdims=True) + jnp.sum(pc, axis=-1, keepdims=True)
        o = _dot(pp.astype(BF16), vp_ref[:, sl]) + _dot(pc.astype(BF16), vc_ref[:, sl])
        o_ref[:, sl] = (o / l).astype(o_ref.dtype)
        lse_tile = jnp.where(lane == h, m + jnp.log(l), lse_tile)
    lse_ref[...] = lse_tile


def _dil_attn(qkv, batch, seq, dil):
    blk = DIL_BLOCK
    hd = DIL_HEADS * DIL_HEAD_DIM
    nb = seq // dil // blk
    cur = lambda c: (lambda z, i: (z * nb + i, c))
    prev = lambda c: (lambda z, i: (z * nb + jnp.maximum(i - 1, 0), c))
    out_map = lambda z, i: ((z // dil) * nb + i, z % dil)
    return pl.pallas_call(
        _dil_attn_body,
        grid=(batch * dil, nb),
        in_specs=[pl.BlockSpec((blk, hd), cur(0)), pl.BlockSpec((blk, hd), prev(1)),
                  pl.BlockSpec((blk, hd), cur(1)), pl.BlockSpec((blk, hd), prev(2)),
                  pl.BlockSpec((blk, hd), cur(2))],
        out_specs=[pl.BlockSpec((blk, hd), out_map), pl.BlockSpec((blk, LANES), out_map)],
        out_shape=[jax.ShapeDtypeStruct((batch * seq // dil, dil * hd), BF16),
                   jax.ShapeDtypeStruct((batch * seq // dil, dil * LANES), F32)],
        compiler_params=_cparams(("parallel", "arbitrary")),
        name=f"dil_attn_{dil}",
    )(qkv, qkv, qkv, qkv, qkv)


def _dil_out_body(o0_ref, o1_ref, o2_ref, l0_ref, l1_ref, l2_ref, e_ref, w_ref, res_ref,
                  g_ref, b_ref, o_ref):
    l0, l1, l2 = l0_ref[...], l1_ref[...], l2_ref[...]
    m = jnp.maximum(jnp.maximum(l0, l1), l2)
    e0, e1, e2 = jnp.exp(l0 - m), jnp.exp(l1 - m), jnp.exp(l2 - m)
    den = e0 + e1 + e2
    e = e_ref[...]
    mixed = None
    for eg, og in ((e0, o0_ref), (e1, o1_ref), (e2, o2_ref)):
        hi, lo = _split2(eg / den)
        wfull = _dot(hi, e) + _dot(lo, e)
        term = wfull * og[...].astype(F32)
        mixed = term if mixed is None else mixed + term
    acc = _dot(mixed.astype(BF16), w_ref[...])
    o_ref[...] = _layer_norm(DN_ALPHA * res_ref[...] + acc, g_ref[...], b_ref[...])


def _dil_out(outs, lses, w, res, g, b, *, tm=512):
    m, d = res.shape
    hd = DIL_HEADS * DIL_HEAD_DIM
    expand = (jnp.arange(LANES)[:, None] == (jnp.arange(hd) // DIL_HEAD_DIM)[None, :]).astype(BF16)
    row = lambda i: (i, 0)
    return pl.pallas_call(
        _dil_out_body,
        grid=(m // tm,),
        in_specs=[pl.BlockSpec((tm, hd), row)] * 3 + [pl.BlockSpec((tm, LANES), row)] * 3
        + [_resident((LANES, hd)), _resident((hd, d)), pl.BlockSpec((tm, d), row),
           _resident((1, d)), _resident((1, d))],
        out_specs=pl.BlockSpec((tm, d), row),
        out_shape=jax.ShapeDtypeStruct((m, d), F32),
        compiler_params=_cparams(("parallel",)),
        name="dil_out",
    )(*outs, *lses, expand, w, res, g, b)


def _roll_rope_tables(seq, rot, theta, width, scale):
    half = rot // 2
    inv_freq = theta ** (-jnp.arange(half, dtype=F32) / half)
    ang = jnp.arange(seq, dtype=F32)[:, None] * inv_freq[None, :]
    cos, sin = jnp.cos(ang), jnp.sin(ang)
    pad = LANES - rot
    c = jnp.concatenate([cos, cos, jnp.where(jnp.arange(pad) < width - rot, 1.0, 0.0)[None, :]
                         * jnp.ones((seq, pad), F32)], axis=1)
    s1 = jnp.concatenate([-sin, jnp.zeros((seq, LANES - half), F32)], axis=1)
    s2 = jnp.concatenate([jnp.zeros((seq, half), F32), sin, jnp.zeros((seq, pad), F32)], axis=1)
    return c * scale, s1 * scale, s2 * scale


def _dilated(x, w_in, batch, seq):
    hd = DIL_HEADS * DIL_HEAD_DIM
    cq, s1q, s2q = _roll_rope_tables(seq, DIL_ROT, ROPE_THETA, LANES, DIL_HEAD_DIM ** -0.5)
    ck, s1k, s2k = _roll_rope_tables(seq, DIL_ROT, ROPE_THETA, LANES, 1.0)
    outs, lses = [], []
    for gi, (_, dil) in enumerate(DIL_PAIRS):
        sub = seq // dil
        tm = min(512, sub)
        nt = sub // tm
        xv = x.reshape(batch * sub, dil * D_MODEL)
        tabs = [jnp.stack([a, b]).reshape(2, sub, dil * LANES) for a, b in
                ((cq, ck), (s1q, s1k), (s2q, s2k))]
        x_map = lambda i, j, dil=dil, nt=nt: ((i // (dil * nt)) * nt + i % nt, (i // nt) % dil)
        tab_map = lambda i, j, dil=dil, nt=nt: (jnp.minimum(j, 1), i % nt, (i // nt) % dil)
        qkv = _mm(xv, w_in[:, gi * 3 * hd:(gi + 1) * 3 * hd], tm=tm, tn=hd, m_rows=batch * seq,
                  x_map=x_map, tabs=tabs, tab_map=tab_map, pattern=("rope",) * DIL_HEADS,
                  shift=DIL_ROT // 2, rope_jmax=2, name=f"dil_proj_{dil}")
        o, lse = _dil_attn(qkv, batch, seq, dil)
        outs.append(o.reshape(batch * seq, hd))
        lses.append(lse.reshape(batch * seq, LANES))
    return outs, lses


def _mla_down_body(x_ref, w_ref, nq_ref, nkv_ref, c_ref, s1_ref, s2_ref, cq_ref, ckv_ref, kpe_ref):
    acc = _dot(x_ref[...].astype(BF16), w_ref[...])
    cq = acc[:, :MLA_Q_RANK]
    ckv = acc[:, MLA_Q_RANK:MLA_Q_RANK + MLA_KV_RANK]
    kpe = acc[:, MLA_Q_RANK + MLA_KV_RANK:]
    cq = cq * lax.rsqrt(jnp.mean(cq * cq, axis=-1, keepdims=True) + RMS_EPS) * nq_ref[...]
    ckv = ckv * lax.rsqrt(jnp.mean(ckv * ckv, axis=-1, keepdims=True) + RMS_EPS) * nkv_ref[...]
    sh = MLA_ROPE // 2
    kpe = (kpe * c_ref[...] + pltpu.roll(kpe, LANES - sh, 1) * s1_ref[...]
           + pltpu.roll(kpe, sh, 1) * s2_ref[...])
    cq_ref[...] = cq.astype(cq_ref.dtype)
    ckv_ref[...] = ckv.astype(ckv_ref.dtype)
    kpe_ref[...] = kpe.astype(kpe_ref.dtype)


def _mla_down(x, w, nq, nkv, tabs, seq, *, tm=512):
    m, d = x.shape
    n = w.shape[1]
    row = lambda i: (i, 0)
    ns = seq // tm
    tab = lambda i: (i % ns, 0)
    return pl.pallas_call(
        _mla_down_body,
        grid=(m // tm,),
        in_specs=[pl.BlockSpec((tm, d), row), _resident((d, n)), _resident((1, MLA_Q_RANK)),
                  _resident((1, MLA_KV_RANK))] + [pl.BlockSpec((tm, LANES), tab)] * 3,
        out_specs=[pl.BlockSpec((tm, MLA_Q_RANK), row), pl.BlockSpec((tm, MLA_KV_RANK), row),
                   pl.BlockSpec((tm, LANES), row)],
        out_shape=[jax.ShapeDtypeStruct((m, MLA_Q_RANK), BF16),
                   jax.ShapeDtypeStruct((m, MLA_KV_RANK), BF16),
                   jax.ShapeDtypeStruct((m, LANES), BF16)],
        compiler_params=_cparams(("parallel",)),
        name="mla_down",
    )(x, w, nq, nkv, *tabs)


def _mla_flash_body(q_ref, kn_ref, kpe_ref, v_ref, o_ref, kcat_ref):
    blk = MLA_BLOCK
    iq = pl.program_id(2)

    @pl.when(iq == 0)
    def _():
        kcat_ref[:, :MLA_NOPE] = kn_ref[...]
        kcat_ref[:, MLA_NOPE:] = kpe_ref[...]

    q = q_ref[...]

    def step(j, carry, masked):
        m, l, acc = carry
        start = pl.multiple_of(j * blk, blk)
        s = _dot_nt(q, kcat_ref[pl.ds(start, blk), :])
        if masked:
            row = lax.broadcasted_iota(jnp.int32, (blk, blk), 0)
            col = lax.broadcasted_iota(jnp.int32, (blk, blk), 1)
            s = jnp.where(row >= col, s, NEG)
        m_new = jnp.maximum(m, jnp.max(s, axis=-1, keepdims=True))
        alpha = jnp.exp(m - m_new)
        p = jnp.exp(s - m_new)
        l = alpha * l + jnp.sum(p, axis=-1, keepdims=True)
        acc = alpha * acc + _dot(p.astype(BF16), v_ref[pl.ds(start, blk), :])
        return m_new, l, acc

    init = (jnp.full((blk, 1), NEG, F32), jnp.zeros((blk, 1), F32), jnp.zeros((blk, MLA_V), F32))
    carry = lax.fori_loop(0, iq, lambda j, c: step(j, c, False), init)
    _, l, acc = step(iq, carry, True)
    o_ref[...] = (acc / l).astype(o_ref.dtype)


def _mla_flash(q, kv, kpe, batch, seq):
    blk = MLA_BLOCK
    nq = seq // blk
    h_ = MLA_HEADS
    qw = MLA_NOPE + LANES
    return pl.pallas_call(
        _mla_flash_body,
        grid=(batch, h_, nq),
        in_specs=[pl.BlockSpec((blk, qw), lambda b, h, i: (b * nq + i, h)),
                  pl.BlockSpec((seq, MLA_NOPE), lambda b, h, i: (b, h)),
                  pl.BlockSpec((seq, LANES), lambda b, h, i: (b, 0)),
                  pl.BlockSpec((seq, MLA_V), lambda b, h, i: (b, h_ + h))],
        out_specs=pl.BlockSpec((blk, MLA_V), lambda b, h, i: (b * nq + i, h)),
        out_shape=jax.ShapeDtypeStruct((batch * seq, h_ * MLA_V), BF16),
        scratch_shapes=[pltpu.VMEM((seq, qw), BF16)],
        compiler_params=_cparams(("parallel", "parallel", "arbitrary")),
        name="mla_flash",
    )(q, kv, kpe, kv)


def _mla(x, w_down, norm_q, norm_kv, w_uq, w_ukv, batch, seq):
    h_ = MLA_HEADS
    npad = LANES - MLA_ROPE
    wd = jnp.pad(w_down, ((0, 0), (0, npad))).astype(BF16)
    wq = w_uq.reshape(MLA_Q_RANK, h_, MLA_NOPE + MLA_ROPE)
    wq = jnp.pad(wq, ((0, 0), (0, 0), (0, npad))).reshape(MLA_Q_RANK, -1).astype(BF16)
    wkv = w_ukv.reshape(MLA_KV_RANK, h_, MLA_NOPE + MLA_V)
    wkv = jnp.concatenate([wkv[:, :, :MLA_NOPE].reshape(MLA_KV_RANK, -1),
                           wkv[:, :, MLA_NOPE:].reshape(MLA_KV_RANK, -1)], axis=1).astype(BF16)
    scale = (MLA_NOPE + MLA_ROPE) ** -0.5
    tk = _roll_rope_tables(seq, MLA_ROPE, MLA_THETA, MLA_ROPE, 1.0)
    tq = [t[None] for t in _roll_rope_tables(seq, MLA_ROPE, MLA_THETA, MLA_ROPE, scale)]
    cq, ckv, kpe = _mla_down(x, wd, norm_q[None, :], norm_kv[None, :], tk, seq)
    tm = 512
    ns = seq // tm
    q = _mm(cq, wq, tm=tm, tn=1024, m_rows=batch * seq, tabs=tq,
            tab_map=lambda i, j: (0, i % ns, 0), pattern=("scale", "rope") * 4,
            shift=MLA_ROPE // 2, scale=scale, name="mla_q")
    kv = _mm(ckv, wkv, tm=tm, tn=1024, m_rows=batch * seq, name="mla_kv")
    return _mla_flash(q, kv, kpe, batch, seq)


def _head_sum(z, ones_bd):
    hi, lo = _split2(z)
    return _dot(hi, ones_bd) + _dot(lo, ones_bd)


def _rwkv_prep_body(x_ref, xp_ref, mu_ref, wr_ref, wk_ref, wv_ref, la0_ref, lb0_ref, la1_ref,
                    lb1_ref, ga_ref, gb_ref, vec_ref, bd_ref, r_ref, lw_ref, k_ref, v_ref,
                    kk_ref, b_ref, g_ref, *, tiles_per_seq):
    x = x_ref[...]
    tm = x.shape[0]
    first = pl.program_id(0) % tiles_per_seq == 0
    prev_row = jnp.where(first, 0.0, xp_ref[7:8, :])
    rows = lax.broadcasted_iota(jnp.int32, x.shape, 0)
    shifted = jnp.where(rows == 0, prev_row, pltpu.roll(x, 1, 0))
    xx = shifted - x
    mix = lambda i: (x + xx * mu_ref[i:i + 1, :]).astype(BF16)
    r = _dot(mix(0), wr_ref[...])
    k_raw = _dot(mix(2), wk_ref[...])
    v = _dot(mix(3), wv_ref[...])
    w0, a0, k_k, k_a = (vec_ref[i:i + 1, :] for i in range(4))
    wl = w0 + _dot(jnp.tanh(_dot(mix(1), la0_ref[...])).astype(BF16), lb0_ref[...])
    z = -wl
    softplus = jnp.maximum(z, 0.0) + jnp.log(1.0 + jnp.exp(-jnp.abs(z)))
    lw_ref[...] = -jnp.exp(-softplus - 0.5)
    al = a0 + _dot(_dot(mix(4), la1_ref[...]).astype(BF16), lb1_ref[...])
    a = 1.0 / (1.0 + jnp.exp(-al))
    gl = _dot(mix(5), ga_ref[...])
    g = _dot((1.0 / (1.0 + jnp.exp(-gl))).astype(BF16), gb_ref[...])
    kk = k_raw * k_k
    bd = bd_ref[...]
    for s in range(D_MODEL // LANES):
        sl = slice(s * LANES, (s + 1) * LANES)
        t = kk[:, sl]
        nrm = jnp.maximum(jnp.sqrt(_head_sum(t * t, bd)), 1e-12)
        t = t / nrm
        kk_ref[:, sl] = t.astype(kk_ref.dtype)
        b_ref[:, sl] = (t * a[:, sl]).astype(b_ref.dtype)
    r_ref[...] = r.astype(r_ref.dtype)
    k_ref[...] = (k_raw * (1.0 + (a - 1.0) * k_a)).astype(k_ref.dtype)
    v_ref[...] = v.astype(v_ref.dtype)
    g_ref[...] = g.astype(g_ref.dtype)


def _head_ones():
    idx = jnp.arange(LANES) // RWKV_HEAD
    return (idx[:, None] == idx[None, :]).astype(BF16)


def _rwkv_prep(x, mu, w_rkv, vec, lora_a, lora_b, gate_a, gate_b, seq, *, tm=256):
    m, d = x.shape
    gpad = RWKV_GATE_PAD - gate_a.shape[1]
    ga = jnp.pad(gate_a, ((0, 0), (0, gpad))).astype(BF16)
    gb = jnp.pad(gate_b, ((0, gpad), (0, 0))).astype(BF16)
    wts = [w_rkv[0].astype(BF16), w_rkv[1].astype(BF16), w_rkv[2].astype(BF16),
           lora_a[0].astype(BF16), lora_b[0].astype(BF16), lora_a[1].astype(BF16),
           lora_b[1].astype(BF16), ga, gb]
    vec8 = jnp.pad(vec, ((0, 3), (0, 0)))
    mu8 = jnp.pad(mu, ((0, 2), (0, 0)))
    row = lambda i: (i, 0)
    sub = tm // 8
    out = jax.ShapeDtypeStruct((m, d), BF16)
    return pl.pallas_call(
        functools.partial(_rwkv_prep_body, tiles_per_seq=seq // tm),
        grid=(m // tm,),
        in_specs=[pl.BlockSpec((tm, d), row),
                  pl.BlockSpec((8, d), lambda i: (jnp.maximum(i * sub - 1, 0), 0)),
                  _resident(mu8.shape)] + [_resident(w.shape) for w in wts]
        + [_resident(vec8.shape), _resident((LANES, LANES))],
        out_specs=[pl.BlockSpec((tm, d), row)] * 7,
        out_shape=[out, jax.ShapeDtypeStruct((m, d), F32), out, out, out, out, out],
        compiler_params=_cparams(("parallel",)),
        name="rwkv_prep",
    )(x, x, mu8, *wts, vec8, _head_ones())


def _rwkv_wkv_body(r_ref, lw_ref, k_ref, v_ref, kk_ref, b_ref, tri_ref, y_ref, state_ref):
    c = RWKV_CHUNK
    two = 2 * c

    @pl.when(pl.program_id(1) == 0)
    def _():
        state_ref[...] = jnp.zeros(state_ref.shape, F32)

    lw = lw_ref[...]
    tri = tri_ref[...]
    h1 = lw.astype(BF16)
    r1 = lw - h1.astype(F32)
    h2 = r1.astype(BF16)
    h3 = (r1 - h2.astype(F32)).astype(BF16)
    cum = _dot(tri, h1) + _dot(tri, h2) + _dot(tri, h3)
    gam = jnp.exp(cum)
    gam_ex = jnp.exp(cum - lw)
    gam_inv = jnp.exp(-cum)
    rt = r_ref[...].astype(F32) * gam
    kkt = kk_ref[...].astype(F32) * gam_ex
    bt = b_ref[...].astype(F32) * gam_inv
    kt = k_ref[...].astype(F32) * gam_inv
    gam_end = gam[c - 1:c, :]

    lane_lo = lax.broadcasted_iota(jnp.int32, (c, LANES), 1) < RWKV_HEAD
    row2 = lax.broadcasted_iota(jnp.int32, (two, two), 0)
    col2 = lax.broadcasted_iota(jnp.int32, (two, two), 1)
    same = (row2 // c) == (col2 // c)
    strict = jnp.logical_and(same, row2 > col2)
    incl = jnp.logical_and(same, row2 >= col2)
    eye = (row2 == col2).astype(F32)

    def stack_masked(t):
        return jnp.concatenate([jnp.where(lane_lo, t, 0.0), jnp.where(lane_lo, 0.0, t)], axis=0)

    for p in range(D_MODEL // LANES):
        sl = slice(p * LANES, (p + 1) * LANES)
        xs = jnp.concatenate([stack_masked(kkt[:, sl]), stack_masked(rt[:, sl])], axis=0).astype(BF16)
        bd_ = jnp.concatenate([bt[:, sl], bt[:, sl]], axis=0).astype(BF16)
        kd_ = jnp.concatenate([kt[:, sl], kt[:, sl]], axis=0).astype(BF16)
        yd = jnp.concatenate([bd_, kd_], axis=0)
        a_all = _dot_nt(xs, yd)
        nmat = jnp.where(strict, -a_all[:two, :two], 0.0)
        lk = jnp.where(strict, a_all[:two, two:], 0.0)
        arb = jnp.where(incl, a_all[two:, :two], 0.0)
        ark = jnp.where(incl, a_all[two:, two:], 0.0)
        inv = eye + nmat
        pw = nmat
        for _ in range(int(math.log2(c)) - 1):
            pwb = pw.astype(BF16)
            pw = _dot(pwb, pwb)
            inv = inv + _dot(inv.astype(BF16), pw.astype(BF16))
        s2 = state_ref[p]
        xs_state = _dot_nt(xs, s2.astype(BF16))
        v2 = v_ref[:, sl].astype(F32)
        vs = jnp.where(same, jnp.concatenate([v2, v2], axis=0), 0.0).astype(BF16)
        rhs = xs_state[:two] + _dot(lk.astype(BF16), vs)
        u = -_dot(inv.astype(BF16), rhs.astype(BF16))
        ub = u.astype(BF16)
        ys = xs_state[two:] + _dot(arb.astype(BF16), ub) + _dot(ark.astype(BF16), vs)
        y_ref[:, sl] = ys[:c] + ys[c:]
        ds = _dot_tn(ub, bd_) + _dot_tn(vs, kd_)
        state_ref[p] = jnp.where(same, (s2 + ds) * gam_end[:, sl], 0.0)


def _rwkv_wkv(r, lw, k, v, kk, b, batch, seq):
    c = RWKV_CHUNK
    n = seq // c
    d = D_MODEL
    tri = (jnp.arange(c)[:, None] >= jnp.arange(c)[None, :]).astype(BF16)
    blk = pl.BlockSpec((c, d), lambda bi, i: (bi * n + i, 0))
    return pl.pallas_call(
        _rwkv_wkv_body,
        grid=(batch, n),
        in_specs=[blk] * 6 + [_resident((c, c))],
        out_specs=blk,
        out_shape=jax.ShapeDtypeStruct((batch * seq, d), F32),
        scratch_shapes=[pltpu.VMEM((d // LANES, LANES, LANES), F32)],
        compiler_params=_cparams(("parallel", "arbitrary")),
        name="rwkv_wkv",
    )(r, lw, k, v, kk, b, tri)


def _rwkv_out_body(y_ref, r_ref, k_ref, v_ref, g_ref, vec_ref, bd_ref, w_ref, res_ref, lg_ref,
                   lb_ref, o_ref, a_ref):
    bd = bd_ref[...]
    inv_n = 1.0 / RWKV_HEAD
    for s in range(D_MODEL // LANES):
        sl = slice(s * LANES, (s + 1) * LANES)
        y = y_ref[:, sl]
        mu = _head_sum(y, bd) * inv_n
        dlt = y - mu
        var = _head_sum(dlt * dlt, bd) * inv_n
        yn = dlt * lax.rsqrt(var + RWKV_GN_EPS) * vec_ref[0:1, sl] + vec_ref[1:2, sl]
        rk = r_ref[:, sl].astype(F32) * k_ref[:, sl].astype(F32) * vec_ref[2:3, sl]
        bonus = _head_sum(rk, bd) * v_ref[:, sl].astype(F32)
        a_ref[:, sl] = ((yn + bonus) * g_ref[:, sl].astype(F32)).astype(BF16)
    acc = _dot(a_ref[...], w_ref[...])
    o_ref[...] = _layer_norm(DN_ALPHA * res_ref[...] + acc, lg_ref[...], lb_ref[...])


def _rwkv_out(y, r, k, v, g, vec, w, res, lg, lb, *, tm=512):
    m, d = res.shape
    row = lambda i: (i, 0)
    act = pl.BlockSpec((tm, d), row)
    return pl.pallas_call(
        _rwkv_out_body,
        grid=(m // tm,),
        in_specs=[act] * 5 + [_resident(vec.shape), _resident((LANES, LANES)), _resident((d, d)),
                              act, _resident((1, d)), _resident((1, d))],
        out_specs=act,
        out_shape=jax.ShapeDtypeStruct((m, d), F32),
        scratch_shapes=[pltpu.VMEM((tm, d), BF16)],
        compiler_params=_cparams(("parallel",)),
        name="rwkv_out",
    )(y, r, k, v, g, vec, _head_ones(), w, res, lg, lb)


def kernel(x, ret_w_in, ret_gn, ret_w_out, dil_w_in, dil_w_out, mla_w_down, mla_norm_q,
           mla_norm_kv, mla_w_uq, mla_w_ukv, mla_w_out, rwkv_mu, rwkv_w_rkv, rwkv_w_out,
           rwkv_vec, rwkv_lora_a, rwkv_lora_b, rwkv_gate_a, rwkv_gate_b, rwkv_ln_x,
           mlp_w1, mlp_w2, ln_g, ln_b):
    batch, seq, d = x.shape
    xf = x.reshape(batch * seq, d)
    n_mixers = 4
    for i in range(DEPTH):
        mixer, j = i % n_mixers, i // n_mixers
        lg, lb = ln_g[i, 0][None, :], ln_b[i, 0][None, :]
        if mixer == 0:
            proj = _mm(xf, ret_w_in[j].astype(BF16), tm=1024, tn=1024, m_rows=batch * seq,
                       name="ret_proj")
            gated = _retention(proj, ret_gn[j], batch, seq)
            xf = _mm_res_ln(gated, ret_w_out[j].astype(BF16), xf, lg, lb, name="ret_out")
        elif mixer == 1:
            outs, lses = _dilated(xf, dil_w_in[j].astype(BF16), batch, seq)
            xf = _dil_out(outs, lses, dil_w_out[j].astype(BF16), xf, lg, lb)
        elif mixer == 2:
            o = _mla(xf, mla_w_down[j], mla_norm_q[j], mla_norm_kv[j], mla_w_uq[j], mla_w_ukv[j],
                     batch, seq)
            xf = _mm_res_ln(o, mla_w_out[j].astype(BF16), xf, lg, lb, name="mla_out")
        else:
            r, lw, k, v, kk, b, g = _rwkv_prep(xf, rwkv_mu[j], rwkv_w_rkv[j], rwkv_vec[j],
                                               rwkv_lora_a[j], rwkv_lora_b[j], rwkv_gate_a[j],
                                               rwkv_gate_b[j], seq)
            y = _rwkv_wkv(r, lw, k, v, kk, b, batch, seq)
            vec = jnp.concatenate([rwkv_ln_x[j], rwkv_vec[j][4:5],
                                   jnp.zeros((5, d), F32)], axis=0)
            xf = _rwkv_out(y, r, k, v, g, vec, rwkv_w_out[j].astype(BF16), xf, lg, lb)
        xf = _mlp(xf, mlp_w1[i].astype(BF16), mlp_w2[i].astype(BF16),
                  ln_g[i, 1][None, :], ln_b[i, 1][None, :])
    return xf.reshape(batch, seq, d)
```

```python
import functools
import math

import jax
import jax.numpy as jnp
from jax import lax
from jax.experimental import pallas as pl
from jax.experimental.pallas import tpu as pltpu

F32 = jnp.float32
BF16 = jnp.bfloat16

D_MODEL = 1024
DEPTH = 4
D_FF = 4 * D_MODEL
LN_EPS = 1e-5
RMS_EPS = 1e-6
GN_EPS = 1e-5
DN_ALPHA = (2.0 * DEPTH) ** 0.25
NEG = -1e30
LANES = 128
ROPE_PARTNER = 64
MM_SUB = 256

RET_HEADS = 4
RET_QK_DIM = 256
RET_V_DIM = 512
RET_CHUNK = 128
RET_THETA = 10000.0

DIL_PAIRS = ((128, 1), (512, 4), (2048, 16))
DIL_HEADS = 8
DIL_HEAD_DIM = 128
DIL_ROT = 32
DIL_BLOCK = 128
ROPE_THETA = 500000.0

MLA_HEADS = 16
MLA_NOPE = 128
MLA_ROPE = 64
MLA_V = 128
MLA_Q_RANK = 256
MLA_KV_RANK = 128
MLA_THETA = 10000.0
MLA_TQ = 512
MLA_TK = 1024
MLA_ROW_SPLIT = 2

RWKV_HEAD = 64
RWKV_HEADS = D_MODEL // RWKV_HEAD
RWKV_GN_EPS = 64e-5
RWKV_CHUNK = 64
RWKV_GATE_PAD = 256

VMEM_LIMIT = 56 * 1024 * 1024


def _cparams(sem):
    return pltpu.CompilerParams(dimension_semantics=sem, vmem_limit_bytes=VMEM_LIMIT)


def _resident(shape):
    nd = len(shape)
    return pl.BlockSpec(shape, lambda *_: (0,) * nd, pipeline_mode=pl.Buffered(1))


def _layer_norm(z, g, b):
    mu = jnp.mean(z, axis=-1, keepdims=True)
    d = z - mu
    var = jnp.mean(d * d, axis=-1, keepdims=True)
    return d * lax.rsqrt(var + LN_EPS) * g + b


def _dot(a, b):
    return jnp.dot(a, b, preferred_element_type=F32)


def _dot_nt(a, b):
    return lax.dot_general(a, b, (((1,), (1,)), ((), ())), preferred_element_type=F32)


def _dot_tn(a, b):
    return lax.dot_general(a, b, (((0,), (0,)), ((), ())), preferred_element_type=F32)


def _split2(z):
    hi = z.astype(BF16)
    lo = (z - hi.astype(F32)).astype(BF16)
    return hi, lo


def _rope_tile(a, c, s):
    return a * c + pltpu.roll(a, ROPE_PARTNER, 1) * s


def _mm_body(x_ref, w_ref, *rest, pattern, rope_jmax, scale, has_tabs):
    if has_tabs:
        c_ref, s_ref, o_ref, xb_ref = rest
    else:
        o_ref, xb_ref = rest
    j = pl.program_id(1)

    @pl.when(j == 0)
    def _():
        xb_ref[...] = x_ref[...].astype(BF16)

    def plain():
        o_ref[...] = _dot(xb_ref[...], w_ref[...]).astype(o_ref.dtype)

    def fancy():
        xb = xb_ref[...]
        for c0 in range(0, len(pattern) * LANES, MM_SUB):
            acc = _dot(xb, w_ref[:, c0:c0 + MM_SUB])
            for t in range(MM_SUB // LANES):
                a = acc[:, t * LANES:(t + 1) * LANES]
                mode = pattern[c0 // LANES + t]
                if mode == "rope":
                    a = _rope_tile(a, c_ref[0], s_ref[0])
                elif mode == "scale":
                    a = a * scale
                o_ref[:, c0 + t * LANES:c0 + (t + 1) * LANES] = a.astype(o_ref.dtype)

    if pattern is None:
        plain()
    elif rope_jmax is None:
        fancy()
    else:
        pl.when(j < rope_jmax)(fancy)
        pl.when(j >= rope_jmax)(plain)


def _mm(x, w, *, tm, tn, m_rows, out_dtype=BF16, x_map=None, tabs=None, tab_map=None,
        pattern=None, rope_jmax=None, scale=1.0, name="mm"):
    k, n = w.shape
    grid = (m_rows // tm, n // tn)
    if x_map is None:
        x_map = lambda i, j: (i, 0)
    in_specs = [pl.BlockSpec((tm, k), x_map), pl.BlockSpec((k, tn), lambda i, j: (0, j))]
    args = [x, w]
    if tabs is not None:
        for t in tabs:
            in_specs.append(pl.BlockSpec((1, tm, LANES), tab_map))
            args.append(t)
    body = functools.partial(_mm_body, pattern=pattern, rope_jmax=rope_jmax,
                             scale=scale, has_tabs=tabs is not None)
    return pl.pallas_call(
        body,
        grid=grid,
        in_specs=in_specs,
        out_specs=pl.BlockSpec((tm, tn), lambda i, j: (i, j)),
        out_shape=jax.ShapeDtypeStruct((m_rows, n), out_dtype),
        scratch_shapes=[pltpu.VMEM((tm, k), BF16)],
        compiler_params=_cparams(("parallel", "arbitrary")),
        name=name,
    )(*args)


def _mm_res_ln_body(a_ref, w_ref, res_ref, g_ref, b_ref, o_ref):
    acc = _dot(a_ref[...], w_ref[...])
    o_ref[...] = _layer_norm(DN_ALPHA * res_ref[...] + acc, g_ref[...], b_ref[...])


def _mm_res_ln(a, w, res, g, b, *, tm=512, name="mm_res_ln"):
    m, k = a.shape
    d = w.shape[1]
    row = lambda i: (i, 0)
    return pl.pallas_call(
        _mm_res_ln_body,
        grid=(m // tm,),
        in_specs=[pl.BlockSpec((tm, k), row), _resident((k, d)), pl.BlockSpec((tm, d), row),
                  _resident((1, d)), _resident((1, d))],
        out_specs=pl.BlockSpec((tm, d), row),
        out_shape=jax.ShapeDtypeStruct((m, d), F32),
        compiler_params=_cparams(("parallel",)),
        name=name,
    )(a, w, res, g, b)


def _mlp_body(x_ref, w1_ref, w2_ref, g_ref, b_ref, o_ref, *, fchunk):
    x = x_ref[...]
    xb = x.astype(BF16)
    acc = jnp.zeros(x.shape, F32)
    for c in range(D_FF // fchunk):
        h = _dot(xb, w1_ref[:, c * fchunk:(c + 1) * fchunk])
        h = jnp.maximum(h, 0.0)
        h = (h * h).astype(BF16)
        acc = acc + _dot(h, w2_ref[c * fchunk:(c + 1) * fchunk, :])
    o_ref[...] = _layer_norm(DN_ALPHA * x + acc, g_ref[...], b_ref[...])


def _mlp(x, w1, w2, g, b, *, tm=512, fchunk=1024):
    m, d = x.shape
    row = lambda i: (i, 0)
    return pl.pallas_call(
        functools.partial(_mlp_body, fchunk=fchunk),
        grid=(m // tm,),
        in_specs=[pl.BlockSpec((tm, d), row), _resident((d, D_FF)), _resident((D_FF, d)),
                  _resident((1, d)), _resident((1, d))],
        out_specs=pl.BlockSpec((tm, d), row),
        out_shape=jax.ShapeDtypeStruct((m, d), F32),
        compiler_params=_cparams(("parallel",)),
        name="mlp",
    )(x, w1, w2, g, b)


def _ret_body(q_ref, k_ref, v_ref, g_ref, cos_ref, sin_ref, intra_ref, qdec_ref, kdec_ref,
              cdec_ref, gn_ref, o_ref, state_ref):
    dk, dv, half = RET_QK_DIM, RET_V_DIM, RET_QK_DIM // 2

    @pl.when(pl.program_id(1) == 0)
    def _():
        state_ref[...] = jnp.zeros(state_ref.shape, F32)

    cos = cos_ref[...]
    sin = sin_ref[...]

    def rope(t):
        t1 = t[:, :half].astype(F32)
        t2 = t[:, half:].astype(F32)
        return jnp.concatenate([t1 * cos - t2 * sin, t2 * cos + t1 * sin], axis=-1)

    heads = range(RET_HEADS)
    q = [rope(q_ref[:, h * dk:(h + 1) * dk]) for h in heads]
    k = [rope(k_ref[:, h * dk:(h + 1) * dk]) * (dk ** -0.5) for h in heads]
    v = [v_ref[:, h * dv:(h + 1) * dv] for h in heads]
    qb = [t.astype(BF16) for t in q]
    state = [state_ref[h] for h in heads]
    scores = [(_dot_nt(qb[h], k[h].astype(BF16)) * intra_ref[h]).astype(BF16) for h in heads]
    cross = [_dot(qb[h], state[h].astype(BF16)) * qdec_ref[h] for h in heads]
    o = [_dot(scores[h], v[h]) + cross[h] for h in heads]
    for h in heads:
        state_ref[h] = (state[h] * cdec_ref[h, 0:1, :]
                        + _dot_tn((k[h] * kdec_ref[h]).astype(BF16), v[h]))
    for h in heads:
        sl = slice(h * dv, (h + 1) * dv)
        mu = jnp.mean(o[h], axis=-1, keepdims=True)
        d = o[h] - mu
        var = jnp.mean(d * d, axis=-1, keepdims=True)
        on = d * lax.rsqrt(var + GN_EPS) * gn_ref[0:1, sl] + gn_ref[1:2, sl]
        gate = g_ref[:, sl].astype(F32)
        gate = gate * (1.0 / (1.0 + jnp.exp(-gate)))
        o_ref[:, sl] = (gate * on).astype(o_ref.dtype)


def _retention(proj, gn, batch, seq):
    h_, dk, dv, c = RET_HEADS, RET_QK_DIM, RET_V_DIM, RET_CHUNK
    n = seq // c
    half = dk // 2
    pos = jnp.arange(seq, dtype=F32)
    inv_freq = RET_THETA ** (-jnp.arange(half, dtype=F32) / half)
    ang = pos[:, None] * inv_freq[None, :]
    cos, sin = jnp.cos(ang), jnp.sin(ang)
    log_gamma = jnp.log(1.0 - 2.0 ** (-5.0 - jnp.arange(h_, dtype=F32)))
    idx = jnp.arange(c, dtype=F32)
    diff = idx[:, None] - idx[None, :]
    intra = jnp.where(diff >= 0, jnp.exp(log_gamma[:, None, None] * jnp.maximum(diff, 0.0)), 0.0)
    qdec = jnp.broadcast_to(jnp.exp(log_gamma[:, None] * (idx + 1.0))[:, :, None], (h_, c, dv))
    kdec = jnp.broadcast_to(jnp.exp(log_gamma[:, None] * (c - 1.0 - idx))[:, :, None], (h_, c, dk))
    cdec = jnp.broadcast_to(jnp.exp(log_gamma * c)[:, None, None], (h_, 8, dv))
    qk_w, vg_w = h_ * dk, h_ * dv
    return pl.pallas_call(
        _ret_body,
        grid=(batch, n),
        in_specs=[
            pl.BlockSpec((c, qk_w), lambda b, i: (b * n + i, 0)),
            pl.BlockSpec((c, qk_w), lambda b, i: (b * n + i, 1)),
            pl.BlockSpec((c, vg_w), lambda b, i: (b * n + i, 2 * qk_w // vg_w)),
            pl.BlockSpec((c, vg_w), lambda b, i: (b * n + i, 2 * qk_w // vg_w + 1)),
            pl.BlockSpec((c, half), lambda b, i: (i, 0)),
            pl.BlockSpec((c, half), lambda b, i: (i, 0)),
            _resident((h_, c, c)), _resident((h_, c, dv)), _resident((h_, c, dk)),
            _resident((h_, 8, dv)), _resident((2, vg_w)),
        ],
        out_specs=pl.BlockSpec((c, vg_w), lambda b, i: (b * n + i, 0)),
        out_shape=jax.ShapeDtypeStruct((batch * seq, vg_w), BF16),
        scratch_shapes=[pltpu.VMEM((h_, dk, dv), F32)],
        compiler_params=_cparams(("parallel", "arbitrary")),
        name="retention",
    )(proj, proj, proj, proj, cos, sin, intra, qdec, kdec, cdec, gn)


def _dil_attn_body(q_ref, kp_ref, kc_ref, vp_ref, vc_ref, o_ref, lse_ref):
    blk = DIL_BLOCK
    has_prev = pl.program_id(1) > 0
    qi = lax.broadcasted_iota(jnp.int32, (blk, 2 * blk), 0)
    ki = lax.broadcasted_iota(jnp.int32, (blk, 2 * blk), 1)
    valid = jnp.logical_and(jnp.logical_and(ki >= qi, ki <= qi + blk),
                            jnp.logical_or(ki >= blk, has_prev))
    lane = lax.broadcasted_iota(jnp.int32, (blk, LANES), 1)
    ones = jnp.ones((2 * blk, LANES), BF16)
    heads = range(DIL_HEADS)
    sls = [slice(h * DIL_HEAD_DIM, (h + 1) * DIL_HEAD_DIM) for h in heads]
    s = [jnp.where(valid, _dot_nt(q_ref[:, sl], jnp.concatenate([kp_ref[:, sl], kc_ref[:, sl]], axis=0)),
                   NEG) for sl in sls]
    m = [jnp.max(t, axis=-1, keepdims=True) for t in s]
    p = [jnp.exp(s[h] - m[h]).astype(BF16) for h in heads]
    pv = [_dot(p[h], jnp.concatenate(
        [jnp.concatenate([vp_ref[:, sls[h]], vc_ref[:, sls[h]]], axis=0), ones], axis=1)) for h in heads]
    lse_tile = jnp.zeros((blk, LANES), F32)
    for h in heads:
        l = pv[h][:, DIL_HEAD_DIM:]
        o_ref[:, sls[h]] = (pv[h][:, :DIL_HEAD_DIM] / l).astype(o_ref.dtype)
        lse_tile = jnp.where(lane == h, m[h] + jnp.log(l), lse_tile)
    lse_ref[...] = lse_tile


def _dil_attn(qkv, batch, seq, dil):
    blk = DIL_BLOCK
    hd = DIL_HEADS * DIL_HEAD_DIM
    nb = seq // dil // blk
    cur = lambda c: (lambda z, i: (z * nb + i, c))
    prev = lambda c: (lambda z, i: (z * nb + jnp.maximum(i - 1, 0), c))
    out_map = lambda z, i: ((z // dil) * nb + i, z % dil)
    return pl.pallas_call(
        _dil_attn_body,
        grid=(batch * dil, nb),
        in_specs=[pl.BlockSpec((blk, hd), cur(0)), pl.BlockSpec((blk, hd), prev(1)),
                  pl.BlockSpec((blk, hd), cur(1)), pl.BlockSpec((blk, hd), prev(2)),
                  pl.BlockSpec((blk, hd), cur(2))],
        out_specs=[pl.BlockSpec((blk, hd), out_map), pl.BlockSpec((blk, LANES), out_map)],
        out_shape=[jax.ShapeDtypeStruct((batch * seq // dil, dil * hd), BF16),
                   jax.ShapeDtypeStruct((batch * seq // dil, dil * LANES), F32)],
        compiler_params=_cparams(("parallel", "arbitrary")),
        name=f"dil_attn_{dil}",
    )(qkv, qkv, qkv, qkv, qkv)


def _dil_out_body(o0_ref, o1_ref, o2_ref, l0_ref, l1_ref, l2_ref, e_ref, w_ref, res_ref,
                  g_ref, b_ref, o_ref):
    l0, l1, l2 = l0_ref[...], l1_ref[...], l2_ref[...]
    m = jnp.maximum(jnp.maximum(l0, l1), l2)
    e0, e1, e2 = jnp.exp(l0 - m), jnp.exp(l1 - m), jnp.exp(l2 - m)
    den = e0 + e1 + e2
    e = e_ref[...]
    mixed = None
    for eg, og in ((e0, o0_ref), (e1, o1_ref), (e2, o2_ref)):
        hi, lo = _split2(eg / den)
        wfull = _dot(hi, e) + _dot(lo, e)
        term = wfull * og[...].astype(F32)
        mixed = term if mixed is None else mixed + term
    acc = _dot(mixed.astype(BF16), w_ref[...])
    o_ref[...] = _layer_norm(DN_ALPHA * res_ref[...] + acc, g_ref[...], b_ref[...])


def _dil_out(outs, lses, w, res, g, b, *, tm=512):
    m, d = res.shape
    hd = DIL_HEADS * DIL_HEAD_DIM
    expand = (jnp.arange(LANES)[:, None] == (jnp.arange(hd) // DIL_HEAD_DIM)[None, :]).astype(BF16)
    row = lambda i: (i, 0)
    return pl.pallas_call(
        _dil_out_body,
        grid=(m // tm,),
        in_specs=[pl.BlockSpec((tm, hd), row)] * 3 + [pl.BlockSpec((tm, LANES), row)] * 3
        + [_resident((LANES, hd)), _resident((hd, d)), pl.BlockSpec((tm, d), row),
           _resident((1, d)), _resident((1, d))],
        out_specs=pl.BlockSpec((tm, d), row),
        out_shape=jax.ShapeDtypeStruct((m, d), F32),
        compiler_params=_cparams(("parallel",)),
        name="dil_out",
    )(*outs, *lses, expand, w, res, g, b)


def _rope_tables(seq, rot, theta, scale, passthrough):
    half = rot // 2
    inv_freq = theta ** (-jnp.arange(half, dtype=F32) / half)
    ang = jnp.arange(seq, dtype=F32)[:, None] * inv_freq[None, :]
    cos, sin = jnp.cos(ang), jnp.sin(ang)
    fill = jnp.full((seq, ROPE_PARTNER - half), passthrough, F32)
    zero = jnp.zeros((seq, ROPE_PARTNER - half), F32)
    c = jnp.concatenate([cos, fill, cos, fill], axis=1)
    s = jnp.concatenate([-sin, zero, sin, zero], axis=1)
    return c * scale, s * scale


def _rope_lane_order(rot, width):
    half = rot // 2
    rest = list(range(rot, width))
    cut = ROPE_PARTNER - half
    return jnp.array(list(range(half)) + rest[:cut] + list(range(half, rot)) + rest[cut:])


def _dilated(x, w_in, batch, seq):
    hd = DIL_HEADS * DIL_HEAD_DIM
    cq, sq = _rope_tables(seq, DIL_ROT, ROPE_THETA, DIL_HEAD_DIM ** -0.5, 1.0)
    ck, sk = _rope_tables(seq, DIL_ROT, ROPE_THETA, 1.0, 1.0)
    order = _rope_lane_order(DIL_ROT, DIL_HEAD_DIM)
    outs, lses = [], []
    for gi, (_, dil) in enumerate(DIL_PAIRS):
        sub = seq // dil
        tm = min(512, sub)
        nt = sub // tm
        xv = x.reshape(batch * sub, dil * D_MODEL)
        wg = w_in[:, gi * 3 * hd:(gi + 1) * 3 * hd].reshape(D_MODEL, 3, DIL_HEADS, DIL_HEAD_DIM)
        wg = jnp.concatenate([wg[:, :2][..., order], wg[:, 2:]], axis=1).reshape(D_MODEL, 3 * hd)
        tabs = [jnp.stack([a, b]).reshape(2, sub, dil * LANES) for a, b in ((cq, ck), (sq, sk))]
        x_map = lambda i, j, dil=dil, nt=nt: ((i // (dil * nt)) * nt + i % nt, (i // nt) % dil)
        tab_map = lambda i, j, dil=dil, nt=nt: (jnp.minimum(j, 1), i % nt, (i // nt) % dil)
        qkv = _mm(xv, wg, tm=tm, tn=hd, m_rows=batch * seq,
                  x_map=x_map, tabs=tabs, tab_map=tab_map, pattern=("rope",) * DIL_HEADS,
                  rope_jmax=2, name=f"dil_proj_{dil}")
        o, lse = _dil_attn(qkv, batch, seq, dil)
        outs.append(o.reshape(batch * seq, hd))
        lses.append(lse.reshape(batch * seq, LANES))
    return outs, lses


def _mla_down_body(x_ref, w_ref, nq_ref, nkv_ref, c_ref, s_ref, cq_ref, ckv_ref, kpe_ref):
    acc = _dot(x_ref[...].astype(BF16), w_ref[...])
    cq = acc[:, :MLA_Q_RANK]
    ckv = acc[:, MLA_Q_RANK:MLA_Q_RANK + MLA_KV_RANK]
    kpe = acc[:, MLA_Q_RANK + MLA_KV_RANK:]
    cq = cq * lax.rsqrt(jnp.mean(cq * cq, axis=-1, keepdims=True) + RMS_EPS) * nq_ref[...]
    ckv = ckv * lax.rsqrt(jnp.mean(ckv * ckv, axis=-1, keepdims=True) + RMS_EPS) * nkv_ref[...]
    kpe = _rope_tile(kpe, c_ref[...], s_ref[...])
    cq_ref[...] = cq.astype(cq_ref.dtype)
    ckv_ref[...] = ckv.astype(ckv_ref.dtype)
    kpe_ref[...] = kpe.astype(kpe_ref.dtype)


def _mla_down(x, w, nq, nkv, tabs, seq, *, tm=512):
    m, d = x.shape
    n = w.shape[1]
    row = lambda i: (i, 0)
    ns = seq // tm
    tab = lambda i: (i % ns, 0)
    return pl.pallas_call(
        _mla_down_body,
        grid=(m // tm,),
        in_specs=[pl.BlockSpec((tm, d), row), _resident((d, n)), _resident((1, MLA_Q_RANK)),
                  _resident((1, MLA_KV_RANK))] + [pl.BlockSpec((tm, LANES), tab)] * 2,
        out_specs=[pl.BlockSpec((tm, MLA_Q_RANK), row), pl.BlockSpec((tm, MLA_KV_RANK), row),
                   pl.BlockSpec((tm, LANES), row)],
        out_shape=[jax.ShapeDtypeStruct((m, MLA_Q_RANK), BF16),
                   jax.ShapeDtypeStruct((m, MLA_KV_RANK), BF16),
                   jax.ShapeDtypeStruct((m, LANES), BF16)],
        compiler_params=_cparams(("parallel",)),
        name="mla_down",
    )(x, w, nq, nkv, *tabs)


def _mla_flash_body(q_ref, kn_ref, kpe_ref, v_ref, o_ref, kcat_ref):
    tq, tk, sub = MLA_TQ, MLA_TK, MLA_TQ // MLA_ROW_SPLIT
    iq = pl.program_id(2)

    @pl.when(iq == 0)
    def _():
        kcat_ref[:, :MLA_NOPE] = kn_ref[...]
        kcat_ref[:, MLA_NOPE:] = kpe_ref[...]

    def step(start, carry, masked):
        start = pl.multiple_of(start, tk)
        kb = kcat_ref[pl.ds(start, tk), :]
        vb = v_ref[pl.ds(start, tk), :]
        out = []
        for part, (m, l, acc) in enumerate(carry):
            q = q_ref[part * sub:(part + 1) * sub, :]
            s = _dot_nt(q, kb)
            if masked:
                row = lax.broadcasted_iota(jnp.int32, (sub, tk), 0) + (iq * tq + part * sub - start)
                col = lax.broadcasted_iota(jnp.int32, (sub, tk), 1)
                s = jnp.where(row >= col, s, NEG)
            m_new = jnp.maximum(m, jnp.max(s, axis=-1, keepdims=True))
            alpha = jnp.exp(m - m_new)
            p = jnp.exp(s - m_new)
            l = alpha * l + jnp.sum(p, axis=-1, keepdims=True)
            acc = alpha * acc + _dot(p.astype(BF16), vb)
            out.append((m_new, l, acc))
        return tuple(out)

    init = tuple((jnp.full((sub, 1), NEG, F32), jnp.zeros((sub, 1), F32),
                  jnp.zeros((sub, MLA_V), F32)) for _ in range(MLA_ROW_SPLIT))
    n_full = (iq * tq) // tk
    carry = lax.fori_loop(0, n_full, lambda j, c: step(j * tk, c, False), init)
    carry = step(n_full * tk, carry, True)
    for part, (_, l, acc) in enumerate(carry):
        o_ref[part * sub:(part + 1) * sub, :] = (acc / l).astype(o_ref.dtype)


def _mla_flash(q, kv, kpe, batch, seq):
    blk = MLA_TQ
    nq = seq // blk
    h_ = MLA_HEADS
    qw = MLA_NOPE + LANES
    return pl.pallas_call(
        _mla_flash_body,
        grid=(batch, h_, nq),
        in_specs=[pl.BlockSpec((blk, qw), lambda b, h, i: (b * nq + i, h)),
                  pl.BlockSpec((seq, MLA_NOPE), lambda b, h, i: (b, h)),
                  pl.BlockSpec((seq, LANES), lambda b, h, i: (b, 0)),
                  pl.BlockSpec((seq, MLA_V), lambda b, h, i: (b, h_ + h))],
        out_specs=pl.BlockSpec((blk, MLA_V), lambda b, h, i: (b * nq + i, h)),
        out_shape=jax.ShapeDtypeStruct((batch * seq, h_ * MLA_V), BF16),
        scratch_shapes=[pltpu.VMEM((seq, qw), BF16)],
        compiler_params=_cparams(("parallel", "parallel", "arbitrary")),
        name="mla_flash",
    )(q, kv, kpe, kv)


def _mla(x, w_down, norm_q, norm_kv, w_uq, w_ukv, batch, seq):
    h_ = MLA_HEADS
    half = MLA_ROPE // 2

    def pe_tile(w):
        z = jnp.zeros(w.shape[:-1] + (ROPE_PARTNER - half,), w.dtype)
        return jnp.concatenate([w[..., :half], z, w[..., half:], z], axis=-1)

    n_lat = MLA_Q_RANK + MLA_KV_RANK
    wd = jnp.concatenate([w_down[:, :n_lat], pe_tile(w_down[:, n_lat:])], axis=1).astype(BF16)
    wq = w_uq.reshape(MLA_Q_RANK, h_, MLA_NOPE + MLA_ROPE)
    wq = jnp.concatenate([wq[..., :MLA_NOPE], pe_tile(wq[..., MLA_NOPE:])], axis=-1)
    wq = wq.reshape(MLA_Q_RANK, -1).astype(BF16)
    wkv = w_ukv.reshape(MLA_KV_RANK, h_, MLA_NOPE + MLA_V)
    wkv = jnp.concatenate([wkv[:, :, :MLA_NOPE].reshape(MLA_KV_RANK, -1),
                           wkv[:, :, MLA_NOPE:].reshape(MLA_KV_RANK, -1)], axis=1).astype(BF16)
    scale = (MLA_NOPE + MLA_ROPE) ** -0.5
    tk = _rope_tables(seq, MLA_ROPE, MLA_THETA, 1.0, 0.0)
    tq = [t[None] for t in _rope_tables(seq, MLA_ROPE, MLA_THETA, scale, 0.0)]
    cq, ckv, kpe = _mla_down(x, wd, norm_q[None, :], norm_kv[None, :], tk, seq)
    tm = 512
    ns = seq // tm
    q = _mm(cq, wq, tm=tm, tn=1024, m_rows=batch * seq, tabs=tq,
            tab_map=lambda i, j: (0, i % ns, 0), pattern=("scale", "rope") * 4,
            scale=scale, name="mla_q")
    kv = _mm(ckv, wkv, tm=tm, tn=1024, m_rows=batch * seq, name="mla_kv")
    return _mla_flash(q, kv, kpe, batch, seq)


def _head_sum(z, ones_bd):
    hi, lo = _split2(z)
    return _dot(hi, ones_bd) + _dot(lo, ones_bd)


def _rwkv_prep_body(x_ref, xp_ref, mu_ref, wr_ref, wk_ref, wv_ref, la0_ref, lb0_ref, la1_ref,
                    lb1_ref, ga_ref, gb_ref, vec_ref, bd_ref, r_ref, lw_ref, k_ref, v_ref,
                    kk_ref, b_ref, g_ref, *, tiles_per_seq):
    x = x_ref[...]
    tm = x.shape[0]
    first = pl.program_id(0) % tiles_per_seq == 0
    prev_row = jnp.where(first, 0.0, xp_ref[7:8, :])
    rows = lax.broadcasted_iota(jnp.int32, x.shape, 0)
    shifted = jnp.where(rows == 0, prev_row, pltpu.roll(x, 1, 0))
    xx = shifted - x
    mix = lambda i: (x + xx * mu_ref[i:i + 1, :]).astype(BF16)
    r = _dot(mix(0), wr_ref[...])
    k_raw = _dot(mix(2), wk_ref[...])
    v = _dot(mix(3), wv_ref[...])
    w0, a0, k_k, k_a = (vec_ref[i:i + 1, :] for i in range(4))
    wl = w0 + _dot(jnp.tanh(_dot(mix(1), la0_ref[...])).astype(BF16), lb0_ref[...])
    z = -wl
    softplus = jnp.maximum(z, 0.0) + jnp.log(1.0 + jnp.exp(-jnp.abs(z)))
    lw_ref[...] = -jnp.exp(-softplus - 0.5)
    al = a0 + _dot(_dot(mix(4), la1_ref[...]).astype(BF16), lb1_ref[...])
    a = 1.0 / (1.0 + jnp.exp(-al))
    gl = _dot(mix(5), ga_ref[...])
    g = _dot((1.0 / (1.0 + jnp.exp(-gl))).astype(BF16), gb_ref[...])
    kk = k_raw * k_k
    bd = bd_ref[...]
    for s in range(D_MODEL // LANES):
        sl = slice(s * LANES, (s + 1) * LANES)
        t = kk[:, sl]
        nrm = jnp.maximum(jnp.sqrt(_head_sum(t * t, bd)), 1e-12)
        t = t / nrm
        kk_ref[:, sl] = t.astype(kk_ref.dtype)
        b_ref[:, sl] = (t * a[:, sl]).astype(b_ref.dtype)
    r_ref[...] = r.astype(r_ref.dtype)
    k_ref[...] = (k_raw * (1.0 + (a - 1.0) * k_a)).astype(k_ref.dtype)
    v_ref[...] = v.astype(v_ref.dtype)
    g_ref[...] = g.astype(g_ref.dtype)


def _head_ones():
    idx = jnp.arange(LANES) // RWKV_HEAD
    return (idx[:, None] == idx[None, :]).astype(BF16)


def _rwkv_prep(x, mu, w_rkv, vec, lora_a, lora_b, gate_a, gate_b, seq, *, tm=256):
    m, d = x.shape
    gpad = RWKV_GATE_PAD - gate_a.shape[1]
    ga = jnp.pad(gate_a, ((0, 0), (0, gpad))).astype(BF16)
    gb = jnp.pad(gate_b, ((0, gpad), (0, 0))).astype(BF16)
    wts = [w_rkv[0].astype(BF16), w_rkv[1].astype(BF16), w_rkv[2].astype(BF16),
           lora_a[0].astype(BF16), lora_b[0].astype(BF16), lora_a[1].astype(BF16),
           lora_b[1].astype(BF16), ga, gb]
    vec8 = jnp.pad(vec, ((0, 3), (0, 0)))
    mu8 = jnp.pad(mu, ((0, 2), (0, 0)))
    row = lambda i: (i, 0)
    sub = tm // 8
    out = jax.ShapeDtypeStruct((m, d), BF16)
    return pl.pallas_call(
        functools.partial(_rwkv_prep_body, tiles_per_seq=seq // tm),
        grid=(m // tm,),
        in_specs=[pl.BlockSpec((tm, d), row),
                  pl.BlockSpec((8, d), lambda i: (jnp.maximum(i * sub - 1, 0), 0)),
                  _resident(mu8.shape)] + [_resident(w.shape) for w in wts]
        + [_resident(vec8.shape), _resident((LANES, LANES))],
        out_specs=[pl.BlockSpec((tm, d), row)] * 7,
        out_shape=[out, jax.ShapeDtypeStruct((m, d), F32), out, out, out, out, out],
        compiler_params=_cparams(("parallel",)),
        name="rwkv_prep",
    )(x, x, mu8, *wts, vec8, _head_ones())


def _rwkv_wkv_body(r_ref, lw_ref, k_ref, v_ref, kk_ref, b_ref, tri_ref, y_ref, state_ref):
    c = RWKV_CHUNK
    two = 2 * c

    @pl.when(pl.program_id(1) == 0)
    def _():
        state_ref[...] = jnp.zeros(state_ref.shape, F32)

    lw = lw_ref[...]
    tri = tri_ref[...]
    h1 = lw.astype(BF16)
    r1 = lw - h1.astype(F32)
    h2 = r1.astype(BF16)
    h3 = (r1 - h2.astype(F32)).astype(BF16)
    cum = _dot(tri, h1) + _dot(tri, h2) + _dot(tri, h3)
    gam = jnp.exp(cum)
    gam_ex = jnp.exp(cum - lw)
    gam_inv = jnp.exp(-cum)
    rt = r_ref[...].astype(F32) * gam
    kkt = kk_ref[...].astype(F32) * gam_ex
    bt = b_ref[...].astype(F32) * gam_inv
    kt = k_ref[...].astype(F32) * gam_inv
    gam_end = gam[c - 1:c, :]

    lane_lo = lax.broadcasted_iota(jnp.int32, (c, LANES), 1) < RWKV_HEAD
    row2 = lax.broadcasted_iota(jnp.int32, (two, two), 0)
    col2 = lax.broadcasted_iota(jnp.int32, (two, two), 1)
    same = (row2 // c) == (col2 // c)
    strict = jnp.logical_and(same, row2 > col2)
    incl = jnp.logical_and(same, row2 >= col2)
    eye = (row2 == col2).astype(F32)

    def stack_masked(t):
        return jnp.concatenate([jnp.where(lane_lo, t, 0.0), jnp.where(lane_lo, 0.0, t)], axis=0)

    pairs = range(D_MODEL // LANES)
    sls = [slice(p * LANES, (p + 1) * LANES) for p in pairs]
    xs = [jnp.concatenate([stack_masked(kkt[:, sl]), stack_masked(rt[:, sl])], axis=0).astype(BF16)
          for sl in sls]
    bds = [jnp.concatenate([bt[:, sl], bt[:, sl]], axis=0).astype(BF16) for sl in sls]
    kds = [jnp.concatenate([kt[:, sl], kt[:, sl]], axis=0).astype(BF16) for sl in sls]
    vss = []
    for sl in sls:
        v2 = v_ref[:, sl].astype(F32)
        vss.append(jnp.where(same, jnp.concatenate([v2, v2], axis=0), 0.0).astype(BF16))
    s2s = [state_ref[p] for p in pairs]
    a_all = [_dot_nt(xs[p], jnp.concatenate([bds[p], kds[p]], axis=0)) for p in pairs]
    xs_state = [_dot_nt(xs[p], s2s[p].astype(BF16)) for p in pairs]
    nmat = [jnp.where(strict, -a[:two, :two], 0.0) for a in a_all]
    lk = [jnp.where(strict, a[:two, two:], 0.0).astype(BF16) for a in a_all]
    arb = [jnp.where(incl, a[two:, :two], 0.0).astype(BF16) for a in a_all]
    ark = [jnp.where(incl, a[two:, two:], 0.0).astype(BF16) for a in a_all]
    rhs = [xs_state[p][:two] + _dot(lk[p], vss[p]) for p in pairs]
    inv = [eye + n_ for n_ in nmat]
    pw = nmat
    for _ in range(int(math.log2(c)) - 1):
        pwb = [t.astype(BF16) for t in pw]
        pw = [_dot(t, t) for t in pwb]
        inv = [inv[p] + _dot(inv[p].astype(BF16), pw[p].astype(BF16)) for p in pairs]
    ub = [(-_dot(inv[p].astype(BF16), rhs[p].astype(BF16))).astype(BF16) for p in pairs]
    for p in pairs:
        ys = xs_state[p][two:] + _dot(arb[p], ub[p]) + _dot(ark[p], vss[p])
        y_ref[:, sls[p]] = ys[:c] + ys[c:]
    for p in pairs:
        ds = _dot_tn(ub[p], bds[p]) + _dot_tn(vss[p], kds[p])
        state_ref[p] = jnp.where(same, (s2s[p] + ds) * gam_end[:, sls[p]], 0.0)


def _rwkv_wkv(r, lw, k, v, kk, b, batch, seq):
    c = RWKV_CHUNK
    n = seq // c
    d = D_MODEL
    tri = (jnp.arange(c)[:, None] >= jnp.arange(c)[None, :]).astype(BF16)
    blk = pl.BlockSpec((c, d), lambda bi, i: (bi * n + i, 0))
    return pl.pallas_call(
        _rwkv_wkv_body,
        grid=(batch, n),
        in_specs=[blk] * 6 + [_resident((c, c))],
        out_specs=blk,
        out_shape=jax.ShapeDtypeStruct((batch * seq, d), F32),
        scratch_shapes=[pltpu.VMEM((d // LANES, LANES, LANES), F32)],
        compiler_params=_cparams(("parallel", "arbitrary")),
        name="rwkv_wkv",
    )(r, lw, k, v, kk, b, tri)


def _rwkv_out_body(y_ref, r_ref, k_ref, v_ref, g_ref, vec_ref, bd_ref, w_ref, res_ref, lg_ref,
                   lb_ref, o_ref, a_ref):
    bd = bd_ref[...]
    inv_n = 1.0 / RWKV_HEAD
    for s in range(D_MODEL // LANES):
        sl = slice(s * LANES, (s + 1) * LANES)
        y = y_ref[:, sl]
        mu = _head_sum(y, bd) * inv_n
        dlt = y - mu
        var = _head_sum(dlt * dlt, bd) * inv_n
        yn = dlt * lax.rsqrt(var + RWKV_GN_EPS) * vec_ref[0:1, sl] + vec_ref[1:2, sl]
        rk = r_ref[:, sl].astype(F32) * k_ref[:, sl].astype(F32) * vec_ref[2:3, sl]
        bonus = _head_sum(rk, bd) * v_ref[:, sl].astype(F32)
        a_ref[:, sl] = ((yn + bonus) * g_ref[:, sl].astype(F32)).astype(BF16)
    acc = _dot(a_ref[...], w_ref[...])
    o_ref[...] = _layer_norm(DN_ALPHA * res_ref[...] + acc, lg_ref[...], lb_ref[...])


def _rwkv_out(y, r, k, v, g, vec, w, res, lg, lb, *, tm=512):
    m, d = res.shape
    row = lambda i: (i, 0)
    act = pl.BlockSpec((tm, d), row)
    return pl.pallas_call(
        _rwkv_out_body,
        grid=(m // tm,),
        in_specs=[act] * 5 + [_resident(vec.shape), _resident((LANES, LANES)), _resident((d, d)),
                              act, _resident((1, d)), _resident((1, d))],
        out_specs=act,
        out_shape=jax.ShapeDtypeStruct((m, d), F32),
        scratch_shapes=[pltpu.VMEM((tm, d), BF16)],
        compiler_params=_cparams(("parallel",)),
        name="rwkv_out",
    )(y, r, k, v, g, vec, _head_ones(), w, res, lg, lb)


def kernel(x, ret_w_in, ret_gn, ret_w_out, dil_w_in, dil_w_out, mla_w_down, mla_norm_q,
           mla_norm_kv, mla_w_uq, mla_w_ukv, mla_w_out, rwkv_mu, rwkv_w_rkv, rwkv_w_out,
           rwkv_vec, rwkv_lora_a, rwkv_lora_b, rwkv_gate_a, rwkv_gate_b, rwkv_ln_x,
           mlp_w1, mlp_w2, ln_g, ln_b):
    batch, seq, d = x.shape
    xf = x.reshape(batch * seq, d)
    n_mixers = 4
    for i in range(DEPTH):
        mixer, j = i % n_mixers, i // n_mixers
        lg, lb = ln_g[i, 0][None, :], ln_b[i, 0][None, :]
        if mixer == 0:
            proj = _mm(xf, ret_w_in[j].astype(BF16), tm=1024, tn=1024, m_rows=batch * seq,
                       name="ret_proj")
            gated = _retention(proj, ret_gn[j], batch, seq)
            xf = _mm_res_ln(gated, ret_w_out[j].astype(BF16), xf, lg, lb, name="ret_out")
        elif mixer == 1:
            outs, lses = _dilated(xf, dil_w_in[j].astype(BF16), batch, seq)
            xf = _dil_out(outs, lses, dil_w_out[j].astype(BF16), xf, lg, lb)
        elif mixer == 2:
            o = _mla(xf, mla_w_down[j], mla_norm_q[j], mla_norm_kv[j], mla_w_uq[j], mla_w_ukv[j],
                     batch, seq)
            xf = _mm_res_ln(o, mla_w_out[j].astype(BF16), xf, lg, lb, name="mla_out")
        else:
            r, lw, k, v, kk, b, g = _rwkv_prep(xf, rwkv_mu[j], rwkv_w_rkv[j], rwkv_vec[j],
                                               rwkv_lora_a[j], rwkv_lora_b[j], rwkv_gate_a[j],
                                               rwkv_gate_b[j], seq)
            y = _rwkv_wkv(r, lw, k, v, kk, b, batch, seq)
            vec = jnp.concatenate([rwkv_ln_x[j], rwkv_vec[j][4:5],
                                   jnp.zeros((5, d), F32)], axis=0)
            xf = _rwkv_out(y, r, k, v, g, vec, rwkv_w_out[j].astype(BF16), xf, lg, lb)
        xf = _mlp(xf, mlp_w1[i].astype(BF16), mlp_w2[i].astype(BF16),
                  ln_g[i, 1][None, :], ln_b[i, 1][None, :])
    return xf.reshape(batch, seq, d)
```

```python
import functools
import math

import jax
import jax.numpy as jnp
from jax import lax
from jax.experimental import pallas as pl
from jax.experimental.pallas import tpu as pltpu

F32 = jnp.float32
BF16 = jnp.bfloat16

D_MODEL = 1024
DEPTH = 4
D_FF = 4 * D_MODEL
LN_EPS = 1e-5
RMS_EPS = 1e-6
GN_EPS = 1e-5
DN_ALPHA = (2.0 * DEPTH) ** 0.25
NEG = -1e30
LANES = 128
ROPE_PARTNER = 64
MM_SUB = 256

RET_HEADS = 4
RET_QK_DIM = 256
RET_V_DIM = 512
RET_CHUNK = 128
RET_THETA = 10000.0

DIL_PAIRS = ((128, 1), (512, 4), (2048, 16))
DIL_HEADS = 8
DIL_HEAD_DIM = 128
DIL_ROT = 32
DIL_BLOCK = 128
ROPE_THETA = 500000.0

MLA_HEADS = 16
MLA_NOPE = 128
MLA_ROPE = 64
MLA_V = 128
MLA_Q_RANK = 256
MLA_KV_RANK = 128
MLA_THETA = 10000.0
MLA_TQ = 512
MLA_TK = 1024
MLA_ROW_SPLIT = 2

RWKV_HEAD = 64
RWKV_HEADS = D_MODEL // RWKV_HEAD
RWKV_GN_EPS = 64e-5
RWKV_CHUNK = 64
RWKV_GATE_PAD = 256

VMEM_LIMIT = 56 * 1024 * 1024


def _cparams(sem):
    return pltpu.CompilerParams(dimension_semantics=sem, vmem_limit_bytes=VMEM_LIMIT)


def _resident(shape):
    nd = len(shape)
    return pl.BlockSpec(shape, lambda *_: (0,) * nd, pipeline_mode=pl.Buffered(1))


def _layer_norm(z, g, b):
    mu = jnp.mean(z, axis=-1, keepdims=True)
    d = z - mu
    var = jnp.mean(d * d, axis=-1, keepdims=True)
    return d * lax.rsqrt(var + LN_EPS) * g + b


def _dot(a, b):
    return jnp.dot(a, b, preferred_element_type=F32)


def _dot_nt(a, b):
    return lax.dot_general(a, b, (((1,), (1,)), ((), ())), preferred_element_type=F32)


def _dot_tn(a, b):
    return lax.dot_general(a, b, (((0,), (0,)), ((), ())), preferred_element_type=F32)


def _split2(z):
    hi = z.astype(BF16)
    lo = (z - hi.astype(F32)).astype(BF16)
    return hi, lo


def _rope_tile(a, c, s):
    return a * c + pltpu.roll(a, ROPE_PARTNER, 1) * s


def _mm_body(x_ref, w_ref, *rest, pattern, rope_jmax, scale, has_tabs):
    if has_tabs:
        c_ref, s_ref, o_ref, xb_ref = rest
    else:
        o_ref, xb_ref = rest
    j = pl.program_id(1)

    @pl.when(j == 0)
    def _():
        xb_ref[...] = x_ref[...].astype(BF16)

    def plain():
        o_ref[...] = _dot(xb_ref[...], w_ref[...]).astype(o_ref.dtype)

    def fancy():
        xb = xb_ref[...]
        for c0 in range(0, len(pattern) * LANES, MM_SUB):
            acc = _dot(xb, w_ref[:, c0:c0 + MM_SUB])
            for t in range(MM_SUB // LANES):
                a = acc[:, t * LANES:(t + 1) * LANES]
                mode = pattern[c0 // LANES + t]
                if mode == "rope":
                    a = _rope_tile(a, c_ref[0], s_ref[0])
                elif mode == "scale":
                    a = a * scale
                o_ref[:, c0 + t * LANES:c0 + (t + 1) * LANES] = a.astype(o_ref.dtype)

    if pattern is None:
        plain()
    elif rope_jmax is None:
        fancy()
    else:
        pl.when(j < rope_jmax)(fancy)
        pl.when(j >= rope_jmax)(plain)


def _mm(x, w, *, tm, tn, m_rows, out_dtype=BF16, x_map=None, tabs=None, tab_map=None,
        pattern=None, rope_jmax=None, scale=1.0, name="mm"):
    k, n = w.shape
    grid = (m_rows // tm, n // tn)
    if x_map is None:
        x_map = lambda i, j: (i, 0)
    in_specs = [pl.BlockSpec((tm, k), x_map), pl.BlockSpec((k, tn), lambda i, j: (0, j))]
    args = [x, w]
    if tabs is not None:
        for t in tabs:
            in_specs.append(pl.BlockSpec((1, tm, LANES), tab_map))
            args.append(t)
    body = functools.partial(_mm_body, pattern=pattern, rope_jmax=rope_jmax,
                             scale=scale, has_tabs=tabs is not None)
    return pl.pallas_call(
        body,
        grid=grid,
        in_specs=in_specs,
        out_specs=pl.BlockSpec((tm, tn), lambda i, j: (i, j)),
        out_shape=jax.ShapeDtypeStruct((m_rows, n), out_dtype),
        scratch_shapes=[pltpu.VMEM((tm, k), BF16)],
        compiler_params=_cparams(("parallel", "arbitrary")),
        name=name,
    )(*args)


def _mm_res_ln_body(a_ref, w_ref, res_ref, g_ref, b_ref, o_ref):
    acc = _dot(a_ref[...], w_ref[...])
    o_ref[...] = _layer_norm(DN_ALPHA * res_ref[...] + acc, g_ref[...], b_ref[...])


def _mm_res_ln(a, w, res, g, b, *, tm=512, name="mm_res_ln"):
    m, k = a.shape
    d = w.shape[1]
    row = lambda i: (i, 0)
    return pl.pallas_call(
        _mm_res_ln_body,
        grid=(m // tm,),
        in_specs=[pl.BlockSpec((tm, k), row), _resident((k, d)), pl.BlockSpec((tm, d), row),
                  _resident((1, d)), _resident((1, d))],
        out_specs=pl.BlockSpec((tm, d), row),
        out_shape=jax.ShapeDtypeStruct((m, d), F32),
        compiler_params=_cparams(("parallel",)),
        name=name,
    )(a, w, res, g, b)


def _mlp_body(x_ref, w1_ref, w2_ref, g_ref, b_ref, o_ref, *, fchunk):
    x = x_ref[...]
    xb = x.astype(BF16)
    acc = jnp.zeros(x.shape, F32)
    for c in range(D_FF // fchunk):
        h = _dot(xb, w1_ref[:, c * fchunk:(c + 1) * fchunk])
        h = jnp.maximum(h, 0.0)
        h = (h * h).astype(BF16)
        acc = acc + _dot(h, w2_ref[c * fchunk:(c + 1) * fchunk, :])
    o_ref[...] = _layer_norm(DN_ALPHA * x + acc, g_ref[...], b_ref[...])


def _mlp(x, w1, w2, g, b, *, tm=512, fchunk=1024):
    m, d = x.shape
    row = lambda i: (i, 0)
    return pl.pallas_call(
        functools.partial(_mlp_body, fchunk=fchunk),
        grid=(m // tm,),
        in_specs=[pl.BlockSpec((tm, d), row), _resident((d, D_FF)), _resident((D_FF, d)),
                  _resident((1, d)), _resident((1, d))],
        out_specs=pl.BlockSpec((tm, d), row),
        out_shape=jax.ShapeDtypeStruct((m, d), F32),
        compiler_params=_cparams(("parallel",)),
        name="mlp",
    )(x, w1, w2, g, b)


def _ret_body(q_ref, k_ref, v_ref, g_ref, cos_ref, sin_ref, intra_ref, qdec_ref, kdec_ref,
              cdec_ref, gn_ref, o_ref, state_ref):
    dk, dv, half = RET_QK_DIM, RET_V_DIM, RET_QK_DIM // 2

    @pl.when(pl.program_id(1) == 0)
    def _():
        state_ref[...] = jnp.zeros(state_ref.shape, F32)

    cos = cos_ref[...]
    sin = sin_ref[...]

    def rope(t):
        t1 = t[:, :half].astype(F32)
        t2 = t[:, half:].astype(F32)
        return jnp.concatenate([t1 * cos - t2 * sin, t2 * cos + t1 * sin], axis=-1)

    heads = range(RET_HEADS)
    q = [rope(q_ref[:, h * dk:(h + 1) * dk]) for h in heads]
    k = [rope(k_ref[:, h * dk:(h + 1) * dk]) * (dk ** -0.5) for h in heads]
    v = [v_ref[:, h * dv:(h + 1) * dv] for h in heads]
    qb = [t.astype(BF16) for t in q]
    state = [state_ref[h] for h in heads]
    scores = [(_dot_nt(qb[h], k[h].astype(BF16)) * intra_ref[h]).astype(BF16) for h in heads]
    cross = [_dot(qb[h], state[h].astype(BF16)) * qdec_ref[h] for h in heads]
    o = [_dot(scores[h], v[h]) + cross[h] for h in heads]
    for h in heads:
        state_ref[h] = (state[h] * cdec_ref[h, 0:1, :]
                        + _dot_tn((k[h] * kdec_ref[h]).astype(BF16), v[h]))
    for h in heads:
        sl = slice(h * dv, (h + 1) * dv)
        mu = jnp.mean(o[h], axis=-1, keepdims=True)
        d = o[h] - mu
        var = jnp.mean(d * d, axis=-1, keepdims=True)
        on = d * lax.rsqrt(var + GN_EPS) * gn_ref[0:1, sl] + gn_ref[1:2, sl]
        gate = g_ref[:, sl].astype(F32)
        gate = gate * (1.0 / (1.0 + jnp.exp(-gate)))
        o_ref[:, sl] = (gate * on).astype(o_ref.dtype)


def _retention(proj, gn, batch, seq):
    h_, dk, dv, c = RET_HEADS, RET_QK_DIM, RET_V_DIM, RET_CHUNK
    n = seq // c
    half = dk // 2
    pos = jnp.arange(seq, dtype=F32)
    inv_freq = RET_THETA ** (-jnp.arange(half, dtype=F32) / half)
    ang = pos[:, None] * inv_freq[None, :]
    cos, sin = jnp.cos(ang), jnp.sin(ang)
    log_gamma = jnp.log(1.0 - 2.0 ** (-5.0 - jnp.arange(h_, dtype=F32)))
    idx = jnp.arange(c, dtype=F32)
    diff = idx[:, None] - idx[None, :]
    intra = jnp.where(diff >= 0, jnp.exp(log_gamma[:, None, None] * jnp.maximum(diff, 0.0)), 0.0)
    qdec = jnp.broadcast_to(jnp.exp(log_gamma[:, None] * (idx + 1.0))[:, :, None], (h_, c, dv))
    kdec = jnp.broadcast_to(jnp.exp(log_gamma[:, None] * (c - 1.0 - idx))[:, :, None], (h_, c, dk))
    cdec = jnp.broadcast_to(jnp.exp(log_gamma * c)[:, None, None], (h_, 8, dv))
    qk_w, vg_w = h_ * dk, h_ * dv
    return pl.pallas_call(
        _ret_body,
        grid=(batch, n),
        in_specs=[
            pl.BlockSpec((c, qk_w), lambda b, i: (b * n + i, 0)),
            pl.BlockSpec((c, qk_w), lambda b, i: (b * n + i, 1)),
            pl.BlockSpec((c, vg_w), lambda b, i: (b * n + i, 2 * qk_w // vg_w)),
            pl.BlockSpec((c, vg_w), lambda b, i: (b * n + i, 2 * qk_w // vg_w + 1)),
            pl.BlockSpec((c, half), lambda b, i: (i, 0)),
            pl.BlockSpec((c, half), lambda b, i: (i, 0)),
            _resident((h_, c, c)), _resident((h_, c, dv)), _resident((h_, c, dk)),
            _resident((h_, 8, dv)), _resident((2, vg_w)),
        ],
        out_specs=pl.BlockSpec((c, vg_w), lambda b, i: (b * n + i, 0)),
        out_shape=jax.ShapeDtypeStruct((batch * seq, vg_w), BF16),
        scratch_shapes=[pltpu.VMEM((h_, dk, dv), F32)],
        compiler_params=_cparams(("parallel", "arbitrary")),
        name="retention",
    )(proj, proj, proj, proj, cos, sin, intra, qdec, kdec, cdec, gn)


def _dil_attn_body(q_ref, kp_ref, kc_ref, vp_ref, vc_ref, o_ref, lse_ref):
    blk = DIL_BLOCK
    has_prev = pl.program_id(1) > 0
    qi = lax.broadcasted_iota(jnp.int32, (blk, 2 * blk), 0)
    ki = lax.broadcasted_iota(jnp.int32, (blk, 2 * blk), 1)
    valid = jnp.logical_and(jnp.logical_and(ki >= qi, ki <= qi + blk),
                            jnp.logical_or(ki >= blk, has_prev))
    lane = lax.broadcasted_iota(jnp.int32, (blk, LANES), 1)
    ones = jnp.ones((2 * blk, LANES), BF16)
    heads = range(DIL_HEADS)
    sls = [slice(h * DIL_HEAD_DIM, (h + 1) * DIL_HEAD_DIM) for h in heads]
    s = [jnp.where(valid, _dot_nt(q_ref[:, sl], jnp.concatenate([kp_ref[:, sl], kc_ref[:, sl]], axis=0)),
                   NEG) for sl in sls]
    m = [jnp.max(t, axis=-1, keepdims=True) for t in s]
    p = [jnp.exp(s[h] - m[h]).astype(BF16) for h in heads]
    pv = [_dot(p[h], jnp.concatenate(
        [jnp.concatenate([vp_ref[:, sls[h]], vc_ref[:, sls[h]]], axis=0), ones], axis=1)) for h in heads]
    lse_tile = jnp.zeros((blk, LANES), F32)
    for h in heads:
        l = pv[h][:, DIL_HEAD_DIM:]
        o_ref[:, sls[h]] = (pv[h][:, :DIL_HEAD_DIM] / l).astype(o_ref.dtype)
        lse_tile = jnp.where(lane == h, m[h] + jnp.log(l), lse_tile)
    lse_ref[...] = lse_tile


def _dil_attn(qkv, batch, seq, dil):
    blk = DIL_BLOCK
    hd = DIL_HEADS * DIL_HEAD_DIM
    nb = seq // dil // blk
    cur = lambda c: (lambda z, i: (z * nb + i, c))
    prev = lambda c: (lambda z, i: (z * nb + jnp.maximum(i - 1, 0), c))
    out_map = lambda z, i: ((z // dil) * nb + i, z % dil)
    return pl.pallas_call(
        _dil_attn_body,
        grid=(batch * dil, nb),
        in_specs=[pl.BlockSpec((blk, hd), cur(0)), pl.BlockSpec((blk, hd), prev(1)),
                  pl.BlockSpec((blk, hd), cur(1)), pl.BlockSpec((blk, hd), prev(2)),
                  pl.BlockSpec((blk, hd), cur(2))],
        out_specs=[pl.BlockSpec((blk, hd), out_map), pl.BlockSpec((blk, LANES), out_map)],
        out_shape=[jax.ShapeDtypeStruct((batch * seq // dil, dil * hd), BF16),
                   jax.ShapeDtypeStruct((batch * seq // dil, dil * LANES), F32)],
        compiler_params=_cparams(("parallel", "arbitrary")),
        name=f"dil_attn_{dil}",
    )(qkv, qkv, qkv, qkv, qkv)


def _dil_out_body(o0_ref, o1_ref, o2_ref, l0_ref, l1_ref, l2_ref, e_ref, w_ref, res_ref,
                  g_ref, b_ref, o_ref):
    l0, l1, l2 = l0_ref[...], l1_ref[...], l2_ref[...]
    m = jnp.maximum(jnp.maximum(l0, l1), l2)
    e0, e1, e2 = jnp.exp(l0 - m), jnp.exp(l1 - m), jnp.exp(l2 - m)
    den = e0 + e1 + e2
    e = e_ref[...]
    mixed = None
    for eg, og in ((e0, o0_ref), (e1, o1_ref), (e2, o2_ref)):
        hi, lo = _split2(eg / den)
        wfull = _dot(hi, e) + _dot(lo, e)
        term = wfull * og[...].astype(F32)
        mixed = term if mixed is None else mixed + term
    acc = _dot(mixed.astype(BF16), w_ref[...])
    o_ref[...] = _layer_norm(DN_ALPHA * res_ref[...] + acc, g_ref[...], b_ref[...])


def _dil_out(outs, lses, w, res, g, b, *, tm=512):
    m, d = res.shape
    hd = DIL_HEADS * DIL_HEAD_DIM
    expand = (jnp.arange(LANES)[:, None] == (jnp.arange(hd) // DIL_HEAD_DIM)[None, :]).astype(BF16)
    row = lambda i: (i, 0)
    return pl.pallas_call(
        _dil_out_body,
        grid=(m // tm,),
        in_specs=[pl.BlockSpec((tm, hd), row)] * 3 + [pl.BlockSpec((tm, LANES), row)] * 3
        + [_resident((LANES, hd)), _resident((hd, d)), pl.BlockSpec((tm, d), row),
           _resident((1, d)), _resident((1, d))],
        out_specs=pl.BlockSpec((tm, d), row),
        out_shape=jax.ShapeDtypeStruct((m, d), F32),
        compiler_params=_cparams(("parallel",)),
        name="dil_out",
    )(*outs, *lses, expand, w, res, g, b)


def _rope_tables(seq, rot, theta, scale, passthrough):
    half = rot // 2
    inv_freq = theta ** (-jnp.arange(half, dtype=F32) / half)
    ang = jnp.arange(seq, dtype=F32)[:, None] * inv_freq[None, :]
    cos, sin = jnp.cos(ang), jnp.sin(ang)
    fill = jnp.full((seq, ROPE_PARTNER - half), passthrough, F32)
    zero = jnp.zeros((seq, ROPE_PARTNER - half), F32)
    c = jnp.concatenate([cos, fill, cos, fill], axis=1)
    s = jnp.concatenate([-sin, zero, sin, zero], axis=1)
    return c * scale, s * scale


def _rope_lane_order(rot, width):
    half = rot // 2
    rest = list(range(rot, width))
    cut = ROPE_PARTNER - half
    return jnp.array(list(range(half)) + rest[:cut] + list(range(half, rot)) + rest[cut:])


def _dilated(x, w_in, batch, seq):
    hd = DIL_HEADS * DIL_HEAD_DIM
    cq, sq = _rope_tables(seq, DIL_ROT, ROPE_THETA, DIL_HEAD_DIM ** -0.5, 1.0)
    ck, sk = _rope_tables(seq, DIL_ROT, ROPE_THETA, 1.0, 1.0)
    order = _rope_lane_order(DIL_ROT, DIL_HEAD_DIM)
    outs, lses = [], []
    for gi, (_, dil) in enumerate(DIL_PAIRS):
        sub = seq // dil
        tm = min(512, sub)
        nt = sub // tm
        xv = x.reshape(batch * sub, dil * D_MODEL)
        wg = w_in[:, gi * 3 * hd:(gi + 1) * 3 * hd].reshape(D_MODEL, 3, DIL_HEADS, DIL_HEAD_DIM)
        wg = jnp.concatenate([wg[:, :2][..., order], wg[:, 2:]], axis=1).reshape(D_MODEL, 3 * hd)
        tabs = [jnp.stack([a, b]).reshape(2, sub, dil * LANES) for a, b in ((cq, ck), (sq, sk))]
        x_map = lambda i, j, dil=dil, nt=nt: ((i // (dil * nt)) * nt + i % nt, (i // nt) % dil)
        tab_map = lambda i, j, dil=dil, nt=nt: (jnp.minimum(j, 1), i % nt, (i // nt) % dil)
        qkv = _mm(xv, wg, tm=tm, tn=hd, m_rows=batch * seq,
                  x_map=x_map, tabs=tabs, tab_map=tab_map, pattern=("rope",) * DIL_HEADS,
                  rope_jmax=2, name=f"dil_proj_{dil}")
        o, lse = _dil_attn(qkv, batch, seq, dil)
        outs.append(o.reshape(batch * seq, hd))
        lses.append(lse.reshape(batch * seq, LANES))
    return outs, lses


def _mla_down_body(x_ref, w_ref, nq_ref, nkv_ref, c_ref, s_ref, cq_ref, ckv_ref, kpe_ref):
    acc = _dot(x_ref[...].astype(BF16), w_ref[...])
    cq = acc[:, :MLA_Q_RANK]
    ckv = acc[:, MLA_Q_RANK:MLA_Q_RANK + MLA_KV_RANK]
    kpe = acc[:, MLA_Q_RANK + MLA_KV_RANK:]
    cq = cq * lax.rsqrt(jnp.mean(cq * cq, axis=-1, keepdims=True) + RMS_EPS) * nq_ref[...]
    ckv = ckv * lax.rsqrt(jnp.mean(ckv * ckv, axis=-1, keepdims=True) + RMS_EPS) * nkv_ref[...]
    kpe = _rope_tile(kpe, c_ref[...], s_ref[...])
    cq_ref[...] = cq.astype(cq_ref.dtype)
    ckv_ref[...] = ckv.astype(ckv_ref.dtype)
    kpe_ref[...] = kpe.astype(kpe_ref.dtype)


def _mla_down(x, w, nq, nkv, tabs, seq, *, tm=512):
    m, d = x.shape
    n = w.shape[1]
    row = lambda i: (i, 0)
    ns = seq // tm
    tab = lambda i: (i % ns, 0)
    return pl.pallas_call(
        _mla_down_body,
        grid=(m // tm,),
        in_specs=[pl.BlockSpec((tm, d), row), _resident((d, n)), _resident((1, MLA_Q_RANK)),
                  _resident((1, MLA_KV_RANK))] + [pl.BlockSpec((tm, LANES), tab)] * 2,
        out_specs=[pl.BlockSpec((tm, MLA_Q_RANK), row), pl.BlockSpec((tm, MLA_KV_RANK), row),
                   pl.BlockSpec((tm, LANES), row)],
        out_shape=[jax.ShapeDtypeStruct((m, MLA_Q_RANK), BF16),
                   jax.ShapeDtypeStruct((m, MLA_KV_RANK), BF16),
                   jax.ShapeDtypeStruct((m, LANES), BF16)],
        compiler_params=_cparams(("parallel",)),
        name="mla_down",
    )(x, w, nq, nkv, *tabs)


def _mla_flash_body(q_ref, kn_ref, kpe_ref, v_ref, o_ref, kcat_ref, vaug_ref):
    tq, tk, sub = MLA_TQ, MLA_TK, MLA_TQ // MLA_ROW_SPLIT
    iq = pl.program_id(2)

    @pl.when(iq == 0)
    def _():
        kcat_ref[:, :MLA_NOPE] = kn_ref[...]
        kcat_ref[:, MLA_NOPE:] = kpe_ref[...]
        vaug_ref[:, :MLA_V] = v_ref[...]
        vaug_ref[:, MLA_V:] = jnp.ones((v_ref.shape[0], LANES), BF16)

    parts = range(MLA_ROW_SPLIT)
    col_minus_row = (lax.broadcasted_iota(jnp.int32, (sub, tk), 1)
                     - lax.broadcasted_iota(jnp.int32, (sub, tk), 0))

    def scores(chunk, diagonal):
        kb = kcat_ref[chunk * tk:(chunk + 1) * tk, :]
        out = []
        for part in parts:
            s = _dot_nt(q_ref[part * sub:(part + 1) * sub, :], kb)
            if diagonal:
                s = jnp.where(col_minus_row <= iq * tq + part * sub - chunk * tk, s, NEG)
            out.append(s)
        return out

    def update(chunk, s_all, carry):
        vb = vaug_ref[chunk * tk:(chunk + 1) * tk, :]
        out = []
        for s, (m, acc) in zip(s_all, carry):
            m_new = jnp.maximum(m, jnp.max(s, axis=-1, keepdims=True))
            alpha = jnp.exp2(m - m_new)
            p = jnp.exp2((s - m_new).astype(BF16))
            out.append((m_new, alpha * acc + _dot(p, vb)))
        return out

    def attend(n_chunks):
        carry = [(jnp.full((sub, 1), NEG, F32), jnp.zeros((sub, MLA_V + LANES), F32)) for _ in parts]
        s = scores(0, n_chunks == 1)
        for c in range(n_chunks):
            s_next = scores(c + 1, c + 2 == n_chunks) if c + 1 < n_chunks else None
            carry = update(c, s, carry)
            s = s_next
        for part, (_, acc) in enumerate(carry):
            o_ref[part * sub:(part + 1) * sub, :] = (acc[:, :MLA_V] / acc[:, MLA_V:]).astype(o_ref.dtype)

    n_chunks = (iq * tq) // tk + 1
    for n in range(1, kcat_ref.shape[0] // tk + 1):
        pl.when(n_chunks == n)(functools.partial(attend, n))


def _mla_flash(q, kv, kpe, batch, seq):
    blk = MLA_TQ
    nq = seq // blk
    h_ = MLA_HEADS
    qw = MLA_NOPE + LANES
    return pl.pallas_call(
        _mla_flash_body,
        grid=(batch, h_, nq),
        in_specs=[pl.BlockSpec((blk, qw), lambda b, h, i: (b * nq + i, h)),
                  pl.BlockSpec((seq, MLA_NOPE), lambda b, h, i: (b, h)),
                  pl.BlockSpec((seq, LANES), lambda b, h, i: (b, 0)),
                  pl.BlockSpec((seq, MLA_V), lambda b, h, i: (b, h_ + h))],
        out_specs=pl.BlockSpec((blk, MLA_V), lambda b, h, i: (b * nq + i, h)),
        out_shape=jax.ShapeDtypeStruct((batch * seq, h_ * MLA_V), BF16),
        scratch_shapes=[pltpu.VMEM((seq, qw), BF16), pltpu.VMEM((seq, MLA_V + LANES), BF16)],
        compiler_params=_cparams(("parallel", "parallel", "arbitrary")),
        name="mla_flash",
    )(q, kv, kpe, kv)


def _mla(x, w_down, norm_q, norm_kv, w_uq, w_ukv, batch, seq):
    h_ = MLA_HEADS
    half = MLA_ROPE // 2

    def pe_tile(w):
        z = jnp.zeros(w.shape[:-1] + (ROPE_PARTNER - half,), w.dtype)
        return jnp.concatenate([w[..., :half], z, w[..., half:], z], axis=-1)

    n_lat = MLA_Q_RANK + MLA_KV_RANK
    wd = jnp.concatenate([w_down[:, :n_lat], pe_tile(w_down[:, n_lat:])], axis=1).astype(BF16)
    wq = w_uq.reshape(MLA_Q_RANK, h_, MLA_NOPE + MLA_ROPE)
    wq = jnp.concatenate([wq[..., :MLA_NOPE], pe_tile(wq[..., MLA_NOPE:])], axis=-1)
    wq = wq.reshape(MLA_Q_RANK, -1).astype(BF16)
    wkv = w_ukv.reshape(MLA_KV_RANK, h_, MLA_NOPE + MLA_V)
    wkv = jnp.concatenate([wkv[:, :, :MLA_NOPE].reshape(MLA_KV_RANK, -1),
                           wkv[:, :, MLA_NOPE:].reshape(MLA_KV_RANK, -1)], axis=1).astype(BF16)
    scale = (MLA_NOPE + MLA_ROPE) ** -0.5 * math.log2(math.e)
    tk = _rope_tables(seq, MLA_ROPE, MLA_THETA, 1.0, 0.0)
    tq = [t[None] for t in _rope_tables(seq, MLA_ROPE, MLA_THETA, scale, 0.0)]
    cq, ckv, kpe = _mla_down(x, wd, norm_q[None, :], norm_kv[None, :], tk, seq)
    tm = 512
    ns = seq // tm
    q = _mm(cq, wq, tm=tm, tn=1024, m_rows=batch * seq, tabs=tq,
            tab_map=lambda i, j: (0, i % ns, 0), pattern=("scale", "rope") * 4,
            scale=scale, name="mla_q")
    kv = _mm(ckv, wkv, tm=tm, tn=1024, m_rows=batch * seq, name="mla_kv")
    return _mla_flash(q, kv, kpe, batch, seq)


def _head_sum(z, ones_bd):
    hi, lo = _split2(z)
    return _dot(hi, ones_bd) + _dot(lo, ones_bd)


def _rwkv_prep_body(x_ref, xp_ref, mu_ref, wr_ref, wk_ref, wv_ref, la0_ref, lb0_ref, la1_ref,
                    lb1_ref, ga_ref, gb_ref, vec_ref, bd_ref, r_ref, lw_ref, k_ref, v_ref,
                    kk_ref, b_ref, g_ref, *, tiles_per_seq):
    x = x_ref[...]
    tm = x.shape[0]
    first = pl.program_id(0) % tiles_per_seq == 0
    prev_row = jnp.where(first, 0.0, xp_ref[7:8, :])
    rows = lax.broadcasted_iota(jnp.int32, x.shape, 0)
    shifted = jnp.where(rows == 0, prev_row, pltpu.roll(x, 1, 0))
    xx = shifted - x
    mix = lambda i: (x + xx * mu_ref[i:i + 1, :]).astype(BF16)
    r = _dot(mix(0), wr_ref[...])
    k_raw = _dot(mix(2), wk_ref[...])
    v = _dot(mix(3), wv_ref[...])
    w0, a0, k_k, k_a = (vec_ref[i:i + 1, :] for i in range(4))
    wl = w0 + _dot(jnp.tanh(_dot(mix(1), la0_ref[...])).astype(BF16), lb0_ref[...])
    z = -wl
    softplus = jnp.maximum(z, 0.0) + jnp.log(1.0 + jnp.exp(-jnp.abs(z)))
    lw_ref[...] = -jnp.exp(-softplus - 0.5)
    al = a0 + _dot(_dot(mix(4), la1_ref[...]).astype(BF16), lb1_ref[...])
    a = 1.0 / (1.0 + jnp.exp(-al))
    gl = _dot(mix(5), ga_ref[...])
    g = _dot((1.0 / (1.0 + jnp.exp(-gl))).astype(BF16), gb_ref[...])
    kk = k_raw * k_k
    bd = bd_ref[...]
    for s in range(D_MODEL // LANES):
        sl = slice(s * LANES, (s + 1) * LANES)
        t = kk[:, sl]
        nrm = jnp.maximum(jnp.sqrt(_head_sum(t * t, bd)), 1e-12)
        t = t / nrm
        kk_ref[:, sl] = t.astype(kk_ref.dtype)
        b_ref[:, sl] = (t * a[:, sl]).astype(b_ref.dtype)
    r_ref[...] = r.astype(r_ref.dtype)
    k_ref[...] = (k_raw * (1.0 + (a - 1.0) * k_a)).astype(k_ref.dtype)
    v_ref[...] = v.astype(v_ref.dtype)
    g_ref[...] = g.astype(g_ref.dtype)


def _head_ones():
    idx = jnp.arange(LANES) // RWKV_HEAD
    return (idx[:, None] == idx[None, :]).astype(BF16)


def _rwkv_prep(x, mu, w_rkv, vec, lora_a, lora_b, gate_a, gate_b, seq, *, tm=256):
    m, d = x.shape
    gpad = RWKV_GATE_PAD - gate_a.shape[1]
    ga = jnp.pad(gate_a, ((0, 0), (0, gpad))).astype(BF16)
    gb = jnp.pad(gate_b, ((0, gpad), (0, 0))).astype(BF16)
    wts = [w_rkv[0].astype(BF16), w_rkv[1].astype(BF16), w_rkv[2].astype(BF16),
           lora_a[0].astype(BF16), lora_b[0].astype(BF16), lora_a[1].astype(BF16),
           lora_b[1].astype(BF16), ga, gb]
    vec8 = jnp.pad(vec, ((0, 3), (0, 0)))
    mu8 = jnp.pad(mu, ((0, 2), (0, 0)))
    row = lambda i: (i, 0)
    sub = tm // 8
    out = jax.ShapeDtypeStruct((m, d), BF16)
    return pl.pallas_call(
        functools.partial(_rwkv_prep_body, tiles_per_seq=seq // tm),
        grid=(m // tm,),
        in_specs=[pl.BlockSpec((tm, d), row),
                  pl.BlockSpec((8, d), lambda i: (jnp.maximum(i * sub - 1, 0), 0)),
                  _resident(mu8.shape)] + [_resident(w.shape) for w in wts]
        + [_resident(vec8.shape), _resident((LANES, LANES))],
        out_specs=[pl.BlockSpec((tm, d), row)] * 7,
        out_shape=[out, jax.ShapeDtypeStruct((m, d), F32), out, out, out, out, out],
        compiler_params=_cparams(("parallel",)),
        name="rwkv_prep",
    )(x, x, mu8, *wts, vec8, _head_ones())


def _rwkv_wkv_body(r_ref, lw_ref, k_ref, v_ref, kk_ref, b_ref, tri_ref, y_ref, state_ref):
    c = RWKV_CHUNK
    two = 2 * c

    @pl.when(pl.program_id(1) == 0)
    def _():
        state_ref[...] = jnp.zeros(state_ref.shape, F32)

    lw = lw_ref[...]
    tri = tri_ref[...]
    h1 = lw.astype(BF16)
    r1 = lw - h1.astype(F32)
    h2 = r1.astype(BF16)
    h3 = (r1 - h2.astype(F32)).astype(BF16)
    cum = _dot(tri, h1) + _dot(tri, h2) + _dot(tri, h3)
    gam = jnp.exp(cum)
    gam_ex = jnp.exp(cum - lw)
    gam_inv = jnp.exp(-cum)
    rt = r_ref[...].astype(F32) * gam
    kkt = kk_ref[...].astype(F32) * gam_ex
    bt = b_ref[...].astype(F32) * gam_inv
    kt = k_ref[...].astype(F32) * gam_inv
    gam_end = gam[c - 1:c, :]

    lane_lo = lax.broadcasted_iota(jnp.int32, (c, LANES), 1) < RWKV_HEAD
    row2 = lax.broadcasted_iota(jnp.int32, (two, two), 0)
    col2 = lax.broadcasted_iota(jnp.int32, (two, two), 1)
    same = (row2 // c) == (col2 // c)
    strict = jnp.logical_and(same, row2 > col2)
    incl = jnp.logical_and(same, row2 >= col2)
    eye = (row2 == col2).astype(F32)

    def stack_masked(t):
        return jnp.concatenate([jnp.where(lane_lo, t, 0.0), jnp.where(lane_lo, 0.0, t)], axis=0)

    pairs = range(D_MODEL // LANES)
    sls = [slice(p * LANES, (p + 1) * LANES) for p in pairs]
    xs = [jnp.concatenate([stack_masked(kkt[:, sl]), stack_masked(rt[:, sl])], axis=0).astype(BF16)
          for sl in sls]
    bds = [jnp.concatenate([bt[:, sl], bt[:, sl]], axis=0).astype(BF16) for sl in sls]
    kds = [jnp.concatenate([kt[:, sl], kt[:, sl]], axis=0).astype(BF16) for sl in sls]
    vss = []
    for sl in sls:
        v2 = v_ref[:, sl].astype(F32)
        vss.append(jnp.where(same, jnp.concatenate([v2, v2], axis=0), 0.0).astype(BF16))
    s2s = [state_ref[p] for p in pairs]
    a_all = [_dot_nt(xs[p], jnp.concatenate([bds[p], kds[p]], axis=0)) for p in pairs]
    xs_state = [_dot_nt(xs[p], s2s[p].astype(BF16)) for p in pairs]
    nmat = [jnp.where(strict, -a[:two, :two], 0.0) for a in a_all]
    lk = [jnp.where(strict, a[:two, two:], 0.0).astype(BF16) for a in a_all]
    arb = [jnp.where(incl, a[two:, :two], 0.0).astype(BF16) for a in a_all]
    ark = [jnp.where(incl, a[two:, two:], 0.0).astype(BF16) for a in a_all]
    rhs = [xs_state[p][:two] + _dot(lk[p], vss[p]) for p in pairs]
    inv = [eye + n_ for n_ in nmat]
    pw = nmat
    for _ in range(int(math.log2(c)) - 1):
        pwb = [t.astype(BF16) for t in pw]
        pw = [_dot(t, t) for t in pwb]
        inv = [inv[p] + _dot(inv[p].astype(BF16), pw[p].astype(BF16)) for p in pairs]
    ub = [(-_dot(inv[p].astype(BF16), rhs[p].astype(BF16))).astype(BF16) for p in pairs]
    for p in pairs:
        ys = xs_state[p][two:] + _dot(arb[p], ub[p]) + _dot(ark[p], vss[p])
        y_ref[:, sls[p]] = ys[:c] + ys[c:]
    for p in pairs:
        ds = _dot_tn(ub[p], bds[p]) + _dot_tn(vss[p], kds[p])
        state_ref[p] = jnp.where(same, (s2s[p] + ds) * gam_end[:, sls[p]], 0.0)


def _rwkv_wkv(r, lw, k, v, kk, b, batch, seq):
    c = RWKV_CHUNK
    n = seq // c
    d = D_MODEL
    tri = (jnp.arange(c)[:, None] >= jnp.arange(c)[None, :]).astype(BF16)
    blk = pl.BlockSpec((c, d), lambda bi, i: (bi * n + i, 0))
    return pl.pallas_call(
        _rwkv_wkv_body,
        grid=(batch, n),
        in_specs=[blk] * 6 + [_resident((c, c))],
        out_specs=blk,
        out_shape=jax.ShapeDtypeStruct((batch * seq, d), F32),
        scratch_shapes=[pltpu.VMEM((d // LANES, LANES, LANES), F32)],
        compiler_params=_cparams(("parallel", "arbitrary")),
        name="rwkv_wkv",
    )(r, lw, k, v, kk, b, tri)


def _rwkv_out_body(y_ref, r_ref, k_ref, v_ref, g_ref, vec_ref, bd_ref, w_ref, res_ref, lg_ref,
                   lb_ref, o_ref, a_ref):
    bd = bd_ref[...]
    inv_n = 1.0 / RWKV_HEAD
    for s in range(D_MODEL // LANES):
        sl = slice(s * LANES, (s + 1) * LANES)
        y = y_ref[:, sl]
        mu = _head_sum(y, bd) * inv_n
        dlt = y - mu
        var = _head_sum(dlt * dlt, bd) * inv_n
        yn = dlt * lax.rsqrt(var + RWKV_GN_EPS) * vec_ref[0:1, sl] + vec_ref[1:2, sl]
        rk = r_ref[:, sl].astype(F32) * k_ref[:, sl].astype(F32) * vec_ref[2:3, sl]
        bonus = _head_sum(rk, bd) * v_ref[:, sl].astype(F32)
        a_ref[:, sl] = ((yn + bonus) * g_ref[:, sl].astype(F32)).astype(BF16)
    acc = _dot(a_ref[...], w_ref[...])
    o_ref[...] = _layer_norm(DN_ALPHA * res_ref[...] + acc, lg_ref[...], lb_ref[...])


def _rwkv_out(y, r, k, v, g, vec, w, res, lg, lb, *, tm=512):
    m, d = res.shape
    row = lambda i: (i, 0)
    act = pl.BlockSpec((tm, d), row)
    return pl.pallas_call(
        _rwkv_out_body,
        grid=(m // tm,),
        in_specs=[act] * 5 + [_resident(vec.shape), _resident((LANES, LANES)), _resident((d, d)),
                              act, _resident((1, d)), _resident((1, d))],
        out_specs=act,
        out_shape=jax.ShapeDtypeStruct((m, d), F32),
        scratch_shapes=[pltpu.VMEM((tm, d), BF16)],
        compiler_params=_cparams(("parallel",)),
        name="rwkv_out",
    )(y, r, k, v, g, vec, _head_ones(), w, res, lg, lb)


def kernel(x, ret_w_in, ret_gn, ret_w_out, dil_w_in, dil_w_out, mla_w_down, mla_norm_q,
           mla_norm_kv, mla_w_uq, mla_w_ukv, mla_w_out, rwkv_mu, rwkv_w_rkv, rwkv_w_out,
           rwkv_vec, rwkv_lora_a, rwkv_lora_b, rwkv_gate_a, rwkv_gate_b, rwkv_ln_x,
           mlp_w1, mlp_w2, ln_g, ln_b):
    batch, seq, d = x.shape
    xf = x.reshape(batch * seq, d)
    n_mixers = 4
    for i in range(DEPTH):
        mixer, j = i % n_mixers, i // n_mixers
        lg, lb = ln_g[i, 0][None, :], ln_b[i, 0][None, :]
        if mixer == 0:
            proj = _mm(xf, ret_w_in[j].astype(BF16), tm=1024, tn=1024, m_rows=batch * seq,
                       name="ret_proj")
            gated = _retention(proj, ret_gn[j], batch, seq)
            xf = _mm_res_ln(gated, ret_w_out[j].astype(BF16), xf, lg, lb, name="ret_out")
        elif mixer == 1:
            outs, lses = _dilated(xf, dil_w_in[j].astype(BF16), batch, seq)
            xf = _dil_out(outs, lses, dil_w_out[j].astype(BF16), xf, lg, lb)
        elif mixer == 2:
            o = _mla(xf, mla_w_down[j], mla_norm_q[j], mla_norm_kv[j], mla_w_uq[j], mla_w_ukv[j],
                     batch, seq)
            xf = _mm_res_ln(o, mla_w_out[j].astype(BF16), xf, lg, lb, name="mla_out")
        else:
            r, lw, k, v, kk, b, g = _rwkv_prep(xf, rwkv_mu[j], rwkv_w_rkv[j], rwkv_vec[j],
                                               rwkv_lora_a[j], rwkv_lora_b[j], rwkv_gate_a[j],
                                               rwkv_gate_b[j], seq)
            y = _rwkv_wkv(r, lw, k, v, kk, b, batch, seq)
            vec = jnp.concatenate([rwkv_ln_x[j], rwkv_vec[j][4:5],
                                   jnp.zeros((5, d), F32)], axis=0)
            xf = _rwkv_out(y, r, k, v, g, vec, rwkv_w_out[j].astype(BF16), xf, lg, lb)
        xf = _mlp(xf, mlp_w1[i].astype(BF16), mlp_w2[i].astype(BF16),
                  ln_g[i, 1][None, :], ln_b[i, 1][None, :])
    return xf.reshape(batch, seq, d)
```

```python
import functools
import math

import jax
import jax.numpy as jnp
from jax import lax
from jax.experimental import pallas as pl
from jax.experimental.pallas import tpu as pltpu

F32 = jnp.float32
BF16 = jnp.bfloat16

D_MODEL = 1024
DEPTH = 4
D_FF = 4 * D_MODEL
LN_EPS = 1e-5
RMS_EPS = 1e-6
GN_EPS = 1e-5
DN_ALPHA = (2.0 * DEPTH) ** 0.25
NEG = -1e30
LANES = 128
ROPE_PARTNER = 64
MM_SUB = 256

RET_HEADS = 4
RET_QK_DIM = 256
RET_V_DIM = 512
RET_CHUNK = 128
RET_THETA = 10000.0

DIL_PAIRS = ((128, 1), (512, 4), (2048, 16))
DIL_HEADS = 8
DIL_HEAD_DIM = 128
DIL_ROT = 32
DIL_BLOCK = 128
ROPE_THETA = 500000.0

MLA_HEADS = 16
MLA_NOPE = 128
MLA_ROPE = 64
MLA_V = 128
MLA_Q_RANK = 256
MLA_KV_RANK = 128
MLA_THETA = 10000.0
MLA_TQ = 1024
MLA_TK = 1024
MLA_ROW_SPLIT = 4
assert MLA_TQ == MLA_TK

RWKV_HEAD = 64
RWKV_HEADS = D_MODEL // RWKV_HEAD
RWKV_GN_EPS = 64e-5
RWKV_CHUNK = 64
RWKV_GATE_PAD = 256

VMEM_LIMIT = 56 * 1024 * 1024


def _cparams(sem):
    return pltpu.CompilerParams(dimension_semantics=sem, vmem_limit_bytes=VMEM_LIMIT)


def _resident(shape):
    nd = len(shape)
    return pl.BlockSpec(shape, lambda *_: (0,) * nd, pipeline_mode=pl.Buffered(1))


def _layer_norm(z, g, b):
    mu = jnp.mean(z, axis=-1, keepdims=True)
    d = z - mu
    var = jnp.mean(d * d, axis=-1, keepdims=True)
    return d * lax.rsqrt(var + LN_EPS) * g + b


def _dot(a, b):
    return jnp.dot(a, b, preferred_element_type=F32)


def _dot_nt(a, b):
    return lax.dot_general(a, b, (((1,), (1,)), ((), ())), preferred_element_type=F32)


def _dot_tn(a, b):
    return lax.dot_general(a, b, (((0,), (0,)), ((), ())), preferred_element_type=F32)


def _split2(z):
    hi = z.astype(BF16)
    lo = (z - hi.astype(F32)).astype(BF16)
    return hi, lo


def _rope_tile(a, c, s):
    return a * c + pltpu.roll(a, ROPE_PARTNER, 1) * s


def _mm_body(x_ref, w_ref, *rest, pattern, rope_jmax, scale, has_tabs):
    if has_tabs:
        c_ref, s_ref, o_ref, xb_ref = rest
    else:
        o_ref, xb_ref = rest
    j = pl.program_id(1)

    @pl.when(j == 0)
    def _():
        xb_ref[...] = x_ref[...].astype(BF16)

    def plain():
        o_ref[...] = _dot(xb_ref[...], w_ref[...]).astype(o_ref.dtype)

    def fancy():
        xb = xb_ref[...]
        for c0 in range(0, len(pattern) * LANES, MM_SUB):
            acc = _dot(xb, w_ref[:, c0:c0 + MM_SUB])
            for t in range(MM_SUB // LANES):
                a = acc[:, t * LANES:(t + 1) * LANES]
                mode = pattern[c0 // LANES + t]
                if mode == "rope":
                    a = _rope_tile(a, c_ref[0], s_ref[0])
                elif mode == "scale":
                    a = a * scale
                o_ref[:, c0 + t * LANES:c0 + (t + 1) * LANES] = a.astype(o_ref.dtype)

    if pattern is None:
        plain()
    elif rope_jmax is None:
        fancy()
    else:
        pl.when(j < rope_jmax)(fancy)
        pl.when(j >= rope_jmax)(plain)


def _mm(x, w, *, tm, tn, m_rows, out_dtype=BF16, x_map=None, tabs=None, tab_map=None,
        pattern=None, rope_jmax=None, scale=1.0, name="mm"):
    k, n = w.shape
    grid = (m_rows // tm, n // tn)
    if x_map is None:
        x_map = lambda i, j: (i, 0)
    in_specs = [pl.BlockSpec((tm, k), x_map), pl.BlockSpec((k, tn), lambda i, j: (0, j))]
    args = [x, w]
    if tabs is not None:
        for t in tabs:
            in_specs.append(pl.BlockSpec((1, tm, LANES), tab_map))
            args.append(t)
    body = functools.partial(_mm_body, pattern=pattern, rope_jmax=rope_jmax,
                             scale=scale, has_tabs=tabs is not None)
    return pl.pallas_call(
        body,
        grid=grid,
        in_specs=in_specs,
        out_specs=pl.BlockSpec((tm, tn), lambda i, j: (i, j)),
        out_shape=jax.ShapeDtypeStruct((m_rows, n), out_dtype),
        scratch_shapes=[pltpu.VMEM((tm, k), BF16)],
        compiler_params=_cparams(("parallel", "arbitrary")),
        name=name,
    )(*args)


def _mm_res_ln_body(a_ref, w_ref, res_ref, g_ref, b_ref, o_ref):
    acc = _dot(a_ref[...], w_ref[...])
    o_ref[...] = _layer_norm(DN_ALPHA * res_ref[...] + acc, g_ref[...], b_ref[...])


def _mm_res_ln(a, w, res, g, b, *, tm=512, name="mm_res_ln"):
    m, k = a.shape
    d = w.shape[1]
    row = lambda i: (i, 0)
    return pl.pallas_call(
        _mm_res_ln_body,
        grid=(m // tm,),
        in_specs=[pl.BlockSpec((tm, k), row), _resident((k, d)), pl.BlockSpec((tm, d), row),
                  _resident((1, d)), _resident((1, d))],
        out_specs=pl.BlockSpec((tm, d), row),
        out_shape=jax.ShapeDtypeStruct((m, d), F32),
        compiler_params=_cparams(("parallel",)),
        name=name,
    )(a, w, res, g, b)


def _mlp_body(x_ref, w1_ref, w2_ref, g_ref, b_ref, o_ref, *, fchunk):
    x = x_ref[...]
    xb = x.astype(BF16)
    acc = jnp.zeros(x.shape, F32)
    for c in range(D_FF // fchunk):
        h = _dot(xb, w1_ref[:, c * fchunk:(c + 1) * fchunk])
        h = jnp.maximum(h, 0.0)
        h = (h * h).astype(BF16)
        acc = acc + _dot(h, w2_ref[c * fchunk:(c + 1) * fchunk, :])
    o_ref[...] = _layer_norm(DN_ALPHA * x + acc, g_ref[...], b_ref[...])


def _mlp(x, w1, w2, g, b, *, tm=512, fchunk=1024):
    m, d = x.shape
    row = lambda i: (i, 0)
    return pl.pallas_call(
        functools.partial(_mlp_body, fchunk=fchunk),
        grid=(m // tm,),
        in_specs=[pl.BlockSpec((tm, d), row), _resident((d, D_FF)), _resident((D_FF, d)),
                  _resident((1, d)), _resident((1, d))],
        out_specs=pl.BlockSpec((tm, d), row),
        out_shape=jax.ShapeDtypeStruct((m, d), F32),
        compiler_params=_cparams(("parallel",)),
        name="mlp",
    )(x, w1, w2, g, b)


def _ret_body(q_ref, k_ref, v_ref, g_ref, cos_ref, sin_ref, intra_ref, qdec_ref, kdec_ref,
              cdec_ref, gn_ref, o_ref, state_ref):
    dk, dv, half = RET_QK_DIM, RET_V_DIM, RET_QK_DIM // 2

    @pl.when(pl.program_id(1) == 0)
    def _():
        state_ref[...] = jnp.zeros(state_ref.shape, F32)

    cos = cos_ref[...]
    sin = sin_ref[...]

    def rope(t):
        t1 = t[:, :half].astype(F32)
        t2 = t[:, half:].astype(F32)
        return jnp.concatenate([t1 * cos - t2 * sin, t2 * cos + t1 * sin], axis=-1)

    heads = range(RET_HEADS)
    q = [rope(q_ref[:, h * dk:(h + 1) * dk]) for h in heads]
    k = [rope(k_ref[:, h * dk:(h + 1) * dk]) * (dk ** -0.5) for h in heads]
    v = [v_ref[:, h * dv:(h + 1) * dv] for h in heads]
    qb = [t.astype(BF16) for t in q]
    state = [state_ref[h] for h in heads]
    scores = [(_dot_nt(qb[h], k[h].astype(BF16)) * intra_ref[h]).astype(BF16) for h in heads]
    cross = [_dot(qb[h], state[h].astype(BF16)) * qdec_ref[h] for h in heads]
    o = [_dot(scores[h], v[h]) + cross[h] for h in heads]
    for h in heads:
        state_ref[h] = (state[h] * cdec_ref[h, 0:1, :]
                        + _dot_tn((k[h] * kdec_ref[h]).astype(BF16), v[h]))
    for h in heads:
        sl = slice(h * dv, (h + 1) * dv)
        mu = jnp.mean(o[h], axis=-1, keepdims=True)
        d = o[h] - mu
        var = jnp.mean(d * d, axis=-1, keepdims=True)
        on = d * lax.rsqrt(var + GN_EPS) * gn_ref[0:1, sl] + gn_ref[1:2, sl]
        gate = g_ref[:, sl].astype(F32)
        gate = gate * (1.0 / (1.0 + jnp.exp(-gate)))
        o_ref[:, sl] = (gate * on).astype(o_ref.dtype)


def _retention(proj, gn, batch, seq):
    h_, dk, dv, c = RET_HEADS, RET_QK_DIM, RET_V_DIM, RET_CHUNK
    n = seq // c
    half = dk // 2
    pos = jnp.arange(seq, dtype=F32)
    inv_freq = RET_THETA ** (-jnp.arange(half, dtype=F32) / half)
    ang = pos[:, None] * inv_freq[None, :]
    cos, sin = jnp.cos(ang), jnp.sin(ang)
    log_gamma = jnp.log(1.0 - 2.0 ** (-5.0 - jnp.arange(h_, dtype=F32)))
    idx = jnp.arange(c, dtype=F32)
    diff = idx[:, None] - idx[None, :]
    intra = jnp.where(diff >= 0, jnp.exp(log_gamma[:, None, None] * jnp.maximum(diff, 0.0)), 0.0)
    qdec = jnp.broadcast_to(jnp.exp(log_gamma[:, None] * (idx + 1.0))[:, :, None], (h_, c, dv))
    kdec = jnp.broadcast_to(jnp.exp(log_gamma[:, None] * (c - 1.0 - idx))[:, :, None], (h_, c, dk))
    cdec = jnp.broadcast_to(jnp.exp(log_gamma * c)[:, None, None], (h_, 8, dv))
    qk_w, vg_w = h_ * dk, h_ * dv
    return pl.pallas_call(
        _ret_body,
        grid=(batch, n),
        in_specs=[
            pl.BlockSpec((c, qk_w), lambda b, i: (b * n + i, 0)),
            pl.BlockSpec((c, qk_w), lambda b, i: (b * n + i, 1)),
            pl.BlockSpec((c, vg_w), lambda b, i: (b * n + i, 2 * qk_w // vg_w)),
            pl.BlockSpec((c, vg_w), lambda b, i: (b * n + i, 2 * qk_w // vg_w + 1)),
            pl.BlockSpec((c, half), lambda b, i: (i, 0)),
            pl.BlockSpec((c, half), lambda b, i: (i, 0)),
            _resident((h_, c, c)), _resident((h_, c, dv)), _resident((h_, c, dk)),
            _resident((h_, 8, dv)), _resident((2, vg_w)),
        ],
        out_specs=pl.BlockSpec((c, vg_w), lambda b, i: (b * n + i, 0)),
        out_shape=jax.ShapeDtypeStruct((batch * seq, vg_w), BF16),
        scratch_shapes=[pltpu.VMEM((h_, dk, dv), F32)],
        compiler_params=_cparams(("parallel", "arbitrary")),
        name="retention",
    )(proj, proj, proj, proj, cos, sin, intra, qdec, kdec, cdec, gn)


def _dil_attn_body(q_ref, kp_ref, kc_ref, vp_ref, vc_ref, o_ref, lse_ref):
    blk = DIL_BLOCK
    has_prev = pl.program_id(1) > 0
    qi = lax.broadcasted_iota(jnp.int32, (blk, 2 * blk), 0)
    ki = lax.broadcasted_iota(jnp.int32, (blk, 2 * blk), 1)
    valid = jnp.logical_and(jnp.logical_and(ki >= qi, ki <= qi + blk),
                            jnp.logical_or(ki >= blk, has_prev))
    lane = lax.broadcasted_iota(jnp.int32, (blk, LANES), 1)
    ones = jnp.ones((2 * blk, LANES), BF16)
    heads = range(DIL_HEADS)
    sls = [slice(h * DIL_HEAD_DIM, (h + 1) * DIL_HEAD_DIM) for h in heads]
    s = [jnp.where(valid, _dot_nt(q_ref[:, sl], jnp.concatenate([kp_ref[:, sl], kc_ref[:, sl]], axis=0)),
                   NEG) for sl in sls]
    m = [jnp.max(t, axis=-1, keepdims=True) for t in s]
    p = [jnp.exp(s[h] - m[h]).astype(BF16) for h in heads]
    pv = [_dot(p[h], jnp.concatenate(
        [jnp.concatenate([vp_ref[:, sls[h]], vc_ref[:, sls[h]]], axis=0), ones], axis=1)) for h in heads]
    lse_tile = jnp.zeros((blk, LANES), F32)
    for h in heads:
        l = pv[h][:, DIL_HEAD_DIM:]
        o_ref[:, sls[h]] = (pv[h][:, :DIL_HEAD_DIM] / l).astype(o_ref.dtype)
        lse_tile = jnp.where(lane == h, m[h] + jnp.log(l), lse_tile)
    lse_ref[...] = lse_tile


def _dil_attn(qkv, batch, seq, dil):
    blk = DIL_BLOCK
    hd = DIL_HEADS * DIL_HEAD_DIM
    nb = seq // dil // blk
    cur = lambda c: (lambda z, i: (z * nb + i, c))
    prev = lambda c: (lambda z, i: (z * nb + jnp.maximum(i - 1, 0), c))
    out_map = lambda z, i: (z * nb + i, 0)
    return pl.pallas_call(
        _dil_attn_body,
        grid=(batch * dil, nb),
        in_specs=[pl.BlockSpec((blk, hd), cur(0)), pl.BlockSpec((blk, hd), prev(1)),
                  pl.BlockSpec((blk, hd), cur(1)), pl.BlockSpec((blk, hd), prev(2)),
                  pl.BlockSpec((blk, hd), cur(2))],
        out_specs=[pl.BlockSpec((blk, hd), out_map), pl.BlockSpec((blk, LANES), out_map)],
        out_shape=[jax.ShapeDtypeStruct((batch * seq, hd), BF16),
                   jax.ShapeDtypeStruct((batch * seq, LANES), F32)],
        compiler_params=_cparams(("parallel", "arbitrary")),
        name=f"dil_attn_{dil}",
    )(qkv, qkv, qkv, qkv, qkv)


def _dil_out_body(o0_ref, o1_ref, o2_ref, l0_ref, l1_ref, l2_ref, e_ref, w_ref, res_ref,
                  g_ref, b_ref, o_ref, osc_ref, lsc_ref, mix_ref):
    tm = o_ref.shape[0]
    nh = DIL_HEADS
    for gi, (og, lg) in enumerate(((o0_ref, l0_ref), (o1_ref, l1_ref), (o2_ref, l2_ref))):
        dil = og.shape[1]
        n = tm // dil
        for r in range(dil):
            rows = pl.ds(r, n, stride=dil) if dil > 1 else slice(None)
            for h in range(nh):
                osc_ref[gi * nh + h, rows, :] = og[0, r, :, h * LANES:(h + 1) * LANES].astype(F32)
            lsc_ref[gi, rows, :] = lg[0, r]
    l0, l1, l2 = lsc_ref[0], lsc_ref[1], lsc_ref[2]
    m = jnp.maximum(jnp.maximum(l0, l1), l2)
    e0, e1, e2 = jnp.exp(l0 - m), jnp.exp(l1 - m), jnp.exp(l2 - m)
    den = e0 + e1 + e2
    e = e_ref[...]
    wfull = []
    for eg in (e0, e1, e2):
        hi, lo = _split2(eg / den)
        wfull.append(_dot(hi, e) + _dot(lo, e))
    for h in range(nh):
        sl = slice(h * LANES, (h + 1) * LANES)
        mixed = sum(wfull[gi][:, sl] * osc_ref[gi * nh + h] for gi in range(3))
        mix_ref[:, sl] = mixed.astype(BF16)
    acc = _dot(mix_ref[...], w_ref[...])
    o_ref[...] = _layer_norm(DN_ALPHA * res_ref[...] + acc, g_ref[...], b_ref[...])


def _dil_out(outs, lses, w, res, g, b, seq, *, tm=512):
    m, d = res.shape
    hd = DIL_HEADS * DIL_HEAD_DIM
    expand = (jnp.arange(LANES)[:, None] == (jnp.arange(hd) // DIL_HEAD_DIM)[None, :]).astype(BF16)
    row = lambda i: (i, 0)
    nt = seq // tm
    grp = lambda i: (i // nt, 0, i % nt, 0)
    dils = [dil for _, dil in DIL_PAIRS]
    batch = m // seq
    o4 = [o.reshape(batch, dil, seq // dil, hd) for o, dil in zip(outs, dils)]
    l4 = [l.reshape(batch, dil, seq // dil, LANES) for l, dil in zip(lses, dils)]
    return pl.pallas_call(
        _dil_out_body,
        grid=(m // tm,),
        in_specs=[pl.BlockSpec((1, dil, tm // dil, hd), grp) for dil in dils]
        + [pl.BlockSpec((1, dil, tm // dil, LANES), grp) for dil in dils]
        + [_resident((LANES, hd)), _resident((hd, d)), pl.BlockSpec((tm, d), row),
           _resident((1, d)), _resident((1, d))],
        out_specs=pl.BlockSpec((tm, d), row),
        out_shape=jax.ShapeDtypeStruct((m, d), F32),
        scratch_shapes=[pltpu.VMEM((len(dils) * DIL_HEADS, tm, LANES), F32),
                        pltpu.VMEM((len(dils), tm, LANES), F32), pltpu.VMEM((tm, hd), BF16)],
        compiler_params=_cparams(("parallel",)),
        name="dil_out",
    )(*o4, *l4, expand, w, res, g, b)


def _rope_tables(seq, rot, theta, scale, passthrough):
    half = rot // 2
    inv_freq = theta ** (-jnp.arange(half, dtype=F32) / half)
    ang = jnp.arange(seq, dtype=F32)[:, None] * inv_freq[None, :]
    cos, sin = jnp.cos(ang), jnp.sin(ang)
    fill = jnp.full((seq, ROPE_PARTNER - half), passthrough, F32)
    zero = jnp.zeros((seq, ROPE_PARTNER - half), F32)
    c = jnp.concatenate([cos, fill, cos, fill], axis=1)
    s = jnp.concatenate([-sin, zero, sin, zero], axis=1)
    return c * scale, s * scale


def _rope_lane_order(rot, width):
    half = rot // 2
    rest = list(range(rot, width))
    cut = ROPE_PARTNER - half
    return jnp.array(list(range(half)) + rest[:cut] + list(range(half, rot)) + rest[cut:])


def _dil_proj_body(x_ref, w_ref, c_ref, s_ref, o_ref, xb_ref, xs_ref, *, dil):
    tm = x_ref.shape[0]
    n = tm // dil
    j = pl.program_id(1)

    @pl.when(j == 0)
    def _():
        if dil == 1:
            xb_ref[...] = x_ref[...].astype(BF16)
        else:
            for c in range(D_MODEL // LANES):
                xs_ref[c] = x_ref[:, c * LANES:(c + 1) * LANES]
            for r in range(dil):
                for c in range(D_MODEL // LANES):
                    xb_ref[r * n:(r + 1) * n, c * LANES:(c + 1) * LANES] = (
                        xs_ref[c, pl.ds(r, n, stride=dil), :].astype(BF16))

    def project(rope):
        xb = xb_ref[...]
        for c0 in range(0, o_ref.shape[-1], MM_SUB):
            acc = _dot(xb, w_ref[:, c0:c0 + MM_SUB])
            for t in range(MM_SUB // LANES):
                a = acc[:, t * LANES:(t + 1) * LANES]
                if rope:
                    a = _rope_tile(a, c_ref[0], s_ref[0])
                lanes = slice(c0 + t * LANES, c0 + (t + 1) * LANES)
                o_ref[0, :, :, lanes] = a.astype(o_ref.dtype).reshape(dil, n, LANES)

    pl.when(j < 2)(functools.partial(project, True))
    pl.when(j >= 2)(functools.partial(project, False))


def _dil_proj(x, w, tabs, batch, seq, dil, *, tm=512):
    hd = DIL_HEADS * DIL_HEAD_DIM
    nt = seq // tm
    n = tm // dil
    tab_spec = pl.BlockSpec((1, tm, LANES), lambda i, j: (jnp.minimum(j, 1), i % nt, 0))
    return pl.pallas_call(
        functools.partial(_dil_proj_body, dil=dil),
        grid=(batch * nt, 3),
        in_specs=[pl.BlockSpec((tm, D_MODEL), lambda i, j: (i, 0)),
                  pl.BlockSpec((D_MODEL, hd), lambda i, j: (0, j)), tab_spec, tab_spec],
        out_specs=pl.BlockSpec((1, dil, n, hd), lambda i, j: (i // nt, 0, i % nt, j)),
        out_shape=jax.ShapeDtypeStruct((batch, dil, seq // dil, 3 * hd), BF16),
        scratch_shapes=[pltpu.VMEM((tm, D_MODEL), BF16),
                        pltpu.VMEM((D_MODEL // LANES, tm, LANES), F32)],
        compiler_params=_cparams(("parallel", "arbitrary")),
        name=f"dil_proj_{dil}",
    )(x, w, *tabs)


def _dilated(x, w_in, batch, seq, *, tm=512):
    hd = DIL_HEADS * DIL_HEAD_DIM
    cq, sq = _rope_tables(seq, DIL_ROT, ROPE_THETA, DIL_HEAD_DIM ** -0.5, 1.0)
    ck, sk = _rope_tables(seq, DIL_ROT, ROPE_THETA, 1.0, 1.0)
    order = _rope_lane_order(DIL_ROT, DIL_HEAD_DIM)
    outs, lses = [], []
    for gi, (_, dil) in enumerate(DIL_PAIRS):
        wg = w_in[:, gi * 3 * hd:(gi + 1) * 3 * hd].reshape(D_MODEL, 3, DIL_HEADS, DIL_HEAD_DIM)
        wg = jnp.concatenate([wg[:, :2][..., order], wg[:, 2:]], axis=1).reshape(D_MODEL, 3 * hd)

        def by_residue(t):
            t = t.reshape(seq // tm, tm // dil, dil, LANES)
            return jnp.swapaxes(t, 1, 2).reshape(seq, LANES)

        tabs = [jnp.stack([by_residue(a), by_residue(b)]) for a, b in ((cq, ck), (sq, sk))]
        qkv = _dil_proj(x, wg, tabs, batch, seq, dil, tm=tm)
        o, lse = _dil_attn(qkv.reshape(batch * seq, 3 * hd), batch, seq, dil)
        outs.append(o)
        lses.append(lse)
    return outs, lses


def _mla_down_body(x_ref, w_ref, nq_ref, nkv_ref, c_ref, s_ref, cq_ref, ckv_ref, kpe_ref):
    acc = _dot(x_ref[...].astype(BF16), w_ref[...])
    cq = acc[:, :MLA_Q_RANK]
    ckv = acc[:, MLA_Q_RANK:MLA_Q_RANK + MLA_KV_RANK]
    kpe = acc[:, MLA_Q_RANK + MLA_KV_RANK:]
    cq = cq * lax.rsqrt(jnp.mean(cq * cq, axis=-1, keepdims=True) + RMS_EPS) * nq_ref[...]
    ckv = ckv * lax.rsqrt(jnp.mean(ckv * ckv, axis=-1, keepdims=True) + RMS_EPS) * nkv_ref[...]
    kpe = _rope_tile(kpe, c_ref[...], s_ref[...])
    cq_ref[...] = cq.astype(cq_ref.dtype)
    ckv_ref[...] = ckv.astype(ckv_ref.dtype)
    kpe_ref[...] = kpe.astype(kpe_ref.dtype)


def _mla_down(x, w, nq, nkv, tabs, seq, *, tm=512):
    m, d = x.shape
    n = w.shape[1]
    row = lambda i: (i, 0)
    ns = seq // tm
    tab = lambda i: (i % ns, 0)
    return pl.pallas_call(
        _mla_down_body,
        grid=(m // tm,),
        in_specs=[pl.BlockSpec((tm, d), row), _resident((d, n)), _resident((1, MLA_Q_RANK)),
                  _resident((1, MLA_KV_RANK))] + [pl.BlockSpec((tm, LANES), tab)] * 2,
        out_specs=[pl.BlockSpec((tm, MLA_Q_RANK), row), pl.BlockSpec((tm, MLA_KV_RANK), row),
                   pl.BlockSpec((tm, LANES), row)],
        out_shape=[jax.ShapeDtypeStruct((m, MLA_Q_RANK), BF16),
                   jax.ShapeDtypeStruct((m, MLA_KV_RANK), BF16),
                   jax.ShapeDtypeStruct((m, LANES), BF16)],
        compiler_params=_cparams(("parallel",)),
        name="mla_down",
    )(x, w, nq, nkv, *tabs)


def _mla_flash_body(q_ref, kn_ref, kpe_ref, v_ref, o_ref, kcat_ref, vaug_ref):
    tq, tk, sub = MLA_TQ, MLA_TK, MLA_TQ // MLA_ROW_SPLIT
    iq = pl.program_id(2)

    @pl.when(iq == 0)
    def _():
        kcat_ref[:, :MLA_NOPE] = kn_ref[...]
        kcat_ref[:, MLA_NOPE:] = kpe_ref[...]
        vaug_ref[:, :MLA_V] = v_ref[...]
        vaug_ref[:, MLA_V:] = jnp.ones((v_ref.shape[0], LANES), BF16)

    parts = range(MLA_ROW_SPLIT)
    col_minus_row = (lax.broadcasted_iota(jnp.int32, (sub, tk), 1)
                     - lax.broadcasted_iota(jnp.int32, (sub, tk), 0))

    def scores(chunk, diagonal):
        out = []
        for part in parts:
            width = (part + 1) * sub if diagonal else tk
            s = _dot_nt(q_ref[part * sub:(part + 1) * sub, :], kcat_ref[chunk * tk:chunk * tk + width, :])
            if diagonal:
                s = jnp.where(col_minus_row[:, :width] <= part * sub, s, NEG)
            out.append(s)
        return out

    def update(chunk, s_all, carry):
        out = []
        for s, (m, acc) in zip(s_all, carry):
            vb = vaug_ref[chunk * tk:chunk * tk + s.shape[1], :]
            m_new = jnp.maximum(m, jnp.max(s, axis=-1, keepdims=True))
            alpha = jnp.exp2(m - m_new)
            p = jnp.exp2((s - m_new).astype(BF16))
            out.append((m_new, alpha * acc + _dot(p, vb)))
        return out

    def attend(n_chunks):
        carry = [(jnp.full((sub, 1), NEG, F32), jnp.zeros((sub, MLA_V + LANES), F32)) for _ in parts]
        s = scores(0, n_chunks == 1)
        for c in range(n_chunks):
            s_next = scores(c + 1, c + 2 == n_chunks) if c + 1 < n_chunks else None
            carry = update(c, s, carry)
            s = s_next
        for part, (_, acc) in enumerate(carry):
            o_ref[part * sub:(part + 1) * sub, :] = (acc[:, :MLA_V] / acc[:, MLA_V:]).astype(o_ref.dtype)

    n_chunks = (iq * tq) // tk + 1
    for n in range(1, kcat_ref.shape[0] // tk + 1):
        pl.when(n_chunks == n)(functools.partial(attend, n))


def _mla_flash(q, kv, kpe, batch, seq):
    blk = MLA_TQ
    nq = seq // blk
    h_ = MLA_HEADS
    qw = MLA_NOPE + LANES
    return pl.pallas_call(
        _mla_flash_body,
        grid=(batch, h_, nq),
        in_specs=[pl.BlockSpec((blk, qw), lambda b, h, i: (b * nq + i, h)),
                  pl.BlockSpec((seq, MLA_NOPE), lambda b, h, i: (b, h)),
                  pl.BlockSpec((seq, LANES), lambda b, h, i: (b, 0)),
                  pl.BlockSpec((seq, MLA_V), lambda b, h, i: (b, h_ + h))],
        out_specs=pl.BlockSpec((blk, MLA_V), lambda b, h, i: (b * nq + i, h)),
        out_shape=jax.ShapeDtypeStruct((batch * seq, h_ * MLA_V), BF16),
        scratch_shapes=[pltpu.VMEM((seq, qw), BF16), pltpu.VMEM((seq, MLA_V + LANES), BF16)],
        compiler_params=_cparams(("parallel", "parallel", "arbitrary")),
        name="mla_flash",
    )(q, kv, kpe, kv)


def _mla(x, w_down, norm_q, norm_kv, w_uq, w_ukv, batch, seq):
    h_ = MLA_HEADS
    half = MLA_ROPE // 2

    def pe_tile(w):
        z = jnp.zeros(w.shape[:-1] + (ROPE_PARTNER - half,), w.dtype)
        return jnp.concatenate([w[..., :half], z, w[..., half:], z], axis=-1)

    n_lat = MLA_Q_RANK + MLA_KV_RANK
    wd = jnp.concatenate([w_down[:, :n_lat], pe_tile(w_down[:, n_lat:])], axis=1).astype(BF16)
    wq = w_uq.reshape(MLA_Q_RANK, h_, MLA_NOPE + MLA_ROPE)
    wq = jnp.concatenate([wq[..., :MLA_NOPE], pe_tile(wq[..., MLA_NOPE:])], axis=-1)
    wq = wq.reshape(MLA_Q_RANK, -1).astype(BF16)
    wkv = w_ukv.reshape(MLA_KV_RANK, h_, MLA_NOPE + MLA_V)
    wkv = jnp.concatenate([wkv[:, :, :MLA_NOPE].reshape(MLA_KV_RANK, -1),
                           wkv[:, :, MLA_NOPE:].reshape(MLA_KV_RANK, -1)], axis=1).astype(BF16)
    scale = (MLA_NOPE + MLA_ROPE) ** -0.5 * math.log2(math.e)
    tk = _rope_tables(seq, MLA_ROPE, MLA_THETA, 1.0, 0.0)
    tq = [t[None] for t in _rope_tables(seq, MLA_ROPE, MLA_THETA, scale, 0.0)]
    cq, ckv, kpe = _mla_down(x, wd, norm_q[None, :], norm_kv[None, :], tk, seq)
    tm = 512
    ns = seq // tm
    q = _mm(cq, wq, tm=tm, tn=1024, m_rows=batch * seq, tabs=tq,
            tab_map=lambda i, j: (0, i % ns, 0), pattern=("scale", "rope") * 4,
            scale=scale, name="mla_q")
    kv = _mm(ckv, wkv, tm=tm, tn=1024, m_rows=batch * seq, name="mla_kv")
    return _mla_flash(q, kv, kpe, batch, seq)


def _head_sum(z, ones_bd):
    hi, lo = _split2(z)
    return _dot(hi, ones_bd) + _dot(lo, ones_bd)


def _rwkv_prep_body(x_ref, xp_ref, mu_ref, wr_ref, wk_ref, wv_ref, la0_ref, lb0_ref, la1_ref,
                    lb1_ref, ga_ref, gb_ref, vec_ref, bd_ref, r_ref, lw_ref, k_ref, v_ref,
                    kk_ref, b_ref, g_ref, *, tiles_per_seq):
    x = x_ref[...]
    tm = x.shape[0]
    first = pl.program_id(0) % tiles_per_seq == 0
    prev_row = jnp.where(first, 0.0, xp_ref[7:8, :])
    rows = lax.broadcasted_iota(jnp.int32, x.shape, 0)
    shifted = jnp.where(rows == 0, prev_row, pltpu.roll(x, 1, 0))
    xx = shifted - x
    mix = lambda i: (x + xx * mu_ref[i:i + 1, :]).astype(BF16)
    r = _dot(mix(0), wr_ref[...])
    k_raw = _dot(mix(2), wk_ref[...])
    v = _dot(mix(3), wv_ref[...])
    w0, a0, k_k, k_a = (vec_ref[i:i + 1, :] for i in range(4))
    wl = w0 + _dot(jnp.tanh(_dot(mix(1), la0_ref[...])).astype(BF16), lb0_ref[...])
    z = -wl
    softplus = jnp.maximum(z, 0.0) + jnp.log(1.0 + jnp.exp(-jnp.abs(z)))
    lw_ref[...] = -jnp.exp(-softplus - 0.5)
    al = a0 + _dot(_dot(mix(4), la1_ref[...]).astype(BF16), lb1_ref[...])
    a = 1.0 / (1.0 + jnp.exp(-al))
    gl = _dot(mix(5), ga_ref[...])
    g = _dot((1.0 / (1.0 + jnp.exp(-gl))).astype(BF16), gb_ref[...])
    kk = k_raw * k_k
    bd = bd_ref[...]
    for s in range(D_MODEL // LANES):
        sl = slice(s * LANES, (s + 1) * LANES)
        t = kk[:, sl]
        nrm = jnp.maximum(jnp.sqrt(_head_sum(t * t, bd)), 1e-12)
        t = t / nrm
        kk_ref[:, sl] = t.astype(kk_ref.dtype)
        b_ref[:, sl] = (t * a[:, sl]).astype(b_ref.dtype)
    r_ref[...] = r.astype(r_ref.dtype)
    k_ref[...] = (k_raw * (1.0 + (a - 1.0) * k_a)).astype(k_ref.dtype)
    v_ref[...] = v.astype(v_ref.dtype)
    g_ref[...] = g.astype(g_ref.dtype)


def _head_ones():
    idx = jnp.arange(LANES) // RWKV_HEAD
    return (idx[:, None] == idx[None, :]).astype(BF16)


def _rwkv_prep(x, mu, w_rkv, vec, lora_a, lora_b, gate_a, gate_b, seq, *, tm=256):
    m, d = x.shape
    gpad = RWKV_GATE_PAD - gate_a.shape[1]
    ga = jnp.pad(gate_a, ((0, 0), (0, gpad))).astype(BF16)
    gb = jnp.pad(gate_b, ((0, gpad), (0, 0))).astype(BF16)
    wts = [w_rkv[0].astype(BF16), w_rkv[1].astype(BF16), w_rkv[2].astype(BF16),
           lora_a[0].astype(BF16), lora_b[0].astype(BF16), lora_a[1].astype(BF16),
           lora_b[1].astype(BF16), ga, gb]
    vec8 = jnp.pad(vec, ((0, 3), (0, 0)))
    mu8 = jnp.pad(mu, ((0, 2), (0, 0)))
    row = lambda i: (i, 0)
    sub = tm // 8
    out = jax.ShapeDtypeStruct((m, d), BF16)
    return pl.pallas_call(
        functools.partial(_rwkv_prep_body, tiles_per_seq=seq // tm),
        grid=(m // tm,),
        in_specs=[pl.BlockSpec((tm, d), row),
                  pl.BlockSpec((8, d), lambda i: (jnp.maximum(i * sub - 1, 0), 0)),
                  _resident(mu8.shape)] + [_resident(w.shape) for w in wts]
        + [_resident(vec8.shape), _resident((LANES, LANES))],
        out_specs=[pl.BlockSpec((tm, d), row)] * 7,
        out_shape=[out, jax.ShapeDtypeStruct((m, d), F32), out, out, out, out, out],
        compiler_params=_cparams(("parallel",)),
        name="rwkv_prep",
    )(x, x, mu8, *wts, vec8, _head_ones())


def _rwkv_wkv_body(r_ref, lw_ref, k_ref, v_ref, kk_ref, b_ref, tri_ref, y_ref, state_ref):
    c = RWKV_CHUNK
    two = 2 * c

    @pl.when(pl.program_id(1) == 0)
    def _():
        state_ref[...] = jnp.zeros(state_ref.shape, F32)

    lw = lw_ref[...]
    tri = tri_ref[...]
    h1 = lw.astype(BF16)
    r1 = lw - h1.astype(F32)
    h2 = r1.astype(BF16)
    h3 = (r1 - h2.astype(F32)).astype(BF16)
    cum = _dot(tri, h1) + _dot(tri, h2) + _dot(tri, h3)
    gam = jnp.exp(cum)
    gam_ex = jnp.exp(cum - lw)
    gam_inv = jnp.exp(-cum)
    rt = r_ref[...].astype(F32) * gam
    kkt = kk_ref[...].astype(F32) * gam_ex
    bt = b_ref[...].astype(F32) * gam_inv
    kt = k_ref[...].astype(F32) * gam_inv
    gam_end = gam[c - 1:c, :]

    lane_lo = lax.broadcasted_iota(jnp.int32, (c, LANES), 1) < RWKV_HEAD
    row2 = lax.broadcasted_iota(jnp.int32, (two, two), 0)
    col2 = lax.broadcasted_iota(jnp.int32, (two, two), 1)
    same = (row2 // c) == (col2 // c)
    strict = jnp.logical_and(same, row2 > col2)
    incl = jnp.logical_and(same, row2 >= col2)
    eye = (row2 == col2).astype(F32)

    def stack_masked(t):
        return jnp.concatenate([jnp.where(lane_lo, t, 0.0), jnp.where(lane_lo, 0.0, t)], axis=0)

    pairs = range(D_MODEL // LANES)
    sls = [slice(p * LANES, (p + 1) * LANES) for p in pairs]
    xs = [jnp.concatenate([stack_masked(kkt[:, sl]), stack_masked(rt[:, sl])], axis=0).astype(BF16)
          for sl in sls]
    bds = [jnp.concatenate([bt[:, sl], bt[:, sl]], axis=0).astype(BF16) for sl in sls]
    kds = [jnp.concatenate([kt[:, sl], kt[:, sl]], axis=0).astype(BF16) for sl in sls]
    vss = []
    for sl in sls:
        v2 = v_ref[:, sl].astype(F32)
        vss.append(jnp.where(same, jnp.concatenate([v2, v2], axis=0), 0.0).astype(BF16))
    s2s = [state_ref[p] for p in pairs]
    a_all = [_dot_nt(xs[p], jnp.concatenate([bds[p], kds[p]], axis=0)) for p in pairs]
    xs_state = [_dot_nt(xs[p], s2s[p].astype(BF16)) for p in pairs]
    nmat = [jnp.where(strict, -a[:two, :two], 0.0) for a in a_all]
    lk = [jnp.where(strict, a[:two, two:], 0.0).astype(BF16) for a in a_all]
    arb = [jnp.where(incl, a[two:, :two], 0.0).astype(BF16) for a in a_all]
    ark = [jnp.where(incl, a[two:, two:], 0.0).astype(BF16) for a in a_all]
    rhs = [xs_state[p][:two] + _dot(lk[p], vss[p]) for p in pairs]
    inv = [eye + n_ for n_ in nmat]
    pw = nmat
    for _ in range(int(math.log2(c)) - 1):
        pwb = [t.astype(BF16) for t in pw]
        pw = [_dot(t, t) for t in pwb]
        inv = [inv[p] + _dot(inv[p].astype(BF16), pw[p].astype(BF16)) for p in pairs]
    ub = [(-_dot(inv[p].astype(BF16), rhs[p].astype(BF16))).astype(BF16) for p in pairs]
    for p in pairs:
        ys = xs_state[p][two:] + _dot(arb[p], ub[p]) + _dot(ark[p], vss[p])
        y_ref[:, sls[p]] = ys[:c] + ys[c:]
    for p in pairs:
        ds = _dot_tn(ub[p], bds[p]) + _dot_tn(vss[p], kds[p])
        state_ref[p] = jnp.where(same, (s2s[p] + ds) * gam_end[:, sls[p]], 0.0)


def _rwkv_wkv(r, lw, k, v, kk, b, batch, seq):
    c = RWKV_CHUNK
    n = seq // c
    d = D_MODEL
    tri = (jnp.arange(c)[:, None] >= jnp.arange(c)[None, :]).astype(BF16)
    blk = pl.BlockSpec((c, d), lambda bi, i: (bi * n + i, 0))
    return pl.pallas_call(
        _rwkv_wkv_body,
        grid=(batch, n),
        in_specs=[blk] * 6 + [_resident((c, c))],
        out_specs=blk,
        out_shape=jax.ShapeDtypeStruct((batch * seq, d), F32),
        scratch_shapes=[pltpu.VMEM((d // LANES, LANES, LANES), F32)],
        compiler_params=_cparams(("parallel", "arbitrary")),
        name="rwkv_wkv",
    )(r, lw, k, v, kk, b, tri)


def _rwkv_out_body(y_ref, r_ref, k_ref, v_ref, g_ref, vec_ref, bd_ref, w_ref, res_ref, lg_ref,
                   lb_ref, o_ref, a_ref):
    bd = bd_ref[...]
    inv_n = 1.0 / RWKV_HEAD
    for s in range(D_MODEL // LANES):
        sl = slice(s * LANES, (s + 1) * LANES)
        y = y_ref[:, sl]
        mu = _head_sum(y, bd) * inv_n
        dlt = y - mu
        var = _head_sum(dlt * dlt, bd) * inv_n
        yn = dlt * lax.rsqrt(var + RWKV_GN_EPS) * vec_ref[0:1, sl] + vec_ref[1:2, sl]
        rk = r_ref[:, sl].astype(F32) * k_ref[:, sl].astype(F32) * vec_ref[2:3, sl]
        bonus = _head_sum(rk, bd) * v_ref[:, sl].astype(F32)
        a_ref[:, sl] = ((yn + bonus) * g_ref[:, sl].astype(F32)).astype(BF16)
    acc = _dot(a_ref[...], w_ref[...])
    o_ref[...] = _layer_norm(DN_ALPHA * res_ref[...] + acc, lg_ref[...], lb_ref[...])


def _rwkv_out(y, r, k, v, g, vec, w, res, lg, lb, *, tm=512):
    m, d = res.shape
    row = lambda i: (i, 0)
    act = pl.BlockSpec((tm, d), row)
    return pl.pallas_call(
        _rwkv_out_body,
        grid=(m // tm,),
        in_specs=[act] * 5 + [_resident(vec.shape), _resident((LANES, LANES)), _resident((d, d)),
                              act, _resident((1, d)), _resident((1, d))],
        out_specs=act,
        out_shape=jax.ShapeDtypeStruct((m, d), F32),
        scratch_shapes=[pltpu.VMEM((tm, d), BF16)],
        compiler_params=_cparams(("parallel",)),
        name="rwkv_out",
    )(y, r, k, v, g, vec, _head_ones(), w, res, lg, lb)


def kernel(x, ret_w_in, ret_gn, ret_w_out, dil_w_in, dil_w_out, mla_w_down, mla_norm_q,
           mla_norm_kv, mla_w_uq, mla_w_ukv, mla_w_out, rwkv_mu, rwkv_w_rkv, rwkv_w_out,
           rwkv_vec, rwkv_lora_a, rwkv_lora_b, rwkv_gate_a, rwkv_gate_b, rwkv_ln_x,
           mlp_w1, mlp_w2, ln_g, ln_b):
    batch, seq, d = x.shape
    xf = x.reshape(batch * seq, d)
    n_mixers = 4
    for i in range(DEPTH):
        mixer, j = i % n_mixers, i // n_mixers
        lg, lb = ln_g[i, 0][None, :], ln_b[i, 0][None, :]
        if mixer == 0:
            proj = _mm(xf, ret_w_in[j].astype(BF16), tm=1024, tn=1024, m_rows=batch * seq,
                       name="ret_proj")
            gated = _retention(proj, ret_gn[j], batch, seq)
            xf = _mm_res_ln(gated, ret_w_out[j].astype(BF16), xf, lg, lb, name="ret_out")
        elif mixer == 1:
            outs, lses = _dilated(xf, dil_w_in[j].astype(BF16), batch, seq)
            xf = _dil_out(outs, lses, dil_w_out[j].astype(BF16), xf, lg, lb, seq)
        elif mixer == 2:
            o = _mla(xf, mla_w_down[j], mla_norm_q[j], mla_norm_kv[j], mla_w_uq[j], mla_w_ukv[j],
                     batch, seq)
            xf = _mm_res_ln(o, mla_w_out[j].astype(BF16), xf, lg, lb, name="mla_out")
        else:
            r, lw, k, v, kk, b, g = _rwkv_prep(xf, rwkv_mu[j], rwkv_w_rkv[j], rwkv_vec[j],
                                               rwkv_lora_a[j], rwkv_lora_b[j], rwkv_gate_a[j],
                                               rwkv_gate_b[j], seq)
            y = _rwkv_wkv(r, lw, k, v, kk, b, batch, seq)
            vec = jnp.concatenate([rwkv_ln_x[j], rwkv_vec[j][4:5],
                                   jnp.zeros((5, d), F32)], axis=0)
            xf = _rwkv_out(y, r, k, v, g, vec, rwkv_w_out[j].astype(BF16), xf, lg, lb)
        xf = _mlp(xf, mlp_w1[i].astype(BF16), mlp_w2[i].astype(BF16),
                  ln_g[i, 1][None, :], ln_b[i, 1][None, :])
    return xf.reshape(batch, seq, d)
```

```python
import functools
import math

import jax
import jax.numpy as jnp
from jax import lax
from jax.experimental import pallas as pl
from jax.experimental.pallas import tpu as pltpu

F32 = jnp.float32
BF16 = jnp.bfloat16

D_MODEL = 1024
DEPTH = 4
D_FF = 4 * D_MODEL
LN_EPS = 1e-5
RMS_EPS = 1e-6
GN_EPS = 1e-5
DN_ALPHA = (2.0 * DEPTH) ** 0.25
NEG = -1e30
LANES = 128
ROPE_PARTNER = 64
MM_SUB = 256

RET_HEADS = 4
RET_QK_DIM = 256
RET_V_DIM = 512
RET_CHUNK = 128
RET_THETA = 10000.0

DIL_PAIRS = ((128, 1), (512, 4), (2048, 16))
DIL_HEADS = 8
DIL_HEAD_DIM = 128
DIL_ROT = 32
DIL_BLOCK = 128
DIL_TQ = 512
ROPE_THETA = 500000.0

MLA_HEADS = 16
MLA_NOPE = 128
MLA_ROPE = 64
MLA_V = 128
MLA_Q_RANK = 256
MLA_KV_RANK = 128
MLA_THETA = 10000.0
MLA_TQ = 1024
MLA_TK = 1024
MLA_ROW_SPLIT = 4
assert MLA_TQ == MLA_TK

RWKV_HEAD = 64
RWKV_HEADS = D_MODEL // RWKV_HEAD
RWKV_GN_EPS = 64e-5
RWKV_CHUNK = 64
RWKV_GATE_PAD = 256
RWKV_BATCH_ROWS = 2

VMEM_LIMIT = 56 * 1024 * 1024


def _cparams(sem):
    return pltpu.CompilerParams(dimension_semantics=sem, vmem_limit_bytes=VMEM_LIMIT)


def _resident(shape):
    nd = len(shape)
    return pl.BlockSpec(shape, lambda *_: (0,) * nd, pipeline_mode=pl.Buffered(1))


def _layer_norm(z, g, b):
    mu = jnp.mean(z, axis=-1, keepdims=True)
    d = z - mu
    var = jnp.mean(d * d, axis=-1, keepdims=True)
    return d * lax.rsqrt(var + LN_EPS) * g + b


def _dot(a, b):
    return jnp.dot(a, b, preferred_element_type=F32)


def _dot_nt(a, b):
    return lax.dot_general(a, b, (((1,), (1,)), ((), ())), preferred_element_type=F32)


def _dot_tn(a, b):
    return lax.dot_general(a, b, (((0,), (0,)), ((), ())), preferred_element_type=F32)


def _split2(z):
    hi = z.astype(BF16)
    lo = (z - hi.astype(F32)).astype(BF16)
    return hi, lo


def _rope_tile(a, c, s):
    return a * c + pltpu.roll(a, ROPE_PARTNER, 1) * s


def _mm_body(x_ref, w_ref, *rest, pattern, rope_jmax, scale, has_tabs):
    if has_tabs:
        c_ref, s_ref, o_ref, xb_ref = rest
    else:
        o_ref, xb_ref = rest
    j = pl.program_id(1)

    @pl.when(j == 0)
    def _():
        xb_ref[...] = x_ref[...].astype(BF16)

    def plain():
        o_ref[...] = _dot(xb_ref[...], w_ref[...]).astype(o_ref.dtype)

    def fancy():
        xb = xb_ref[...]
        for c0 in range(0, len(pattern) * LANES, MM_SUB):
            acc = _dot(xb, w_ref[:, c0:c0 + MM_SUB])
            for t in range(MM_SUB // LANES):
                a = acc[:, t * LANES:(t + 1) * LANES]
                mode = pattern[c0 // LANES + t]
                if mode == "rope":
                    a = _rope_tile(a, c_ref[0], s_ref[0])
                elif mode == "scale":
                    a = a * scale
                o_ref[:, c0 + t * LANES:c0 + (t + 1) * LANES] = a.astype(o_ref.dtype)

    if pattern is None:
        plain()
    elif rope_jmax is None:
        fancy()
    else:
        pl.when(j < rope_jmax)(fancy)
        pl.when(j >= rope_jmax)(plain)


def _mm(x, w, *, tm, tn, m_rows, out_dtype=BF16, x_map=None, tabs=None, tab_map=None,
        pattern=None, rope_jmax=None, scale=1.0, name="mm"):
    k, n = w.shape
    grid = (m_rows // tm, n // tn)
    if x_map is None:
        x_map = lambda i, j: (i, 0)
    in_specs = [pl.BlockSpec((tm, k), x_map), pl.BlockSpec((k, tn), lambda i, j: (0, j))]
    args = [x, w]
    if tabs is not None:
        for t in tabs:
            in_specs.append(pl.BlockSpec((1, tm, LANES), tab_map))
            args.append(t)
    body = functools.partial(_mm_body, pattern=pattern, rope_jmax=rope_jmax,
                             scale=scale, has_tabs=tabs is not None)
    return pl.pallas_call(
        body,
        grid=grid,
        in_specs=in_specs,
        out_specs=pl.BlockSpec((tm, tn), lambda i, j: (i, j)),
        out_shape=jax.ShapeDtypeStruct((m_rows, n), out_dtype),
        scratch_shapes=[pltpu.VMEM((tm, k), BF16)],
        compiler_params=_cparams(("parallel", "arbitrary")),
        name=name,
    )(*args)


def _mm_res_ln_body(a_ref, w_ref, res_ref, g_ref, b_ref, o_ref):
    acc = _dot(a_ref[...], w_ref[...])
    o_ref[...] = _layer_norm(DN_ALPHA * res_ref[...] + acc, g_ref[...], b_ref[...])


def _mm_res_ln(a, w, res, g, b, *, tm=512, name="mm_res_ln"):
    m, k = a.shape
    d = w.shape[1]
    row = lambda i: (i, 0)
    return pl.pallas_call(
        _mm_res_ln_body,
        grid=(m // tm,),
        in_specs=[pl.BlockSpec((tm, k), row), _resident((k, d)), pl.BlockSpec((tm, d), row),
                  _resident((1, d)), _resident((1, d))],
        out_specs=pl.BlockSpec((tm, d), row),
        out_shape=jax.ShapeDtypeStruct((m, d), F32),
        compiler_params=_cparams(("parallel",)),
        name=name,
    )(a, w, res, g, b)


def _mlp_body(x_ref, w1_ref, w2_ref, g_ref, b_ref, o_ref, *, fchunk):
    x = x_ref[...]
    xb = x.astype(BF16)
    acc = jnp.zeros(x.shape, F32)
    for c in range(D_FF // fchunk):
        h = _dot(xb, w1_ref[:, c * fchunk:(c + 1) * fchunk])
        h = jnp.maximum(h, 0.0)
        h = (h * h).astype(BF16)
        acc = acc + _dot(h, w2_ref[c * fchunk:(c + 1) * fchunk, :])
    o_ref[...] = _layer_norm(DN_ALPHA * x + acc, g_ref[...], b_ref[...])


def _mlp(x, w1, w2, g, b, *, tm=512, fchunk=1024):
    m, d = x.shape
    row = lambda i: (i, 0)
    return pl.pallas_call(
        functools.partial(_mlp_body, fchunk=fchunk),
        grid=(m // tm,),
        in_specs=[pl.BlockSpec((tm, d), row), _resident((d, D_FF)), _resident((D_FF, d)),
                  _resident((1, d)), _resident((1, d))],
        out_specs=pl.BlockSpec((tm, d), row),
        out_shape=jax.ShapeDtypeStruct((m, d), F32),
        compiler_params=_cparams(("parallel",)),
        name="mlp",
    )(x, w1, w2, g, b)


def _ret_body(q_ref, k_ref, v_ref, g_ref, cos_ref, sin_ref, intra_ref, qdec_ref, kdec_ref,
              cdec_ref, gn_ref, o_ref, state_ref):
    dk, dv, half = RET_QK_DIM, RET_V_DIM, RET_QK_DIM // 2

    @pl.when(pl.program_id(1) == 0)
    def _():
        state_ref[...] = jnp.zeros(state_ref.shape, F32)

    cos = cos_ref[...]
    sin = sin_ref[...]

    def rope(t):
        t1 = t[:, :half].astype(F32)
        t2 = t[:, half:].astype(F32)
        return jnp.concatenate([t1 * cos - t2 * sin, t2 * cos + t1 * sin], axis=-1)

    heads = range(RET_HEADS)
    q = [rope(q_ref[:, h * dk:(h + 1) * dk]) for h in heads]
    k = [rope(k_ref[:, h * dk:(h + 1) * dk]) * (dk ** -0.5) for h in heads]
    v = [v_ref[:, h * dv:(h + 1) * dv] for h in heads]
    qb = [t.astype(BF16) for t in q]
    state = [state_ref[h] for h in heads]
    scores = [(_dot_nt(qb[h], k[h].astype(BF16)) * intra_ref[h]).astype(BF16) for h in heads]
    cross = [_dot(qb[h], state[h].astype(BF16)) * qdec_ref[h] for h in heads]
    o = [_dot(scores[h], v[h]) + cross[h] for h in heads]
    for h in heads:
        state_ref[h] = (state[h] * cdec_ref[h, 0:1, :]
                        + _dot_tn((k[h] * kdec_ref[h]).astype(BF16), v[h]))
    for h in heads:
        sl = slice(h * dv, (h + 1) * dv)
        mu = jnp.mean(o[h], axis=-1, keepdims=True)
        d = o[h] - mu
        var = jnp.mean(d * d, axis=-1, keepdims=True)
        on = d * lax.rsqrt(var + GN_EPS) * gn_ref[0:1, sl] + gn_ref[1:2, sl]
        gate = g_ref[:, sl].astype(F32)
        gate = gate * (1.0 / (1.0 + jnp.exp(-gate)))
        o_ref[:, sl] = (gate * on).astype(o_ref.dtype)


def _retention(proj, gn, batch, seq):
    h_, dk, dv, c = RET_HEADS, RET_QK_DIM, RET_V_DIM, RET_CHUNK
    n = seq // c
    half = dk // 2
    pos = jnp.arange(seq, dtype=F32)
    inv_freq = RET_THETA ** (-jnp.arange(half, dtype=F32) / half)
    ang = pos[:, None] * inv_freq[None, :]
    cos, sin = jnp.cos(ang), jnp.sin(ang)
    log_gamma = jnp.log(1.0 - 2.0 ** (-5.0 - jnp.arange(h_, dtype=F32)))
    idx = jnp.arange(c, dtype=F32)
    diff = idx[:, None] - idx[None, :]
    intra = jnp.where(diff >= 0, jnp.exp(log_gamma[:, None, None] * jnp.maximum(diff, 0.0)), 0.0)
    qdec = jnp.broadcast_to(jnp.exp(log_gamma[:, None] * (idx + 1.0))[:, :, None], (h_, c, dv))
    kdec = jnp.broadcast_to(jnp.exp(log_gamma[:, None] * (c - 1.0 - idx))[:, :, None], (h_, c, dk))
    cdec = jnp.broadcast_to(jnp.exp(log_gamma * c)[:, None, None], (h_, 8, dv))
    qk_w, vg_w = h_ * dk, h_ * dv
    return pl.pallas_call(
        _ret_body,
        grid=(batch, n),
        in_specs=[
            pl.BlockSpec((c, qk_w), lambda b, i: (b * n + i, 0)),
            pl.BlockSpec((c, qk_w), lambda b, i: (b * n + i, 1)),
            pl.BlockSpec((c, vg_w), lambda b, i: (b * n + i, 2 * qk_w // vg_w)),
            pl.BlockSpec((c, vg_w), lambda b, i: (b * n + i, 2 * qk_w // vg_w + 1)),
            pl.BlockSpec((c, half), lambda b, i: (i, 0)),
            pl.BlockSpec((c, half), lambda b, i: (i, 0)),
            _resident((h_, c, c)), _resident((h_, c, dv)), _resident((h_, c, dk)),
            _resident((h_, 8, dv)), _resident((2, vg_w)),
        ],
        out_specs=pl.BlockSpec((c, vg_w), lambda b, i: (b * n + i, 0)),
        out_shape=jax.ShapeDtypeStruct((batch * seq, vg_w), BF16),
        scratch_shapes=[pltpu.VMEM((h_, dk, dv), F32)],
        compiler_params=_cparams(("parallel", "arbitrary")),
        name="retention",
    )(proj, proj, proj, proj, cos, sin, intra, qdec, kdec, cdec, gn)


def _dil_attn_body(q_ref, kp_ref, kc_ref, vp_ref, vc_ref, o_ref, lse_ref):
    blk = DIL_BLOCK
    n_sub = q_ref.shape[0] // blk
    has_prev = pl.program_id(1) > 0
    qi = lax.broadcasted_iota(jnp.int32, (blk, 2 * blk), 0)
    ki = lax.broadcasted_iota(jnp.int32, (blk, 2 * blk), 1)
    band = jnp.logical_and(ki >= qi, ki <= qi + blk)
    band_first = jnp.logical_and(band, jnp.logical_or(ki >= blk, has_prev))
    lane = lax.broadcasted_iota(jnp.int32, (blk, LANES), 1)
    ones = jnp.ones((2 * blk, LANES), BF16)
    items = [(j, h) for j in range(n_sub) for h in range(DIL_HEADS)]
    sls = [slice(h * DIL_HEAD_DIM, (h + 1) * DIL_HEAD_DIM) for _, h in items]
    rows = [slice(j * blk, (j + 1) * blk) for j, _ in items]

    def keys(prev_ref, cur_ref, j, sl):
        if j == 0:
            return jnp.concatenate([prev_ref[:, sl], cur_ref[:blk, sl]], axis=0)
        return cur_ref[(j - 1) * blk:(j + 1) * blk, sl]

    s = [jnp.where(band_first if j == 0 else band,
                   _dot_nt(q_ref[rows[i], sls[i]], keys(kp_ref, kc_ref, j, sls[i])), NEG)
         for i, (j, _) in enumerate(items)]
    m = [jnp.max(t, axis=-1, keepdims=True) for t in s]
    p = [jnp.exp(s[i] - m[i]).astype(BF16) for i in range(len(items))]
    pv = [_dot(p[i], jnp.concatenate([keys(vp_ref, vc_ref, j, sls[i]), ones], axis=1))
          for i, (j, _) in enumerate(items)]
    lse_tiles = [jnp.zeros((blk, LANES), F32) for _ in range(n_sub)]
    for i, (j, h) in enumerate(items):
        l = pv[i][:, DIL_HEAD_DIM:]
        o_ref[rows[i], sls[i]] = (pv[i][:, :DIL_HEAD_DIM] / l).astype(o_ref.dtype)
        lse_tiles[j] = jnp.where(lane == h, m[i] + jnp.log(l), lse_tiles[j])
    for j in range(n_sub):
        lse_ref[j * blk:(j + 1) * blk, :] = lse_tiles[j]


def _dil_attn(qkv, batch, seq, dil):
    blk = DIL_BLOCK
    hd = DIL_HEADS * DIL_HEAD_DIM
    sub = seq // dil
    tq = min(DIL_TQ, sub)
    nb = sub // tq
    per = tq // blk
    cur = lambda c: (lambda z, i: (z * nb + i, c))
    prev = lambda c: (lambda z, i: (jnp.maximum((z * nb + i) * per - 1, 0), c))
    out_map = lambda z, i: (z * nb + i, 0)
    return pl.pallas_call(
        _dil_attn_body,
        grid=(batch * dil, nb),
        in_specs=[pl.BlockSpec((tq, hd), cur(0)), pl.BlockSpec((blk, hd), prev(1)),
                  pl.BlockSpec((tq, hd), cur(1)), pl.BlockSpec((blk, hd), prev(2)),
                  pl.BlockSpec((tq, hd), cur(2))],
        out_specs=[pl.BlockSpec((tq, hd), out_map), pl.BlockSpec((tq, LANES), out_map)],
        out_shape=[jax.ShapeDtypeStruct((batch * seq, hd), BF16),
                   jax.ShapeDtypeStruct((batch * seq, LANES), F32)],
        compiler_params=_cparams(("parallel", "arbitrary")),
        name=f"dil_attn_{dil}",
    )(qkv, qkv, qkv, qkv, qkv)


def _dil_out_body(o0_ref, o1_ref, o2_ref, l0_ref, l1_ref, l2_ref, e_ref, w_ref, res_ref,
                  g_ref, b_ref, o_ref, osc_ref, lsc_ref, mix_ref):
    tm = o_ref.shape[0]
    nh = DIL_HEADS
    for gi, (og, lg) in enumerate(((o0_ref, l0_ref), (o1_ref, l1_ref), (o2_ref, l2_ref))):
        dil = og.shape[1]
        n = tm // dil
        for r in range(dil):
            rows = pl.ds(r, n, stride=dil) if dil > 1 else slice(None)
            for h in range(nh):
                osc_ref[gi * nh + h, rows, :] = og[0, r, :, h * LANES:(h + 1) * LANES].astype(F32)
            lsc_ref[gi, rows, :] = lg[0, r]
    l0, l1, l2 = lsc_ref[0], lsc_ref[1], lsc_ref[2]
    m = jnp.maximum(jnp.maximum(l0, l1), l2)
    e0, e1, e2 = jnp.exp(l0 - m), jnp.exp(l1 - m), jnp.exp(l2 - m)
    den = e0 + e1 + e2
    e = e_ref[...]
    wfull = []
    for eg in (e0, e1, e2):
        hi, lo = _split2(eg / den)
        wfull.append(_dot(hi, e) + _dot(lo, e))
    for h in range(nh):
        sl = slice(h * LANES, (h + 1) * LANES)
        mixed = sum(wfull[gi][:, sl] * osc_ref[gi * nh + h] for gi in range(3))
        mix_ref[:, sl] = mixed.astype(BF16)
    acc = _dot(mix_ref[...], w_ref[...])
    o_ref[...] = _layer_norm(DN_ALPHA * res_ref[...] + acc, g_ref[...], b_ref[...])


def _dil_out(outs, lses, w, res, g, b, seq, *, tm=512):
    m, d = res.shape
    hd = DIL_HEADS * DIL_HEAD_DIM
    expand = (jnp.arange(LANES)[:, None] == (jnp.arange(hd) // DIL_HEAD_DIM)[None, :]).astype(BF16)
    row = lambda i: (i, 0)
    nt = seq // tm
    grp = lambda i: (i // nt, 0, i % nt, 0)
    dils = [dil for _, dil in DIL_PAIRS]
    batch = m // seq
    o4 = [o.reshape(batch, dil, seq // dil, hd) for o, dil in zip(outs, dils)]
    l4 = [l.reshape(batch, dil, seq // dil, LANES) for l, dil in zip(lses, dils)]
    return pl.pallas_call(
        _dil_out_body,
        grid=(m // tm,),
        in_specs=[pl.BlockSpec((1, dil, tm // dil, hd), grp) for dil in dils]
        + [pl.BlockSpec((1, dil, tm // dil, LANES), grp) for dil in dils]
        + [_resident((LANES, hd)), _resident((hd, d)), pl.BlockSpec((tm, d), row),
           _resident((1, d)), _resident((1, d))],
        out_specs=pl.BlockSpec((tm, d), row),
        out_shape=jax.ShapeDtypeStruct((m, d), F32),
        scratch_shapes=[pltpu.VMEM((len(dils) * DIL_HEADS, tm, LANES), F32),
                        pltpu.VMEM((len(dils), tm, LANES), F32), pltpu.VMEM((tm, hd), BF16)],
        compiler_params=_cparams(("parallel",)),
        name="dil_out",
    )(*o4, *l4, expand, w, res, g, b)


def _rope_tables(seq, rot, theta, scale, passthrough):
    half = rot // 2
    inv_freq = theta ** (-jnp.arange(half, dtype=F32) / half)
    ang = jnp.arange(seq, dtype=F32)[:, None] * inv_freq[None, :]
    cos, sin = jnp.cos(ang), jnp.sin(ang)
    fill = jnp.full((seq, ROPE_PARTNER - half), passthrough, F32)
    zero = jnp.zeros((seq, ROPE_PARTNER - half), F32)
    c = jnp.concatenate([cos, fill, cos, fill], axis=1)
    s = jnp.concatenate([-sin, zero, sin, zero], axis=1)
    return c * scale, s * scale


def _rope_lane_order(rot, width):
    half = rot // 2
    rest = list(range(rot, width))
    cut = ROPE_PARTNER - half
    return jnp.array(list(range(half)) + rest[:cut] + list(range(half, rot)) + rest[cut:])


def _dil_proj_body(x_ref, w_ref, c_ref, s_ref, o_ref, xb_ref, xs_ref, *, dil):
    tm = x_ref.shape[0]
    n = tm // dil
    j = pl.program_id(1)

    @pl.when(j == 0)
    def _():
        if dil == 1:
            xb_ref[...] = x_ref[...].astype(BF16)
        else:
            for c in range(D_MODEL // LANES):
                xs_ref[c] = x_ref[:, c * LANES:(c + 1) * LANES]
            for r in range(dil):
                for c in range(D_MODEL // LANES):
                    xb_ref[r * n:(r + 1) * n, c * LANES:(c + 1) * LANES] = (
                        xs_ref[c, pl.ds(r, n, stride=dil), :].astype(BF16))

    def project(rope):
        xb = xb_ref[...]
        for c0 in range(0, o_ref.shape[-1], MM_SUB):
            acc = _dot(xb, w_ref[:, c0:c0 + MM_SUB])
            for t in range(MM_SUB // LANES):
                a = acc[:, t * LANES:(t + 1) * LANES]
                if rope:
                    a = _rope_tile(a, c_ref[0], s_ref[0])
                lanes = slice(c0 + t * LANES, c0 + (t + 1) * LANES)
                o_ref[0, :, :, lanes] = a.astype(o_ref.dtype).reshape(dil, n, LANES)

    pl.when(j < 2)(functools.partial(project, True))
    pl.when(j >= 2)(functools.partial(project, False))


def _dil_proj(x, w, tabs, batch, seq, dil, *, tm=512):
    hd = DIL_HEADS * DIL_HEAD_DIM
    nt = seq // tm
    n = tm // dil
    tab_spec = pl.BlockSpec((1, tm, LANES), lambda i, j: (jnp.minimum(j, 1), i % nt, 0))
    return pl.pallas_call(
        functools.partial(_dil_proj_body, dil=dil),
        grid=(batch * nt, 3),
        in_specs=[pl.BlockSpec((tm, D_MODEL), lambda i, j: (i, 0)),
                  pl.BlockSpec((D_MODEL, hd), lambda i, j: (0, j)), tab_spec, tab_spec],
        out_specs=pl.BlockSpec((1, dil, n, hd), lambda i, j: (i // nt, 0, i % nt, j)),
        out_shape=jax.ShapeDtypeStruct((batch, dil, seq // dil, 3 * hd), BF16),
        scratch_shapes=[pltpu.VMEM((tm, D_MODEL), BF16),
                        pltpu.VMEM((D_MODEL // LANES, tm, LANES), F32)],
        compiler_params=_cparams(("parallel", "arbitrary")),
        name=f"dil_proj_{dil}",
    )(x, w, *tabs)


def _dilated(x, w_in, batch, seq, *, tm=512):
    hd = DIL_HEADS * DIL_HEAD_DIM
    cq, sq = _rope_tables(seq, DIL_ROT, ROPE_THETA, DIL_HEAD_DIM ** -0.5, 1.0)
    ck, sk = _rope_tables(seq, DIL_ROT, ROPE_THETA, 1.0, 1.0)
    order = _rope_lane_order(DIL_ROT, DIL_HEAD_DIM)
    outs, lses = [], []
    for gi, (_, dil) in enumerate(DIL_PAIRS):
        wg = w_in[:, gi * 3 * hd:(gi + 1) * 3 * hd].reshape(D_MODEL, 3, DIL_HEADS, DIL_HEAD_DIM)
        wg = jnp.concatenate([wg[:, :2][..., order], wg[:, 2:]], axis=1).reshape(D_MODEL, 3 * hd)

        def by_residue(t):
            t = t.reshape(seq // tm, tm // dil, dil, LANES)
            return jnp.swapaxes(t, 1, 2).reshape(seq, LANES)

        tabs = [jnp.stack([by_residue(a), by_residue(b)]) for a, b in ((cq, ck), (sq, sk))]
        qkv = _dil_proj(x, wg, tabs, batch, seq, dil, tm=tm)
        o, lse = _dil_attn(qkv.reshape(batch * seq, 3 * hd), batch, seq, dil)
        outs.append(o)
        lses.append(lse)
    return outs, lses


def _mla_down_body(x_ref, w_ref, nq_ref, nkv_ref, c_ref, s_ref, cq_ref, ckv_ref, kpe_ref):
    acc = _dot(x_ref[...].astype(BF16), w_ref[...])
    cq = acc[:, :MLA_Q_RANK]
    ckv = acc[:, MLA_Q_RANK:MLA_Q_RANK + MLA_KV_RANK]
    kpe = acc[:, MLA_Q_RANK + MLA_KV_RANK:]
    cq = cq * lax.rsqrt(jnp.mean(cq * cq, axis=-1, keepdims=True) + RMS_EPS) * nq_ref[...]
    ckv = ckv * lax.rsqrt(jnp.mean(ckv * ckv, axis=-1, keepdims=True) + RMS_EPS) * nkv_ref[...]
    kpe = _rope_tile(kpe, c_ref[...], s_ref[...])
    cq_ref[...] = cq.astype(cq_ref.dtype)
    ckv_ref[...] = ckv.astype(ckv_ref.dtype)
    kpe_ref[...] = kpe.astype(kpe_ref.dtype)


def _mla_down(x, w, nq, nkv, tabs, seq, *, tm=512):
    m, d = x.shape
    n = w.shape[1]
    row = lambda i: (i, 0)
    ns = seq // tm
    tab = lambda i: (i % ns, 0)
    return pl.pallas_call(
        _mla_down_body,
        grid=(m // tm,),
        in_specs=[pl.BlockSpec((tm, d), row), _resident((d, n)), _resident((1, MLA_Q_RANK)),
                  _resident((1, MLA_KV_RANK))] + [pl.BlockSpec((tm, LANES), tab)] * 2,
        out_specs=[pl.BlockSpec((tm, MLA_Q_RANK), row), pl.BlockSpec((tm, MLA_KV_RANK), row),
                   pl.BlockSpec((tm, LANES), row)],
        out_shape=[jax.ShapeDtypeStruct((m, MLA_Q_RANK), BF16),
                   jax.ShapeDtypeStruct((m, MLA_KV_RANK), BF16),
                   jax.ShapeDtypeStruct((m, LANES), BF16)],
        compiler_params=_cparams(("parallel",)),
        name="mla_down",
    )(x, w, nq, nkv, *tabs)


def _mla_flash_body(q_ref, kn_ref, kpe_ref, v_ref, o_ref, kcat_ref, vaug_ref):
    tq, tk, sub = MLA_TQ, MLA_TK, MLA_TQ // MLA_ROW_SPLIT
    iq = pl.program_id(2)

    @pl.when(iq == 0)
    def _():
        kcat_ref[:, :MLA_NOPE] = kn_ref[...]
        kcat_ref[:, MLA_NOPE:] = kpe_ref[...]
        vaug_ref[:, :MLA_V] = v_ref[...]
        vaug_ref[:, MLA_V:] = jnp.ones((v_ref.shape[0], LANES), BF16)

    parts = range(MLA_ROW_SPLIT)
    col_minus_row = (lax.broadcasted_iota(jnp.int32, (sub, tk), 1)
                     - lax.broadcasted_iota(jnp.int32, (sub, tk), 0))

    def scores(chunk, diagonal):
        out = []
        for part in parts:
            width = (part + 1) * sub if diagonal else tk
            s = _dot_nt(q_ref[part * sub:(part + 1) * sub, :], kcat_ref[chunk * tk:chunk * tk + width, :])
            if diagonal:
                s = jnp.where(col_minus_row[:, :width] <= part * sub, s, NEG)
            out.append(s)
        return out

    def update(chunk, s_all, carry):
        out = []
        for s, (m, acc) in zip(s_all, carry):
            vb = vaug_ref[chunk * tk:chunk * tk + s.shape[1], :]
            m_new = jnp.maximum(m, jnp.max(s, axis=-1, keepdims=True))
            alpha = jnp.exp2(m - m_new)
            p = jnp.exp2((s - m_new).astype(BF16))
            out.append((m_new, alpha * acc + _dot(p, vb)))
        return out

    def attend(n_chunks):
        carry = [(jnp.full((sub, 1), NEG, F32), jnp.zeros((sub, MLA_V + LANES), F32)) for _ in parts]
        s = scores(0, n_chunks == 1)
        for c in range(n_chunks):
            s_next = scores(c + 1, c + 2 == n_chunks) if c + 1 < n_chunks else None
            carry = update(c, s, carry)
            s = s_next
        for part, (_, acc) in enumerate(carry):
            o_ref[part * sub:(part + 1) * sub, :] = (acc[:, :MLA_V] / acc[:, MLA_V:]).astype(o_ref.dtype)

    n_chunks = (iq * tq) // tk + 1
    for n in range(1, kcat_ref.shape[0] // tk + 1):
        pl.when(n_chunks == n)(functools.partial(attend, n))


def _mla_flash(q, kv, kpe, batch, seq):
    blk = MLA_TQ
    nq = seq // blk
    h_ = MLA_HEADS
    qw = MLA_NOPE + LANES
    return pl.pallas_call(
        _mla_flash_body,
        grid=(batch, h_, nq),
        in_specs=[pl.BlockSpec((blk, qw), lambda b, h, i: (b * nq + i, h)),
                  pl.BlockSpec((seq, MLA_NOPE), lambda b, h, i: (b, h)),
                  pl.BlockSpec((seq, LANES), lambda b, h, i: (b, 0)),
                  pl.BlockSpec((seq, MLA_V), lambda b, h, i: (b, h_ + h))],
        out_specs=pl.BlockSpec((blk, MLA_V), lambda b, h, i: (b * nq + i, h)),
        out_shape=jax.ShapeDtypeStruct((batch * seq, h_ * MLA_V), BF16),
        scratch_shapes=[pltpu.VMEM((seq, qw), BF16), pltpu.VMEM((seq, MLA_V + LANES), BF16)],
        compiler_params=_cparams(("parallel", "parallel", "arbitrary")),
        name="mla_flash",
    )(q, kv, kpe, kv)


def _mla(x, w_down, norm_q, norm_kv, w_uq, w_ukv, batch, seq):
    h_ = MLA_HEADS
    half = MLA_ROPE // 2

    def pe_tile(w):
        z = jnp.zeros(w.shape[:-1] + (ROPE_PARTNER - half,), w.dtype)
        return jnp.concatenate([w[..., :half], z, w[..., half:], z], axis=-1)

    n_lat = MLA_Q_RANK + MLA_KV_RANK
    wd = jnp.concatenate([w_down[:, :n_lat], pe_tile(w_down[:, n_lat:])], axis=1).astype(BF16)
    wq = w_uq.reshape(MLA_Q_RANK, h_, MLA_NOPE + MLA_ROPE)
    wq = jnp.concatenate([wq[..., :MLA_NOPE], pe_tile(wq[..., MLA_NOPE:])], axis=-1)
    wq = wq.reshape(MLA_Q_RANK, -1).astype(BF16)
    wkv = w_ukv.reshape(MLA_KV_RANK, h_, MLA_NOPE + MLA_V)
    wkv = jnp.concatenate([wkv[:, :, :MLA_NOPE].reshape(MLA_KV_RANK, -1),
                           wkv[:, :, MLA_NOPE:].reshape(MLA_KV_RANK, -1)], axis=1).astype(BF16)
    scale = (MLA_NOPE + MLA_ROPE) ** -0.5 * math.log2(math.e)
    tk = _rope_tables(seq, MLA_ROPE, MLA_THETA, 1.0, 0.0)
    tq = [t[None] for t in _rope_tables(seq, MLA_ROPE, MLA_THETA, scale, 0.0)]
    cq, ckv, kpe = _mla_down(x, wd, norm_q[None, :], norm_kv[None, :], tk, seq)
    tm = 512
    ns = seq // tm
    q = _mm(cq, wq, tm=tm, tn=1024, m_rows=batch * seq, tabs=tq,
            tab_map=lambda i, j: (0, i % ns, 0), pattern=("scale", "rope") * 4,
            scale=scale, name="mla_q")
    kv = _mm(ckv, wkv, tm=tm, tn=1024, m_rows=batch * seq, name="mla_kv")
    return _mla_flash(q, kv, kpe, batch, seq)


def _head_sum(z, ones_bd):
    hi, lo = _split2(z)
    return _dot(hi, ones_bd) + _dot(lo, ones_bd)


def _rwkv_prep_body(x_ref, xp_ref, mu_ref, wr_ref, wk_ref, wv_ref, la0_ref, lb0_ref, la1_ref,
                    lb1_ref, ga_ref, gb_ref, vec_ref, bd_ref, r_ref, lw_ref, k_ref, v_ref,
                    kk_ref, b_ref, g_ref, *, tiles_per_seq):
    x = x_ref[...]
    tm = x.shape[0]
    first = pl.program_id(0) % tiles_per_seq == 0
    prev_row = jnp.where(first, 0.0, xp_ref[7:8, :])
    rows = lax.broadcasted_iota(jnp.int32, x.shape, 0)
    shifted = jnp.where(rows == 0, prev_row, pltpu.roll(x, 1, 0))
    xx = shifted - x
    mix = lambda i: (x + xx * mu_ref[i:i + 1, :]).astype(BF16)
    r = _dot(mix(0), wr_ref[...])
    k_raw = _dot(mix(2), wk_ref[...])
    v = _dot(mix(3), wv_ref[...])
    w0, a0, k_k, k_a = (vec_ref[i:i + 1, :] for i in range(4))
    wl = w0 + _dot(jnp.tanh(_dot(mix(1), la0_ref[...])).astype(BF16), lb0_ref[...])
    z = -wl
    softplus = jnp.maximum(z, 0.0) + jnp.log(1.0 + jnp.exp(-jnp.abs(z)))
    lw_ref[...] = -jnp.exp(-softplus - 0.5)
    al = a0 + _dot(_dot(mix(4), la1_ref[...]).astype(BF16), lb1_ref[...])
    a = 1.0 / (1.0 + jnp.exp(-al))
    gl = _dot(mix(5), ga_ref[...])
    g = _dot((1.0 / (1.0 + jnp.exp(-gl))).astype(BF16), gb_ref[...])
    kk = k_raw * k_k
    bd = bd_ref[...]
    for s in range(D_MODEL // LANES):
        sl = slice(s * LANES, (s + 1) * LANES)
        t = kk[:, sl]
        nrm = jnp.maximum(jnp.sqrt(_head_sum(t * t, bd)), 1e-12)
        t = t / nrm
        kk_ref[:, sl] = t.astype(kk_ref.dtype)
        b_ref[:, sl] = (t * a[:, sl]).astype(b_ref.dtype)
    r_ref[...] = r.astype(r_ref.dtype)
    k_ref[...] = (k_raw * (1.0 + (a - 1.0) * k_a)).astype(k_ref.dtype)
    v_ref[...] = v.astype(v_ref.dtype)
    g_ref[...] = g.astype(g_ref.dtype)


def _head_ones():
    idx = jnp.arange(LANES) // RWKV_HEAD
    return (idx[:, None] == idx[None, :]).astype(BF16)


def _rwkv_prep(x, mu, w_rkv, vec, lora_a, lora_b, gate_a, gate_b, seq, *, tm=256):
    m, d = x.shape
    gpad = RWKV_GATE_PAD - gate_a.shape[1]
    ga = jnp.pad(gate_a, ((0, 0), (0, gpad))).astype(BF16)
    gb = jnp.pad(gate_b, ((0, gpad), (0, 0))).astype(BF16)
    wts = [w_rkv[0].astype(BF16), w_rkv[1].astype(BF16), w_rkv[2].astype(BF16),
           lora_a[0].astype(BF16), lora_b[0].astype(BF16), lora_a[1].astype(BF16),
           lora_b[1].astype(BF16), ga, gb]
    vec8 = jnp.pad(vec, ((0, 3), (0, 0)))
    mu8 = jnp.pad(mu, ((0, 2), (0, 0)))
    row = lambda i: (i, 0)
    sub = tm // 8
    out = jax.ShapeDtypeStruct((m, d), BF16)
    return pl.pallas_call(
        functools.partial(_rwkv_prep_body, tiles_per_seq=seq // tm),
        grid=(m // tm,),
        in_specs=[pl.BlockSpec((tm, d), row),
                  pl.BlockSpec((8, d), lambda i: (jnp.maximum(i * sub - 1, 0), 0)),
                  _resident(mu8.shape)] + [_resident(w.shape) for w in wts]
        + [_resident(vec8.shape), _resident((LANES, LANES))],
        out_specs=[pl.BlockSpec((tm, d), row)] * 7,
        out_shape=[out, jax.ShapeDtypeStruct((m, d), F32), out, out, out, out, out],
        compiler_params=_cparams(("parallel",)),
        name="rwkv_prep",
    )(x, x, mu8, *wts, vec8, _head_ones())


def _rwkv_wkv_body(r_ref, lw_ref, k_ref, v_ref, kk_ref, b_ref, tri_ref, y_ref, state_ref):
    c = RWKV_CHUNK
    two = 2 * c

    @pl.when(pl.program_id(1) == 0)
    def _():
        state_ref[...] = jnp.zeros(state_ref.shape, F32)

    tri = tri_ref[...]

    def decayed(bi):
        lw = lw_ref[bi]
        h1 = lw.astype(BF16)
        r1 = lw - h1.astype(F32)
        h2 = r1.astype(BF16)
        h3 = (r1 - h2.astype(F32)).astype(BF16)
        cum = _dot(tri, h1) + _dot(tri, h2) + _dot(tri, h3)
        gam = jnp.exp(cum)
        gam_inv = jnp.exp(-cum)
        return (r_ref[bi].astype(F32) * gam, kk_ref[bi].astype(F32) * jnp.exp(cum - lw),
                b_ref[bi].astype(F32) * gam_inv, k_ref[bi].astype(F32) * gam_inv, gam[c - 1:c, :])

    lane_lo = lax.broadcasted_iota(jnp.int32, (c, LANES), 1) < RWKV_HEAD
    row2 = lax.broadcasted_iota(jnp.int32, (two, two), 0)
    col2 = lax.broadcasted_iota(jnp.int32, (two, two), 1)
    same = (row2 // c) == (col2 // c)
    strict = jnp.logical_and(same, row2 > col2)
    incl = jnp.logical_and(same, row2 >= col2)
    eye = (row2 == col2).astype(F32)

    def stack_masked(t):
        return jnp.concatenate([jnp.where(lane_lo, t, 0.0), jnp.where(lane_lo, 0.0, t)], axis=0)

    nb = r_ref.shape[0]
    items = [(bi, p) for bi in range(nb) for p in range(D_MODEL // LANES)]
    idx = range(len(items))
    sls = [slice(p * LANES, (p + 1) * LANES) for _, p in items]
    dec = [decayed(bi) for bi in range(nb)]
    xs, bds, kds, vss, gend = [], [], [], [], []
    for (bi, _), sl in zip(items, sls):
        rt, kkt, bt, kt, gam_end = dec[bi]
        xs.append(jnp.concatenate([stack_masked(kkt[:, sl]), stack_masked(rt[:, sl])], axis=0).astype(BF16))
        bds.append(jnp.concatenate([bt[:, sl], bt[:, sl]], axis=0).astype(BF16))
        kds.append(jnp.concatenate([kt[:, sl], kt[:, sl]], axis=0).astype(BF16))
        v2 = v_ref[bi, :, sl].astype(F32)
        vss.append(jnp.where(same, jnp.concatenate([v2, v2], axis=0), 0.0).astype(BF16))
        gend.append(gam_end[:, sl])
    s2s = [state_ref[bi, p] for bi, p in items]
    a_all = [_dot_nt(xs[i], jnp.concatenate([bds[i], kds[i]], axis=0)) for i in idx]
    xs_state = [_dot_nt(xs[i], s2s[i].astype(BF16)) for i in idx]
    nmat = [jnp.where(strict, -a[:two, :two], 0.0) for a in a_all]
    lk = [jnp.where(strict, a[:two, two:], 0.0).astype(BF16) for a in a_all]
    arb = [jnp.where(incl, a[two:, :two], 0.0).astype(BF16) for a in a_all]
    ark = [jnp.where(incl, a[two:, two:], 0.0).astype(BF16) for a in a_all]
    rhs = [xs_state[i][:two] + _dot(lk[i], vss[i]) for i in idx]
    pw = [n_.astype(BF16) for n_ in nmat]
    inv = [eye + n_ for n_ in nmat]
    pw = [_dot(t, t).astype(BF16) for t in pw]
    for _ in range(int(math.log2(c)) - 2):
        both = [_dot(pw[i], jnp.concatenate([pw[i], inv[i].astype(BF16)], axis=1)) for i in idx]
        inv = [inv[i] + both[i][:, two:] for i in idx]
        pw = [t[:, :two].astype(BF16) for t in both]
    inv = [inv[i] + _dot(pw[i], inv[i].astype(BF16)) for i in idx]
    ub = [(-_dot(inv[i].astype(BF16), rhs[i].astype(BF16))).astype(BF16) for i in idx]
    for i, (bi, _) in enumerate(items):
        ys = xs_state[i][two:] + _dot(arb[i], ub[i]) + _dot(ark[i], vss[i])
        y_ref[bi, :, sls[i]] = ys[:c] + ys[c:]
    for i, (bi, p) in enumerate(items):
        ds = _dot_tn(ub[i], bds[i]) + _dot_tn(vss[i], kds[i])
        state_ref[bi, p] = jnp.where(same, (s2s[i] + ds) * gend[i], 0.0)


def _rwkv_wkv(r, lw, k, v, kk, b, batch, seq):
    c = RWKV_CHUNK
    n = seq // c
    d = D_MODEL
    tri = (jnp.arange(c)[:, None] >= jnp.arange(c)[None, :]).astype(BF16)
    nb = RWKV_BATCH_ROWS if batch % RWKV_BATCH_ROWS == 0 else 1
    blk = pl.BlockSpec((nb, c, d), lambda bi, i: (bi, i, 0))
    as3d = lambda t: t.reshape(batch, seq, d)
    y = pl.pallas_call(
        _rwkv_wkv_body,
        grid=(batch // nb, n),
        in_specs=[blk] * 6 + [_resident((c, c))],
        out_specs=blk,
        out_shape=jax.ShapeDtypeStruct((batch, seq, d), F32),
        scratch_shapes=[pltpu.VMEM((nb, d // LANES, LANES, LANES), F32)],
        compiler_params=_cparams(("parallel", "arbitrary")),
        name="rwkv_wkv",
    )(as3d(r), as3d(lw), as3d(k), as3d(v), as3d(kk), as3d(b), tri)
    return y.reshape(batch * seq, d)


def _rwkv_out_body(y_ref, r_ref, k_ref, v_ref, g_ref, vec_ref, bd_ref, w_ref, res_ref, lg_ref,
                   lb_ref, o_ref, a_ref):
    bd = bd_ref[...]
    inv_n = 1.0 / RWKV_HEAD
    for s in range(D_MODEL // LANES):
        sl = slice(s * LANES, (s + 1) * LANES)
        y = y_ref[:, sl]
        mu = _head_sum(y, bd) * inv_n
        dlt = y - mu
        var = _head_sum(dlt * dlt, bd) * inv_n
        yn = dlt * lax.rsqrt(var + RWKV_GN_EPS) * vec_ref[0:1, sl] + vec_ref[1:2, sl]
        rk = r_ref[:, sl].astype(F32) * k_ref[:, sl].astype(F32) * vec_ref[2:3, sl]
        bonus = _head_sum(rk, bd) * v_ref[:, sl].astype(F32)
        a_ref[:, sl] = ((yn + bonus) * g_ref[:, sl].astype(F32)).astype(BF16)
    acc = _dot(a_ref[...], w_ref[...])
    o_ref[...] = _layer_norm(DN_ALPHA * res_ref[...] + acc, lg_ref[...], lb_ref[...])


def _rwkv_out(y, r, k, v, g, vec, w, res, lg, lb, *, tm=512):
    m, d = res.shape
    row = lambda i: (i, 0)
    act = pl.BlockSpec((tm, d), row)
    return pl.pallas_call(
        _rwkv_out_body,
        grid=(m // tm,),
        in_specs=[act] * 5 + [_resident(vec.shape), _resident((LANES, LANES)), _resident((d, d)),
                              act, _resident((1, d)), _resident((1, d))],
        out_specs=act,
        out_shape=jax.ShapeDtypeStruct((m, d), F32),
        scratch_shapes=[pltpu.VMEM((tm, d), BF16)],
        compiler_params=_cparams(("parallel",)),
        name="rwkv_out",
    )(y, r, k, v, g, vec, _head_ones(), w, res, lg, lb)


def kernel(x, ret_w_in, ret_gn, ret_w_out, dil_w_in, dil_w_out, mla_w_down, mla_norm_q,
           mla_norm_kv, mla_w_uq, mla_w_ukv, mla_w_out, rwkv_mu, rwkv_w_rkv, rwkv_w_out,
           rwkv_vec, rwkv_lora_a, rwkv_lora_b, rwkv_gate_a, rwkv_gate_b, rwkv_ln_x,
           mlp_w1, mlp_w2, ln_g, ln_b):
    batch, seq, d = x.shape
    xf = x.reshape(batch * seq, d)
    n_mixers = 4
    for i in range(DEPTH):
        mixer, j = i % n_mixers, i // n_mixers
        lg, lb = ln_g[i, 0][None, :], ln_b[i, 0][None, :]
        if mixer == 0:
            proj = _mm(xf, ret_w_in[j].astype(BF16), tm=1024, tn=1024, m_rows=batch * seq,
                       name="ret_proj")
            gated = _retention(proj, ret_gn[j], batch, seq)
            xf = _mm_res_ln(gated, ret_w_out[j].astype(BF16), xf, lg, lb, name="ret_out")
        elif mixer == 1:
            outs, lses = _dilated(xf, dil_w_in[j].astype(BF16), batch, seq)
            xf = _dil_out(outs, lses, dil_w_out[j].astype(BF16), xf, lg, lb, seq)
        elif mixer == 2:
            o = _mla(xf, mla_w_down[j], mla_norm_q[j], mla_norm_kv[j], mla_w_uq[j], mla_w_ukv[j],
                     batch, seq)
            xf = _mm_res_ln(o, mla_w_out[j].astype(BF16), xf, lg, lb, name="mla_out")
        else:
            r, lw, k, v, kk, b, g = _rwkv_prep(xf, rwkv_mu[j], rwkv_w_rkv[j], rwkv_vec[j],
                                               rwkv_lora_a[j], rwkv_lora_b[j], rwkv_gate_a[j],
                                               rwkv_gate_b[j], seq)
            y = _rwkv_wkv(r, lw, k, v, kk, b, batch, seq)
            vec = jnp.concatenate([rwkv_ln_x[j], rwkv_vec[j][4:5],
                                   jnp.zeros((5, d), F32)], axis=0)
            xf = _rwkv_out(y, r, k, v, g, vec, rwkv_w_out[j].astype(BF16), xf, lg, lb)
        xf = _mlp(xf, mlp_w1[i].astype(BF16), mlp_w2[i].astype(BF16),
                  ln_g[i, 1][None, :], ln_b[i, 1][None, :])
    return xf.reshape(batch, seq, d)
```

```python
import functools
import math

import jax
import jax.numpy as jnp
from jax import lax
from jax.experimental import pallas as pl
from jax.experimental.pallas import tpu as pltpu

F32 = jnp.float32
BF16 = jnp.bfloat16

D_MODEL = 1024
DEPTH = 4
D_FF = 4 * D_MODEL
LN_EPS = 1e-5
RMS_EPS = 1e-6
GN_EPS = 1e-5
DN_ALPHA = (2.0 * DEPTH) ** 0.25
NEG = -1e30
LANES = 128
ROPE_PARTNER = 64
MM_SUB = 256

RET_HEADS = 4
RET_QK_DIM = 256
RET_V_DIM = 512
RET_CHUNK = 128
RET_THETA = 10000.0

DIL_PAIRS = ((128, 1), (512, 4), (2048, 16))
DIL_HEADS = 8
DIL_HEAD_DIM = 128
DIL_ROT = 32
DIL_BLOCK = 128
DIL_TQ = 512
ROPE_THETA = 500000.0

MLA_HEADS = 16
MLA_NOPE = 128
MLA_ROPE = 64
MLA_V = 128
MLA_Q_RANK = 256
MLA_KV_RANK = 128
MLA_THETA = 10000.0
MLA_TQ = 1024
MLA_TK = 1024
MLA_ROW_SPLIT = 4
assert MLA_TQ == MLA_TK

RWKV_HEAD = 64
RWKV_HEADS = D_MODEL // RWKV_HEAD
RWKV_GN_EPS = 64e-5
RWKV_CHUNK = 64
RWKV_GATE_PAD = 256
RWKV_BATCH_ROWS = 2

VMEM_LIMIT = 56 * 1024 * 1024


def _cparams(sem):
    return pltpu.CompilerParams(dimension_semantics=sem, vmem_limit_bytes=VMEM_LIMIT)


def _resident(shape):
    nd = len(shape)
    return pl.BlockSpec(shape, lambda *_: (0,) * nd, pipeline_mode=pl.Buffered(1))


def _layer_norm(z, g, b):
    mu = jnp.mean(z, axis=-1, keepdims=True)
    d = z - mu
    var = jnp.mean(d * d, axis=-1, keepdims=True)
    return d * lax.rsqrt(var + LN_EPS) * g + b


def _dot(a, b):
    return jnp.dot(a, b, preferred_element_type=F32)


def _dot_nt(a, b):
    return lax.dot_general(a, b, (((1,), (1,)), ((), ())), preferred_element_type=F32)


def _dot_tn(a, b):
    return lax.dot_general(a, b, (((0,), (0,)), ((), ())), preferred_element_type=F32)


def _split2(z):
    hi = z.astype(BF16)
    lo = (z - hi.astype(F32)).astype(BF16)
    return hi, lo


def _rope_tile(a, c, s):
    return a * c + pltpu.roll(a, ROPE_PARTNER, 1) * s


def _project_columns(xb, w_ref, store, modes, tabs, scale):
    n = w_ref.shape[1]
    for c0 in range(0, n, MM_SUB):
        acc = _dot(xb, w_ref[:, c0:c0 + MM_SUB])
        for t in range(MM_SUB // LANES):
            a = acc[:, t * LANES:(t + 1) * LANES]
            mode = modes[c0 // LANES + t] if modes is not None else None
            if mode == "scale":
                a = a * scale
            elif mode is not None:
                c_ref, s_ref = tabs
                a = _rope_tile(a, c_ref[mode[1]], s_ref[mode[1]])
            store(c0 + t * LANES, a)


def _mm_body(x_ref, w_ref, *rest, modes, scale):
    tabs, o_ref = rest[:-1], rest[-1]

    def store(c0, a):
        o_ref[:, c0:c0 + LANES] = a.astype(o_ref.dtype)

    _project_columns(x_ref[...].astype(BF16), w_ref, store, modes, tabs, scale)


def _mm(x, w, *, tm, out_dtype=BF16, tabs=None, tab_map=None, modes=None, scale=1.0, name="mm"):
    m = x.shape[0]
    k, n = w.shape
    row = lambda i: (i, 0)
    in_specs = [pl.BlockSpec((tm, k), row), _resident((k, n))]
    args = [x, w]
    if tabs is not None:
        for t in tabs:
            in_specs.append(pl.BlockSpec((t.shape[0], tm, LANES), tab_map))
            args.append(t)
    return pl.pallas_call(
        functools.partial(_mm_body, modes=modes, scale=scale),
        grid=(m // tm,),
        in_specs=in_specs,
        out_specs=pl.BlockSpec((tm, n), row),
        out_shape=jax.ShapeDtypeStruct((m, n), out_dtype),
        compiler_params=_cparams(("parallel",)),
        name=name,
    )(*args)


def _mm_res_ln_body(a_ref, w_ref, res_ref, g_ref, b_ref, o_ref):
    acc = _dot(a_ref[...], w_ref[...])
    o_ref[...] = _layer_norm(DN_ALPHA * res_ref[...] + acc, g_ref[...], b_ref[...])


def _mm_res_ln(a, w, res, g, b, *, tm=512, name="mm_res_ln"):
    m, k = a.shape
    d = w.shape[1]
    row = lambda i: (i, 0)
    return pl.pallas_call(
        _mm_res_ln_body,
        grid=(m // tm,),
        in_specs=[pl.BlockSpec((tm, k), row), _resident((k, d)), pl.BlockSpec((tm, d), row),
                  _resident((1, d)), _resident((1, d))],
        out_specs=pl.BlockSpec((tm, d), row),
        out_shape=jax.ShapeDtypeStruct((m, d), F32),
        compiler_params=_cparams(("parallel",)),
        name=name,
    )(a, w, res, g, b)


def _mlp_body(x_ref, w1_ref, w2_ref, g_ref, b_ref, o_ref, *, fchunk):
    x = x_ref[...]
    xb = x.astype(BF16)
    acc = jnp.zeros(x.shape, F32)
    for c in range(D_FF // fchunk):
        h = _dot(xb, w1_ref[:, c * fchunk:(c + 1) * fchunk])
        h = jnp.maximum(h, 0.0)
        h = (h * h).astype(BF16)
        acc = acc + _dot(h, w2_ref[c * fchunk:(c + 1) * fchunk, :])
    o_ref[...] = _layer_norm(DN_ALPHA * x + acc, g_ref[...], b_ref[...])


def _mlp(x, w1, w2, g, b, *, tm=512, fchunk=1024):
    m, d = x.shape
    row = lambda i: (i, 0)
    return pl.pallas_call(
        functools.partial(_mlp_body, fchunk=fchunk),
        grid=(m // tm,),
        in_specs=[pl.BlockSpec((tm, d), row), _resident((d, D_FF)), _resident((D_FF, d)),
                  _resident((1, d)), _resident((1, d))],
        out_specs=pl.BlockSpec((tm, d), row),
        out_shape=jax.ShapeDtypeStruct((m, d), F32),
        compiler_params=_cparams(("parallel",)),
        name="mlp",
    )(x, w1, w2, g, b)


def _ret_body(q_ref, k_ref, v_ref, g_ref, cos_ref, sin_ref, intra_ref, qdec_ref, kdec_ref,
              cdec_ref, gn_ref, o_ref, state_ref):
    dk, dv, half = RET_QK_DIM, RET_V_DIM, RET_QK_DIM // 2

    @pl.when(pl.program_id(1) == 0)
    def _():
        state_ref[...] = jnp.zeros(state_ref.shape, F32)

    cos = cos_ref[...]
    sin = sin_ref[...]

    def rope(t):
        t1 = t[:, :half].astype(F32)
        t2 = t[:, half:].astype(F32)
        return jnp.concatenate([t1 * cos - t2 * sin, t2 * cos + t1 * sin], axis=-1)

    heads = range(RET_HEADS)
    q = [rope(q_ref[:, h * dk:(h + 1) * dk]) for h in heads]
    k = [rope(k_ref[:, h * dk:(h + 1) * dk]) * (dk ** -0.5) for h in heads]
    v = [v_ref[:, h * dv:(h + 1) * dv] for h in heads]
    qb = [t.astype(BF16) for t in q]
    state = [state_ref[h] for h in heads]
    scores = [(_dot_nt(qb[h], k[h].astype(BF16)) * intra_ref[h]).astype(BF16) for h in heads]
    cross = [_dot(qb[h], state[h].astype(BF16)) * qdec_ref[h] for h in heads]
    o = [_dot(scores[h], v[h]) + cross[h] for h in heads]
    for h in heads:
        state_ref[h] = (state[h] * cdec_ref[h, 0:1, :]
                        + _dot_tn((k[h] * kdec_ref[h]).astype(BF16), v[h]))
    for h in heads:
        sl = slice(h * dv, (h + 1) * dv)
        mu = jnp.mean(o[h], axis=-1, keepdims=True)
        d = o[h] - mu
        var = jnp.mean(d * d, axis=-1, keepdims=True)
        on = d * lax.rsqrt(var + GN_EPS) * gn_ref[0:1, sl] + gn_ref[1:2, sl]
        gate = g_ref[:, sl].astype(F32)
        gate = gate * (1.0 / (1.0 + jnp.exp(-gate)))
        o_ref[:, sl] = (gate * on).astype(o_ref.dtype)


def _retention(proj, gn, batch, seq):
    h_, dk, dv, c = RET_HEADS, RET_QK_DIM, RET_V_DIM, RET_CHUNK
    n = seq // c
    half = dk // 2
    pos = jnp.arange(seq, dtype=F32)
    inv_freq = RET_THETA ** (-jnp.arange(half, dtype=F32) / half)
    ang = pos[:, None] * inv_freq[None, :]
    cos, sin = jnp.cos(ang), jnp.sin(ang)
    log_gamma = jnp.log(1.0 - 2.0 ** (-5.0 - jnp.arange(h_, dtype=F32)))
    idx = jnp.arange(c, dtype=F32)
    diff = idx[:, None] - idx[None, :]
    intra = jnp.where(diff >= 0, jnp.exp(log_gamma[:, None, None] * jnp.maximum(diff, 0.0)), 0.0)
    qdec = jnp.broadcast_to(jnp.exp(log_gamma[:, None] * (idx + 1.0))[:, :, None], (h_, c, dv))
    kdec = jnp.broadcast_to(jnp.exp(log_gamma[:, None] * (c - 1.0 - idx))[:, :, None], (h_, c, dk))
    cdec = jnp.broadcast_to(jnp.exp(log_gamma * c)[:, None, None], (h_, 8, dv))
    qk_w, vg_w = h_ * dk, h_ * dv
    return pl.pallas_call(
        _ret_body,
        grid=(batch, n),
        in_specs=[
            pl.BlockSpec((c, qk_w), lambda b, i: (b * n + i, 0)),
            pl.BlockSpec((c, qk_w), lambda b, i: (b * n + i, 1)),
            pl.BlockSpec((c, vg_w), lambda b, i: (b * n + i, 2 * qk_w // vg_w)),
            pl.BlockSpec((c, vg_w), lambda b, i: (b * n + i, 2 * qk_w // vg_w + 1)),
            pl.BlockSpec((c, half), lambda b, i: (i, 0)),
            pl.BlockSpec((c, half), lambda b, i: (i, 0)),
            _resident((h_, c, c)), _resident((h_, c, dv)), _resident((h_, c, dk)),
            _resident((h_, 8, dv)), _resident((2, vg_w)),
        ],
        out_specs=pl.BlockSpec((c, vg_w), lambda b, i: (b * n + i, 0)),
        out_shape=jax.ShapeDtypeStruct((batch * seq, vg_w), BF16),
        scratch_shapes=[pltpu.VMEM((h_, dk, dv), F32)],
        compiler_params=_cparams(("parallel", "arbitrary")),
        name="retention",
    )(proj, proj, proj, proj, cos, sin, intra, qdec, kdec, cdec, gn)


def _dil_attn_body(q_ref, kp_ref, kc_ref, vp_ref, vc_ref, o_ref, lse_ref):
    blk = DIL_BLOCK
    n_sub = q_ref.shape[0] // blk
    has_prev = pl.program_id(1) > 0
    qi = lax.broadcasted_iota(jnp.int32, (blk, 2 * blk), 0)
    ki = lax.broadcasted_iota(jnp.int32, (blk, 2 * blk), 1)
    band = jnp.logical_and(ki >= qi, ki <= qi + blk)
    band_first = jnp.logical_and(band, jnp.logical_or(ki >= blk, has_prev))
    lane = lax.broadcasted_iota(jnp.int32, (blk, LANES), 1)
    ones = jnp.ones((2 * blk, LANES), BF16)
    items = [(j, h) for j in range(n_sub) for h in range(DIL_HEADS)]
    sls = [slice(h * DIL_HEAD_DIM, (h + 1) * DIL_HEAD_DIM) for _, h in items]
    rows = [slice(j * blk, (j + 1) * blk) for j, _ in items]

    def keys(prev_ref, cur_ref, j, sl):
        if j == 0:
            return jnp.concatenate([prev_ref[:, sl], cur_ref[:blk, sl]], axis=0)
        return cur_ref[(j - 1) * blk:(j + 1) * blk, sl]

    s = [jnp.where(band_first if j == 0 else band,
                   _dot_nt(q_ref[rows[i], sls[i]], keys(kp_ref, kc_ref, j, sls[i])), NEG)
         for i, (j, _) in enumerate(items)]
    m = [jnp.max(t, axis=-1, keepdims=True) for t in s]
    p = [jnp.exp(s[i] - m[i]).astype(BF16) for i in range(len(items))]
    pv = [_dot(p[i], jnp.concatenate([keys(vp_ref, vc_ref, j, sls[i]), ones], axis=1))
          for i, (j, _) in enumerate(items)]
    lse_tiles = [jnp.zeros((blk, LANES), F32) for _ in range(n_sub)]
    for i, (j, h) in enumerate(items):
        l = pv[i][:, DIL_HEAD_DIM:]
        o_ref[rows[i], sls[i]] = (pv[i][:, :DIL_HEAD_DIM] / l).astype(o_ref.dtype)
        lse_tiles[j] = jnp.where(lane == h, m[i] + jnp.log(l), lse_tiles[j])
    for j in range(n_sub):
        lse_ref[j * blk:(j + 1) * blk, :] = lse_tiles[j]


def _dil_attn(qkv, batch, seq, dil):
    blk = DIL_BLOCK
    hd = DIL_HEADS * DIL_HEAD_DIM
    sub = seq // dil
    tq = min(DIL_TQ, sub)
    nb = sub // tq
    per = tq // blk
    cur = lambda c: (lambda z, i: (z * nb + i, c))
    prev = lambda c: (lambda z, i: (jnp.maximum((z * nb + i) * per - 1, 0), c))
    out_map = lambda z, i: (z * nb + i, 0)
    return pl.pallas_call(
        _dil_attn_body,
        grid=(batch * dil, nb),
        in_specs=[pl.BlockSpec((tq, hd), cur(0)), pl.BlockSpec((blk, hd), prev(1)),
                  pl.BlockSpec((tq, hd), cur(1)), pl.BlockSpec((blk, hd), prev(2)),
                  pl.BlockSpec((tq, hd), cur(2))],
        out_specs=[pl.BlockSpec((tq, hd), out_map), pl.BlockSpec((tq, LANES), out_map)],
        out_shape=[jax.ShapeDtypeStruct((batch * seq, hd), BF16),
                   jax.ShapeDtypeStruct((batch * seq, LANES), F32)],
        compiler_params=_cparams(("parallel", "arbitrary")),
        name=f"dil_attn_{dil}",
    )(qkv, qkv, qkv, qkv, qkv)


def _dil_out_body(o0_ref, o1_ref, o2_ref, l0_ref, l1_ref, l2_ref, e_ref, w_ref, res_ref,
                  g_ref, b_ref, o_ref, osc_ref, lsc_ref, mix_ref):
    tm = o_ref.shape[0]
    nh = DIL_HEADS
    for gi, (og, lg) in enumerate(((o0_ref, l0_ref), (o1_ref, l1_ref), (o2_ref, l2_ref))):
        dil = og.shape[1]
        n = tm // dil
        for r in range(dil):
            rows = pl.ds(r, n, stride=dil) if dil > 1 else slice(None)
            for h in range(nh):
                osc_ref[gi * nh + h, rows, :] = og[0, r, :, h * LANES:(h + 1) * LANES].astype(F32)
            lsc_ref[gi, rows, :] = lg[0, r]
    l0, l1, l2 = lsc_ref[0], lsc_ref[1], lsc_ref[2]
    m = jnp.maximum(jnp.maximum(l0, l1), l2)
    e0, e1, e2 = jnp.exp(l0 - m), jnp.exp(l1 - m), jnp.exp(l2 - m)
    den = e0 + e1 + e2
    e = e_ref[...]
    wfull = []
    for eg in (e0, e1, e2):
        hi, lo = _split2(eg / den)
        wfull.append(_dot(hi, e) + _dot(lo, e))
    for h in range(nh):
        sl = slice(h * LANES, (h + 1) * LANES)
        mixed = sum(wfull[gi][:, sl] * osc_ref[gi * nh + h] for gi in range(3))
        mix_ref[:, sl] = mixed.astype(BF16)
    acc = _dot(mix_ref[...], w_ref[...])
    o_ref[...] = _layer_norm(DN_ALPHA * res_ref[...] + acc, g_ref[...], b_ref[...])


def _dil_out(outs, lses, w, res, g, b, seq, *, tm=512):
    m, d = res.shape
    hd = DIL_HEADS * DIL_HEAD_DIM
    expand = (jnp.arange(LANES)[:, None] == (jnp.arange(hd) // DIL_HEAD_DIM)[None, :]).astype(BF16)
    row = lambda i: (i, 0)
    nt = seq // tm
    grp = lambda i: (i // nt, 0, i % nt, 0)
    dils = [dil for _, dil in DIL_PAIRS]
    batch = m // seq
    o4 = [o.reshape(batch, dil, seq // dil, hd) for o, dil in zip(outs, dils)]
    l4 = [l.reshape(batch, dil, seq // dil, LANES) for l, dil in zip(lses, dils)]
    return pl.pallas_call(
        _dil_out_body,
        grid=(m // tm,),
        in_specs=[pl.BlockSpec((1, dil, tm // dil, hd), grp) for dil in dils]
        + [pl.BlockSpec((1, dil, tm // dil, LANES), grp) for dil in dils]
        + [_resident((LANES, hd)), _resident((hd, d)), pl.BlockSpec((tm, d), row),
           _resident((1, d)), _resident((1, d))],
        out_specs=pl.BlockSpec((tm, d), row),
        out_shape=jax.ShapeDtypeStruct((m, d), F32),
        scratch_shapes=[pltpu.VMEM((len(dils) * DIL_HEADS, tm, LANES), F32),
                        pltpu.VMEM((len(dils), tm, LANES), F32), pltpu.VMEM((tm, hd), BF16)],
        compiler_params=_cparams(("parallel",)),
        name="dil_out",
    )(*o4, *l4, expand, w, res, g, b)


def _rope_tables(seq, rot, theta, scale, passthrough):
    half = rot // 2
    inv_freq = theta ** (-jnp.arange(half, dtype=F32) / half)
    ang = jnp.arange(seq, dtype=F32)[:, None] * inv_freq[None, :]
    cos, sin = jnp.cos(ang), jnp.sin(ang)
    fill = jnp.full((seq, ROPE_PARTNER - half), passthrough, F32)
    zero = jnp.zeros((seq, ROPE_PARTNER - half), F32)
    c = jnp.concatenate([cos, fill, cos, fill], axis=1)
    s = jnp.concatenate([-sin, zero, sin, zero], axis=1)
    return c * scale, s * scale


def _rope_lane_order(rot, width):
    half = rot // 2
    rest = list(range(rot, width))
    cut = ROPE_PARTNER - half
    return jnp.array(list(range(half)) + rest[:cut] + list(range(half, rot)) + rest[cut:])


def _dil_proj_body(x_ref, w_ref, c_ref, s_ref, o_ref, xb_ref, xs_ref, *, dil):
    tm = x_ref.shape[0]
    n = tm // dil
    if dil == 1:
        xb = x_ref[...].astype(BF16)
    else:
        for c in range(D_MODEL // LANES):
            xs_ref[c] = x_ref[:, c * LANES:(c + 1) * LANES]
        for r in range(dil):
            for c in range(D_MODEL // LANES):
                xb_ref[r * n:(r + 1) * n, c * LANES:(c + 1) * LANES] = (
                    xs_ref[c, pl.ds(r, n, stride=dil), :].astype(BF16))
        xb = xb_ref[...]

    def store(c0, a):
        o_ref[0, :, :, c0:c0 + LANES] = a.astype(o_ref.dtype).reshape(dil, n, LANES)

    modes = [("rope", 0)] * DIL_HEADS + [("rope", 1)] * DIL_HEADS + [None] * DIL_HEADS
    _project_columns(xb, w_ref, store, modes, (c_ref, s_ref), 1.0)


def _dil_proj(x, w, tabs, batch, seq, dil, *, tm=512):
    hd = DIL_HEADS * DIL_HEAD_DIM
    nt = seq // tm
    n = tm // dil
    tab_spec = pl.BlockSpec((2, tm, LANES), lambda i: (0, i % nt, 0))
    return pl.pallas_call(
        functools.partial(_dil_proj_body, dil=dil),
        grid=(batch * nt,),
        in_specs=[pl.BlockSpec((tm, D_MODEL), lambda i: (i, 0)), _resident((D_MODEL, 3 * hd)),
                  tab_spec, tab_spec],
        out_specs=pl.BlockSpec((1, dil, n, 3 * hd), lambda i: (i // nt, 0, i % nt, 0)),
        out_shape=jax.ShapeDtypeStruct((batch, dil, seq // dil, 3 * hd), BF16),
        scratch_shapes=[pltpu.VMEM((tm, D_MODEL), BF16),
                        pltpu.VMEM((D_MODEL // LANES, tm, LANES), F32)],
        compiler_params=_cparams(("parallel",)),
        name=f"dil_proj_{dil}",
    )(x, w, *tabs)


def _dilated(x, w_in, batch, seq, *, tm=512):
    hd = DIL_HEADS * DIL_HEAD_DIM
    cq, sq = _rope_tables(seq, DIL_ROT, ROPE_THETA, DIL_HEAD_DIM ** -0.5, 1.0)
    ck, sk = _rope_tables(seq, DIL_ROT, ROPE_THETA, 1.0, 1.0)
    order = _rope_lane_order(DIL_ROT, DIL_HEAD_DIM)
    outs, lses = [], []
    for gi, (_, dil) in enumerate(DIL_PAIRS):
        wg = w_in[:, gi * 3 * hd:(gi + 1) * 3 * hd].reshape(D_MODEL, 3, DIL_HEADS, DIL_HEAD_DIM)
        wg = jnp.concatenate([wg[:, :2][..., order], wg[:, 2:]], axis=1).reshape(D_MODEL, 3 * hd)

        def by_residue(t):
            t = t.reshape(seq // tm, tm // dil, dil, LANES)
            return jnp.swapaxes(t, 1, 2).reshape(seq, LANES)

        tabs = [jnp.stack([by_residue(a), by_residue(b)]) for a, b in ((cq, ck), (sq, sk))]
        qkv = _dil_proj(x, wg, tabs, batch, seq, dil, tm=tm)
        o, lse = _dil_attn(qkv.reshape(batch * seq, 3 * hd), batch, seq, dil)
        outs.append(o)
        lses.append(lse)
    return outs, lses


def _mla_down_body(x_ref, w_ref, nq_ref, nkv_ref, c_ref, s_ref, cq_ref, ckv_ref, kpe_ref):
    acc = _dot(x_ref[...].astype(BF16), w_ref[...])
    cq = acc[:, :MLA_Q_RANK]
    ckv = acc[:, MLA_Q_RANK:MLA_Q_RANK + MLA_KV_RANK]
    kpe = acc[:, MLA_Q_RANK + MLA_KV_RANK:]
    cq = cq * lax.rsqrt(jnp.mean(cq * cq, axis=-1, keepdims=True) + RMS_EPS) * nq_ref[...]
    ckv = ckv * lax.rsqrt(jnp.mean(ckv * ckv, axis=-1, keepdims=True) + RMS_EPS) * nkv_ref[...]
    kpe = _rope_tile(kpe, c_ref[...], s_ref[...])
    cq_ref[...] = cq.astype(cq_ref.dtype)
    ckv_ref[...] = ckv.astype(ckv_ref.dtype)
    kpe_ref[...] = kpe.astype(kpe_ref.dtype)


def _mla_down(x, w, nq, nkv, tabs, seq, *, tm=512):
    m, d = x.shape
    n = w.shape[1]
    row = lambda i: (i, 0)
    ns = seq // tm
    tab = lambda i: (i % ns, 0)
    return pl.pallas_call(
        _mla_down_body,
        grid=(m // tm,),
        in_specs=[pl.BlockSpec((tm, d), row), _resident((d, n)), _resident((1, MLA_Q_RANK)),
                  _resident((1, MLA_KV_RANK))] + [pl.BlockSpec((tm, LANES), tab)] * 2,
        out_specs=[pl.BlockSpec((tm, MLA_Q_RANK), row), pl.BlockSpec((tm, MLA_KV_RANK), row),
                   pl.BlockSpec((tm, LANES), row)],
        out_shape=[jax.ShapeDtypeStruct((m, MLA_Q_RANK), BF16),
                   jax.ShapeDtypeStruct((m, MLA_KV_RANK), BF16),
                   jax.ShapeDtypeStruct((m, LANES), BF16)],
        compiler_params=_cparams(("parallel",)),
        name="mla_down",
    )(x, w, nq, nkv, *tabs)


def _mla_flash_body(q_ref, kn_ref, kpe_ref, v_ref, o_ref, kcat_ref, vaug_ref):
    tq, tk, sub = MLA_TQ, MLA_TK, MLA_TQ // MLA_ROW_SPLIT
    iq = pl.program_id(2)

    @pl.when(iq == 0)
    def _():
        kcat_ref[:, :MLA_NOPE] = kn_ref[...]
        kcat_ref[:, MLA_NOPE:] = kpe_ref[...]
        vaug_ref[:, :MLA_V] = v_ref[...]
        vaug_ref[:, MLA_V:] = jnp.ones((v_ref.shape[0], LANES), BF16)

    parts = range(MLA_ROW_SPLIT)
    col_minus_row = (lax.broadcasted_iota(jnp.int32, (sub, tk), 1)
                     - lax.broadcasted_iota(jnp.int32, (sub, tk), 0))

    def scores(chunk, diagonal):
        out = []
        for part in parts:
            width = (part + 1) * sub if diagonal else tk
            s = _dot_nt(q_ref[part * sub:(part + 1) * sub, :], kcat_ref[chunk * tk:chunk * tk + width, :])
            if diagonal:
                s = jnp.where(col_minus_row[:, :width] <= part * sub, s, NEG)
            out.append(s)
        return out

    def update(chunk, s_all, carry):
        out = []
        for s, (m, acc) in zip(s_all, carry):
            vb = vaug_ref[chunk * tk:chunk * tk + s.shape[1], :]
            m_new = jnp.maximum(m, jnp.max(s, axis=-1, keepdims=True))
            alpha = jnp.exp2(m - m_new)
            p = jnp.exp2((s - m_new).astype(BF16))
            out.append((m_new, alpha * acc + _dot(p, vb)))
        return out

    def attend(n_chunks):
        carry = [(jnp.full((sub, 1), NEG, F32), jnp.zeros((sub, MLA_V + LANES), F32)) for _ in parts]
        s = scores(0, n_chunks == 1)
        for c in range(n_chunks):
            s_next = scores(c + 1, c + 2 == n_chunks) if c + 1 < n_chunks else None
            carry = update(c, s, carry)
            s = s_next
        for part, (_, acc) in enumerate(carry):
            o_ref[part * sub:(part + 1) * sub, :] = (acc[:, :MLA_V] / acc[:, MLA_V:]).astype(o_ref.dtype)

    n_chunks = (iq * tq) // tk + 1
    for n in range(1, kcat_ref.shape[0] // tk + 1):
        pl.when(n_chunks == n)(functools.partial(attend, n))


def _mla_flash(q, kv, kpe, batch, seq):
    blk = MLA_TQ
    nq = seq // blk
    h_ = MLA_HEADS
    qw = MLA_NOPE + LANES
    return pl.pallas_call(
        _mla_flash_body,
        grid=(batch, h_, nq),
        in_specs=[pl.BlockSpec((blk, qw), lambda b, h, i: (b * nq + i, h)),
                  pl.BlockSpec((seq, MLA_NOPE), lambda b, h, i: (b, h)),
                  pl.BlockSpec((seq, LANES), lambda b, h, i: (b, 0)),
                  pl.BlockSpec((seq, MLA_V), lambda b, h, i: (b, h_ + h))],
        out_specs=pl.BlockSpec((blk, MLA_V), lambda b, h, i: (b * nq + i, h)),
        out_shape=jax.ShapeDtypeStruct((batch * seq, h_ * MLA_V), BF16),
        scratch_shapes=[pltpu.VMEM((seq, qw), BF16), pltpu.VMEM((seq, MLA_V + LANES), BF16)],
        compiler_params=_cparams(("parallel", "parallel", "arbitrary")),
        name="mla_flash",
    )(q, kv, kpe, kv)


def _mla(x, w_down, norm_q, norm_kv, w_uq, w_ukv, batch, seq):
    h_ = MLA_HEADS
    half = MLA_ROPE // 2

    def pe_tile(w):
        z = jnp.zeros(w.shape[:-1] + (ROPE_PARTNER - half,), w.dtype)
        return jnp.concatenate([w[..., :half], z, w[..., half:], z], axis=-1)

    n_lat = MLA_Q_RANK + MLA_KV_RANK
    wd = jnp.concatenate([w_down[:, :n_lat], pe_tile(w_down[:, n_lat:])], axis=1).astype(BF16)
    wq = w_uq.reshape(MLA_Q_RANK, h_, MLA_NOPE + MLA_ROPE)
    wq = jnp.concatenate([wq[..., :MLA_NOPE], pe_tile(wq[..., MLA_NOPE:])], axis=-1)
    wq = wq.reshape(MLA_Q_RANK, -1).astype(BF16)
    wkv = w_ukv.reshape(MLA_KV_RANK, h_, MLA_NOPE + MLA_V)
    wkv = jnp.concatenate([wkv[:, :, :MLA_NOPE].reshape(MLA_KV_RANK, -1),
                           wkv[:, :, MLA_NOPE:].reshape(MLA_KV_RANK, -1)], axis=1).astype(BF16)
    scale = (MLA_NOPE + MLA_ROPE) ** -0.5 * math.log2(math.e)
    tk = _rope_tables(seq, MLA_ROPE, MLA_THETA, 1.0, 0.0)
    tq = [t[None] for t in _rope_tables(seq, MLA_ROPE, MLA_THETA, scale, 0.0)]
    cq, ckv, kpe = _mla_down(x, wd, norm_q[None, :], norm_kv[None, :], tk, seq)
    tm = 512
    ns = seq // tm
    q = _mm(cq, wq, tm=tm, tabs=tq, tab_map=lambda i: (0, i % ns, 0),
            modes=["scale", ("rope", 0)] * h_, scale=scale, name="mla_q")
    kv = _mm(ckv, wkv, tm=tm, name="mla_kv")
    return _mla_flash(q, kv, kpe, batch, seq)


def _head_sum(z, ones_bd):
    hi, lo = _split2(z)
    return _dot(hi, ones_bd) + _dot(lo, ones_bd)


def _rwkv_prep_body(x_ref, xp_ref, mu_ref, wr_ref, wk_ref, wv_ref, la0_ref, lb0_ref, la1_ref,
                    lb1_ref, ga_ref, gb_ref, vec_ref, bd_ref, r_ref, lw_ref, k_ref, v_ref,
                    kk_ref, b_ref, g_ref, *, tiles_per_seq):
    x = x_ref[...]
    tm = x.shape[0]
    first = pl.program_id(0) % tiles_per_seq == 0
    prev_row = jnp.where(first, 0.0, xp_ref[7:8, :])
    rows = lax.broadcasted_iota(jnp.int32, x.shape, 0)
    shifted = jnp.where(rows == 0, prev_row, pltpu.roll(x, 1, 0))
    xx = shifted - x
    mix = lambda i: (x + xx * mu_ref[i:i + 1, :]).astype(BF16)
    r = _dot(mix(0), wr_ref[...])
    k_raw = _dot(mix(2), wk_ref[...])
    v = _dot(mix(3), wv_ref[...])
    w0, a0, k_k, k_a = (vec_ref[i:i + 1, :] for i in range(4))
    wl = w0 + _dot(jnp.tanh(_dot(mix(1), la0_ref[...])).astype(BF16), lb0_ref[...])
    z = -wl
    softplus = jnp.maximum(z, 0.0) + jnp.log(1.0 + jnp.exp(-jnp.abs(z)))
    lw_ref[...] = -jnp.exp(-softplus - 0.5)
    al = a0 + _dot(_dot(mix(4), la1_ref[...]).astype(BF16), lb1_ref[...])
    a = 1.0 / (1.0 + jnp.exp(-al))
    gl = _dot(mix(5), ga_ref[...])
    g = _dot((1.0 / (1.0 + jnp.exp(-gl))).astype(BF16), gb_ref[...])
    kk = k_raw * k_k
    bd = bd_ref[...]
    for s in range(D_MODEL // LANES):
        sl = slice(s * LANES, (s + 1) * LANES)
        t = kk[:, sl]
        nrm = jnp.maximum(jnp.sqrt(_head_sum(t * t, bd)), 1e-12)
        t = t / nrm
        kk_ref[:, sl] = t.astype(kk_ref.dtype)
        b_ref[:, sl] = (t * a[:, sl]).astype(b_ref.dtype)
    r_ref[...] = r.astype(r_ref.dtype)
    k_ref[...] = (k_raw * (1.0 + (a - 1.0) * k_a)).astype(k_ref.dtype)
    v_ref[...] = v.astype(v_ref.dtype)
    g_ref[...] = g.astype(g_ref.dtype)


def _head_ones():
    idx = jnp.arange(LANES) // RWKV_HEAD
    return (idx[:, None] == idx[None, :]).astype(BF16)


def _rwkv_prep(x, mu, w_rkv, vec, lora_a, lora_b, gate_a, gate_b, seq, *, tm=256):
    m, d = x.shape
    gpad = RWKV_GATE_PAD - gate_a.shape[1]
    ga = jnp.pad(gate_a, ((0, 0), (0, gpad))).astype(BF16)
    gb = jnp.pad(gate_b, ((0, gpad), (0, 0))).astype(BF16)
    wts = [w_rkv[0].astype(BF16), w_rkv[1].astype(BF16), w_rkv[2].astype(BF16),
           lora_a[0].astype(BF16), lora_b[0].astype(BF16), lora_a[1].astype(BF16),
           lora_b[1].astype(BF16), ga, gb]
    vec8 = jnp.pad(vec, ((0, 3), (0, 0)))
    mu8 = jnp.pad(mu, ((0, 2), (0, 0)))
    row = lambda i: (i, 0)
    sub = tm // 8
    out = jax.ShapeDtypeStruct((m, d), BF16)
    return pl.pallas_call(
        functools.partial(_rwkv_prep_body, tiles_per_seq=seq // tm),
        grid=(m // tm,),
        in_specs=[pl.BlockSpec((tm, d), row),
                  pl.BlockSpec((8, d), lambda i: (jnp.maximum(i * sub - 1, 0), 0)),
                  _resident(mu8.shape)] + [_resident(w.shape) for w in wts]
        + [_resident(vec8.shape), _resident((LANES, LANES))],
        out_specs=[pl.BlockSpec((tm, d), row)] * 7,
        out_shape=[out, jax.ShapeDtypeStruct((m, d), F32), out, out, out, out, out],
        compiler_params=_cparams(("parallel",)),
        name="rwkv_prep",
    )(x, x, mu8, *wts, vec8, _head_ones())


def _rwkv_wkv_body(r_ref, lw_ref, k_ref, v_ref, kk_ref, b_ref, tri_ref, y_ref, state_ref):
    c = RWKV_CHUNK
    two = 2 * c

    @pl.when(pl.program_id(1) == 0)
    def _():
        state_ref[...] = jnp.zeros(state_ref.shape, F32)

    tri = tri_ref[...]

    def decayed(bi):
        lw = lw_ref[bi]
        h1 = lw.astype(BF16)
        r1 = lw - h1.astype(F32)
        h2 = r1.astype(BF16)
        h3 = (r1 - h2.astype(F32)).astype(BF16)
        cum = _dot(tri, h1) + _dot(tri, h2) + _dot(tri, h3)
        gam = jnp.exp(cum)
        gam_inv = jnp.exp(-cum)
        return (r_ref[bi].astype(F32) * gam, kk_ref[bi].astype(F32) * jnp.exp(cum - lw),
                b_ref[bi].astype(F32) * gam_inv, k_ref[bi].astype(F32) * gam_inv, gam[c - 1:c, :])

    lane_lo = lax.broadcasted_iota(jnp.int32, (c, LANES), 1) < RWKV_HEAD
    row2 = lax.broadcasted_iota(jnp.int32, (two, two), 0)
    col2 = lax.broadcasted_iota(jnp.int32, (two, two), 1)
    same = (row2 // c) == (col2 // c)
    strict = jnp.logical_and(same, row2 > col2)
    incl = jnp.logical_and(same, row2 >= col2)
    eye = (row2 == col2).astype(F32)

    def stack_masked(t):
        return jnp.concatenate([jnp.where(lane_lo, t, 0.0), jnp.where(lane_lo, 0.0, t)], axis=0)

    nb = r_ref.shape[0]
    items = [(bi, p) for bi in range(nb) for p in range(D_MODEL // LANES)]
    idx = range(len(items))
    sls = [slice(p * LANES, (p + 1) * LANES) for _, p in items]
    dec = [decayed(bi) for bi in range(nb)]
    xs, bds, kds, vss, gend = [], [], [], [], []
    for (bi, _), sl in zip(items, sls):
        rt, kkt, bt, kt, gam_end = dec[bi]
        xs.append(jnp.concatenate([stack_masked(kkt[:, sl]), stack_masked(rt[:, sl])], axis=0).astype(BF16))
        bds.append(jnp.concatenate([bt[:, sl], bt[:, sl]], axis=0).astype(BF16))
        kds.append(jnp.concatenate([kt[:, sl], kt[:, sl]], axis=0).astype(BF16))
        v2 = v_ref[bi, :, sl].astype(F32)
        vss.append(jnp.where(same, jnp.concatenate([v2, v2], axis=0), 0.0).astype(BF16))
        gend.append(gam_end[:, sl])
    s2s = [state_ref[bi, p] for bi, p in items]
    a_all = [_dot_nt(xs[i], jnp.concatenate([bds[i], kds[i]], axis=0)) for i in idx]
    xs_state = [_dot_nt(xs[i], s2s[i].astype(BF16)) for i in idx]
    nmat = [jnp.where(strict, -a[:two, :two], 0.0) for a in a_all]
    lk = [jnp.where(strict, a[:two, two:], 0.0).astype(BF16) for a in a_all]
    arb = [jnp.where(incl, a[two:, :two], 0.0).astype(BF16) for a in a_all]
    ark = [jnp.where(incl, a[two:, two:], 0.0).astype(BF16) for a in a_all]
    rhs = [xs_state[i][:two] + _dot(lk[i], vss[i]) for i in idx]
    pw = [n_.astype(BF16) for n_ in nmat]
    inv = [eye + n_ for n_ in nmat]
    pw = [_dot(t, t).astype(BF16) for t in pw]
    for _ in range(int(math.log2(c)) - 2):
        both = [_dot(pw[i], jnp.concatenate([pw[i], inv[i].astype(BF16)], axis=1)) for i in idx]
        inv = [inv[i] + both[i][:, two:] for i in idx]
        pw = [t[:, :two].astype(BF16) for t in both]
    inv = [inv[i] + _dot(pw[i], inv[i].astype(BF16)) for i in idx]
    ub = [(-_dot(inv[i].astype(BF16), rhs[i].astype(BF16))).astype(BF16) for i in idx]
    for i, (bi, _) in enumerate(items):
        ys = xs_state[i][two:] + _dot(arb[i], ub[i]) + _dot(ark[i], vss[i])
        y_ref[bi, :, sls[i]] = ys[:c] + ys[c:]
    for i, (bi, p) in enumerate(items):
        ds = _dot_tn(ub[i], bds[i]) + _dot_tn(vss[i], kds[i])
        state_ref[bi, p] = jnp.where(same, (s2s[i] + ds) * gend[i], 0.0)


def _rwkv_wkv(r, lw, k, v, kk, b, batch, seq):
    c = RWKV_CHUNK
    n = seq // c
    d = D_MODEL
    tri = (jnp.arange(c)[:, None] >= jnp.arange(c)[None, :]).astype(BF16)
    nb = RWKV_BATCH_ROWS if batch % RWKV_BATCH_ROWS == 0 else 1
    blk = pl.BlockSpec((nb, c, d), lambda bi, i: (bi, i, 0))
    as3d = lambda t: t.reshape(batch, seq, d)
    y = pl.pallas_call(
        _rwkv_wkv_body,
        grid=(batch // nb, n),
        in_specs=[blk] * 6 + [_resident((c, c))],
        out_specs=blk,
        out_shape=jax.ShapeDtypeStruct((batch, seq, d), F32),
        scratch_shapes=[pltpu.VMEM((nb, d // LANES, LANES, LANES), F32)],
        compiler_params=_cparams(("parallel", "arbitrary")),
        name="rwkv_wkv",
    )(as3d(r), as3d(lw), as3d(k), as3d(v), as3d(kk), as3d(b), tri)
    return y.reshape(batch * seq, d)


def _rwkv_out_body(y_ref, r_ref, k_ref, v_ref, g_ref, vec_ref, bd_ref, w_ref, res_ref, lg_ref,
                   lb_ref, o_ref, a_ref):
    bd = bd_ref[...]
    inv_n = 1.0 / RWKV_HEAD
    for s in range(D_MODEL // LANES):
        sl = slice(s * LANES, (s + 1) * LANES)
        y = y_ref[:, sl]
        mu = _head_sum(y, bd) * inv_n
        dlt = y - mu
        var = _head_sum(dlt * dlt, bd) * inv_n
        yn = dlt * lax.rsqrt(var + RWKV_GN_EPS) * vec_ref[0:1, sl] + vec_ref[1:2, sl]
        rk = r_ref[:, sl].astype(F32) * k_ref[:, sl].astype(F32) * vec_ref[2:3, sl]
        bonus = _head_sum(rk, bd) * v_ref[:, sl].astype(F32)
        a_ref[:, sl] = ((yn + bonus) * g_ref[:, sl].astype(F32)).astype(BF16)
    acc = _dot(a_ref[...], w_ref[...])
    o_ref[...] = _layer_norm(DN_ALPHA * res_ref[...] + acc, lg_ref[...], lb_ref[...])


def _rwkv_out(y, r, k, v, g, vec, w, res, lg, lb, *, tm=512):
    m, d = res.shape
    row = lambda i: (i, 0)
    act = pl.BlockSpec((tm, d), row)
    return pl.pallas_call(
        _rwkv_out_body,
        grid=(m // tm,),
        in_specs=[act] * 5 + [_resident(vec.shape), _resident((LANES, LANES)), _resident((d, d)),
                              act, _resident((1, d)), _resident((1, d))],
        out_specs=act,
        out_shape=jax.ShapeDtypeStruct((m, d), F32),
        scratch_shapes=[pltpu.VMEM((tm, d), BF16)],
        compiler_params=_cparams(("parallel",)),
        name="rwkv_out",
    )(y, r, k, v, g, vec, _head_ones(), w, res, lg, lb)


def kernel(x, ret_w_in, ret_gn, ret_w_out, dil_w_in, dil_w_out, mla_w_down, mla_norm_q,
           mla_norm_kv, mla_w_uq, mla_w_ukv, mla_w_out, rwkv_mu, rwkv_w_rkv, rwkv_w_out,
           rwkv_vec, rwkv_lora_a, rwkv_lora_b, rwkv_gate_a, rwkv_gate_b, rwkv_ln_x,
           mlp_w1, mlp_w2, ln_g, ln_b):
    batch, seq, d = x.shape
    xf = x.reshape(batch * seq, d)
    n_mixers = 4
    for i in range(DEPTH):
        mixer, j = i % n_mixers, i // n_mixers
        lg, lb = ln_g[i, 0][None, :], ln_b[i, 0][None, :]
        if mixer == 0:
            proj = _mm(xf, ret_w_in[j].astype(BF16), tm=512, name="ret_proj")
            gated = _retention(proj, ret_gn[j], batch, seq)
            xf = _mm_res_ln(gated, ret_w_out[j].astype(BF16), xf, lg, lb, name="ret_out")
        elif mixer == 1:
            outs, lses = _dilated(xf, dil_w_in[j].astype(BF16), batch, seq)
            xf = _dil_out(outs, lses, dil_w_out[j].astype(BF16), xf, lg, lb, seq)
        elif mixer == 2:
            o = _mla(xf, mla_w_down[j], mla_norm_q[j], mla_norm_kv[j], mla_w_uq[j], mla_w_ukv[j],
                     batch, seq)
            xf = _mm_res_ln(o, mla_w_out[j].astype(BF16), xf, lg, lb, name="mla_out")
        else:
            r, lw, k, v, kk, b, g = _rwkv_prep(xf, rwkv_mu[j], rwkv_w_rkv[j], rwkv_vec[j],
                                               rwkv_lora_a[j], rwkv_lora_b[j], rwkv_gate_a[j],
                                               rwkv_gate_b[j], seq)
            y = _rwkv_wkv(r, lw, k, v, kk, b, batch, seq)
            vec = jnp.concatenate([rwkv_ln_x[j], rwkv_vec[j][4:5],
                                   jnp.zeros((5, d), F32)], axis=0)
            xf = _rwkv_out(y, r, k, v, g, vec, rwkv_w_out[j].astype(BF16), xf, lg, lb)
        xf = _mlp(xf, mlp_w1[i].astype(BF16), mlp_w2[i].astype(BF16),
                  ln_g[i, 1][None, :], ln_b[i, 1][None, :])
    return xf.reshape(batch, seq, d)
```

```python
import functools
import math

import jax
import jax.numpy as jnp
import numpy as np
from jax import lax
from jax.experimental import pallas as pl
from jax.experimental.pallas import tpu as pltpu

F32 = jnp.float32
BF16 = jnp.bfloat16

D_MODEL = 1024
DEPTH = 4
D_FF = 4 * D_MODEL
LN_EPS = 1e-5
RMS_EPS = 1e-6
GN_EPS = 1e-5
DN_ALPHA = (2.0 * DEPTH) ** 0.25
NEG = -1e30
LANES = 128
ROPE_PARTNER = 64
MM_SUB = 256

RET_HEADS = 4
RET_QK_DIM = 256
RET_V_DIM = 512
RET_CHUNK = 128
RET_THETA = 10000.0

DIL_PAIRS = ((128, 1), (512, 4), (2048, 16))
DIL_HEADS = 8
DIL_HEAD_DIM = 128
DIL_ROT = 32
DIL_BLOCK = 128
DIL_TQ = 512
ROPE_THETA = 500000.0

MLA_HEADS = 16
MLA_NOPE = 128
MLA_ROPE = 64
MLA_V = 128
MLA_Q_RANK = 256
MLA_KV_RANK = 128
MLA_THETA = 10000.0
MLA_TQ = 1024
MLA_TK = 1024
MLA_ROW_SPLIT = 4
assert MLA_TQ == MLA_TK

RWKV_HEAD = 64
RWKV_HEADS = D_MODEL // RWKV_HEAD
RWKV_GN_EPS = 64e-5
RWKV_CHUNK = 64
RWKV_GATE_PAD = 256
RWKV_BATCH_ROWS = 2

VMEM_LIMIT = 56 * 1024 * 1024


def _cparams(sem):
    return pltpu.CompilerParams(dimension_semantics=sem, vmem_limit_bytes=VMEM_LIMIT)


def _resident(shape):
    nd = len(shape)
    return pl.BlockSpec(shape, lambda *_: (0,) * nd, pipeline_mode=pl.Buffered(1))


def _layer_norm(z, g, b):
    mu = jnp.mean(z, axis=-1, keepdims=True)
    d = z - mu
    var = jnp.mean(d * d, axis=-1, keepdims=True)
    return d * lax.rsqrt(var + LN_EPS) * g + b


def _dot(a, b):
    return jnp.dot(a, b, preferred_element_type=F32)


def _dot_nt(a, b):
    return lax.dot_general(a, b, (((1,), (1,)), ((), ())), preferred_element_type=F32)


def _dot_tn(a, b):
    return lax.dot_general(a, b, (((0,), (0,)), ((), ())), preferred_element_type=F32)


def _rope_tile(a, c, s):
    return a * c + pltpu.roll(a, ROPE_PARTNER, 1) * s


def _project_columns(xb, w_ref, store, modes, tabs, scale):
    n = w_ref.shape[1]
    for c0 in range(0, n, MM_SUB):
        acc = _dot(xb, w_ref[:, c0:c0 + MM_SUB])
        for t in range(MM_SUB // LANES):
            a = acc[:, t * LANES:(t + 1) * LANES]
            mode = modes[c0 // LANES + t] if modes is not None else None
            if mode == "scale":
                a = a * scale
            elif mode is not None:
                c_ref, s_ref = tabs
                a = _rope_tile(a, c_ref[mode[1]], s_ref[mode[1]])
            store(c0 + t * LANES, a)


def _mm_body(x_ref, w_ref, *rest, modes, scale):
    tabs, o_ref = rest[:-1], rest[-1]

    def store(c0, a):
        o_ref[:, c0:c0 + LANES] = a.astype(o_ref.dtype)

    _project_columns(x_ref[...].astype(BF16), w_ref, store, modes, tabs, scale)


def _mm(x, w, *, tm, out_dtype=BF16, tabs=None, tab_map=None, modes=None, scale=1.0, name="mm"):
    m = x.shape[0]
    k, n = w.shape
    row = lambda i: (i, 0)
    in_specs = [pl.BlockSpec((tm, k), row), _resident((k, n))]
    args = [x, w]
    if tabs is not None:
        for t in tabs:
            in_specs.append(pl.BlockSpec((t.shape[0], tm, LANES), tab_map))
            args.append(t)
    return pl.pallas_call(
        functools.partial(_mm_body, modes=modes, scale=scale),
        grid=(m // tm,),
        in_specs=in_specs,
        out_specs=pl.BlockSpec((tm, n), row),
        out_shape=jax.ShapeDtypeStruct((m, n), out_dtype),
        compiler_params=_cparams(("parallel",)),
        name=name,
    )(*args)


def _mm_res_ln_body(a_ref, w_ref, res_ref, g_ref, b_ref, o_ref):
    acc = _dot(a_ref[...], w_ref[...])
    o_ref[...] = _layer_norm(DN_ALPHA * res_ref[...] + acc, g_ref[...], b_ref[...])


def _mm_res_ln(a, w, res, g, b, *, tm=512, name="mm_res_ln"):
    m, k = a.shape
    d = w.shape[1]
    row = lambda i: (i, 0)
    return pl.pallas_call(
        _mm_res_ln_body,
        grid=(m // tm,),
        in_specs=[pl.BlockSpec((tm, k), row), _resident((k, d)), pl.BlockSpec((tm, d), row),
                  _resident((1, d)), _resident((1, d))],
        out_specs=pl.BlockSpec((tm, d), row),
        out_shape=jax.ShapeDtypeStruct((m, d), F32),
        compiler_params=_cparams(("parallel",)),
        name=name,
    )(a, w, res, g, b)


def _mlp_body(x_ref, w1_ref, w2_ref, g_ref, b_ref, o_ref, *, fchunk):
    x = x_ref[...]
    xb = x.astype(BF16)
    acc = jnp.zeros(x.shape, F32)
    for c in range(D_FF // fchunk):
        h = _dot(xb, w1_ref[:, c * fchunk:(c + 1) * fchunk])
        h = jnp.maximum(h, 0.0)
        h = (h * h).astype(BF16)
        acc = acc + _dot(h, w2_ref[c * fchunk:(c + 1) * fchunk, :])
    o_ref[...] = _layer_norm(DN_ALPHA * x + acc, g_ref[...], b_ref[...])


def _mlp(x, w1, w2, g, b, *, tm=512, fchunk=1024):
    m, d = x.shape
    row = lambda i: (i, 0)
    return pl.pallas_call(
        functools.partial(_mlp_body, fchunk=fchunk),
        grid=(m // tm,),
        in_specs=[pl.BlockSpec((tm, d), row), _resident((d, D_FF)), _resident((D_FF, d)),
                  _resident((1, d)), _resident((1, d))],
        out_specs=pl.BlockSpec((tm, d), row),
        out_shape=jax.ShapeDtypeStruct((m, d), F32),
        compiler_params=_cparams(("parallel",)),
        name="mlp",
    )(x, w1, w2, g, b)


def _ret_body(q_ref, k_ref, v_ref, g_ref, cos_ref, sin_ref, intra_ref, qdec_ref, kdec_ref,
              cdec_ref, gn_ref, o_ref, state_ref):
    dk, dv, half = RET_QK_DIM, RET_V_DIM, RET_QK_DIM // 2

    @pl.when(pl.program_id(1) == 0)
    def _():
        state_ref[...] = jnp.zeros(state_ref.shape, F32)

    cos = cos_ref[...]
    sin = sin_ref[...]

    def rope(t):
        t1 = t[:, :half].astype(F32)
        t2 = t[:, half:].astype(F32)
        return jnp.concatenate([t1 * cos - t2 * sin, t2 * cos + t1 * sin], axis=-1)

    heads = range(RET_HEADS)
    q = [rope(q_ref[:, h * dk:(h + 1) * dk]) for h in heads]
    k = [rope(k_ref[:, h * dk:(h + 1) * dk]) * (dk ** -0.5) for h in heads]
    v = [v_ref[:, h * dv:(h + 1) * dv] for h in heads]
    qb = [t.astype(BF16) for t in q]
    state = [state_ref[h] for h in heads]
    scores = [(_dot_nt(qb[h], k[h].astype(BF16)) * intra_ref[h]).astype(BF16) for h in heads]
    cross = [_dot(qb[h], state[h].astype(BF16)) * qdec_ref[h] for h in heads]
    o = [_dot(scores[h], v[h]) + cross[h] for h in heads]
    for h in heads:
        state_ref[h] = (state[h] * cdec_ref[h, 0:1, :]
                        + _dot_tn((k[h] * kdec_ref[h]).astype(BF16), v[h]))
    for h in heads:
        sl = slice(h * dv, (h + 1) * dv)
        mu = jnp.mean(o[h], axis=-1, keepdims=True)
        d = o[h] - mu
        var = jnp.mean(d * d, axis=-1, keepdims=True)
        on = d * lax.rsqrt(var + GN_EPS) * gn_ref[0:1, sl] + gn_ref[1:2, sl]
        gate = g_ref[:, sl].astype(F32)
        gate = gate * (1.0 / (1.0 + jnp.exp(-gate)))
        o_ref[:, sl] = (gate * on).astype(o_ref.dtype)


def _retention(proj, gn, batch, seq):
    h_, dk, dv, c = RET_HEADS, RET_QK_DIM, RET_V_DIM, RET_CHUNK
    n = seq // c
    half = dk // 2
    f32 = np.float32
    pos = np.arange(seq, dtype=f32)
    inv_freq = f32(RET_THETA) ** (-np.arange(half, dtype=f32) / f32(half))
    ang = pos[:, None] * inv_freq[None, :]
    cos, sin = np.cos(ang), np.sin(ang)
    log_gamma = np.log(f32(1.0) - f32(2.0) ** (f32(-5.0) - np.arange(h_, dtype=f32)))
    idx = np.arange(c, dtype=f32)
    diff = idx[:, None] - idx[None, :]
    intra = np.where(diff >= 0, np.exp(log_gamma[:, None, None] * np.maximum(diff, f32(0.0))), f32(0.0))
    intra = intra.astype(f32)
    qdec = np.broadcast_to(np.exp(log_gamma[:, None] * (idx + f32(1.0)))[:, :, None], (h_, c, dv))
    kdec = np.broadcast_to(np.exp(log_gamma[:, None] * (f32(c - 1.0) - idx))[:, :, None], (h_, c, dk))
    cdec = np.broadcast_to(np.exp(log_gamma * f32(c))[:, None, None], (h_, 8, dv))
    qdec, kdec, cdec = (np.ascontiguousarray(t, dtype=f32) for t in (qdec, kdec, cdec))
    qk_w, vg_w = h_ * dk, h_ * dv
    return pl.pallas_call(
        _ret_body,
        grid=(batch, n),
        in_specs=[
            pl.BlockSpec((c, qk_w), lambda b, i: (b * n + i, 0)),
            pl.BlockSpec((c, qk_w), lambda b, i: (b * n + i, 1)),
            pl.BlockSpec((c, vg_w), lambda b, i: (b * n + i, 2 * qk_w // vg_w)),
            pl.BlockSpec((c, vg_w), lambda b, i: (b * n + i, 2 * qk_w // vg_w + 1)),
            pl.BlockSpec((c, half), lambda b, i: (i, 0)),
            pl.BlockSpec((c, half), lambda b, i: (i, 0)),
            _resident((h_, c, c)), _resident((h_, c, dv)), _resident((h_, c, dk)),
            _resident((h_, 8, dv)), _resident((2, vg_w)),
        ],
        out_specs=pl.BlockSpec((c, vg_w), lambda b, i: (b * n + i, 0)),
        out_shape=jax.ShapeDtypeStruct((batch * seq, vg_w), BF16),
        scratch_shapes=[pltpu.VMEM((h_, dk, dv), F32)],
        compiler_params=_cparams(("parallel", "arbitrary")),
        name="retention",
    )(proj, proj, proj, proj, cos, sin, intra, qdec, kdec, cdec, gn)


def _dil_attn_body(q_ref, kp_ref, kc_ref, vp_ref, vc_ref, o_ref, lse_ref):
    blk = DIL_BLOCK
    n_sub = q_ref.shape[0] // blk
    has_prev = pl.program_id(1) > 0
    qi = lax.broadcasted_iota(jnp.int32, (blk, 2 * blk), 0)
    ki = lax.broadcasted_iota(jnp.int32, (blk, 2 * blk), 1)
    band = jnp.logical_and(ki >= qi, ki <= qi + blk)
    band_first = jnp.logical_and(band, jnp.logical_or(ki >= blk, has_prev))
    lane = lax.broadcasted_iota(jnp.int32, (blk, LANES), 1)
    ones = jnp.ones((2 * blk, LANES), BF16)
    items = [(j, h) for j in range(n_sub) for h in range(DIL_HEADS)]
    sls = [slice(h * DIL_HEAD_DIM, (h + 1) * DIL_HEAD_DIM) for _, h in items]
    rows = [slice(j * blk, (j + 1) * blk) for j, _ in items]

    def keys(prev_ref, cur_ref, j, sl):
        if j == 0:
            return jnp.concatenate([prev_ref[:, sl], cur_ref[:blk, sl]], axis=0)
        return cur_ref[(j - 1) * blk:(j + 1) * blk, sl]

    s = [jnp.where(band_first if j == 0 else band,
                   _dot_nt(q_ref[rows[i], sls[i]], keys(kp_ref, kc_ref, j, sls[i])), NEG)
         for i, (j, _) in enumerate(items)]
    m = [jnp.max(t, axis=-1, keepdims=True) for t in s]
    p = [jnp.exp(s[i] - m[i]).astype(BF16) for i in range(len(items))]
    pv = [_dot(p[i], jnp.concatenate([keys(vp_ref, vc_ref, j, sls[i]), ones], axis=1))
          for i, (j, _) in enumerate(items)]
    lse_tiles = [jnp.zeros((blk, LANES), F32) for _ in range(n_sub)]
    for i, (j, h) in enumerate(items):
        l = pv[i][:, DIL_HEAD_DIM:]
        o_ref[rows[i], sls[i]] = (pv[i][:, :DIL_HEAD_DIM] / l).astype(o_ref.dtype)
        lse_tiles[j] = jnp.where(lane == h, m[i] + jnp.log(l), lse_tiles[j])
    for j in range(n_sub):
        lse_ref[j * blk:(j + 1) * blk, :] = lse_tiles[j]


def _dil_attn(qkv, batch, seq, dil):
    blk = DIL_BLOCK
    hd = DIL_HEADS * DIL_HEAD_DIM
    sub = seq // dil
    tq = min(DIL_TQ, sub)
    nb = sub // tq
    per = tq // blk
    cur = lambda c: (lambda z, i: (z * nb + i, c))
    prev = lambda c: (lambda z, i: (jnp.maximum((z * nb + i) * per - 1, 0), c))
    out_map = lambda z, i: (z * nb + i, 0)
    return pl.pallas_call(
        _dil_attn_body,
        grid=(batch * dil, nb),
        in_specs=[pl.BlockSpec((tq, hd), cur(0)), pl.BlockSpec((blk, hd), prev(1)),
                  pl.BlockSpec((tq, hd), cur(1)), pl.BlockSpec((blk, hd), prev(2)),
                  pl.BlockSpec((tq, hd), cur(2))],
        out_specs=[pl.BlockSpec((tq, hd), out_map), pl.BlockSpec((tq, LANES), out_map)],
        out_shape=[jax.ShapeDtypeStruct((batch * seq, hd), BF16),
                   jax.ShapeDtypeStruct((batch * seq, LANES), F32)],
        compiler_params=_cparams(("parallel", "arbitrary")),
        name=f"dil_attn_{dil}",
    )(qkv, qkv, qkv, qkv, qkv)


def _dil_out_body(o0_ref, o1_ref, o2_ref, l0_ref, l1_ref, l2_ref, e_ref, w_ref, res_ref,
                  g_ref, b_ref, o_ref, osc_ref, lsc_ref, mix_ref):
    tm = o_ref.shape[0]
    nh = DIL_HEADS
    for gi, (og, lg) in enumerate(((o0_ref, l0_ref), (o1_ref, l1_ref), (o2_ref, l2_ref))):
        dil = og.shape[1]
        n = tm // dil
        for r in range(dil):
            rows = pl.ds(r, n, stride=dil) if dil > 1 else slice(None)
            for h in range(nh):
                osc_ref[gi * nh + h, rows, :] = og[0, r, :, h * LANES:(h + 1) * LANES].astype(F32)
            lsc_ref[gi, rows, :] = lg[0, r]
    l0, l1, l2 = lsc_ref[0], lsc_ref[1], lsc_ref[2]
    m = jnp.maximum(jnp.maximum(l0, l1), l2)
    e0, e1, e2 = jnp.exp(l0 - m), jnp.exp(l1 - m), jnp.exp(l2 - m)
    den = e0 + e1 + e2
    e = e_ref[...]
    wfull = []
    for eg in (e0, e1, e2):
        wfull.append(_dot((eg / den).astype(BF16), e))
    for h in range(nh):
        sl = slice(h * LANES, (h + 1) * LANES)
        mixed = sum(wfull[gi][:, sl] * osc_ref[gi * nh + h] for gi in range(3))
        mix_ref[:, sl] = mixed.astype(BF16)
    acc = _dot(mix_ref[...], w_ref[...])
    o_ref[...] = _layer_norm(DN_ALPHA * res_ref[...] + acc, g_ref[...], b_ref[...])


def _dil_out(outs, lses, w, res, g, b, seq, *, tm=512):
    m, d = res.shape
    hd = DIL_HEADS * DIL_HEAD_DIM
    expand = (jnp.arange(LANES)[:, None] == (jnp.arange(hd) // DIL_HEAD_DIM)[None, :]).astype(BF16)
    row = lambda i: (i, 0)
    nt = seq // tm
    grp = lambda i: (i // nt, 0, i % nt, 0)
    dils = [dil for _, dil in DIL_PAIRS]
    batch = m // seq
    o4 = [o.reshape(batch, dil, seq // dil, hd) for o, dil in zip(outs, dils)]
    l4 = [l.reshape(batch, dil, seq // dil, LANES) for l, dil in zip(lses, dils)]
    return pl.pallas_call(
        _dil_out_body,
        grid=(m // tm,),
        in_specs=[pl.BlockSpec((1, dil, tm // dil, hd), grp) for dil in dils]
        + [pl.BlockSpec((1, dil, tm // dil, LANES), grp) for dil in dils]
        + [_resident((LANES, hd)), _resident((hd, d)), pl.BlockSpec((tm, d), row),
           _resident((1, d)), _resident((1, d))],
        out_specs=pl.BlockSpec((tm, d), row),
        out_shape=jax.ShapeDtypeStruct((m, d), F32),
        scratch_shapes=[pltpu.VMEM((len(dils) * DIL_HEADS, tm, LANES), F32),
                        pltpu.VMEM((len(dils), tm, LANES), F32), pltpu.VMEM((tm, hd), BF16)],
        compiler_params=_cparams(("parallel",)),
        name="dil_out",
    )(*o4, *l4, expand, w, res, g, b)


def _rope_tables(seq, rot, theta, scale, passthrough):
    half = rot // 2
    inv_freq = np.float32(theta) ** (-np.arange(half, dtype=np.float32) / np.float32(half))
    ang = np.arange(seq, dtype=np.float32)[:, None] * inv_freq[None, :]
    cos, sin = np.cos(ang), np.sin(ang)
    fill = np.full((seq, ROPE_PARTNER - half), passthrough, np.float32)
    zero = np.zeros((seq, ROPE_PARTNER - half), np.float32)
    c = np.concatenate([cos, fill, cos, fill], axis=1)
    s = np.concatenate([-sin, zero, sin, zero], axis=1)
    return (c * np.float32(scale)).astype(np.float32), (s * np.float32(scale)).astype(np.float32)


def _rope_lane_order(rot, width):
    half = rot // 2
    rest = list(range(rot, width))
    cut = ROPE_PARTNER - half
    return jnp.array(list(range(half)) + rest[:cut] + list(range(half, rot)) + rest[cut:])


def _dil_proj_body(x_ref, w_ref, c_ref, s_ref, o_ref, xb_ref, xs_ref, *, dil):
    tm = x_ref.shape[0]
    n = tm // dil
    if dil == 1:
        xb = x_ref[...].astype(BF16)
    else:
        for c in range(D_MODEL // LANES):
            xs_ref[c] = x_ref[:, c * LANES:(c + 1) * LANES]
        for r in range(dil):
            for c in range(D_MODEL // LANES):
                xb_ref[r * n:(r + 1) * n, c * LANES:(c + 1) * LANES] = (
                    xs_ref[c, pl.ds(r, n, stride=dil), :].astype(BF16))
        xb = xb_ref[...]

    def store(c0, a):
        o_ref[0, :, :, c0:c0 + LANES] = a.astype(o_ref.dtype).reshape(dil, n, LANES)

    modes = [("rope", 0)] * DIL_HEADS + [("rope", 1)] * DIL_HEADS + [None] * DIL_HEADS
    _project_columns(xb, w_ref, store, modes, (c_ref, s_ref), 1.0)


def _dil_proj(x, w, tabs, batch, seq, dil, *, tm=512):
    hd = DIL_HEADS * DIL_HEAD_DIM
    nt = seq // tm
    n = tm // dil
    tab_spec = pl.BlockSpec((2, tm, LANES), lambda i: (0, i % nt, 0))
    return pl.pallas_call(
        functools.partial(_dil_proj_body, dil=dil),
        grid=(batch * nt,),
        in_specs=[pl.BlockSpec((tm, D_MODEL), lambda i: (i, 0)), _resident((D_MODEL, 3 * hd)),
                  tab_spec, tab_spec],
        out_specs=pl.BlockSpec((1, dil, n, 3 * hd), lambda i: (i // nt, 0, i % nt, 0)),
        out_shape=jax.ShapeDtypeStruct((batch, dil, seq // dil, 3 * hd), BF16),
        scratch_shapes=[pltpu.VMEM((tm, D_MODEL), BF16),
                        pltpu.VMEM((D_MODEL // LANES, tm, LANES), F32)],
        compiler_params=_cparams(("parallel",)),
        name=f"dil_proj_{dil}",
    )(x, w, *tabs)


def _dilated(x, w_in, batch, seq, *, tm=512):
    hd = DIL_HEADS * DIL_HEAD_DIM
    cq, sq = _rope_tables(seq, DIL_ROT, ROPE_THETA, DIL_HEAD_DIM ** -0.5, 1.0)
    ck, sk = _rope_tables(seq, DIL_ROT, ROPE_THETA, 1.0, 1.0)
    order = _rope_lane_order(DIL_ROT, DIL_HEAD_DIM)
    outs, lses = [], []
    for gi, (_, dil) in enumerate(DIL_PAIRS):
        wg = w_in[:, gi * 3 * hd:(gi + 1) * 3 * hd].reshape(D_MODEL, 3, DIL_HEADS, DIL_HEAD_DIM)
        wg = jnp.concatenate([wg[:, :2][..., order], wg[:, 2:]], axis=1).reshape(D_MODEL, 3 * hd)

        def by_residue(t):
            t = t.reshape(seq // tm, tm // dil, dil, LANES)
            return np.swapaxes(t, 1, 2).reshape(seq, LANES)

        tabs = [np.stack([by_residue(a), by_residue(b)]) for a, b in ((cq, ck), (sq, sk))]
        qkv = _dil_proj(x, wg, tabs, batch, seq, dil, tm=tm)
        o, lse = _dil_attn(qkv.reshape(batch * seq, 3 * hd), batch, seq, dil)
        outs.append(o)
        lses.append(lse)
    return outs, lses


def _mla_proj_body(x_ref, wd_ref, nq_ref, nkv_ref, wq_ref, wkv_ref, c_ref, s_ref,
                   q_ref, kv_ref, kpe_ref, *, scale):
    acc = _dot(x_ref[...].astype(BF16), wd_ref[...])
    cq = acc[:, :MLA_Q_RANK]
    ckv = acc[:, MLA_Q_RANK:MLA_Q_RANK + MLA_KV_RANK]
    kpe = acc[:, MLA_Q_RANK + MLA_KV_RANK:]
    cq = cq * lax.rsqrt(jnp.mean(cq * cq, axis=-1, keepdims=True) + RMS_EPS) * nq_ref[...]
    ckv = ckv * lax.rsqrt(jnp.mean(ckv * ckv, axis=-1, keepdims=True) + RMS_EPS) * nkv_ref[...]
    kpe_ref[...] = _rope_tile(kpe, c_ref[0], s_ref[0]).astype(kpe_ref.dtype)

    def store_q(c0, a):
        q_ref[:, c0:c0 + LANES] = a.astype(q_ref.dtype)

    def store_kv(c0, a):
        kv_ref[:, c0:c0 + LANES] = a.astype(kv_ref.dtype)

    _project_columns(cq.astype(BF16), wq_ref, store_q, ["scale", ("rope", 1)] * MLA_HEADS,
                     (c_ref, s_ref), scale)
    _project_columns(ckv.astype(BF16), wkv_ref, store_kv, None, None, 1.0)


def _mla_proj(x, wd, nq, nkv, wq, wkv, tabs, seq, scale, *, tm=512):
    m, d = x.shape
    row = lambda i: (i, 0)
    ns = seq // tm
    tab = pl.BlockSpec((2, tm, LANES), lambda i: (0, i % ns, 0))
    return pl.pallas_call(
        functools.partial(_mla_proj_body, scale=scale),
        grid=(m // tm,),
        in_specs=[pl.BlockSpec((tm, d), row), _resident(wd.shape), _resident((1, MLA_Q_RANK)),
                  _resident((1, MLA_KV_RANK)), _resident(wq.shape), _resident(wkv.shape), tab, tab],
        out_specs=[pl.BlockSpec((tm, wq.shape[1]), row), pl.BlockSpec((tm, wkv.shape[1]), row),
                   pl.BlockSpec((tm, LANES), row)],
        out_shape=[jax.ShapeDtypeStruct((m, wq.shape[1]), BF16),
                   jax.ShapeDtypeStruct((m, wkv.shape[1]), BF16),
                   jax.ShapeDtypeStruct((m, LANES), BF16)],
        compiler_params=_cparams(("parallel",)),
        name="mla_proj",
    )(x, wd, nq, nkv, wq, wkv, *tabs)


def _mla_flash_body(q_ref, kn_ref, kpe_ref, v_ref, o_ref, kcat_ref, vaug_ref):
    tq, tk, sub = MLA_TQ, MLA_TK, MLA_TQ // MLA_ROW_SPLIT
    iq = pl.program_id(2)

    @pl.when(iq == 0)
    def _():
        kcat_ref[:, :MLA_NOPE] = kn_ref[...]
        kcat_ref[:, MLA_NOPE:] = kpe_ref[...]
        vaug_ref[:, :MLA_V] = v_ref[...]
        vaug_ref[:, MLA_V:] = jnp.ones((v_ref.shape[0], LANES), BF16)

    parts = range(MLA_ROW_SPLIT)
    col_minus_row = (lax.broadcasted_iota(jnp.int32, (sub, tk), 1)
                     - lax.broadcasted_iota(jnp.int32, (sub, tk), 0))

    def scores(chunk, diagonal):
        out = []
        for part in parts:
            width = (part + 1) * sub if diagonal else tk
            s = _dot_nt(q_ref[part * sub:(part + 1) * sub, :], kcat_ref[chunk * tk:chunk * tk + width, :])
            if diagonal:
                s = jnp.where(col_minus_row[:, :width] <= part * sub, s, NEG)
            out.append(s)
        return out

    def update(chunk, s_all, carry):
        out = []
        for s, (m, acc) in zip(s_all, carry):
            vb = vaug_ref[chunk * tk:chunk * tk + s.shape[1], :]
            m_new = jnp.maximum(m, jnp.max(s, axis=-1, keepdims=True))
            alpha = jnp.exp2(m - m_new)
            p = jnp.exp2((s - m_new).astype(BF16))
            out.append((m_new, alpha * acc + _dot(p, vb)))
        return out

    def attend(n_chunks):
        carry = [(jnp.full((sub, 1), NEG, F32), jnp.zeros((sub, MLA_V + LANES), F32)) for _ in parts]
        s = scores(0, n_chunks == 1)
        for c in range(n_chunks):
            s_next = scores(c + 1, c + 2 == n_chunks) if c + 1 < n_chunks else None
            carry = update(c, s, carry)
            s = s_next
        for part, (_, acc) in enumerate(carry):
            o_ref[part * sub:(part + 1) * sub, :] = (acc[:, :MLA_V] / acc[:, MLA_V:]).astype(o_ref.dtype)

    n_chunks = (iq * tq) // tk + 1
    for n in range(1, kcat_ref.shape[0] // tk + 1):
        pl.when(n_chunks == n)(functools.partial(attend, n))


def _mla_flash(q, kv, kpe, batch, seq):
    blk = MLA_TQ
    nq = seq // blk
    h_ = MLA_HEADS
    qw = MLA_NOPE + LANES
    return pl.pallas_call(
        _mla_flash_body,
        grid=(batch, h_, nq),
        in_specs=[pl.BlockSpec((blk, qw), lambda b, h, i: (b * nq + i, h)),
                  pl.BlockSpec((seq, MLA_NOPE), lambda b, h, i: (b, h)),
                  pl.BlockSpec((seq, LANES), lambda b, h, i: (b, 0)),
                  pl.BlockSpec((seq, MLA_V), lambda b, h, i: (b, h_ + h))],
        out_specs=pl.BlockSpec((blk, MLA_V), lambda b, h, i: (b * nq + i, h)),
        out_shape=jax.ShapeDtypeStruct((batch * seq, h_ * MLA_V), BF16),
        scratch_shapes=[pltpu.VMEM((seq, qw), BF16), pltpu.VMEM((seq, MLA_V + LANES), BF16)],
        compiler_params=_cparams(("parallel", "parallel", "arbitrary")),
        name="mla_flash",
    )(q, kv, kpe, kv)


def _mla(x, w_down, norm_q, norm_kv, w_uq, w_ukv, batch, seq):
    h_ = MLA_HEADS
    half = MLA_ROPE // 2

    def pe_tile(w):
        z = jnp.zeros(w.shape[:-1] + (ROPE_PARTNER - half,), w.dtype)
        return jnp.concatenate([w[..., :half], z, w[..., half:], z], axis=-1)

    n_lat = MLA_Q_RANK + MLA_KV_RANK
    wd = jnp.concatenate([w_down[:, :n_lat], pe_tile(w_down[:, n_lat:])], axis=1).astype(BF16)
    wq = w_uq.reshape(MLA_Q_RANK, h_, MLA_NOPE + MLA_ROPE)
    wq = jnp.concatenate([wq[..., :MLA_NOPE], pe_tile(wq[..., MLA_NOPE:])], axis=-1)
    wq = wq.reshape(MLA_Q_RANK, -1).astype(BF16)
    wkv = w_ukv.reshape(MLA_KV_RANK, h_, MLA_NOPE + MLA_V)
    wkv = jnp.concatenate([wkv[:, :, :MLA_NOPE].reshape(MLA_KV_RANK, -1),
                           wkv[:, :, MLA_NOPE:].reshape(MLA_KV_RANK, -1)], axis=1).astype(BF16)
    scale = (MLA_NOPE + MLA_ROPE) ** -0.5 * math.log2(math.e)
    tk = _rope_tables(seq, MLA_ROPE, MLA_THETA, 1.0, 0.0)
    tq = _rope_tables(seq, MLA_ROPE, MLA_THETA, scale, 0.0)
    tabs = [np.stack([a, b]) for a, b in zip(tk, tq)]
    q, kv, kpe = _mla_proj(x, wd, norm_q[None, :], norm_kv[None, :], wq, wkv, tabs, seq, scale)
    return _mla_flash(q, kv, kpe, batch, seq)


def _head_sum(z, ones_bd):
    return _dot(z.astype(BF16), ones_bd)


def _rwkv_prep_body(x_ref, xp_ref, mu_ref, wr_ref, wk_ref, wv_ref, la0_ref, lb0_ref, la1_ref,
                    lb1_ref, ga_ref, gb_ref, vec_ref, bd_ref, r_ref, lw_ref, k_ref, v_ref,
                    kk_ref, b_ref, g_ref, *, tiles_per_seq):
    x = x_ref[...]
    tm = x.shape[0]
    first = pl.program_id(0) % tiles_per_seq == 0
    prev_row = jnp.where(first, 0.0, xp_ref[7:8, :])
    rows = lax.broadcasted_iota(jnp.int32, x.shape, 0)
    shifted = jnp.where(rows == 0, prev_row, pltpu.roll(x, 1, 0))
    xb = x.astype(BF16)
    xxb = (shifted - x).astype(BF16)
    mix = lambda i: xb + xxb * mu_ref[i:i + 1, :].astype(BF16)
    r = _dot(mix(0), wr_ref[...])
    k_raw = _dot(mix(2), wk_ref[...])
    v = _dot(mix(3), wv_ref[...])
    w0, a0, k_k, k_a = (vec_ref[i:i + 1, :] for i in range(4))
    wl = w0 + _dot(jnp.tanh(_dot(mix(1), la0_ref[...])).astype(BF16), lb0_ref[...])
    lw_ref[...] = -math.exp(-0.5) / (1.0 + jnp.exp(-wl))
    al = a0 + _dot(_dot(mix(4), la1_ref[...]).astype(BF16), lb1_ref[...])
    a = 1.0 / (1.0 + jnp.exp(-al))
    gl = _dot(mix(5), ga_ref[...])
    g = _dot((1.0 / (1.0 + jnp.exp(-gl))).astype(BF16), gb_ref[...])
    kk = k_raw * k_k
    bd = bd_ref[...]
    for s in range(D_MODEL // LANES):
        sl = slice(s * LANES, (s + 1) * LANES)
        t = kk[:, sl]
        t = t * lax.rsqrt(jnp.maximum(_head_sum(t * t, bd), 1e-24))
        kk_ref[:, sl] = t.astype(kk_ref.dtype)
        b_ref[:, sl] = (t * a[:, sl]).astype(b_ref.dtype)
    r_ref[...] = r.astype(r_ref.dtype)
    k_ref[...] = (k_raw * (1.0 + (a - 1.0) * k_a)).astype(k_ref.dtype)
    v_ref[...] = v.astype(v_ref.dtype)
    g_ref[...] = g.astype(g_ref.dtype)


def _head_ones():
    idx = jnp.arange(LANES) // RWKV_HEAD
    return (idx[:, None] == idx[None, :]).astype(BF16)


def _rwkv_prep(x, mu, w_rkv, vec, lora_a, lora_b, gate_a, gate_b, seq, *, tm=256):
    m, d = x.shape
    gpad = RWKV_GATE_PAD - gate_a.shape[1]
    ga = jnp.pad(gate_a, ((0, 0), (0, gpad))).astype(BF16)
    gb = jnp.pad(gate_b, ((0, gpad), (0, 0))).astype(BF16)
    wts = [w_rkv[0].astype(BF16), w_rkv[1].astype(BF16), w_rkv[2].astype(BF16),
           lora_a[0].astype(BF16), lora_b[0].astype(BF16), lora_a[1].astype(BF16),
           lora_b[1].astype(BF16), ga, gb]
    vec8 = jnp.pad(vec, ((0, 3), (0, 0)))
    mu8 = jnp.pad(mu, ((0, 2), (0, 0)))
    row = lambda i: (i, 0)
    sub = tm // 8
    out = jax.ShapeDtypeStruct((m, d), BF16)
    return pl.pallas_call(
        functools.partial(_rwkv_prep_body, tiles_per_seq=seq // tm),
        grid=(m // tm,),
        in_specs=[pl.BlockSpec((tm, d), row),
                  pl.BlockSpec((8, d), lambda i: (jnp.maximum(i * sub - 1, 0), 0)),
                  _resident(mu8.shape)] + [_resident(w.shape) for w in wts]
        + [_resident(vec8.shape), _resident((LANES, LANES))],
        out_specs=[pl.BlockSpec((tm, d), row)] * 7,
        out_shape=[out, jax.ShapeDtypeStruct((m, d), F32), out, out, out, out, out],
        compiler_params=_cparams(("parallel",)),
        name="rwkv_prep",
    )(x, x, mu8, *wts, vec8, _head_ones())


def _rwkv_wkv_body(r_ref, lw_ref, k_ref, v_ref, kk_ref, b_ref, tri_ref, y_ref, state_ref):
    c = RWKV_CHUNK
    two = 2 * c

    @pl.when(pl.program_id(1) == 0)
    def _():
        state_ref[...] = jnp.zeros(state_ref.shape, F32)

    tri = tri_ref[...]

    def decayed(bi):
        lw = lw_ref[bi]
        h1 = lw.astype(BF16)
        r1 = lw - h1.astype(F32)
        h2 = r1.astype(BF16)
        h3 = (r1 - h2.astype(F32)).astype(BF16)
        cum = _dot(tri, h1) + _dot(tri, h2) + _dot(tri, h3)
        gam = jnp.exp(cum)
        gam_inv = jnp.exp(-cum)
        return (r_ref[bi].astype(F32) * gam, kk_ref[bi].astype(F32) * jnp.exp(cum - lw),
                b_ref[bi].astype(F32) * gam_inv, k_ref[bi].astype(F32) * gam_inv, gam[c - 1:c, :])

    lane_lo = lax.broadcasted_iota(jnp.int32, (c, LANES), 1) < RWKV_HEAD
    row2 = lax.broadcasted_iota(jnp.int32, (two, two), 0)
    col2 = lax.broadcasted_iota(jnp.int32, (two, two), 1)
    same = (row2 // c) == (col2 // c)
    strict = jnp.logical_and(same, row2 > col2)
    incl = jnp.logical_and(same, row2 >= col2)
    eye = (row2 == col2).astype(F32)

    def stack_masked(t):
        return jnp.concatenate([jnp.where(lane_lo, t, 0.0), jnp.where(lane_lo, 0.0, t)], axis=0)

    nb = r_ref.shape[0]
    items = [(bi, p) for bi in range(nb) for p in range(D_MODEL // LANES)]
    idx = range(len(items))
    sls = [slice(p * LANES, (p + 1) * LANES) for _, p in items]
    dec = [decayed(bi) for bi in range(nb)]
    xs, bds, kds, vss, gend = [], [], [], [], []
    for (bi, _), sl in zip(items, sls):
        rt, kkt, bt, kt, gam_end = dec[bi]
        xs.append(jnp.concatenate([stack_masked(kkt[:, sl]), stack_masked(rt[:, sl])], axis=0).astype(BF16))
        bds.append(jnp.concatenate([bt[:, sl], bt[:, sl]], axis=0).astype(BF16))
        kds.append(jnp.concatenate([kt[:, sl], kt[:, sl]], axis=0).astype(BF16))
        v2 = v_ref[bi, :, sl].astype(F32)
        vss.append(jnp.where(same, jnp.concatenate([v2, v2], axis=0), 0.0).astype(BF16))
        gend.append(gam_end[:, sl])
    s2s = [state_ref[bi, p] for bi, p in items]
    a_all = [_dot_nt(xs[i], jnp.concatenate([bds[i], kds[i]], axis=0)) for i in idx]
    xs_state = [_dot_nt(xs[i], s2s[i].astype(BF16)) for i in idx]
    nmat = [jnp.where(strict, -a[:two, :two], 0.0) for a in a_all]
    lk = [jnp.where(strict, a[:two, two:], 0.0).astype(BF16) for a in a_all]
    arb = [jnp.where(incl, a[two:, :two], 0.0).astype(BF16) for a in a_all]
    ark = [jnp.where(incl, a[two:, two:], 0.0).astype(BF16) for a in a_all]
    rhs = [xs_state[i][:two] + _dot(lk[i], vss[i]) for i in idx]
    pw = [n_.astype(BF16) for n_ in nmat]
    inv = [eye + n_ for n_ in nmat]
    pw = [_dot(t, t).astype(BF16) for t in pw]
    for _ in range(int(math.log2(c)) - 2):
        both = [_dot(pw[i], jnp.concatenate([pw[i], inv[i].astype(BF16)], axis=1)) for i in idx]
        inv = [inv[i] + both[i][:, two:] for i in idx]
        pw = [t[:, :two].astype(BF16) for t in both]
    inv = [inv[i] + _dot(pw[i], inv[i].astype(BF16)) for i in idx]
    ub = [(-_dot(inv[i].astype(BF16), rhs[i].astype(BF16))).astype(BF16) for i in idx]
    for i, (bi, _) in enumerate(items):
        ys = xs_state[i][two:] + _dot(arb[i], ub[i]) + _dot(ark[i], vss[i])
        y_ref[bi, :, sls[i]] = ys[:c] + ys[c:]
    for i, (bi, p) in enumerate(items):
        ds = _dot_tn(ub[i], bds[i]) + _dot_tn(vss[i], kds[i])
        state_ref[bi, p] = jnp.where(same, (s2s[i] + ds) * gend[i], 0.0)


def _rwkv_wkv(r, lw, k, v, kk, b, batch, seq):
    c = RWKV_CHUNK
    n = seq // c
    d = D_MODEL
    tri = (jnp.arange(c)[:, None] >= jnp.arange(c)[None, :]).astype(BF16)
    nb = RWKV_BATCH_ROWS if batch % RWKV_BATCH_ROWS == 0 else 1
    blk = pl.BlockSpec((nb, c, d), lambda bi, i: (bi, i, 0))
    as3d = lambda t: t.reshape(batch, seq, d)
    y = pl.pallas_call(
        _rwkv_wkv_body,
        grid=(batch // nb, n),
        in_specs=[blk] * 6 + [_resident((c, c))],
        out_specs=blk,
        out_shape=jax.ShapeDtypeStruct((batch, seq, d), F32),
        scratch_shapes=[pltpu.VMEM((nb, d // LANES, LANES, LANES), F32)],
        compiler_params=_cparams(("parallel", "arbitrary")),
        name="rwkv_wkv",
    )(as3d(r), as3d(lw), as3d(k), as3d(v), as3d(kk), as3d(b), tri)
    return y.reshape(batch * seq, d)


def _rwkv_out_body(y_ref, r_ref, k_ref, v_ref, g_ref, vec_ref, bd_ref, w_ref, res_ref, lg_ref,
                   lb_ref, o_ref, a_ref):
    bd = bd_ref[...]
    inv_n = 1.0 / RWKV_HEAD
    for s in range(D_MODEL // LANES):
        sl = slice(s * LANES, (s + 1) * LANES)
        y = y_ref[:, sl]
        mu = _head_sum(y, bd) * inv_n
        dlt = y - mu
        var = _head_sum(dlt * dlt, bd) * inv_n
        yn = dlt * lax.rsqrt(var + RWKV_GN_EPS) * vec_ref[0:1, sl] + vec_ref[1:2, sl]
        rk = r_ref[:, sl] * k_ref[:, sl] * vec_ref[2:3, sl].astype(BF16)
        bonus = _head_sum(rk, bd) * v_ref[:, sl].astype(F32)
        a_ref[:, sl] = ((yn + bonus) * g_ref[:, sl].astype(F32)).astype(BF16)
    acc = _dot(a_ref[...], w_ref[...])
    o_ref[...] = _layer_norm(DN_ALPHA * res_ref[...] + acc, lg_ref[...], lb_ref[...])


def _rwkv_out(y, r, k, v, g, vec, w, res, lg, lb, *, tm=512):
    m, d = res.shape
    row = lambda i: (i, 0)
    act = pl.BlockSpec((tm, d), row)
    return pl.pallas_call(
        _rwkv_out_body,
        grid=(m // tm,),
        in_specs=[act] * 5 + [_resident(vec.shape), _resident((LANES, LANES)), _resident((d, d)),
                              act, _resident((1, d)), _resident((1, d))],
        out_specs=act,
        out_shape=jax.ShapeDtypeStruct((m, d), F32),
        scratch_shapes=[pltpu.VMEM((tm, d), BF16)],
        compiler_params=_cparams(("parallel",)),
        name="rwkv_out",
    )(y, r, k, v, g, vec, _head_ones(), w, res, lg, lb)


def kernel(x, ret_w_in, ret_gn, ret_w_out, dil_w_in, dil_w_out, mla_w_down, mla_norm_q,
           mla_norm_kv, mla_w_uq, mla_w_ukv, mla_w_out, rwkv_mu, rwkv_w_rkv, rwkv_w_out,
           rwkv_vec, rwkv_lora_a, rwkv_lora_b, rwkv_gate_a, rwkv_gate_b, rwkv_ln_x,
           mlp_w1, mlp_w2, ln_g, ln_b):
    batch, seq, d = x.shape
    xf = x.reshape(batch * seq, d)
    n_mixers = 4
    for i in range(DEPTH):
        mixer, j = i % n_mixers, i // n_mixers
        lg, lb = ln_g[i, 0][None, :], ln_b[i, 0][None, :]
        if mixer == 0:
            proj = _mm(xf, ret_w_in[j].astype(BF16), tm=512, name="ret_proj")
            gated = _retention(proj, ret_gn[j], batch, seq)
            xf = _mm_res_ln(gated, ret_w_out[j].astype(BF16), xf, lg, lb, name="ret_out")
        elif mixer == 1:
            outs, lses = _dilated(xf, dil_w_in[j].astype(BF16), batch, seq)
            xf = _dil_out(outs, lses, dil_w_out[j].astype(BF16), xf, lg, lb, seq)
        elif mixer == 2:
            o = _mla(xf, mla_w_down[j], mla_norm_q[j], mla_norm_kv[j], mla_w_uq[j], mla_w_ukv[j],
                     batch, seq)
            xf = _mm_res_ln(o, mla_w_out[j].astype(BF16), xf, lg, lb, name="mla_out")
        else:
            r, lw, k, v, kk, b, g = _rwkv_prep(xf, rwkv_mu[j], rwkv_w_rkv[j], rwkv_vec[j],
                                               rwkv_lora_a[j], rwkv_lora_b[j], rwkv_gate_a[j],
                                               rwkv_gate_b[j], seq)
            y = _rwkv_wkv(r, lw, k, v, kk, b, batch, seq)
            vec = jnp.concatenate([rwkv_ln_x[j], rwkv_vec[j][4:5],
                                   jnp.zeros((5, d), F32)], axis=0)
            xf = _rwkv_out(y, r, k, v, g, vec, rwkv_w_out[j].astype(BF16), xf, lg, lb)
        xf = _mlp(xf, mlp_w1[i].astype(BF16), mlp_w2[i].astype(BF16),
                  ln_g[i, 1][None, :], ln_b[i, 1][None, :])
    return xf.reshape(batch, seq, d)
```

```python
import functools
import math

import jax
import jax.numpy as jnp
import numpy as np
from jax import lax
from jax.experimental import pallas as pl
from jax.experimental.pallas import tpu as pltpu

F32 = jnp.float32
BF16 = jnp.bfloat16

D_MODEL = 1024
DEPTH = 4
D_FF = 4 * D_MODEL
LN_EPS = 1e-5
RMS_EPS = 1e-6
GN_EPS = 1e-5
DN_ALPHA = (2.0 * DEPTH) ** 0.25
NEG = -1e30
LANES = 128
ROPE_PARTNER = 64
MM_SUB = 256

RET_HEADS = 4
RET_QK_DIM = 256
RET_V_DIM = 512
RET_CHUNK = 128
RET_THETA = 10000.0

DIL_PAIRS = ((128, 1), (512, 4), (2048, 16))
DIL_HEADS = 8
DIL_HEAD_DIM = 128
DIL_ROT = 32
DIL_BLOCK = 128
DIL_TQ = 512
ROPE_THETA = 500000.0

MLA_HEADS = 16
MLA_NOPE = 128
MLA_ROPE = 64
MLA_V = 128
MLA_Q_RANK = 256
MLA_KV_RANK = 128
MLA_THETA = 10000.0
MLA_TQ = 1024
MLA_TK = 1024
MLA_ROW_SPLIT = 4
assert MLA_TQ == MLA_TK

RWKV_HEAD = 64
RWKV_HEADS = D_MODEL // RWKV_HEAD
RWKV_GN_EPS = 64e-5
RWKV_CHUNK = 64
RWKV_GATE_PAD = 256
RWKV_BATCH_ROWS = 2

VMEM_LIMIT = 56 * 1024 * 1024


def _cparams(sem):
    return pltpu.CompilerParams(dimension_semantics=sem, vmem_limit_bytes=VMEM_LIMIT)


def _resident(shape):
    nd = len(shape)
    return pl.BlockSpec(shape, lambda *_: (0,) * nd, pipeline_mode=pl.Buffered(1))


def _layer_norm(z, g, b):
    mu = jnp.mean(z, axis=-1, keepdims=True)
    d = z - mu
    var = jnp.mean(d * d, axis=-1, keepdims=True)
    return d * lax.rsqrt(var + LN_EPS) * g + b


def _dot(a, b):
    return jnp.dot(a, b, preferred_element_type=F32)


def _dot_nt(a, b):
    return lax.dot_general(a, b, (((1,), (1,)), ((), ())), preferred_element_type=F32)


def _dot_tn(a, b):
    return lax.dot_general(a, b, (((0,), (0,)), ((), ())), preferred_element_type=F32)


def _rope_tile(a, c, s):
    return a * c + pltpu.roll(a, ROPE_PARTNER, 1) * s


def _project_columns(xb, w_ref, store, modes, tabs, scale):
    n = w_ref.shape[1]
    for c0 in range(0, n, MM_SUB):
        acc = _dot(xb, w_ref[:, c0:c0 + MM_SUB])
        for t in range(MM_SUB // LANES):
            a = acc[:, t * LANES:(t + 1) * LANES]
            mode = modes[c0 // LANES + t] if modes is not None else None
            if mode == "scale":
                a = a * scale
            elif mode is not None:
                c_ref, s_ref = tabs
                a = _rope_tile(a, c_ref[mode[1]], s_ref[mode[1]])
            store(c0 + t * LANES, a)


def _mm_body(x_ref, w_ref, *rest, modes, scale):
    tabs, o_ref = rest[:-1], rest[-1]

    def store(c0, a):
        o_ref[:, c0:c0 + LANES] = a.astype(o_ref.dtype)

    _project_columns(x_ref[...].astype(BF16), w_ref, store, modes, tabs, scale)


def _mm(x, w, *, tm, out_dtype=BF16, tabs=None, tab_map=None, modes=None, scale=1.0, name="mm"):
    m = x.shape[0]
    k, n = w.shape
    row = lambda i: (i, 0)
    in_specs = [pl.BlockSpec((tm, k), row), _resident((k, n))]
    args = [x, w]
    if tabs is not None:
        for t in tabs:
            in_specs.append(pl.BlockSpec((t.shape[0], tm, LANES), tab_map))
            args.append(t)
    return pl.pallas_call(
        functools.partial(_mm_body, modes=modes, scale=scale),
        grid=(m // tm,),
        in_specs=in_specs,
        out_specs=pl.BlockSpec((tm, n), row),
        out_shape=jax.ShapeDtypeStruct((m, n), out_dtype),
        compiler_params=_cparams(("parallel",)),
        name=name,
    )(*args)


def _mm_res_ln_body(a_ref, w_ref, res_ref, g_ref, b_ref, o_ref):
    half = a_ref.shape[0] // 2
    for r0 in (0, half):
        acc = _dot(a_ref[r0:r0 + half, :], w_ref[...])
        o_ref[r0:r0 + half, :] = _layer_norm(DN_ALPHA * res_ref[r0:r0 + half, :] + acc,
                                             g_ref[...], b_ref[...])


def _mm_res_ln(a, w, res, g, b, *, tm=512, name="mm_res_ln"):
    m, k = a.shape
    d = w.shape[1]
    row = lambda i: (i, 0)
    return pl.pallas_call(
        _mm_res_ln_body,
        grid=(m // tm,),
        in_specs=[pl.BlockSpec((tm, k), row), _resident((k, d)), pl.BlockSpec((tm, d), row),
                  _resident((1, d)), _resident((1, d))],
        out_specs=pl.BlockSpec((tm, d), row),
        out_shape=jax.ShapeDtypeStruct((m, d), F32),
        compiler_params=_cparams(("parallel",)),
        name=name,
    )(a, w, res, g, b)


def _mlp_body(x_ref, w1_ref, w2_ref, g_ref, b_ref, o_ref, *, fchunk):
    half = x_ref.shape[0] // 2
    for r0 in (0, half):
        x = x_ref[r0:r0 + half, :]
        xb = x.astype(BF16)
        acc = jnp.zeros(x.shape, F32)
        for c in range(D_FF // fchunk):
            h = _dot(xb, w1_ref[:, c * fchunk:(c + 1) * fchunk])
            h = jnp.maximum(h, 0.0)
            h = (h * h).astype(BF16)
            acc = acc + _dot(h, w2_ref[c * fchunk:(c + 1) * fchunk, :])
        o_ref[r0:r0 + half, :] = _layer_norm(DN_ALPHA * x + acc, g_ref[...], b_ref[...])


def _mlp(x, w1, w2, g, b, *, tm=512, fchunk=1024):
    m, d = x.shape
    row = lambda i: (i, 0)
    return pl.pallas_call(
        functools.partial(_mlp_body, fchunk=fchunk),
        grid=(m // tm,),
        in_specs=[pl.BlockSpec((tm, d), row), _resident((d, D_FF)), _resident((D_FF, d)),
                  _resident((1, d)), _resident((1, d))],
        out_specs=pl.BlockSpec((tm, d), row),
        out_shape=jax.ShapeDtypeStruct((m, d), F32),
        compiler_params=_cparams(("parallel",)),
        name="mlp",
    )(x, w1, w2, g, b)


def _ret_body(q_ref, k_ref, v_ref, g_ref, cos_ref, sin_ref, intra_ref, qdec_ref, kdec_ref,
              cdec_ref, gn_ref, o_ref, state_ref):
    dk, dv, half = RET_QK_DIM, RET_V_DIM, RET_QK_DIM // 2

    @pl.when(pl.program_id(1) == 0)
    def _():
        state_ref[...] = jnp.zeros(state_ref.shape, F32)

    cos = cos_ref[...]
    sin = sin_ref[...]

    def rope(t):
        t1 = t[:, :half].astype(F32)
        t2 = t[:, half:].astype(F32)
        return jnp.concatenate([t1 * cos - t2 * sin, t2 * cos + t1 * sin], axis=-1)

    heads = range(RET_HEADS)
    q = [rope(q_ref[:, h * dk:(h + 1) * dk]) for h in heads]
    k = [rope(k_ref[:, h * dk:(h + 1) * dk]) * (dk ** -0.5) for h in heads]
    v = [v_ref[:, h * dv:(h + 1) * dv] for h in heads]
    qb = [t.astype(BF16) for t in q]
    state = [state_ref[h] for h in heads]
    scores = [(_dot_nt(qb[h], k[h].astype(BF16)) * intra_ref[h]).astype(BF16) for h in heads]
    cross = [_dot(qb[h], state[h].astype(BF16)) * qdec_ref[h] for h in heads]
    o = [_dot(scores[h], v[h]) + cross[h] for h in heads]
    for h in heads:
        state_ref[h] = (state[h] * cdec_ref[h, 0:1, :]
                        + _dot_tn((k[h] * kdec_ref[h]).astype(BF16), v[h]))
    for h in heads:
        sl = slice(h * dv, (h + 1) * dv)
        mu = jnp.mean(o[h], axis=-1, keepdims=True)
        d = o[h] - mu
        var = jnp.mean(d * d, axis=-1, keepdims=True)
        on = d * lax.rsqrt(var + GN_EPS) * gn_ref[0:1, sl] + gn_ref[1:2, sl]
        gate = g_ref[:, sl].astype(F32)
        gate = gate * (1.0 / (1.0 + jnp.exp(-gate)))
        o_ref[:, sl] = (gate * on).astype(o_ref.dtype)


def _retention(proj, gn, batch, seq):
    h_, dk, dv, c = RET_HEADS, RET_QK_DIM, RET_V_DIM, RET_CHUNK
    n = seq // c
    half = dk // 2
    f32 = np.float32
    pos = np.arange(seq, dtype=f32)
    inv_freq = f32(RET_THETA) ** (-np.arange(half, dtype=f32) / f32(half))
    ang = pos[:, None] * inv_freq[None, :]
    cos, sin = np.cos(ang), np.sin(ang)
    log_gamma = np.log(f32(1.0) - f32(2.0) ** (f32(-5.0) - np.arange(h_, dtype=f32)))
    idx = np.arange(c, dtype=f32)
    diff = idx[:, None] - idx[None, :]
    intra = np.where(diff >= 0, np.exp(log_gamma[:, None, None] * np.maximum(diff, f32(0.0))), f32(0.0))
    intra = intra.astype(f32)
    qdec = np.broadcast_to(np.exp(log_gamma[:, None] * (idx + f32(1.0)))[:, :, None], (h_, c, dv))
    kdec = np.broadcast_to(np.exp(log_gamma[:, None] * (f32(c - 1.0) - idx))[:, :, None], (h_, c, dk))
    cdec = np.broadcast_to(np.exp(log_gamma * f32(c))[:, None, None], (h_, 8, dv))
    qdec, kdec, cdec = (np.ascontiguousarray(t, dtype=f32) for t in (qdec, kdec, cdec))
    qk_w, vg_w = h_ * dk, h_ * dv
    return pl.pallas_call(
        _ret_body,
        grid=(batch, n),
        in_specs=[
            pl.BlockSpec((c, qk_w), lambda b, i: (b * n + i, 0)),
            pl.BlockSpec((c, qk_w), lambda b, i: (b * n + i, 1)),
            pl.BlockSpec((c, vg_w), lambda b, i: (b * n + i, 2 * qk_w // vg_w)),
            pl.BlockSpec((c, vg_w), lambda b, i: (b * n + i, 2 * qk_w // vg_w + 1)),
            pl.BlockSpec((c, half), lambda b, i: (i, 0)),
            pl.BlockSpec((c, half), lambda b, i: (i, 0)),
            _resident((h_, c, c)), _resident((h_, c, dv)), _resident((h_, c, dk)),
            _resident((h_, 8, dv)), _resident((2, vg_w)),
        ],
        out_specs=pl.BlockSpec((c, vg_w), lambda b, i: (b * n + i, 0)),
        out_shape=jax.ShapeDtypeStruct((batch * seq, vg_w), BF16),
        scratch_shapes=[pltpu.VMEM((h_, dk, dv), F32)],
        compiler_params=_cparams(("parallel", "arbitrary")),
        name="retention",
    )(proj, proj, proj, proj, cos, sin, intra, qdec, kdec, cdec, gn)


def _dil_attn_body(q_ref, kp_ref, kc_ref, vp_ref, vc_ref, o_ref, lse_ref):
    blk = DIL_BLOCK
    n_sub = q_ref.shape[0] // blk
    has_prev = pl.program_id(1) > 0
    qi = lax.broadcasted_iota(jnp.int32, (blk, 2 * blk), 0)
    ki = lax.broadcasted_iota(jnp.int32, (blk, 2 * blk), 1)
    band = jnp.logical_and(ki >= qi, ki <= qi + blk)
    band_first = jnp.logical_and(band, jnp.logical_or(ki >= blk, has_prev))
    lane = lax.broadcasted_iota(jnp.int32, (blk, LANES), 1)
    ones = jnp.ones((2 * blk, LANES), BF16)
    items = [(j, h) for j in range(n_sub) for h in range(DIL_HEADS)]
    sls = [slice(h * DIL_HEAD_DIM, (h + 1) * DIL_HEAD_DIM) for _, h in items]
    rows = [slice(j * blk, (j + 1) * blk) for j, _ in items]

    def keys(prev_ref, cur_ref, j, sl):
        if j == 0:
            return jnp.concatenate([prev_ref[:, sl], cur_ref[:blk, sl]], axis=0)
        return cur_ref[(j - 1) * blk:(j + 1) * blk, sl]

    s = [jnp.where(band_first if j == 0 else band,
                   _dot_nt(q_ref[rows[i], sls[i]], keys(kp_ref, kc_ref, j, sls[i])), NEG)
         for i, (j, _) in enumerate(items)]
    m = [jnp.max(t, axis=-1, keepdims=True) for t in s]
    p = [jnp.exp((s[i] - m[i]).astype(BF16)) for i in range(len(items))]
    pv = [_dot(p[i], jnp.concatenate([keys(vp_ref, vc_ref, j, sls[i]), ones], axis=1))
          for i, (j, _) in enumerate(items)]
    lse_tiles = [jnp.zeros((blk, LANES), F32) for _ in range(n_sub)]
    for i, (j, h) in enumerate(items):
        l = pv[i][:, DIL_HEAD_DIM:]
        o_ref[rows[i], sls[i]] = (pv[i][:, :DIL_HEAD_DIM] / l).astype(o_ref.dtype)
        lse_tiles[j] = jnp.where(lane == h, m[i] + jnp.log(l), lse_tiles[j])
    for j in range(n_sub):
        lse_ref[j * blk:(j + 1) * blk, :] = lse_tiles[j]


def _dil_attn(qkv, batch, seq, dil):
    blk = DIL_BLOCK
    hd = DIL_HEADS * DIL_HEAD_DIM
    sub = seq // dil
    tq = min(DIL_TQ, sub)
    nb = sub // tq
    per = tq // blk
    cur = lambda c: (lambda z, i: (z * nb + i, c))
    prev = lambda c: (lambda z, i: (jnp.maximum((z * nb + i) * per - 1, 0), c))
    out_map = lambda z, i: (z * nb + i, 0)
    return pl.pallas_call(
        _dil_attn_body,
        grid=(batch * dil, nb),
        in_specs=[pl.BlockSpec((tq, hd), cur(0)), pl.BlockSpec((blk, hd), prev(1)),
                  pl.BlockSpec((tq, hd), cur(1)), pl.BlockSpec((blk, hd), prev(2)),
                  pl.BlockSpec((tq, hd), cur(2))],
        out_specs=[pl.BlockSpec((tq, hd), out_map), pl.BlockSpec((tq, LANES), out_map)],
        out_shape=[jax.ShapeDtypeStruct((batch * seq, hd), BF16),
                   jax.ShapeDtypeStruct((batch * seq, LANES), F32)],
        compiler_params=_cparams(("parallel", "arbitrary")),
        name=f"dil_attn_{dil}",
    )(qkv, qkv, qkv, qkv, qkv)


def _dil_out_body(o0_ref, o1_ref, o2_ref, l0_ref, l1_ref, l2_ref, e_ref, w_ref, res_ref,
                  g_ref, b_ref, o_ref, osc_ref, lsc_ref, mix_ref):
    tm = o_ref.shape[0]
    nh = DIL_HEADS
    for gi, (og, lg) in enumerate(((o0_ref, l0_ref), (o1_ref, l1_ref), (o2_ref, l2_ref))):
        dil = og.shape[1]
        n = tm // dil
        for r in range(dil):
            rows = pl.ds(r, n, stride=dil) if dil > 1 else slice(None)
            for h in range(nh):
                osc_ref[gi * nh + h, rows, :] = og[0, r, :, h * LANES:(h + 1) * LANES].astype(F32)
            lsc_ref[gi, rows, :] = lg[0, r]
    l0, l1, l2 = lsc_ref[0], lsc_ref[1], lsc_ref[2]
    m = jnp.maximum(jnp.maximum(l0, l1), l2)
    e0, e1, e2 = jnp.exp(l0 - m), jnp.exp(l1 - m), jnp.exp(l2 - m)
    den = e0 + e1 + e2
    e = e_ref[...]
    wfull = []
    for eg in (e0, e1, e2):
        wfull.append(_dot((eg / den).astype(BF16), e))
    for h in range(nh):
        sl = slice(h * LANES, (h + 1) * LANES)
        mixed = sum(wfull[gi][:, sl] * osc_ref[gi * nh + h] for gi in range(3))
        mix_ref[:, sl] = mixed.astype(BF16)
    acc = _dot(mix_ref[...], w_ref[...])
    o_ref[...] = _layer_norm(DN_ALPHA * res_ref[...] + acc, g_ref[...], b_ref[...])


def _dil_out(outs, lses, w, res, g, b, seq, *, tm=512):
    m, d = res.shape
    hd = DIL_HEADS * DIL_HEAD_DIM
    expand = (jnp.arange(LANES)[:, None] == (jnp.arange(hd) // DIL_HEAD_DIM)[None, :]).astype(BF16)
    row = lambda i: (i, 0)
    nt = seq // tm
    grp = lambda i: (i // nt, 0, i % nt, 0)
    dils = [dil for _, dil in DIL_PAIRS]
    batch = m // seq
    o4 = [o.reshape(batch, dil, seq // dil, hd) for o, dil in zip(outs, dils)]
    l4 = [l.reshape(batch, dil, seq // dil, LANES) for l, dil in zip(lses, dils)]
    return pl.pallas_call(
        _dil_out_body,
        grid=(m // tm,),
        in_specs=[pl.BlockSpec((1, dil, tm // dil, hd), grp) for dil in dils]
        + [pl.BlockSpec((1, dil, tm // dil, LANES), grp) for dil in dils]
        + [_resident((LANES, hd)), _resident((hd, d)), pl.BlockSpec((tm, d), row),
           _resident((1, d)), _resident((1, d))],
        out_specs=pl.BlockSpec((tm, d), row),
        out_shape=jax.ShapeDtypeStruct((m, d), F32),
        scratch_shapes=[pltpu.VMEM((len(dils) * DIL_HEADS, tm, LANES), F32),
                        pltpu.VMEM((len(dils), tm, LANES), F32), pltpu.VMEM((tm, hd), BF16)],
        compiler_params=_cparams(("parallel",)),
        name="dil_out",
    )(*o4, *l4, expand, w, res, g, b)


def _rope_tables(seq, rot, theta, scale, passthrough):
    half = rot // 2
    inv_freq = np.float32(theta) ** (-np.arange(half, dtype=np.float32) / np.float32(half))
    ang = np.arange(seq, dtype=np.float32)[:, None] * inv_freq[None, :]
    cos, sin = np.cos(ang), np.sin(ang)
    fill = np.full((seq, ROPE_PARTNER - half), passthrough, np.float32)
    zero = np.zeros((seq, ROPE_PARTNER - half), np.float32)
    c = np.concatenate([cos, fill, cos, fill], axis=1)
    s = np.concatenate([-sin, zero, sin, zero], axis=1)
    return (c * np.float32(scale)).astype(np.float32), (s * np.float32(scale)).astype(np.float32)


def _rope_lane_order(rot, width):
    half = rot // 2
    rest = list(range(rot, width))
    cut = ROPE_PARTNER - half
    return jnp.array(list(range(half)) + rest[:cut] + list(range(half, rot)) + rest[cut:])


def _dil_proj_body(x_ref, w_ref, c_ref, s_ref, o_ref, xb_ref, xs_ref, *, dil):
    tm = x_ref.shape[0]
    n = tm // dil
    if dil == 1:
        xb = x_ref[...].astype(BF16)
    else:
        for c in range(D_MODEL // LANES):
            xs_ref[c] = x_ref[:, c * LANES:(c + 1) * LANES]
        for r in range(dil):
            for c in range(D_MODEL // LANES):
                xb_ref[r * n:(r + 1) * n, c * LANES:(c + 1) * LANES] = (
                    xs_ref[c, pl.ds(r, n, stride=dil), :].astype(BF16))
        xb = xb_ref[...]

    def store(c0, a):
        o_ref[0, :, :, c0:c0 + LANES] = a.astype(o_ref.dtype).reshape(dil, n, LANES)

    modes = [("rope", 0)] * DIL_HEADS + [("rope", 1)] * DIL_HEADS + [None] * DIL_HEADS
    _project_columns(xb, w_ref, store, modes, (c_ref, s_ref), 1.0)


def _dil_proj(x, w, tabs, batch, seq, dil, *, tm=512):
    hd = DIL_HEADS * DIL_HEAD_DIM
    nt = seq // tm
    n = tm // dil
    tab_spec = pl.BlockSpec((2, tm, LANES), lambda i: (0, i % nt, 0))
    return pl.pallas_call(
        functools.partial(_dil_proj_body, dil=dil),
        grid=(batch * nt,),
        in_specs=[pl.BlockSpec((tm, D_MODEL), lambda i: (i, 0)), _resident((D_MODEL, 3 * hd)),
                  tab_spec, tab_spec],
        out_specs=pl.BlockSpec((1, dil, n, 3 * hd), lambda i: (i // nt, 0, i % nt, 0)),
        out_shape=jax.ShapeDtypeStruct((batch, dil, seq // dil, 3 * hd), BF16),
        scratch_shapes=[pltpu.VMEM((tm, D_MODEL), BF16),
                        pltpu.VMEM((D_MODEL // LANES, tm, LANES), F32)],
        compiler_params=_cparams(("parallel",)),
        name=f"dil_proj_{dil}",
    )(x, w, *tabs)


def _dilated(x, w_in, batch, seq, *, tm=512):
    hd = DIL_HEADS * DIL_HEAD_DIM
    cq, sq = _rope_tables(seq, DIL_ROT, ROPE_THETA, DIL_HEAD_DIM ** -0.5, 1.0)
    ck, sk = _rope_tables(seq, DIL_ROT, ROPE_THETA, 1.0, 1.0)
    order = _rope_lane_order(DIL_ROT, DIL_HEAD_DIM)
    outs, lses = [], []
    for gi, (_, dil) in enumerate(DIL_PAIRS):
        wg = w_in[:, gi * 3 * hd:(gi + 1) * 3 * hd].reshape(D_MODEL, 3, DIL_HEADS, DIL_HEAD_DIM)
        wg = jnp.concatenate([wg[:, :2][..., order], wg[:, 2:]], axis=1).reshape(D_MODEL, 3 * hd)

        def by_residue(t):
            t = t.reshape(seq // tm, tm // dil, dil, LANES)
            return np.swapaxes(t, 1, 2).reshape(seq, LANES)

        tabs = [np.stack([by_residue(a), by_residue(b)]) for a, b in ((cq, ck), (sq, sk))]
        qkv = _dil_proj(x, wg, tabs, batch, seq, dil, tm=tm)
        o, lse = _dil_attn(qkv.reshape(batch * seq, 3 * hd), batch, seq, dil)
        outs.append(o)
        lses.append(lse)
    return outs, lses


def _mla_proj_body(x_ref, wd_ref, nq_ref, nkv_ref, wq_ref, wkv_ref, c_ref, s_ref,
                   q_ref, kv_ref, kpe_ref, *, scale):
    acc = _dot(x_ref[...].astype(BF16), wd_ref[...])
    cq = acc[:, :MLA_Q_RANK]
    ckv = acc[:, MLA_Q_RANK:MLA_Q_RANK + MLA_KV_RANK]
    kpe = acc[:, MLA_Q_RANK + MLA_KV_RANK:]
    cq = cq * lax.rsqrt(jnp.mean(cq * cq, axis=-1, keepdims=True) + RMS_EPS) * nq_ref[...]
    ckv = ckv * lax.rsqrt(jnp.mean(ckv * ckv, axis=-1, keepdims=True) + RMS_EPS) * nkv_ref[...]
    kpe_ref[...] = _rope_tile(kpe, c_ref[0], s_ref[0]).astype(kpe_ref.dtype)

    def store_q(c0, a):
        q_ref[:, c0:c0 + LANES] = a.astype(q_ref.dtype)

    def store_kv(c0, a):
        kv_ref[:, c0:c0 + LANES] = a.astype(kv_ref.dtype)

    _project_columns(cq.astype(BF16), wq_ref, store_q, ["scale", ("rope", 1)] * MLA_HEADS,
                     (c_ref, s_ref), scale)
    _project_columns(ckv.astype(BF16), wkv_ref, store_kv, None, None, 1.0)


def _mla_proj(x, wd, nq, nkv, wq, wkv, tabs, seq, scale, *, tm=512):
    m, d = x.shape
    row = lambda i: (i, 0)
    ns = seq // tm
    tab = pl.BlockSpec((2, tm, LANES), lambda i: (0, i % ns, 0))
    return pl.pallas_call(
        functools.partial(_mla_proj_body, scale=scale),
        grid=(m // tm,),
        in_specs=[pl.BlockSpec((tm, d), row), _resident(wd.shape), _resident((1, MLA_Q_RANK)),
                  _resident((1, MLA_KV_RANK)), _resident(wq.shape), _resident(wkv.shape), tab, tab],
        out_specs=[pl.BlockSpec((tm, wq.shape[1]), row), pl.BlockSpec((tm, wkv.shape[1]), row),
                   pl.BlockSpec((tm, LANES), row)],
        out_shape=[jax.ShapeDtypeStruct((m, wq.shape[1]), BF16),
                   jax.ShapeDtypeStruct((m, wkv.shape[1]), BF16),
                   jax.ShapeDtypeStruct((m, LANES), BF16)],
        compiler_params=_cparams(("parallel",)),
        name="mla_proj",
    )(x, wd, nq, nkv, wq, wkv, *tabs)


def _mla_flash_body(q_ref, kn_ref, kpe_ref, v_ref, o_ref, kcat_ref, vaug_ref):
    tq, tk, sub = MLA_TQ, MLA_TK, MLA_TQ // MLA_ROW_SPLIT
    kcat_ref[:, :MLA_NOPE] = kn_ref[...]
    kcat_ref[:, MLA_NOPE:] = kpe_ref[...]
    vaug_ref[:, :MLA_V] = v_ref[...]
    vaug_ref[:, MLA_V:] = jnp.ones((v_ref.shape[0], LANES), BF16)

    parts = range(MLA_ROW_SPLIT)
    col_minus_row = (lax.broadcasted_iota(jnp.int32, (sub, tk), 1)
                     - lax.broadcasted_iota(jnp.int32, (sub, tk), 0))

    def scores(tile, chunk, diagonal):
        out = []
        for part in parts:
            width = (part + 1) * sub if diagonal else tk
            q0 = tile * tq + part * sub
            s = _dot_nt(q_ref[q0:q0 + sub, :], kcat_ref[chunk * tk:chunk * tk + width, :])
            if diagonal:
                s = jnp.where(col_minus_row[:, :width] <= part * sub, s, NEG)
            out.append(s)
        return out

    def update(chunk, s_all, carry):
        out = []
        for s, (m, acc) in zip(s_all, carry):
            vb = vaug_ref[chunk * tk:chunk * tk + s.shape[1], :]
            m_new = jnp.maximum(m, jnp.max(s, axis=-1, keepdims=True))
            alpha = jnp.exp2(m - m_new)
            p = jnp.exp2((s - m_new).astype(BF16))
            out.append((m_new, alpha * acc + _dot(p, vb)))
        return out

    for tile in range(q_ref.shape[0] // tq):
        n_chunks = tile + 1
        carry = [(jnp.full((sub, 1), NEG, F32), jnp.zeros((sub, MLA_V + LANES), F32)) for _ in parts]
        s = scores(tile, 0, n_chunks == 1)
        for c in range(n_chunks):
            s_next = scores(tile, c + 1, c + 2 == n_chunks) if c + 1 < n_chunks else None
            carry = update(c, s, carry)
            s = s_next
        for part, (_, acc) in enumerate(carry):
            q0 = tile * tq + part * sub
            o_ref[q0:q0 + sub, :] = (acc[:, :MLA_V] / acc[:, MLA_V:]).astype(o_ref.dtype)


def _mla_flash(q, kv, kpe, batch, seq):
    h_ = MLA_HEADS
    qw = MLA_NOPE + LANES
    return pl.pallas_call(
        _mla_flash_body,
        grid=(batch, h_),
        in_specs=[pl.BlockSpec((seq, qw), lambda b, h: (b, h)),
                  pl.BlockSpec((seq, MLA_NOPE), lambda b, h: (b, h)),
                  pl.BlockSpec((seq, LANES), lambda b, h: (b, 0)),
                  pl.BlockSpec((seq, MLA_V), lambda b, h: (b, h_ + h))],
        out_specs=pl.BlockSpec((seq, MLA_V), lambda b, h: (b, h)),
        out_shape=jax.ShapeDtypeStruct((batch * seq, h_ * MLA_V), BF16),
        scratch_shapes=[pltpu.VMEM((seq, qw), BF16), pltpu.VMEM((seq, MLA_V + LANES), BF16)],
        compiler_params=_cparams(("parallel", "parallel")),
        name="mla_flash",
    )(q, kv, kpe, kv)


def _mla(x, w_down, norm_q, norm_kv, w_uq, w_ukv, batch, seq):
    h_ = MLA_HEADS
    half = MLA_ROPE // 2

    def pe_tile(w):
        z = jnp.zeros(w.shape[:-1] + (ROPE_PARTNER - half,), w.dtype)
        return jnp.concatenate([w[..., :half], z, w[..., half:], z], axis=-1)

    n_lat = MLA_Q_RANK + MLA_KV_RANK
    wd = jnp.concatenate([w_down[:, :n_lat], pe_tile(w_down[:, n_lat:])], axis=1).astype(BF16)
    wq = w_uq.reshape(MLA_Q_RANK, h_, MLA_NOPE + MLA_ROPE)
    wq = jnp.concatenate([wq[..., :MLA_NOPE], pe_tile(wq[..., MLA_NOPE:])], axis=-1)
    wq = wq.reshape(MLA_Q_RANK, -1).astype(BF16)
    wkv = w_ukv.reshape(MLA_KV_RANK, h_, MLA_NOPE + MLA_V)
    wkv = jnp.concatenate([wkv[:, :, :MLA_NOPE].reshape(MLA_KV_RANK, -1),
                           wkv[:, :, MLA_NOPE:].reshape(MLA_KV_RANK, -1)], axis=1).astype(BF16)
    scale = (MLA_NOPE + MLA_ROPE) ** -0.5 * math.log2(math.e)
    tk = _rope_tables(seq, MLA_ROPE, MLA_THETA, 1.0, 0.0)
    tq = _rope_tables(seq, MLA_ROPE, MLA_THETA, scale, 0.0)
    tabs = [np.stack([a, b]) for a, b in zip(tk, tq)]
    q, kv, kpe = _mla_proj(x, wd, norm_q[None, :], norm_kv[None, :], wq, wkv, tabs, seq, scale)
    return _mla_flash(q, kv, kpe, batch, seq)


def _head_sum(z, ones_bd):
    return _dot(z.astype(BF16), ones_bd)


def _rwkv_prep_body(x_ref, xp_ref, mu_ref, wr_ref, wk_ref, wv_ref, la0_ref, lb0_ref, la1_ref,
                    lb1_ref, ga_ref, gb_ref, vec_ref, bd_ref, r_ref, lw_ref, k_ref, v_ref,
                    kk_ref, b_ref, g_ref, *, tiles_per_seq):
    x = x_ref[...]
    tm = x.shape[0]
    first = pl.program_id(0) % tiles_per_seq == 0
    prev_row = jnp.where(first, 0.0, xp_ref[7:8, :])
    rows = lax.broadcasted_iota(jnp.int32, x.shape, 0)
    shifted = jnp.where(rows == 0, prev_row, pltpu.roll(x, 1, 0))
    xb = x.astype(BF16)
    xxb = (shifted - x).astype(BF16)
    mix = lambda i: xb + xxb * mu_ref[i:i + 1, :].astype(BF16)
    w0, a0, k_k, k_a = (vec_ref[i:i + 1, :] for i in range(4))
    wl = w0 + _dot(jnp.tanh(_dot(mix(1), la0_ref[...])).astype(BF16), lb0_ref[...])
    al = a0 + _dot(_dot(mix(4), la1_ref[...]).astype(BF16), lb1_ref[...])
    gl = _dot(mix(5), ga_ref[...])
    k_raw = _dot(mix(2), wk_ref[...])
    lw_ref[...] = -math.exp(-0.5) / (1.0 + jnp.exp(-wl))
    a = 1.0 / (1.0 + jnp.exp(-al))
    g_ref[...] = _dot((1.0 / (1.0 + jnp.exp(-gl))).astype(BF16), gb_ref[...]).astype(g_ref.dtype)
    v_ref[...] = _dot(mix(3), wv_ref[...]).astype(v_ref.dtype)
    kk = k_raw * k_k
    bd = bd_ref[...]
    for s in range(D_MODEL // LANES):
        sl = slice(s * LANES, (s + 1) * LANES)
        t = kk[:, sl]
        t = t * lax.rsqrt(jnp.maximum(_head_sum(t * t, bd), 1e-24))
        kk_ref[:, sl] = t.astype(kk_ref.dtype)
        b_ref[:, sl] = (t * a[:, sl]).astype(b_ref.dtype)
    k_ref[...] = (k_raw * (1.0 + (a - 1.0) * k_a)).astype(k_ref.dtype)
    r_ref[...] = _dot(mix(0), wr_ref[...]).astype(r_ref.dtype)


def _head_ones():
    idx = jnp.arange(LANES) // RWKV_HEAD
    return (idx[:, None] == idx[None, :]).astype(BF16)


def _rwkv_prep(x, mu, w_rkv, vec, lora_a, lora_b, gate_a, gate_b, seq, *, tm=256):
    m, d = x.shape
    gpad = RWKV_GATE_PAD - gate_a.shape[1]
    ga = jnp.pad(gate_a, ((0, 0), (0, gpad))).astype(BF16)
    gb = jnp.pad(gate_b, ((0, gpad), (0, 0))).astype(BF16)
    wts = [w_rkv[0].astype(BF16), w_rkv[1].astype(BF16), w_rkv[2].astype(BF16),
           lora_a[0].astype(BF16), lora_b[0].astype(BF16), lora_a[1].astype(BF16),
           lora_b[1].astype(BF16), ga, gb]
    vec8 = jnp.pad(vec, ((0, 3), (0, 0)))
    mu8 = jnp.pad(mu, ((0, 2), (0, 0)))
    row = lambda i: (i, 0)
    sub = tm // 8
    out = jax.ShapeDtypeStruct((m, d), BF16)
    return pl.pallas_call(
        functools.partial(_rwkv_prep_body, tiles_per_seq=seq // tm),
        grid=(m // tm,),
        in_specs=[pl.BlockSpec((tm, d), row),
                  pl.BlockSpec((8, d), lambda i: (jnp.maximum(i * sub - 1, 0), 0)),
                  _resident(mu8.shape)] + [_resident(w.shape) for w in wts]
        + [_resident(vec8.shape), _resident((LANES, LANES))],
        out_specs=[pl.BlockSpec((tm, d), row)] * 7,
        out_shape=[out, jax.ShapeDtypeStruct((m, d), F32), out, out, out, out, out],
        compiler_params=_cparams(("parallel",)),
        name="rwkv_prep",
    )(x, x, mu8, *wts, vec8, _head_ones())


def _rwkv_wkv_body(r_ref, lw_ref, k_ref, v_ref, kk_ref, b_ref, tri_ref, y_ref, state_ref):
    c = RWKV_CHUNK
    two = 2 * c

    @pl.when(pl.program_id(1) == 0)
    def _():
        state_ref[...] = jnp.zeros(state_ref.shape, F32)

    tri = tri_ref[...]

    def decayed(bi):
        lw = lw_ref[bi]
        h1 = lw.astype(BF16)
        r1 = lw - h1.astype(F32)
        h2 = r1.astype(BF16)
        h3 = (r1 - h2.astype(F32)).astype(BF16)
        cum = _dot(tri, h1) + _dot(tri, h2) + _dot(tri, h3)
        gam = jnp.exp(cum)
        gam_inv = jnp.exp(-cum)
        return (r_ref[bi].astype(F32) * gam, kk_ref[bi].astype(F32) * jnp.exp(cum - lw),
                b_ref[bi].astype(F32) * gam_inv, k_ref[bi].astype(F32) * gam_inv, gam[c - 1:c, :])

    lane_lo = lax.broadcasted_iota(jnp.int32, (c, LANES), 1) < RWKV_HEAD
    row2 = lax.broadcasted_iota(jnp.int32, (two, two), 0)
    col2 = lax.broadcasted_iota(jnp.int32, (two, two), 1)
    same = (row2 // c) == (col2 // c)
    strict = jnp.logical_and(same, row2 > col2)
    incl = jnp.logical_and(same, row2 >= col2)
    eye = (row2 == col2).astype(F32)

    def stack_masked(t):
        return jnp.concatenate([jnp.where(lane_lo, t, 0.0), jnp.where(lane_lo, 0.0, t)], axis=0)

    nb = r_ref.shape[0]
    items = [(bi, p) for bi in range(nb) for p in range(D_MODEL // LANES)]
    idx = range(len(items))
    sls = [slice(p * LANES, (p + 1) * LANES) for _, p in items]
    dec = [decayed(bi) for bi in range(nb)]
    xs, bds, kds, vss, gend = [], [], [], [], []
    for (bi, _), sl in zip(items, sls):
        rt, kkt, bt, kt, gam_end = dec[bi]
        xs.append(jnp.concatenate([stack_masked(kkt[:, sl]), stack_masked(rt[:, sl])], axis=0).astype(BF16))
        bds.append(jnp.concatenate([bt[:, sl], bt[:, sl]], axis=0).astype(BF16))
        kds.append(jnp.concatenate([kt[:, sl], kt[:, sl]], axis=0).astype(BF16))
        v2 = v_ref[bi, :, sl].astype(F32)
        vss.append(jnp.where(same, jnp.concatenate([v2, v2], axis=0), 0.0).astype(BF16))
        gend.append(gam_end[:, sl])
    s2s = [state_ref[bi, p] for bi, p in items]
    a_all = [_dot_nt(xs[i], jnp.concatenate([bds[i], kds[i]], axis=0)) for i in idx]
    xs_state = [_dot_nt(xs[i], s2s[i].astype(BF16)) for i in idx]
    nmat = [jnp.where(strict, -a[:two, :two], 0.0) for a in a_all]
    lk = [jnp.where(strict, a[:two, two:], 0.0).astype(BF16) for a in a_all]
    arb = [jnp.where(incl, a[two:, :two], 0.0).astype(BF16) for a in a_all]
    ark = [jnp.where(incl, a[two:, two:], 0.0).astype(BF16) for a in a_all]
    rhs = [xs_state[i][:two] + _dot(lk[i], vss[i]) for i in idx]
    pw = [n_.astype(BF16) for n_ in nmat]
    inv = [eye + n_ for n_ in nmat]
    pw = [_dot(t, t).astype(BF16) for t in pw]
    for _ in range(int(math.log2(c)) - 2):
        both = [_dot(pw[i], jnp.concatenate([pw[i], inv[i].astype(BF16)], axis=1)) for i in idx]
        inv = [inv[i] + both[i][:, two:] for i in idx]
        pw = [t[:, :two].astype(BF16) for t in both]
    inv = [inv[i] + _dot(pw[i], inv[i].astype(BF16)) for i in idx]
    ub = [(-_dot(inv[i].astype(BF16), rhs[i].astype(BF16))).astype(BF16) for i in idx]
    for i, (bi, _) in enumerate(items):
        ys = xs_state[i][two:] + _dot(arb[i], ub[i]) + _dot(ark[i], vss[i])
        y_ref[bi, :, sls[i]] = ys[:c] + ys[c:]
    for i, (bi, p) in enumerate(items):
        ds = _dot_tn(ub[i], bds[i]) + _dot_tn(vss[i], kds[i])
        state_ref[bi, p] = jnp.where(same, (s2s[i] + ds) * gend[i], 0.0)


def _rwkv_wkv(r, lw, k, v, kk, b, batch, seq):
    c = RWKV_CHUNK
    n = seq // c
    d = D_MODEL
    tri = (jnp.arange(c)[:, None] >= jnp.arange(c)[None, :]).astype(BF16)
    nb = RWKV_BATCH_ROWS if batch % RWKV_BATCH_ROWS == 0 else 1
    blk = pl.BlockSpec((nb, c, d), lambda bi, i: (bi, i, 0))
    as3d = lambda t: t.reshape(batch, seq, d)
    y = pl.pallas_call(
        _rwkv_wkv_body,
        grid=(batch // nb, n),
        in_specs=[blk] * 6 + [_resident((c, c))],
        out_specs=blk,
        out_shape=jax.ShapeDtypeStruct((batch, seq, d), F32),
        scratch_shapes=[pltpu.VMEM((nb, d // LANES, LANES, LANES), F32)],
        compiler_params=_cparams(("parallel", "arbitrary")),
        name="rwkv_wkv",
    )(as3d(r), as3d(lw), as3d(k), as3d(v), as3d(kk), as3d(b), tri)
    return y.reshape(batch * seq, d)


def _rwkv_out_body(y_ref, r_ref, k_ref, v_ref, g_ref, vec_ref, bd_ref, w_ref, res_ref, lg_ref,
                   lb_ref, o_ref, a_ref):
    bd = bd_ref[...]
    inv_n = 1.0 / RWKV_HEAD
    for s in range(D_MODEL // LANES):
        sl = slice(s * LANES, (s + 1) * LANES)
        y = y_ref[:, sl]
        mu = _head_sum(y, bd) * inv_n
        dlt = y - mu
        var = _head_sum(dlt * dlt, bd) * inv_n
        yn = dlt * lax.rsqrt(var + RWKV_GN_EPS) * vec_ref[0:1, sl] + vec_ref[1:2, sl]
        rk = r_ref[:, sl] * k_ref[:, sl] * vec_ref[2:3, sl].astype(BF16)
        bonus = _head_sum(rk, bd) * v_ref[:, sl].astype(F32)
        a_ref[:, sl] = ((yn + bonus) * g_ref[:, sl].astype(F32)).astype(BF16)
    acc = _dot(a_ref[...], w_ref[...])
    o_ref[...] = _layer_norm(DN_ALPHA * res_ref[...] + acc, lg_ref[...], lb_ref[...])


def _rwkv_out(y, r, k, v, g, vec, w, res, lg, lb, *, tm=512):
    m, d = res.shape
    row = lambda i: (i, 0)
    act = pl.BlockSpec((tm, d), row)
    return pl.pallas_call(
        _rwkv_out_body,
        grid=(m // tm,),
        in_specs=[act] * 5 + [_resident(vec.shape), _resident((LANES, LANES)), _resident((d, d)),
                              act, _resident((1, d)), _resident((1, d))],
        out_specs=act,
        out_shape=jax.ShapeDtypeStruct((m, d), F32),
        scratch_shapes=[pltpu.VMEM((tm, d), BF16)],
        compiler_params=_cparams(("parallel",)),
        name="rwkv_out",
    )(y, r, k, v, g, vec, _head_ones(), w, res, lg, lb)


def kernel(x, ret_w_in, ret_gn, ret_w_out, dil_w_in, dil_w_out, mla_w_down, mla_norm_q,
           mla_norm_kv, mla_w_uq, mla_w_ukv, mla_w_out, rwkv_mu, rwkv_w_rkv, rwkv_w_out,
           rwkv_vec, rwkv_lora_a, rwkv_lora_b, rwkv_gate_a, rwkv_gate_b, rwkv_ln_x,
           mlp_w1, mlp_w2, ln_g, ln_b):
    batch, seq, d = x.shape
    xf = x.reshape(batch * seq, d)
    n_mixers = 4
    for i in range(DEPTH):
        mixer, j = i % n_mixers, i // n_mixers
        lg, lb = ln_g[i, 0][None, :], ln_b[i, 0][None, :]
        if mixer == 0:
            proj = _mm(xf, ret_w_in[j].astype(BF16), tm=512, name="ret_proj")
            gated = _retention(proj, ret_gn[j], batch, seq)
            xf = _mm_res_ln(gated, ret_w_out[j].astype(BF16), xf, lg, lb, name="ret_out")
        elif mixer == 1:
            outs, lses = _dilated(xf, dil_w_in[j].astype(BF16), batch, seq)
            xf = _dil_out(outs, lses, dil_w_out[j].astype(BF16), xf, lg, lb, seq)
        elif mixer == 2:
            o = _mla(xf, mla_w_down[j], mla_norm_q[j], mla_norm_kv[j], mla_w_uq[j], mla_w_ukv[j],
                     batch, seq)
            xf = _mm_res_ln(o, mla_w_out[j].astype(BF16), xf, lg, lb, name="mla_out")
        else:
            r, lw, k, v, kk, b, g = _rwkv_prep(xf, rwkv_mu[j], rwkv_w_rkv[j], rwkv_vec[j],
                                               rwkv_lora_a[j], rwkv_lora_b[j], rwkv_gate_a[j],
                                               rwkv_gate_b[j], seq)
            y = _rwkv_wkv(r, lw, k, v, kk, b, batch, seq)
            vec = jnp.concatenate([rwkv_ln_x[j], rwkv_vec[j][4:5],
                                   jnp.zeros((5, d), F32)], axis=0)
            xf = _rwkv_out(y, r, k, v, g, vec, rwkv_w_out[j].astype(BF16), xf, lg, lb)
        xf = _mlp(xf, mlp_w1[i].astype(BF16), mlp_w2[i].astype(BF16),
                  ln_g[i, 1][None, :], ln_b[i, 1][None, :])
    return xf.reshape(batch, seq, d)
```

```python
import functools
import math

import jax
import jax.numpy as jnp
import numpy as np
from jax import lax
from jax.experimental import pallas as pl
from jax.experimental.pallas import tpu as pltpu

F32 = jnp.float32
BF16 = jnp.bfloat16

D_MODEL = 1024
DEPTH = 4
D_FF = 4 * D_MODEL
LN_EPS = 1e-5
RMS_EPS = 1e-6
GN_EPS = 1e-5
DN_ALPHA = (2.0 * DEPTH) ** 0.25
NEG = -1e30
LANES = 128
ROPE_PARTNER = 64
MM_SUB = 256

RET_HEADS = 4
RET_QK_DIM = 256
RET_V_DIM = 512
RET_CHUNK = 128
RET_THETA = 10000.0

DIL_PAIRS = ((128, 1), (512, 4), (2048, 16))
DIL_HEADS = 8
DIL_HEAD_DIM = 128
DIL_ROT = 32
DIL_BLOCK = 128
DIL_TQ = 512
ROPE_THETA = 500000.0

MLA_HEADS = 16
MLA_NOPE = 128
MLA_ROPE = 64
MLA_V = 128
MLA_Q_RANK = 256
MLA_KV_RANK = 128
MLA_THETA = 10000.0
MLA_TQ = 1024
MLA_TK = 1024
MLA_ROW_SPLIT = 4
assert MLA_TQ == MLA_TK

RWKV_HEAD = 64
RWKV_HEADS = D_MODEL // RWKV_HEAD
RWKV_GN_EPS = 64e-5
RWKV_CHUNK = 64
RWKV_GATE_PAD = 256
RWKV_BATCH_ROWS = 4

VMEM_LIMIT = 56 * 1024 * 1024


def _cparams(sem):
    return pltpu.CompilerParams(dimension_semantics=sem, vmem_limit_bytes=VMEM_LIMIT)


def _resident(shape):
    nd = len(shape)
    return pl.BlockSpec(shape, lambda *_: (0,) * nd, pipeline_mode=pl.Buffered(1))


def _layer_norm(z, g, b):
    mu = jnp.mean(z, axis=-1, keepdims=True)
    d = z - mu
    var = jnp.mean(d * d, axis=-1, keepdims=True)
    return d * lax.rsqrt(var + LN_EPS) * g + b


def _dot(a, b):
    return jnp.dot(a, b, preferred_element_type=F32)


def _dot_nt(a, b):
    return lax.dot_general(a, b, (((1,), (1,)), ((), ())), preferred_element_type=F32)


def _dot_tn(a, b):
    return lax.dot_general(a, b, (((0,), (0,)), ((), ())), preferred_element_type=F32)


def _rope_tile(a, c, s):
    return a * c + pltpu.roll(a, ROPE_PARTNER, 1) * s


def _project_columns(xb, w_ref, store, modes, tabs, scale):
    n = w_ref.shape[1]
    for c0 in range(0, n, MM_SUB):
        acc = _dot(xb, w_ref[:, c0:c0 + MM_SUB])
        for t in range(MM_SUB // LANES):
            a = acc[:, t * LANES:(t + 1) * LANES]
            mode = modes[c0 // LANES + t] if modes is not None else None
            if mode == "scale":
                a = a * scale
            elif mode is not None:
                c_ref, s_ref = tabs
                a = _rope_tile(a, c_ref[mode[1]], s_ref[mode[1]])
            store(c0 + t * LANES, a)


def _mm_body(x_ref, w_ref, *rest, modes, scale):
    tabs, o_ref = rest[:-1], rest[-1]

    def store(c0, a):
        o_ref[:, c0:c0 + LANES] = a.astype(o_ref.dtype)

    _project_columns(x_ref[...].astype(BF16), w_ref, store, modes, tabs, scale)


def _mm(x, w, *, tm, out_dtype=BF16, tabs=None, tab_map=None, modes=None, scale=1.0, name="mm"):
    m = x.shape[0]
    k, n = w.shape
    row = lambda i: (i, 0)
    in_specs = [pl.BlockSpec((tm, k), row), _resident((k, n))]
    args = [x, w]
    if tabs is not None:
        for t in tabs:
            in_specs.append(pl.BlockSpec((t.shape[0], tm, LANES), tab_map))
            args.append(t)
    return pl.pallas_call(
        functools.partial(_mm_body, modes=modes, scale=scale),
        grid=(m // tm,),
        in_specs=in_specs,
        out_specs=pl.BlockSpec((tm, n), row),
        out_shape=jax.ShapeDtypeStruct((m, n), out_dtype),
        compiler_params=_cparams(("parallel",)),
        name=name,
    )(*args)


def _mm_res_ln_body(a_ref, w_ref, res_ref, g_ref, b_ref, o_ref):
    half = a_ref.shape[0] // 2
    for r0 in (0, half):
        acc = _dot(a_ref[r0:r0 + half, :], w_ref[...])
        o_ref[r0:r0 + half, :] = _layer_norm(DN_ALPHA * res_ref[r0:r0 + half, :] + acc,
                                             g_ref[...], b_ref[...])


def _mm_res_ln(a, w, res, g, b, *, tm=512, name="mm_res_ln"):
    m, k = a.shape
    d = w.shape[1]
    row = lambda i: (i, 0)
    return pl.pallas_call(
        _mm_res_ln_body,
        grid=(m // tm,),
        in_specs=[pl.BlockSpec((tm, k), row), _resident((k, d)), pl.BlockSpec((tm, d), row),
                  _resident((1, d)), _resident((1, d))],
        out_specs=pl.BlockSpec((tm, d), row),
        out_shape=jax.ShapeDtypeStruct((m, d), F32),
        compiler_params=_cparams(("parallel",)),
        name=name,
    )(a, w, res, g, b)


def _mlp_body(x_ref, w1_ref, w2_ref, g_ref, b_ref, o_ref, *, fchunk):
    half = x_ref.shape[0] // 2
    for r0 in (0, half):
        x = x_ref[r0:r0 + half, :]
        xb = x.astype(BF16)
        acc = jnp.zeros(x.shape, F32)
        for c in range(D_FF // fchunk):
            h = _dot(xb, w1_ref[:, c * fchunk:(c + 1) * fchunk])
            h = jnp.maximum(h, 0.0)
            h = (h * h).astype(BF16)
            acc = acc + _dot(h, w2_ref[c * fchunk:(c + 1) * fchunk, :])
        o_ref[r0:r0 + half, :] = _layer_norm(DN_ALPHA * x + acc, g_ref[...], b_ref[...])


def _mlp(x, w1, w2, g, b, *, tm=512, fchunk=1024):
    m, d = x.shape
    row = lambda i: (i, 0)
    return pl.pallas_call(
        functools.partial(_mlp_body, fchunk=fchunk),
        grid=(m // tm,),
        in_specs=[pl.BlockSpec((tm, d), row), _resident((d, D_FF)), _resident((D_FF, d)),
                  _resident((1, d)), _resident((1, d))],
        out_specs=pl.BlockSpec((tm, d), row),
        out_shape=jax.ShapeDtypeStruct((m, d), F32),
        compiler_params=_cparams(("parallel",)),
        name="mlp",
    )(x, w1, w2, g, b)


def _ret_body(q_ref, k_ref, v_ref, g_ref, cos_ref, sin_ref, intra_ref, qdec_ref, kdec_ref,
              cdec_ref, gn_ref, o_ref, state_ref):
    dk, dv, half = RET_QK_DIM, RET_V_DIM, RET_QK_DIM // 2

    @pl.when(pl.program_id(1) == 0)
    def _():
        state_ref[...] = jnp.zeros(state_ref.shape, F32)

    cos = cos_ref[...]
    sin = sin_ref[...]

    def rope(t):
        t1 = t[:, :half].astype(F32)
        t2 = t[:, half:].astype(F32)
        return jnp.concatenate([t1 * cos - t2 * sin, t2 * cos + t1 * sin], axis=-1)

    heads = range(RET_HEADS)
    q = [rope(q_ref[:, h * dk:(h + 1) * dk]) for h in heads]
    k = [rope(k_ref[:, h * dk:(h + 1) * dk]) * (dk ** -0.5) for h in heads]
    v = [v_ref[:, h * dv:(h + 1) * dv] for h in heads]
    qb = [t.astype(BF16) for t in q]
    state = [state_ref[h] for h in heads]
    scores = [(_dot_nt(qb[h], k[h].astype(BF16)) * intra_ref[h]).astype(BF16) for h in heads]
    cross = [_dot(qb[h], state[h].astype(BF16)) * qdec_ref[h] for h in heads]
    o = [_dot(scores[h], v[h]) + cross[h] for h in heads]
    for h in heads:
        state_ref[h] = (state[h] * cdec_ref[h, 0:1, :]
                        + _dot_tn((k[h] * kdec_ref[h]).astype(BF16), v[h]))
    for h in heads:
        sl = slice(h * dv, (h + 1) * dv)
        mu = jnp.mean(o[h], axis=-1, keepdims=True)
        d = o[h] - mu
        var = jnp.mean(d * d, axis=-1, keepdims=True)
        on = d * lax.rsqrt(var + GN_EPS) * gn_ref[0:1, sl] + gn_ref[1:2, sl]
        gate = g_ref[:, sl].astype(F32)
        gate = gate * (1.0 / (1.0 + jnp.exp(-gate)))
        o_ref[:, sl] = (gate * on).astype(o_ref.dtype)


def _retention(proj, gn, batch, seq):
    h_, dk, dv, c = RET_HEADS, RET_QK_DIM, RET_V_DIM, RET_CHUNK
    n = seq // c
    half = dk // 2
    f32 = np.float32
    pos = np.arange(seq, dtype=f32)
    inv_freq = f32(RET_THETA) ** (-np.arange(half, dtype=f32) / f32(half))
    ang = pos[:, None] * inv_freq[None, :]
    cos, sin = np.cos(ang), np.sin(ang)
    log_gamma = np.log(f32(1.0) - f32(2.0) ** (f32(-5.0) - np.arange(h_, dtype=f32)))
    idx = np.arange(c, dtype=f32)
    diff = idx[:, None] - idx[None, :]
    intra = np.where(diff >= 0, np.exp(log_gamma[:, None, None] * np.maximum(diff, f32(0.0))), f32(0.0))
    intra = intra.astype(f32)
    qdec = np.broadcast_to(np.exp(log_gamma[:, None] * (idx + f32(1.0)))[:, :, None], (h_, c, dv))
    kdec = np.broadcast_to(np.exp(log_gamma[:, None] * (f32(c - 1.0) - idx))[:, :, None], (h_, c, dk))
    cdec = np.broadcast_to(np.exp(log_gamma * f32(c))[:, None, None], (h_, 8, dv))
    qdec, kdec, cdec = (np.ascontiguousarray(t, dtype=f32) for t in (qdec, kdec, cdec))
    qk_w, vg_w = h_ * dk, h_ * dv
    return pl.pallas_call(
        _ret_body,
        grid=(batch, n),
        in_specs=[
            pl.BlockSpec((c, qk_w), lambda b, i: (b * n + i, 0)),
            pl.BlockSpec((c, qk_w), lambda b, i: (b * n + i, 1)),
            pl.BlockSpec((c, vg_w), lambda b, i: (b * n + i, 2 * qk_w // vg_w)),
            pl.BlockSpec((c, vg_w), lambda b, i: (b * n + i, 2 * qk_w // vg_w + 1)),
            pl.BlockSpec((c, half), lambda b, i: (i, 0)),
            pl.BlockSpec((c, half), lambda b, i: (i, 0)),
            _resident((h_, c, c)), _resident((h_, c, dv)), _resident((h_, c, dk)),
            _resident((h_, 8, dv)), _resident((2, vg_w)),
        ],
        out_specs=pl.BlockSpec((c, vg_w), lambda b, i: (b * n + i, 0)),
        out_shape=jax.ShapeDtypeStruct((batch * seq, vg_w), BF16),
        scratch_shapes=[pltpu.VMEM((h_, dk, dv), F32)],
        compiler_params=_cparams(("parallel", "arbitrary")),
        name="retention",
    )(proj, proj, proj, proj, cos, sin, intra, qdec, kdec, cdec, gn)


def _dil_attn_body(q_ref, kp_ref, kc_ref, vp_ref, vc_ref, o_ref, lse_ref):
    blk = DIL_BLOCK
    n_sub = q_ref.shape[0] // blk
    has_prev = pl.program_id(1) > 0
    qi = lax.broadcasted_iota(jnp.int32, (blk, 2 * blk), 0)
    ki = lax.broadcasted_iota(jnp.int32, (blk, 2 * blk), 1)
    band = jnp.logical_and(ki >= qi, ki <= qi + blk)
    band_first = jnp.logical_and(band, jnp.logical_or(ki >= blk, has_prev))
    lane = lax.broadcasted_iota(jnp.int32, (blk, LANES), 1)
    ones = jnp.ones((2 * blk, LANES), BF16)
    items = [(j, h) for j in range(n_sub) for h in range(DIL_HEADS)]
    sls = [slice(h * DIL_HEAD_DIM, (h + 1) * DIL_HEAD_DIM) for _, h in items]
    rows = [slice(j * blk, (j + 1) * blk) for j, _ in items]

    def keys(prev_ref, cur_ref, j, sl):
        if j == 0:
            return jnp.concatenate([prev_ref[:, sl], cur_ref[:blk, sl]], axis=0)
        return cur_ref[(j - 1) * blk:(j + 1) * blk, sl]

    s = [jnp.where(band_first if j == 0 else band,
                   _dot_nt(q_ref[rows[i], sls[i]], keys(kp_ref, kc_ref, j, sls[i])), NEG)
         for i, (j, _) in enumerate(items)]
    m = [jnp.max(t, axis=-1, keepdims=True) for t in s]
    p = [jnp.exp((s[i] - m[i]).astype(BF16)) for i in range(len(items))]
    pv = [_dot(p[i], jnp.concatenate([keys(vp_ref, vc_ref, j, sls[i]), ones], axis=1))
          for i, (j, _) in enumerate(items)]
    lse_tiles = [jnp.zeros((blk, LANES), F32) for _ in range(n_sub)]
    for i, (j, h) in enumerate(items):
        l = pv[i][:, DIL_HEAD_DIM:]
        o_ref[rows[i], sls[i]] = (pv[i][:, :DIL_HEAD_DIM] / l).astype(o_ref.dtype)
        lse_tiles[j] = jnp.where(lane == h, m[i] + jnp.log(l), lse_tiles[j])
    for j in range(n_sub):
        lse_ref[j * blk:(j + 1) * blk, :] = lse_tiles[j]


def _dil_attn(qkv, batch, seq, dil):
    blk = DIL_BLOCK
    hd = DIL_HEADS * DIL_HEAD_DIM
    sub = seq // dil
    tq = min(DIL_TQ, sub)
    nb = sub // tq
    per = tq // blk
    cur = lambda c: (lambda z, i: (z * nb + i, c))
    prev = lambda c: (lambda z, i: (jnp.maximum((z * nb + i) * per - 1, 0), c))
    out_map = lambda z, i: (z * nb + i, 0)
    return pl.pallas_call(
        _dil_attn_body,
        grid=(batch * dil, nb),
        in_specs=[pl.BlockSpec((tq, hd), cur(0)), pl.BlockSpec((blk, hd), prev(1)),
                  pl.BlockSpec((tq, hd), cur(1)), pl.BlockSpec((blk, hd), prev(2)),
                  pl.BlockSpec((tq, hd), cur(2))],
        out_specs=[pl.BlockSpec((tq, hd), out_map), pl.BlockSpec((tq, LANES), out_map)],
        out_shape=[jax.ShapeDtypeStruct((batch * seq, hd), BF16),
                   jax.ShapeDtypeStruct((batch * seq, LANES), F32)],
        compiler_params=_cparams(("parallel", "arbitrary")),
        name=f"dil_attn_{dil}",
    )(qkv, qkv, qkv, qkv, qkv)


def _dil_out_body(o0_ref, o1_ref, o2_ref, l0_ref, l1_ref, l2_ref, e_ref, w_ref, res_ref,
                  g_ref, b_ref, o_ref, osc_ref, lsc_ref, mix_ref):
    tm = o_ref.shape[0]
    nh = DIL_HEADS
    for gi, (og, lg) in enumerate(((o0_ref, l0_ref), (o1_ref, l1_ref), (o2_ref, l2_ref))):
        dil = og.shape[1]
        n = tm // dil
        for r in range(dil):
            rows = pl.ds(r, n, stride=dil) if dil > 1 else slice(None)
            for h in range(nh):
                osc_ref[gi * nh + h, rows, :] = og[0, r, :, h * LANES:(h + 1) * LANES].astype(F32)
            lsc_ref[gi, rows, :] = lg[0, r]
    l0, l1, l2 = lsc_ref[0], lsc_ref[1], lsc_ref[2]
    m = jnp.maximum(jnp.maximum(l0, l1), l2)
    e0, e1, e2 = jnp.exp(l0 - m), jnp.exp(l1 - m), jnp.exp(l2 - m)
    den = e0 + e1 + e2
    e = e_ref[...]
    wfull = []
    for eg in (e0, e1, e2):
        wfull.append(_dot((eg / den).astype(BF16), e))
    for h in range(nh):
        sl = slice(h * LANES, (h + 1) * LANES)
        mixed = sum(wfull[gi][:, sl] * osc_ref[gi * nh + h] for gi in range(3))
        mix_ref[:, sl] = mixed.astype(BF16)
    acc = _dot(mix_ref[...], w_ref[...])
    o_ref[...] = _layer_norm(DN_ALPHA * res_ref[...] + acc, g_ref[...], b_ref[...])


def _dil_out(outs, lses, w, res, g, b, seq, *, tm=512):
    m, d = res.shape
    hd = DIL_HEADS * DIL_HEAD_DIM
    expand = (jnp.arange(LANES)[:, None] == (jnp.arange(hd) // DIL_HEAD_DIM)[None, :]).astype(BF16)
    row = lambda i: (i, 0)
    nt = seq // tm
    grp = lambda i: (i // nt, 0, i % nt, 0)
    dils = [dil for _, dil in DIL_PAIRS]
    batch = m // seq
    o4 = [o.reshape(batch, dil, seq // dil, hd) for o, dil in zip(outs, dils)]
    l4 = [l.reshape(batch, dil, seq // dil, LANES) for l, dil in zip(lses, dils)]
    return pl.pallas_call(
        _dil_out_body,
        grid=(m // tm,),
        in_specs=[pl.BlockSpec((1, dil, tm // dil, hd), grp) for dil in dils]
        + [pl.BlockSpec((1, dil, tm // dil, LANES), grp) for dil in dils]
        + [_resident((LANES, hd)), _resident((hd, d)), pl.BlockSpec((tm, d), row),
           _resident((1, d)), _resident((1, d))],
        out_specs=pl.BlockSpec((tm, d), row),
        out_shape=jax.ShapeDtypeStruct((m, d), F32),
        scratch_shapes=[pltpu.VMEM((len(dils) * DIL_HEADS, tm, LANES), F32),
                        pltpu.VMEM((len(dils), tm, LANES), F32), pltpu.VMEM((tm, hd), BF16)],
        compiler_params=_cparams(("parallel",)),
        name="dil_out",
    )(*o4, *l4, expand, w, res, g, b)


def _rope_tables(seq, rot, theta, scale, passthrough):
    half = rot // 2
    inv_freq = np.float32(theta) ** (-np.arange(half, dtype=np.float32) / np.float32(half))
    ang = np.arange(seq, dtype=np.float32)[:, None] * inv_freq[None, :]
    cos, sin = np.cos(ang), np.sin(ang)
    fill = np.full((seq, ROPE_PARTNER - half), passthrough, np.float32)
    zero = np.zeros((seq, ROPE_PARTNER - half), np.float32)
    c = np.concatenate([cos, fill, cos, fill], axis=1)
    s = np.concatenate([-sin, zero, sin, zero], axis=1)
    return (c * np.float32(scale)).astype(np.float32), (s * np.float32(scale)).astype(np.float32)


def _rope_lane_order(rot, width):
    half = rot // 2
    rest = list(range(rot, width))
    cut = ROPE_PARTNER - half
    return jnp.array(list(range(half)) + rest[:cut] + list(range(half, rot)) + rest[cut:])


def _dil_proj_body(x_ref, w_ref, c_ref, s_ref, o_ref, xb_ref, xs_ref, *, dil):
    tm = x_ref.shape[0]
    n = tm // dil
    if dil == 1:
        xb = x_ref[...].astype(BF16)
    else:
        for c in range(D_MODEL // LANES):
            xs_ref[c] = x_ref[:, c * LANES:(c + 1) * LANES]
        for r in range(dil):
            for c in range(D_MODEL // LANES):
                xb_ref[r * n:(r + 1) * n, c * LANES:(c + 1) * LANES] = (
                    xs_ref[c, pl.ds(r, n, stride=dil), :].astype(BF16))
        xb = xb_ref[...]

    def store(c0, a):
        o_ref[0, :, :, c0:c0 + LANES] = a.astype(o_ref.dtype).reshape(dil, n, LANES)

    modes = [("rope", 0)] * DIL_HEADS + [("rope", 1)] * DIL_HEADS + [None] * DIL_HEADS
    _project_columns(xb, w_ref, store, modes, (c_ref, s_ref), 1.0)


def _dil_proj(x, w, tabs, batch, seq, dil, *, tm=512):
    hd = DIL_HEADS * DIL_HEAD_DIM
    nt = seq // tm
    n = tm // dil
    tab_spec = pl.BlockSpec((2, tm, LANES), lambda i: (0, i % nt, 0))
    return pl.pallas_call(
        functools.partial(_dil_proj_body, dil=dil),
        grid=(batch * nt,),
        in_specs=[pl.BlockSpec((tm, D_MODEL), lambda i: (i, 0)), _resident((D_MODEL, 3 * hd)),
                  tab_spec, tab_spec],
        out_specs=pl.BlockSpec((1, dil, n, 3 * hd), lambda i: (i // nt, 0, i % nt, 0)),
        out_shape=jax.ShapeDtypeStruct((batch, dil, seq // dil, 3 * hd), BF16),
        scratch_shapes=[pltpu.VMEM((tm, D_MODEL), BF16),
                        pltpu.VMEM((D_MODEL // LANES, tm, LANES), F32)],
        compiler_params=_cparams(("parallel",)),
        name=f"dil_proj_{dil}",
    )(x, w, *tabs)


def _dilated(x, w_in, batch, seq, *, tm=512):
    hd = DIL_HEADS * DIL_HEAD_DIM
    cq, sq = _rope_tables(seq, DIL_ROT, ROPE_THETA, DIL_HEAD_DIM ** -0.5, 1.0)
    ck, sk = _rope_tables(seq, DIL_ROT, ROPE_THETA, 1.0, 1.0)
    order = _rope_lane_order(DIL_ROT, DIL_HEAD_DIM)
    outs, lses = [], []
    for gi, (_, dil) in enumerate(DIL_PAIRS):
        wg = w_in[:, gi * 3 * hd:(gi + 1) * 3 * hd].reshape(D_MODEL, 3, DIL_HEADS, DIL_HEAD_DIM)
        wg = jnp.concatenate([wg[:, :2][..., order], wg[:, 2:]], axis=1).reshape(D_MODEL, 3 * hd)

        def by_residue(t):
            t = t.reshape(seq // tm, tm // dil, dil, LANES)
            return np.swapaxes(t, 1, 2).reshape(seq, LANES)

        tabs = [np.stack([by_residue(a), by_residue(b)]) for a, b in ((cq, ck), (sq, sk))]
        qkv = _dil_proj(x, wg, tabs, batch, seq, dil, tm=tm)
        o, lse = _dil_attn(qkv.reshape(batch * seq, 3 * hd), batch, seq, dil)
        outs.append(o)
        lses.append(lse)
    return outs, lses


def _mla_proj_body(x_ref, wd_ref, nq_ref, nkv_ref, wq_ref, wkv_ref, c_ref, s_ref,
                   q_ref, kv_ref, kpe_ref, *, scale):
    acc = _dot(x_ref[...].astype(BF16), wd_ref[...])
    cq = acc[:, :MLA_Q_RANK]
    ckv = acc[:, MLA_Q_RANK:MLA_Q_RANK + MLA_KV_RANK]
    kpe = acc[:, MLA_Q_RANK + MLA_KV_RANK:]
    cq = cq * lax.rsqrt(jnp.mean(cq * cq, axis=-1, keepdims=True) + RMS_EPS) * nq_ref[...]
    ckv = ckv * lax.rsqrt(jnp.mean(ckv * ckv, axis=-1, keepdims=True) + RMS_EPS) * nkv_ref[...]
    kpe_ref[...] = _rope_tile(kpe, c_ref[0], s_ref[0]).astype(kpe_ref.dtype)

    def store_q(c0, a):
        q_ref[:, c0:c0 + LANES] = a.astype(q_ref.dtype)

    def store_kv(c0, a):
        kv_ref[:, c0:c0 + LANES] = a.astype(kv_ref.dtype)

    _project_columns(cq.astype(BF16), wq_ref, store_q, ["scale", ("rope", 1)] * MLA_HEADS,
                     (c_ref, s_ref), scale)
    _project_columns(ckv.astype(BF16), wkv_ref, store_kv, None, None, 1.0)


def _mla_proj(x, wd, nq, nkv, wq, wkv, tabs, seq, scale, *, tm=512):
    m, d = x.shape
    row = lambda i: (i, 0)
    ns = seq // tm
    tab = pl.BlockSpec((2, tm, LANES), lambda i: (0, i % ns, 0))
    return pl.pallas_call(
        functools.partial(_mla_proj_body, scale=scale),
        grid=(m // tm,),
        in_specs=[pl.BlockSpec((tm, d), row), _resident(wd.shape), _resident((1, MLA_Q_RANK)),
                  _resident((1, MLA_KV_RANK)), _resident(wq.shape), _resident(wkv.shape), tab, tab],
        out_specs=[pl.BlockSpec((tm, wq.shape[1]), row), pl.BlockSpec((tm, wkv.shape[1]), row),
                   pl.BlockSpec((tm, LANES), row)],
        out_shape=[jax.ShapeDtypeStruct((m, wq.shape[1]), BF16),
                   jax.ShapeDtypeStruct((m, wkv.shape[1]), BF16),
                   jax.ShapeDtypeStruct((m, LANES), BF16)],
        compiler_params=_cparams(("parallel",)),
        name="mla_proj",
    )(x, wd, nq, nkv, wq, wkv, *tabs)


def _mla_flash_body(q_ref, kn_ref, kpe_ref, v_ref, o_ref, kcat_ref, vaug_ref):
    tq, tk, sub = MLA_TQ, MLA_TK, MLA_TQ // MLA_ROW_SPLIT
    kcat_ref[:, :MLA_NOPE] = kn_ref[...]
    kcat_ref[:, MLA_NOPE:] = kpe_ref[...]
    vaug_ref[:, :MLA_V] = v_ref[...]
    vaug_ref[:, MLA_V:] = jnp.ones((v_ref.shape[0], LANES), BF16)

    parts = range(MLA_ROW_SPLIT)
    col_minus_row = (lax.broadcasted_iota(jnp.int32, (sub, tk), 1)
                     - lax.broadcasted_iota(jnp.int32, (sub, tk), 0))

    def scores(tile, chunk, diagonal):
        out = []
        for part in parts:
            width = (part + 1) * sub if diagonal else tk
            q0 = tile * tq + part * sub
            s = _dot_nt(q_ref[q0:q0 + sub, :], kcat_ref[chunk * tk:chunk * tk + width, :])
            if diagonal:
                s = jnp.where(col_minus_row[:, :width] <= part * sub, s, NEG)
            out.append(s)
        return out

    def update(chunk, s_all, carry):
        out = []
        for s, (m, acc) in zip(s_all, carry):
            vb = vaug_ref[chunk * tk:chunk * tk + s.shape[1], :]
            m_new = jnp.maximum(m, jnp.max(s, axis=-1, keepdims=True))
            alpha = jnp.exp2(m - m_new)
            p = jnp.exp2((s - m_new).astype(BF16))
            out.append((m_new, alpha * acc + _dot(p, vb)))
        return out

    for tile in range(q_ref.shape[0] // tq):
        n_chunks = tile + 1
        carry = [(jnp.full((sub, 1), NEG, F32), jnp.zeros((sub, MLA_V + LANES), F32)) for _ in parts]
        s = scores(tile, 0, n_chunks == 1)
        for c in range(n_chunks):
            s_next = scores(tile, c + 1, c + 2 == n_chunks) if c + 1 < n_chunks else None
            carry = update(c, s, carry)
            s = s_next
        for part, (_, acc) in enumerate(carry):
            q0 = tile * tq + part * sub
            o_ref[q0:q0 + sub, :] = (acc[:, :MLA_V] / acc[:, MLA_V:]).astype(o_ref.dtype)


def _mla_flash(q, kv, kpe, batch, seq):
    h_ = MLA_HEADS
    qw = MLA_NOPE + LANES
    return pl.pallas_call(
        _mla_flash_body,
        grid=(batch, h_),
        in_specs=[pl.BlockSpec((seq, qw), lambda b, h: (b, h)),
                  pl.BlockSpec((seq, MLA_NOPE), lambda b, h: (b, h)),
                  pl.BlockSpec((seq, LANES), lambda b, h: (b, 0)),
                  pl.BlockSpec((seq, MLA_V), lambda b, h: (b, h_ + h))],
        out_specs=pl.BlockSpec((seq, MLA_V), lambda b, h: (b, h)),
        out_shape=jax.ShapeDtypeStruct((batch * seq, h_ * MLA_V), BF16),
        scratch_shapes=[pltpu.VMEM((seq, qw), BF16), pltpu.VMEM((seq, MLA_V + LANES), BF16)],
        compiler_params=_cparams(("parallel", "parallel")),
        name="mla_flash",
    )(q, kv, kpe, kv)


def _mla(x, w_down, norm_q, norm_kv, w_uq, w_ukv, batch, seq):
    h_ = MLA_HEADS
    half = MLA_ROPE // 2

    def pe_tile(w):
        z = jnp.zeros(w.shape[:-1] + (ROPE_PARTNER - half,), w.dtype)
        return jnp.concatenate([w[..., :half], z, w[..., half:], z], axis=-1)

    n_lat = MLA_Q_RANK + MLA_KV_RANK
    wd = jnp.concatenate([w_down[:, :n_lat], pe_tile(w_down[:, n_lat:])], axis=1).astype(BF16)
    wq = w_uq.reshape(MLA_Q_RANK, h_, MLA_NOPE + MLA_ROPE)
    wq = jnp.concatenate([wq[..., :MLA_NOPE], pe_tile(wq[..., MLA_NOPE:])], axis=-1)
    wq = wq.reshape(MLA_Q_RANK, -1).astype(BF16)
    wkv = w_ukv.reshape(MLA_KV_RANK, h_, MLA_NOPE + MLA_V)
    wkv = jnp.concatenate([wkv[:, :, :MLA_NOPE].reshape(MLA_KV_RANK, -1),
                           wkv[:, :, MLA_NOPE:].reshape(MLA_KV_RANK, -1)], axis=1).astype(BF16)
    scale = (MLA_NOPE + MLA_ROPE) ** -0.5 * math.log2(math.e)
    tk = _rope_tables(seq, MLA_ROPE, MLA_THETA, 1.0, 0.0)
    tq = _rope_tables(seq, MLA_ROPE, MLA_THETA, scale, 0.0)
    tabs = [np.stack([a, b]) for a, b in zip(tk, tq)]
    q, kv, kpe = _mla_proj(x, wd, norm_q[None, :], norm_kv[None, :], wq, wkv, tabs, seq, scale)
    return _mla_flash(q, kv, kpe, batch, seq)


def _head_sum(z, ones_bd):
    return _dot(z.astype(BF16), ones_bd)


def _rwkv_prep_body(x_ref, xp_ref, mu_ref, wr_ref, wk_ref, wv_ref, la0_ref, lb0_ref, la1_ref,
                    lb1_ref, ga_ref, gb_ref, vec_ref, bd_ref, r_ref, lw_ref, k_ref, v_ref,
                    kk_ref, b_ref, g_ref, *, tiles_per_seq):
    x = x_ref[...]
    tm = x.shape[0]
    first = pl.program_id(0) % tiles_per_seq == 0
    prev_row = jnp.where(first, 0.0, xp_ref[7:8, :])
    rows = lax.broadcasted_iota(jnp.int32, x.shape, 0)
    shifted = jnp.where(rows == 0, prev_row, pltpu.roll(x, 1, 0))
    xb = x.astype(BF16)
    xxb = (shifted - x).astype(BF16)
    mix = lambda i: xb + xxb * mu_ref[i:i + 1, :].astype(BF16)
    w0, a0, k_k, k_a = (vec_ref[i:i + 1, :] for i in range(4))
    wl = w0 + _dot(jnp.tanh(_dot(mix(1), la0_ref[...])).astype(BF16), lb0_ref[...])
    al = a0 + _dot(_dot(mix(4), la1_ref[...]).astype(BF16), lb1_ref[...])
    gl = _dot(mix(5), ga_ref[...])
    k_raw = _dot(mix(2), wk_ref[...])
    lw_ref[...] = -math.exp(-0.5) / (1.0 + jnp.exp(-wl))
    a = 1.0 / (1.0 + jnp.exp(-al))
    g_ref[...] = _dot((1.0 / (1.0 + jnp.exp(-gl))).astype(BF16), gb_ref[...]).astype(g_ref.dtype)
    v_ref[...] = _dot(mix(3), wv_ref[...]).astype(v_ref.dtype)
    kk = k_raw * k_k
    bd = bd_ref[...]
    for s in range(D_MODEL // LANES):
        sl = slice(s * LANES, (s + 1) * LANES)
        t = kk[:, sl]
        t = t * lax.rsqrt(jnp.maximum(_head_sum(t * t, bd), 1e-24))
        kk_ref[:, sl] = t.astype(kk_ref.dtype)
        b_ref[:, sl] = (t * a[:, sl]).astype(b_ref.dtype)
    k_ref[...] = (k_raw * (1.0 + (a - 1.0) * k_a)).astype(k_ref.dtype)
    r_ref[...] = _dot(mix(0), wr_ref[...]).astype(r_ref.dtype)


def _head_ones():
    idx = jnp.arange(LANES) // RWKV_HEAD
    return (idx[:, None] == idx[None, :]).astype(BF16)


def _rwkv_prep(x, mu, w_rkv, vec, lora_a, lora_b, gate_a, gate_b, seq, *, tm=512):
    m, d = x.shape
    gpad = RWKV_GATE_PAD - gate_a.shape[1]
    ga = jnp.pad(gate_a, ((0, 0), (0, gpad))).astype(BF16)
    gb = jnp.pad(gate_b, ((0, gpad), (0, 0))).astype(BF16)
    wts = [w_rkv[0].astype(BF16), w_rkv[1].astype(BF16), w_rkv[2].astype(BF16),
           lora_a[0].astype(BF16), lora_b[0].astype(BF16), lora_a[1].astype(BF16),
           lora_b[1].astype(BF16), ga, gb]
    vec8 = jnp.pad(vec, ((0, 3), (0, 0)))
    mu8 = jnp.pad(mu, ((0, 2), (0, 0)))
    row = lambda i: (i, 0)
    sub = tm // 8
    out = jax.ShapeDtypeStruct((m, d), BF16)
    return pl.pallas_call(
        functools.partial(_rwkv_prep_body, tiles_per_seq=seq // tm),
        grid=(m // tm,),
        in_specs=[pl.BlockSpec((tm, d), row),
                  pl.BlockSpec((8, d), lambda i: (jnp.maximum(i * sub - 1, 0), 0)),
                  _resident(mu8.shape)] + [_resident(w.shape) for w in wts]
        + [_resident(vec8.shape), _resident((LANES, LANES))],
        out_specs=[pl.BlockSpec((tm, d), row)] * 7,
        out_shape=[out, jax.ShapeDtypeStruct((m, d), F32), out, out, out, out, out],
        compiler_params=_cparams(("parallel",)),
        name="rwkv_prep",
    )(x, x, mu8, *wts, vec8, _head_ones())


def _rwkv_wkv_body(r_ref, lw_ref, k_ref, v_ref, kk_ref, b_ref, tri_ref, y_ref, state_ref):
    c = RWKV_CHUNK
    two = 2 * c

    @pl.when(pl.program_id(1) == 0)
    def _():
        state_ref[...] = jnp.zeros(state_ref.shape, F32)

    tri = tri_ref[...]

    def decayed(bi):
        lw = lw_ref[bi]
        h1 = lw.astype(BF16)
        r1 = lw - h1.astype(F32)
        h2 = r1.astype(BF16)
        h3 = (r1 - h2.astype(F32)).astype(BF16)
        cum = _dot(tri, h1) + _dot(tri, h2) + _dot(tri, h3)
        gam = jnp.exp(cum)
        gam_inv = jnp.exp(-cum)
        return (r_ref[bi].astype(F32) * gam, kk_ref[bi].astype(F32) * jnp.exp(cum - lw),
                b_ref[bi].astype(F32) * gam_inv, k_ref[bi].astype(F32) * gam_inv, gam[c - 1:c, :])

    lane_lo = lax.broadcasted_iota(jnp.int32, (c, LANES), 1) < RWKV_HEAD
    row2 = lax.broadcasted_iota(jnp.int32, (two, two), 0)
    col2 = lax.broadcasted_iota(jnp.int32, (two, two), 1)
    same = (row2 // c) == (col2 // c)
    strict = jnp.logical_and(same, row2 > col2)
    incl = jnp.logical_and(same, row2 >= col2)
    eye = (row2 == col2).astype(F32)

    def stack_masked(t):
        return jnp.concatenate([jnp.where(lane_lo, t, 0.0), jnp.where(lane_lo, 0.0, t)], axis=0)

    nb = r_ref.shape[0]
    items = [(bi, p) for bi in range(nb) for p in range(D_MODEL // LANES)]
    idx = range(len(items))
    sls = [slice(p * LANES, (p + 1) * LANES) for _, p in items]
    dec = [decayed(bi) for bi in range(nb)]
    xs, xu, bds, kds, vss, gend = [], [], [], [], [], []
    for (bi, _), sl in zip(items, sls):
        rt, kkt, bt, kt, gam_end = dec[bi]
        xs.append(jnp.concatenate([stack_masked(kkt[:, sl]), stack_masked(rt[:, sl])], axis=0).astype(BF16))
        xu.append(jnp.concatenate([kkt[:, sl], rt[:, sl]], axis=0).astype(BF16))
        bds.append(jnp.concatenate([bt[:, sl], bt[:, sl]], axis=0).astype(BF16))
        kds.append(jnp.concatenate([kt[:, sl], kt[:, sl]], axis=0).astype(BF16))
        v2 = v_ref[bi, :, sl].astype(F32)
        vss.append(jnp.where(same, jnp.concatenate([v2, v2], axis=0), 0.0).astype(BF16))
        gend.append(gam_end[:, sl])
    s2s = [state_ref[bi, p] for bi, p in items]
    a_all = [_dot_nt(xs[i], jnp.concatenate([bds[i][:c], kds[i][:c]], axis=0)) for i in idx]
    x_state = [_dot_nt(xu[i], s2s[i].astype(BF16)) for i in idx]
    head0 = row2 < c
    nmat, lk, arb, ark = [], [], [], []
    for a in a_all:
        top, bot = a[:two], a[two:]
        top_r, bot_r = pltpu.roll(top, c, 1), pltpu.roll(bot, c, 1)
        nmat.append(jnp.where(strict, -jnp.where(head0, top, top_r), 0.0))
        lk.append(jnp.where(strict, jnp.where(head0, top_r, top), 0.0).astype(BF16))
        arb.append(jnp.where(incl, jnp.where(head0, bot, bot_r), 0.0).astype(BF16))
        ark.append(jnp.where(incl, jnp.where(head0, bot_r, bot), 0.0).astype(BF16))
    rhs = [stack_masked(x_state[i][:c]) + _dot(lk[i], vss[i]) for i in idx]
    pw = [n_.astype(BF16) for n_ in nmat]
    inv = [eye + n_ for n_ in nmat]
    pw = [_dot(t, t).astype(BF16) for t in pw]
    for _ in range(int(math.log2(c)) - 2):
        both = [_dot(pw[i], jnp.concatenate([pw[i], inv[i].astype(BF16)], axis=1)) for i in idx]
        inv = [inv[i] + both[i][:, two:] for i in idx]
        pw = [t[:, :two].astype(BF16) for t in both]
    inv = [inv[i] + _dot(pw[i], inv[i].astype(BF16)) for i in idx]
    ub = [(-_dot(inv[i].astype(BF16), rhs[i].astype(BF16))).astype(BF16) for i in idx]
    for i, (bi, _) in enumerate(items):
        ys = _dot(arb[i], ub[i]) + _dot(ark[i], vss[i])
        y_ref[bi, :, sls[i]] = x_state[i][c:] + ys[:c] + ys[c:]
    for i, (bi, p) in enumerate(items):
        ds = _dot_tn(ub[i], bds[i]) + _dot_tn(vss[i], kds[i])
        state_ref[bi, p] = jnp.where(same, (s2s[i] + ds) * gend[i], 0.0)


def _rwkv_wkv(r, lw, k, v, kk, b, batch, seq):
    c = RWKV_CHUNK
    n = seq // c
    d = D_MODEL
    tri = (jnp.arange(c)[:, None] >= jnp.arange(c)[None, :]).astype(BF16)
    nb = RWKV_BATCH_ROWS if batch % RWKV_BATCH_ROWS == 0 else 1
    blk = pl.BlockSpec((nb, c, d), lambda bi, i: (bi, i, 0))
    as3d = lambda t: t.reshape(batch, seq, d)
    y = pl.pallas_call(
        _rwkv_wkv_body,
        grid=(batch // nb, n),
        in_specs=[blk] * 6 + [_resident((c, c))],
        out_specs=blk,
        out_shape=jax.ShapeDtypeStruct((batch, seq, d), F32),
        scratch_shapes=[pltpu.VMEM((nb, d // LANES, LANES, LANES), F32)],
        compiler_params=_cparams(("parallel", "arbitrary")),
        name="rwkv_wkv",
    )(as3d(r), as3d(lw), as3d(k), as3d(v), as3d(kk), as3d(b), tri)
    return y.reshape(batch * seq, d)


def _rwkv_out_body(y_ref, r_ref, k_ref, v_ref, g_ref, vec_ref, bd_ref, w_ref, res_ref, lg_ref,
                   lb_ref, o_ref, a_ref):
    bd = bd_ref[...]
    inv_n = 1.0 / RWKV_HEAD
    for s in range(D_MODEL // LANES):
        sl = slice(s * LANES, (s + 1) * LANES)
        y = y_ref[:, sl]
        mu = _head_sum(y, bd) * inv_n
        dlt = y - mu
        var = _head_sum(dlt * dlt, bd) * inv_n
        yn = dlt * lax.rsqrt(var + RWKV_GN_EPS) * vec_ref[0:1, sl] + vec_ref[1:2, sl]
        rk = r_ref[:, sl] * k_ref[:, sl] * vec_ref[2:3, sl].astype(BF16)
        bonus = _head_sum(rk, bd) * v_ref[:, sl].astype(F32)
        a_ref[:, sl] = ((yn + bonus) * g_ref[:, sl].astype(F32)).astype(BF16)
    acc = _dot(a_ref[...], w_ref[...])
    o_ref[...] = _layer_norm(DN_ALPHA * res_ref[...] + acc, lg_ref[...], lb_ref[...])


def _rwkv_out(y, r, k, v, g, vec, w, res, lg, lb, *, tm=512):
    m, d = res.shape
    row = lambda i: (i, 0)
    act = pl.BlockSpec((tm, d), row)
    return pl.pallas_call(
        _rwkv_out_body,
        grid=(m // tm,),
        in_specs=[act] * 5 + [_resident(vec.shape), _resident((LANES, LANES)), _resident((d, d)),
                              act, _resident((1, d)), _resident((1, d))],
        out_specs=act,
        out_shape=jax.ShapeDtypeStruct((m, d), F32),
        scratch_shapes=[pltpu.VMEM((tm, d), BF16)],
        compiler_params=_cparams(("parallel",)),
        name="rwkv_out",
    )(y, r, k, v, g, vec, _head_ones(), w, res, lg, lb)


def kernel(x, ret_w_in, ret_gn, ret_w_out, dil_w_in, dil_w_out, mla_w_down, mla_norm_q,
           mla_norm_kv, mla_w_uq, mla_w_ukv, mla_w_out, rwkv_mu, rwkv_w_rkv, rwkv_w_out,
           rwkv_vec, rwkv_lora_a, rwkv_lora_b, rwkv_gate_a, rwkv_gate_b, rwkv_ln_x,
           mlp_w1, mlp_w2, ln_g, ln_b):
    batch, seq, d = x.shape
    xf = x.reshape(batch * seq, d)
    n_mixers = 4
    for i in range(DEPTH):
        mixer, j = i % n_mixers, i // n_mixers
        lg, lb = ln_g[i, 0][None, :], ln_b[i, 0][None, :]
        if mixer == 0:
            proj = _mm(xf, ret_w_in[j].astype(BF16), tm=512, name="ret_proj")
            gated = _retention(proj, ret_gn[j], batch, seq)
            xf = _mm_res_ln(gated, ret_w_out[j].astype(BF16), xf, lg, lb, name="ret_out")
        elif mixer == 1:
            outs, lses = _dilated(xf, dil_w_in[j].astype(BF16), batch, seq)
            xf = _dil_out(outs, lses, dil_w_out[j].astype(BF16), xf, lg, lb, seq)
        elif mixer == 2:
            o = _mla(xf, mla_w_down[j], mla_norm_q[j], mla_norm_kv[j], mla_w_uq[j], mla_w_ukv[j],
                     batch, seq)
            xf = _mm_res_ln(o, mla_w_out[j].astype(BF16), xf, lg, lb, name="mla_out")
        else:
            r, lw, k, v, kk, b, g = _rwkv_prep(xf, rwkv_mu[j], rwkv_w_rkv[j], rwkv_vec[j],
                                               rwkv_lora_a[j], rwkv_lora_b[j], rwkv_gate_a[j],
                                               rwkv_gate_b[j], seq)
            y = _rwkv_wkv(r, lw, k, v, kk, b, batch, seq)
            vec = jnp.concatenate([rwkv_ln_x[j], rwkv_vec[j][4:5],
                                   jnp.zeros((5, d), F32)], axis=0)
            xf = _rwkv_out(y, r, k, v, g, vec, rwkv_w_out[j].astype(BF16), xf, lg, lb)
        xf = _mlp(xf, mlp_w1[i].astype(BF16), mlp_w2[i].astype(BF16),
                  ln_g[i, 1][None, :], ln_b[i, 1][None, :])
    return xf.reshape(batch, seq, d)
```

```python
import functools
import math

import jax
import jax.numpy as jnp
import numpy as np
from jax import lax
from jax.experimental import pallas as pl
from jax.experimental.pallas import tpu as pltpu

F32 = jnp.float32
BF16 = jnp.bfloat16

D_MODEL = 1024
DEPTH = 4
D_FF = 4 * D_MODEL
LN_EPS = 1e-5
RMS_EPS = 1e-6
GN_EPS = 1e-5
DN_ALPHA = (2.0 * DEPTH) ** 0.25
NEG = -1e30
LANES = 128
SUBLANES = 8
ROPE_PARTNER = 64
MM_SUB = 256

RET_HEADS = 4
RET_QK_DIM = 256
RET_V_DIM = 512
RET_CHUNK = 512
RET_THETA = 10000.0

DIL_PAIRS = ((128, 1), (512, 4), (2048, 16))
DIL_HEADS = 8
DIL_HEAD_DIM = 128
DIL_ROT = 32
DIL_BLOCK = 128
DIL_TQ = 512
DIL_WAVE = 4
ROPE_THETA = 500000.0

MLA_HEADS = 16
MLA_NOPE = 128
MLA_ROPE = 64
MLA_V = 128
MLA_Q_RANK = 256
MLA_KV_RANK = 128
MLA_THETA = 10000.0
MLA_TQ = 1024
MLA_TK = 1024
MLA_ROW_SPLIT = 4
assert MLA_TQ == MLA_TK

RWKV_HEAD = 64
RWKV_HEADS = D_MODEL // RWKV_HEAD
RWKV_GN_EPS = 64e-5
RWKV_CHUNK = 64
RWKV_GATE_PAD = 256
RWKV_BATCH_ROWS = 4

V7X_VMEM_BYTES = 64 * 1024 * 1024
VMEM_LIMIT = V7X_VMEM_BYTES // 8 * 7


def _cparams(sem):
    return pltpu.CompilerParams(dimension_semantics=sem, vmem_limit_bytes=VMEM_LIMIT)


def _resident(shape):
    nd = len(shape)
    return pl.BlockSpec(shape, lambda *_: (0,) * nd, pipeline_mode=pl.Buffered(1))


def _layer_norm(z, g, b):
    mu = jnp.mean(z, axis=-1, keepdims=True)
    d = z - mu
    var = jnp.mean(d * d, axis=-1, keepdims=True)
    return d * lax.rsqrt(var + LN_EPS) * g + b


def _dot(a, b):
    return jnp.dot(a, b, preferred_element_type=F32)


def _dot_nt(a, b):
    return lax.dot_general(a, b, (((1,), (1,)), ((), ())), preferred_element_type=F32)


def _dot_tn(a, b):
    return lax.dot_general(a, b, (((0,), (0,)), ((), ())), preferred_element_type=F32)


def _rope_tile(a, c, s):
    return a * c + pltpu.roll(a, ROPE_PARTNER, 1) * s


def _project_columns(xb, w_ref, store, modes, tabs, scale):
    n = w_ref.shape[1]
    for c0 in range(0, n, MM_SUB):
        acc = _dot(xb, w_ref[:, c0:c0 + MM_SUB])
        for t in range(MM_SUB // LANES):
            a = acc[:, t * LANES:(t + 1) * LANES]
            mode = modes[c0 // LANES + t] if modes is not None else None
            if mode == "scale":
                a = a * scale
            elif mode is not None:
                c_ref, s_ref = tabs
                a = _rope_tile(a, c_ref[mode[1]], s_ref[mode[1]])
            store(c0 + t * LANES, a)


def _mm_body(x_ref, w_ref, *rest, modes, scale):
    tabs, o_ref = rest[:-1], rest[-1]

    def store(c0, a):
        o_ref[:, c0:c0 + LANES] = a.astype(o_ref.dtype)

    _project_columns(x_ref[...].astype(BF16), w_ref, store, modes, tabs, scale)


def _mm(x, w, *, tm, out_dtype=BF16, tabs=None, tab_map=None, modes=None, scale=1.0, name="mm"):
    m = x.shape[0]
    k, n = w.shape
    row = lambda i: (i, 0)
    in_specs = [pl.BlockSpec((tm, k), row), _resident((k, n))]
    args = [x, w]
    if tabs is not None:
        for t in tabs:
            in_specs.append(pl.BlockSpec((t.shape[0], tm, LANES), tab_map))
            args.append(t)
    return pl.pallas_call(
        functools.partial(_mm_body, modes=modes, scale=scale),
        grid=(m // tm,),
        in_specs=in_specs,
        out_specs=pl.BlockSpec((tm, n), row),
        out_shape=jax.ShapeDtypeStruct((m, n), out_dtype),
        compiler_params=_cparams(("parallel",)),
        name=name,
    )(*args)


def _mm_res_ln_body(a_ref, w_ref, res_ref, g_ref, b_ref, o_ref):
    half = a_ref.shape[0] // 2
    for r0 in (0, half):
        acc = _dot(a_ref[r0:r0 + half, :], w_ref[...])
        o_ref[r0:r0 + half, :] = _layer_norm(DN_ALPHA * res_ref[r0:r0 + half, :] + acc,
                                             g_ref[...], b_ref[...])


def _mm_res_ln(a, w, res, g, b, *, tm=512, name="mm_res_ln"):
    m, k = a.shape
    d = w.shape[1]
    row = lambda i: (i, 0)
    return pl.pallas_call(
        _mm_res_ln_body,
        grid=(m // tm,),
        in_specs=[pl.BlockSpec((tm, k), row), _resident((k, d)), pl.BlockSpec((tm, d), row),
                  _resident((1, d)), _resident((1, d))],
        out_specs=pl.BlockSpec((tm, d), row),
        out_shape=jax.ShapeDtypeStruct((m, d), F32),
        compiler_params=_cparams(("parallel",)),
        name=name,
    )(a, w, res, g, b)


def _mlp_body(x_ref, w1_ref, w2_ref, g_ref, b_ref, o_ref, *, fchunk):
    half = x_ref.shape[0] // 2
    for r0 in (0, half):
        x = x_ref[r0:r0 + half, :]
        xb = x.astype(BF16)
        acc = jnp.zeros(x.shape, F32)
        for c in range(D_FF // fchunk):
            h = _dot(xb, w1_ref[:, c * fchunk:(c + 1) * fchunk])
            h = jnp.maximum(h, 0.0)
            h = (h * h).astype(BF16)
            acc = acc + _dot(h, w2_ref[c * fchunk:(c + 1) * fchunk, :])
        o_ref[r0:r0 + half, :] = _layer_norm(DN_ALPHA * x + acc, g_ref[...], b_ref[...])


def _mlp(x, w1_all, w2_all, layer, g, b, *, tm=512, fchunk=1024):
    m, d = x.shape
    row = lambda i: (i, 0)
    slab = lambda shape: pl.BlockSpec((None,) + shape, lambda i: (layer, 0, 0),
                                      pipeline_mode=pl.Buffered(1))
    return pl.pallas_call(
        functools.partial(_mlp_body, fchunk=fchunk),
        grid=(m // tm,),
        in_specs=[pl.BlockSpec((tm, d), row), slab((d, D_FF)), slab((D_FF, d)),
                  _resident((1, d)), _resident((1, d))],
        out_specs=pl.BlockSpec((tm, d), row),
        out_shape=jax.ShapeDtypeStruct((m, d), F32),
        compiler_params=_cparams(("parallel",)),
        name="mlp",
    )(x, w1_all, w2_all, g, b)


def _ret_body(q_ref, k_ref, v_ref, g_ref, cos_ref, sin_ref, intra_ref, qdec_ref, kdec_ref,
              cdec_ref, gn_ref, o_ref, state_ref):
    dk, dv, half = RET_QK_DIM, RET_V_DIM, RET_QK_DIM // 2

    @pl.when(pl.program_id(1) == 0)
    def _():
        state_ref[...] = jnp.zeros(state_ref.shape, F32)

    cos = cos_ref[...]
    sin = sin_ref[...]

    def rope(t):
        t1 = t[:, :half].astype(F32)
        t2 = t[:, half:].astype(F32)
        return jnp.concatenate([t1 * cos - t2 * sin, t2 * cos + t1 * sin], axis=-1)

    heads = range(RET_HEADS)
    q = [rope(q_ref[:, h * dk:(h + 1) * dk]) for h in heads]
    k = [rope(k_ref[:, h * dk:(h + 1) * dk]) * (dk ** -0.5) for h in heads]
    v = [v_ref[:, h * dv:(h + 1) * dv] for h in heads]
    qb = [t.astype(BF16) for t in q]
    state = [state_ref[h] for h in heads]
    scores = [(_dot_nt(qb[h], k[h].astype(BF16)) * intra_ref[h]).astype(BF16) for h in heads]
    cross = [_dot(qb[h], state[h].astype(BF16)) * qdec_ref[h] for h in heads]
    o = [_dot(scores[h], v[h]) + cross[h] for h in heads]
    for h in heads:
        state_ref[h] = (state[h] * cdec_ref[h, 0:1, :]
                        + _dot_tn((k[h] * kdec_ref[h]).astype(BF16), v[h]))
    for h in heads:
        sl = slice(h * dv, (h + 1) * dv)
        mu = jnp.mean(o[h], axis=-1, keepdims=True)
        d = o[h] - mu
        var = jnp.mean(d * d, axis=-1, keepdims=True)
        on = d * lax.rsqrt(var + GN_EPS) * gn_ref[0:1, sl] + gn_ref[1:2, sl]
        gate = g_ref[:, sl].astype(F32)
        gate = gate * (1.0 / (1.0 + jnp.exp(-gate)))
        o_ref[:, sl] = (gate * on).astype(o_ref.dtype)


def _retention(proj, gn, batch, seq):
    h_, dk, dv, c = RET_HEADS, RET_QK_DIM, RET_V_DIM, RET_CHUNK
    n = seq // c
    half = dk // 2
    f32 = np.float32
    pos = np.arange(seq, dtype=f32)
    inv_freq = f32(RET_THETA) ** (-np.arange(half, dtype=f32) / f32(half))
    ang = pos[:, None] * inv_freq[None, :]
    cos, sin = np.cos(ang), np.sin(ang)
    log_gamma = np.log(f32(1.0) - f32(2.0) ** (f32(-5.0) - np.arange(h_, dtype=f32)))
    idx = np.arange(c, dtype=f32)
    diff = idx[:, None] - idx[None, :]
    intra = np.where(diff >= 0, np.exp(log_gamma[:, None, None] * np.maximum(diff, f32(0.0))), f32(0.0))
    intra = intra.astype(f32)
    qdec = np.broadcast_to(np.exp(log_gamma[:, None] * (idx + f32(1.0)))[:, :, None], (h_, c, dv))
    kdec = np.broadcast_to(np.exp(log_gamma[:, None] * (f32(c - 1.0) - idx))[:, :, None], (h_, c, dk))
    cdec = np.broadcast_to(np.exp(log_gamma * f32(c))[:, None, None], (h_, SUBLANES, dv))
    qdec, kdec, cdec = (np.ascontiguousarray(t, dtype=f32) for t in (qdec, kdec, cdec))
    qk_w, vg_w = h_ * dk, h_ * dv
    return pl.pallas_call(
        _ret_body,
        grid=(batch, n),
        in_specs=[
            pl.BlockSpec((c, qk_w), lambda b, i: (b * n + i, 0)),
            pl.BlockSpec((c, qk_w), lambda b, i: (b * n + i, 1)),
            pl.BlockSpec((c, vg_w), lambda b, i: (b * n + i, 2 * qk_w // vg_w)),
            pl.BlockSpec((c, vg_w), lambda b, i: (b * n + i, 2 * qk_w // vg_w + 1)),
            pl.BlockSpec((c, half), lambda b, i: (i, 0)),
            pl.BlockSpec((c, half), lambda b, i: (i, 0)),
            _resident((h_, c, c)), _resident((h_, c, dv)), _resident((h_, c, dk)),
            _resident((h_, SUBLANES, dv)), _resident((2, vg_w)),
        ],
        out_specs=pl.BlockSpec((c, vg_w), lambda b, i: (b * n + i, 0)),
        out_shape=jax.ShapeDtypeStruct((batch * seq, vg_w), BF16),
        scratch_shapes=[pltpu.VMEM((h_, dk, dv), F32)],
        compiler_params=_cparams(("parallel", "arbitrary")),
        name="retention",
    )(proj, proj, proj, proj, cos, sin, intra, qdec, kdec, cdec, gn)


def _dil_attn_body(q_ref, kp_ref, kc_ref, vp_ref, vc_ref, o_ref, lse_ref, *, seq_blocks):
    blk = DIL_BLOCK
    n_sub = q_ref.shape[0] // blk
    has_prev = pl.program_id(1) > 0
    qi = lax.broadcasted_iota(jnp.int32, (blk, 2 * blk), 0)
    ki = lax.broadcasted_iota(jnp.int32, (blk, 2 * blk), 1)
    band = jnp.logical_and(ki >= qi, ki <= qi + blk)
    band_first = jnp.logical_and(band, jnp.logical_or(ki >= blk, has_prev))
    band_start = jnp.logical_and(band, ki >= blk)

    def mask(j):
        if j == 0:
            return band_first
        return band_start if j % seq_blocks == 0 else band
    lane = lax.broadcasted_iota(jnp.int32, (blk, LANES), 1)
    ones = jnp.ones((2 * blk, LANES), BF16)
    def keys(prev_ref, cur_ref, j, sl):
        if j == 0:
            return jnp.concatenate([prev_ref[:, sl], cur_ref[:blk, sl]], axis=0)
        return cur_ref[(j - 1) * blk:(j + 1) * blk, sl]

    for j0 in range(0, n_sub, DIL_WAVE):
        items = [(j, h) for j in range(j0, min(j0 + DIL_WAVE, n_sub)) for h in range(DIL_HEADS)]
        sls = [slice(h * DIL_HEAD_DIM, (h + 1) * DIL_HEAD_DIM) for _, h in items]
        rows = [slice(j * blk, (j + 1) * blk) for j, _ in items]
        s = [jnp.where(mask(j),
                       _dot_nt(q_ref[rows[i], sls[i]], keys(kp_ref, kc_ref, j, sls[i])), NEG)
             for i, (j, _) in enumerate(items)]
        m = [jnp.max(t, axis=-1, keepdims=True) for t in s]
        p = [jnp.exp((s[i] - m[i]).astype(BF16)) for i in range(len(items))]
        pv = [_dot(p[i], jnp.concatenate([keys(vp_ref, vc_ref, j, sls[i]), ones], axis=1))
              for i, (j, _) in enumerate(items)]
        lse_tiles = {j: jnp.zeros((blk, LANES), F32) for j, _ in items}
        for i, (j, h) in enumerate(items):
            l = pv[i][:, DIL_HEAD_DIM:]
            o_ref[rows[i], sls[i]] = (pv[i][:, :DIL_HEAD_DIM] / l).astype(o_ref.dtype)
            lse_tiles[j] = jnp.where(lane == h, m[i] + jnp.log(l), lse_tiles[j])
        for j, tile in lse_tiles.items():
            lse_ref[j * blk:(j + 1) * blk, :] = tile


def _dil_attn(qkv, batch, seq, dil):
    blk = DIL_BLOCK
    hd = DIL_HEADS * DIL_HEAD_DIM
    sub = seq // dil
    tq = DIL_TQ
    nb = max(sub // tq, 1)
    n_seq = batch * dil * sub // (nb * tq)
    per = tq // blk
    cur = lambda c: (lambda z, i: (z * nb + i, c))
    prev = lambda c: (lambda z, i: (jnp.maximum((z * nb + i) * per - 1, 0), c))
    out_map = lambda z, i: (z * nb + i, 0)
    return pl.pallas_call(
        functools.partial(_dil_attn_body, seq_blocks=sub // blk),
        grid=(n_seq, nb),
        in_specs=[pl.BlockSpec((tq, hd), cur(0)), pl.BlockSpec((blk, hd), prev(1)),
                  pl.BlockSpec((tq, hd), cur(1)), pl.BlockSpec((blk, hd), prev(2)),
                  pl.BlockSpec((tq, hd), cur(2))],
        out_specs=[pl.BlockSpec((tq, hd), out_map), pl.BlockSpec((tq, LANES), out_map)],
        out_shape=[jax.ShapeDtypeStruct((batch * seq, hd), BF16),
                   jax.ShapeDtypeStruct((batch * seq, LANES), F32)],
        compiler_params=_cparams(("parallel", "arbitrary")),
        name=f"dil_attn_{dil}",
    )(qkv, qkv, qkv, qkv, qkv)


def _dil_out_body(o0_ref, o1_ref, o2_ref, l0_ref, l1_ref, l2_ref, e_ref, w_ref, res_ref,
                  g_ref, b_ref, o_ref, osc_ref, lsc_ref, mix_ref):
    tm = o_ref.shape[0]
    nh = DIL_HEADS
    for gi, (og, lg) in enumerate(((o0_ref, l0_ref), (o1_ref, l1_ref), (o2_ref, l2_ref))):
        dil = og.shape[1]
        n = tm // dil
        for r in range(dil):
            rows = pl.ds(r, n, stride=dil) if dil > 1 else slice(None)
            for h in range(nh):
                osc_ref[gi * nh + h, rows, :] = og[0, r, :, h * LANES:(h + 1) * LANES].astype(F32)
            lsc_ref[gi, rows, :] = lg[0, r]
    l0, l1, l2 = lsc_ref[0], lsc_ref[1], lsc_ref[2]
    m = jnp.maximum(jnp.maximum(l0, l1), l2)
    e0, e1, e2 = jnp.exp(l0 - m), jnp.exp(l1 - m), jnp.exp(l2 - m)
    den = e0 + e1 + e2
    e = e_ref[...]
    wfull = []
    for eg in (e0, e1, e2):
        wfull.append(_dot((eg / den).astype(BF16), e))
    for h in range(nh):
        sl = slice(h * LANES, (h + 1) * LANES)
        mixed = sum(wfull[gi][:, sl] * osc_ref[gi * nh + h] for gi in range(3))
        mix_ref[:, sl] = mixed.astype(BF16)
    acc = _dot(mix_ref[...], w_ref[...])
    o_ref[...] = _layer_norm(DN_ALPHA * res_ref[...] + acc, g_ref[...], b_ref[...])


def _dil_out(outs, lses, w, res, g, b, seq, *, tm=512):
    m, d = res.shape
    hd = DIL_HEADS * DIL_HEAD_DIM
    expand = (jnp.arange(LANES)[:, None] == (jnp.arange(hd) // DIL_HEAD_DIM)[None, :]).astype(BF16)
    row = lambda i: (i, 0)
    nt = seq // tm
    grp = lambda i: (i // nt, 0, i % nt, 0)
    dils = [dil for _, dil in DIL_PAIRS]
    batch = m // seq
    o4 = [o.reshape(batch, dil, seq // dil, hd) for o, dil in zip(outs, dils)]
    l4 = [l.reshape(batch, dil, seq // dil, LANES) for l, dil in zip(lses, dils)]
    return pl.pallas_call(
        _dil_out_body,
        grid=(m // tm,),
        in_specs=[pl.BlockSpec((1, dil, tm // dil, hd), grp) for dil in dils]
        + [pl.BlockSpec((1, dil, tm // dil, LANES), grp) for dil in dils]
        + [_resident((LANES, hd)), _resident((hd, d)), pl.BlockSpec((tm, d), row),
           _resident((1, d)), _resident((1, d))],
        out_specs=pl.BlockSpec((tm, d), row),
        out_shape=jax.ShapeDtypeStruct((m, d), F32),
        scratch_shapes=[pltpu.VMEM((len(dils) * DIL_HEADS, tm, LANES), F32),
                        pltpu.VMEM((len(dils), tm, LANES), F32), pltpu.VMEM((tm, hd), BF16)],
        compiler_params=_cparams(("parallel",)),
        name="dil_out",
    )(*o4, *l4, expand, w, res, g, b)


def _rope_tables(seq, rot, theta, scale, passthrough):
    half = rot // 2
    inv_freq = np.float32(theta) ** (-np.arange(half, dtype=np.float32) / np.float32(half))
    ang = np.arange(seq, dtype=np.float32)[:, None] * inv_freq[None, :]
    cos, sin = np.cos(ang), np.sin(ang)
    fill = np.full((seq, ROPE_PARTNER - half), passthrough, np.float32)
    zero = np.zeros((seq, ROPE_PARTNER - half), np.float32)
    c = np.concatenate([cos, fill, cos, fill], axis=1)
    s = np.concatenate([-sin, zero, sin, zero], axis=1)
    return (c * np.float32(scale)).astype(np.float32), (s * np.float32(scale)).astype(np.float32)


def _rope_lane_order(rot, width):
    half = rot // 2
    rest = list(range(rot, width))
    cut = ROPE_PARTNER - half
    return jnp.array(list(range(half)) + rest[:cut] + list(range(half, rot)) + rest[cut:])


def _dil_proj_body(x_ref, w_ref, c_ref, s_ref, o_ref, xb_ref, xs_ref, *, dil):
    tm = x_ref.shape[0]
    n = tm // dil
    if dil == 1:
        xb = x_ref[...].astype(BF16)
    else:
        for c in range(D_MODEL // LANES):
            xs_ref[c] = x_ref[:, c * LANES:(c + 1) * LANES]
        for r in range(dil):
            for c in range(D_MODEL // LANES):
                xb_ref[r * n:(r + 1) * n, c * LANES:(c + 1) * LANES] = (
                    xs_ref[c, pl.ds(r, n, stride=dil), :].astype(BF16))
        xb = xb_ref[...]

    def store(c0, a):
        o_ref[0, :, :, c0:c0 + LANES] = a.astype(o_ref.dtype).reshape(dil, n, LANES)

    modes = [("rope", 0)] * DIL_HEADS + [("rope", 1)] * DIL_HEADS + [None] * DIL_HEADS
    _project_columns(xb, w_ref, store, modes, (c_ref, s_ref), 1.0)


def _dil_proj(x, w, tabs, batch, seq, dil, *, tm=512):
    hd = DIL_HEADS * DIL_HEAD_DIM
    nt = seq // tm
    n = tm // dil
    tab_spec = pl.BlockSpec((2, tm, LANES), lambda i: (0, i % nt, 0))
    return pl.pallas_call(
        functools.partial(_dil_proj_body, dil=dil),
        grid=(batch * nt,),
        in_specs=[pl.BlockSpec((tm, D_MODEL), lambda i: (i, 0)), _resident((D_MODEL, 3 * hd)),
                  tab_spec, tab_spec],
        out_specs=pl.BlockSpec((1, dil, n, 3 * hd), lambda i: (i // nt, 0, i % nt, 0)),
        out_shape=jax.ShapeDtypeStruct((batch, dil, seq // dil, 3 * hd), BF16),
        scratch_shapes=[pltpu.VMEM((tm, D_MODEL), BF16),
                        pltpu.VMEM((D_MODEL // LANES, tm, LANES), F32)],
        compiler_params=_cparams(("parallel",)),
        name=f"dil_proj_{dil}",
    )(x, w, *tabs)


def _dilated(x, w_in, batch, seq, *, tm=512):
    hd = DIL_HEADS * DIL_HEAD_DIM
    cq, sq = _rope_tables(seq, DIL_ROT, ROPE_THETA, DIL_HEAD_DIM ** -0.5, 1.0)
    ck, sk = _rope_tables(seq, DIL_ROT, ROPE_THETA, 1.0, 1.0)
    order = _rope_lane_order(DIL_ROT, DIL_HEAD_DIM)
    outs, lses = [], []
    for gi, (_, dil) in enumerate(DIL_PAIRS):
        wg = w_in[:, gi * 3 * hd:(gi + 1) * 3 * hd].reshape(D_MODEL, 3, DIL_HEADS, DIL_HEAD_DIM)
        wg = jnp.concatenate([wg[:, :2][..., order], wg[:, 2:]], axis=1).reshape(D_MODEL, 3 * hd)

        def by_residue(t):
            t = t.reshape(seq // tm, tm // dil, dil, LANES)
            return np.swapaxes(t, 1, 2).reshape(seq, LANES)

        tabs = [np.stack([by_residue(a), by_residue(b)]) for a, b in ((cq, ck), (sq, sk))]
        qkv = _dil_proj(x, wg, tabs, batch, seq, dil, tm=tm)
        o, lse = _dil_attn(qkv.reshape(batch * seq, 3 * hd), batch, seq, dil)
        outs.append(o)
        lses.append(lse)
    return outs, lses


def _mla_proj_body(x_ref, wd_ref, nq_ref, nkv_ref, wq_ref, wkv_ref, c_ref, s_ref,
                   q_ref, kv_ref, kpe_ref, *, scale):
    acc = _dot(x_ref[...].astype(BF16), wd_ref[...])
    cq = acc[:, :MLA_Q_RANK]
    ckv = acc[:, MLA_Q_RANK:MLA_Q_RANK + MLA_KV_RANK]
    kpe = acc[:, MLA_Q_RANK + MLA_KV_RANK:]
    cq = cq * lax.rsqrt(jnp.mean(cq * cq, axis=-1, keepdims=True) + RMS_EPS) * nq_ref[...]
    ckv = ckv * lax.rsqrt(jnp.mean(ckv * ckv, axis=-1, keepdims=True) + RMS_EPS) * nkv_ref[...]
    kpe_ref[...] = _rope_tile(kpe, c_ref[0], s_ref[0]).astype(kpe_ref.dtype)

    def store_q(c0, a):
        q_ref[:, c0:c0 + LANES] = a.astype(q_ref.dtype)

    def store_kv(c0, a):
        kv_ref[:, c0:c0 + LANES] = a.astype(kv_ref.dtype)

    _project_columns(cq.astype(BF16), wq_ref, store_q, ["scale", ("rope", 1)] * MLA_HEADS,
                     (c_ref, s_ref), scale)
    _project_columns(ckv.astype(BF16), wkv_ref, store_kv, None, None, 1.0)


def _mla_proj(x, wd, nq, nkv, wq, wkv, tabs, seq, scale, *, tm=512):
    m, d = x.shape
    row = lambda i: (i, 0)
    ns = seq // tm
    tab = pl.BlockSpec((2, tm, LANES), lambda i: (0, i % ns, 0))
    return pl.pallas_call(
        functools.partial(_mla_proj_body, scale=scale),
        grid=(m // tm,),
        in_specs=[pl.BlockSpec((tm, d), row), _resident(wd.shape), _resident((1, MLA_Q_RANK)),
                  _resident((1, MLA_KV_RANK)), _resident(wq.shape), _resident(wkv.shape), tab, tab],
        out_specs=[pl.BlockSpec((tm, wq.shape[1]), row), pl.BlockSpec((tm, wkv.shape[1]), row),
                   pl.BlockSpec((tm, LANES), row)],
        out_shape=[jax.ShapeDtypeStruct((m, wq.shape[1]), BF16),
                   jax.ShapeDtypeStruct((m, wkv.shape[1]), BF16),
                   jax.ShapeDtypeStruct((m, LANES), BF16)],
        compiler_params=_cparams(("parallel",)),
        name="mla_proj",
    )(x, wd, nq, nkv, wq, wkv, *tabs)


def _mla_flash_body(q_ref, kn_ref, kpe_ref, v_ref, o_ref, kcat_ref, vaug_ref):
    tq, tk, sub = MLA_TQ, MLA_TK, MLA_TQ // MLA_ROW_SPLIT
    kcat_ref[:, :MLA_NOPE] = kn_ref[...]
    kcat_ref[:, MLA_NOPE:] = kpe_ref[...]
    vaug_ref[:, :MLA_V] = v_ref[...]
    vaug_ref[:, MLA_V:] = jnp.ones((v_ref.shape[0], LANES), BF16)

    parts = range(MLA_ROW_SPLIT)
    col_minus_row = (lax.broadcasted_iota(jnp.int32, (sub, tk), 1)
                     - lax.broadcasted_iota(jnp.int32, (sub, tk), 0))

    def scores(tile, chunk, diagonal):
        out = []
        for part in parts:
            width = (part + 1) * sub if diagonal else tk
            q0 = tile * tq + part * sub
            s = _dot_nt(q_ref[q0:q0 + sub, :], kcat_ref[chunk * tk:chunk * tk + width, :])
            if diagonal:
                s = jnp.where(col_minus_row[:, :width] <= part * sub, s, NEG)
            out.append(s)
        return out

    def update(chunk, s_all, carry):
        out = []
        for s, (m, acc) in zip(s_all, carry):
            vb = vaug_ref[chunk * tk:chunk * tk + s.shape[1], :]
            m_new = jnp.maximum(m, jnp.max(s, axis=-1, keepdims=True))
            alpha = jnp.exp2(m - m_new)
            p = jnp.exp2((s - m_new).astype(BF16))
            out.append((m_new, alpha * acc + _dot(p, vb)))
        return out

    for tile in range(q_ref.shape[0] // tq):
        n_chunks = tile + 1
        carry = [(jnp.full((sub, 1), NEG, F32), jnp.zeros((sub, MLA_V + LANES), F32)) for _ in parts]
        s = scores(tile, 0, n_chunks == 1)
        for c in range(n_chunks):
            s_next = scores(tile, c + 1, c + 2 == n_chunks) if c + 1 < n_chunks else None
            carry = update(c, s, carry)
            s = s_next
        for part, (_, acc) in enumerate(carry):
            q0 = tile * tq + part * sub
            o_ref[q0:q0 + sub, :] = (acc[:, :MLA_V] / acc[:, MLA_V:]).astype(o_ref.dtype)


def _mla_flash(q, kv, kpe, batch, seq):
    h_ = MLA_HEADS
    qw = MLA_NOPE + LANES
    return pl.pallas_call(
        _mla_flash_body,
        grid=(batch, h_),
        in_specs=[pl.BlockSpec((seq, qw), lambda b, h: (b, h)),
                  pl.BlockSpec((seq, MLA_NOPE), lambda b, h: (b, h)),
                  pl.BlockSpec((seq, LANES), lambda b, h: (b, 0)),
                  pl.BlockSpec((seq, MLA_V), lambda b, h: (b, h_ + h))],
        out_specs=pl.BlockSpec((seq, MLA_V), lambda b, h: (b, h)),
        out_shape=jax.ShapeDtypeStruct((batch * seq, h_ * MLA_V), BF16),
        scratch_shapes=[pltpu.VMEM((seq, qw), BF16), pltpu.VMEM((seq, MLA_V + LANES), BF16)],
        compiler_params=_cparams(("parallel", "parallel")),
        name="mla_flash",
    )(q, kv, kpe, kv)


def _mla(x, w_down, norm_q, norm_kv, w_uq, w_ukv, batch, seq):
    h_ = MLA_HEADS
    half = MLA_ROPE // 2

    def pe_tile(w):
        z = jnp.zeros(w.shape[:-1] + (ROPE_PARTNER - half,), w.dtype)
        return jnp.concatenate([w[..., :half], z, w[..., half:], z], axis=-1)

    n_lat = MLA_Q_RANK + MLA_KV_RANK
    wd = jnp.concatenate([w_down[:, :n_lat], pe_tile(w_down[:, n_lat:])], axis=1).astype(BF16)
    wq = w_uq.reshape(MLA_Q_RANK, h_, MLA_NOPE + MLA_ROPE)
    wq = jnp.concatenate([wq[..., :MLA_NOPE], pe_tile(wq[..., MLA_NOPE:])], axis=-1)
    wq = wq.reshape(MLA_Q_RANK, -1).astype(BF16)
    wkv = w_ukv.reshape(MLA_KV_RANK, h_, MLA_NOPE + MLA_V)
    wkv = jnp.concatenate([wkv[:, :, :MLA_NOPE].reshape(MLA_KV_RANK, -1),
                           wkv[:, :, MLA_NOPE:].reshape(MLA_KV_RANK, -1)], axis=1).astype(BF16)
    scale = (MLA_NOPE + MLA_ROPE) ** -0.5 * math.log2(math.e)
    tk = _rope_tables(seq, MLA_ROPE, MLA_THETA, 1.0, 0.0)
    tq = _rope_tables(seq, MLA_ROPE, MLA_THETA, scale, 0.0)
    tabs = [np.stack([a, b]) for a, b in zip(tk, tq)]
    q, kv, kpe = _mla_proj(x, wd, norm_q[None, :], norm_kv[None, :], wq, wkv, tabs, seq, scale)
    return _mla_flash(q, kv, kpe, batch, seq)


def _head_sum(z, ones_bd):
    return _dot(z.astype(BF16), ones_bd)


def _rwkv_prep_body(x_ref, xp_ref, mu_ref, wr_ref, wk_ref, wv_ref, la0_ref, lb0_ref, la1_ref,
                    lb1_ref, ga_ref, gb_ref, vec_ref, bd_ref, r_ref, lw_ref, k_ref, v_ref,
                    kk_ref, b_ref, g_ref, *, tiles_per_seq):
    x = x_ref[...]
    tm = x.shape[0]
    first = pl.program_id(0) % tiles_per_seq == 0
    prev_row = jnp.where(first, 0.0, xp_ref[SUBLANES - 1:SUBLANES, :])
    rows = lax.broadcasted_iota(jnp.int32, x.shape, 0)
    shifted = jnp.where(rows == 0, prev_row, pltpu.roll(x, 1, 0))
    xb = x.astype(BF16)
    xxb = (shifted - x).astype(BF16)
    mix = lambda i: xb + xxb * mu_ref[i:i + 1, :].astype(BF16)
    w0, a0, k_k, k_a = (vec_ref[i:i + 1, :] for i in range(4))
    wl = w0 + _dot(jnp.tanh(_dot(mix(1), la0_ref[...])).astype(BF16), lb0_ref[...])
    al = a0 + _dot(_dot(mix(4), la1_ref[...]).astype(BF16), lb1_ref[...])
    gl = _dot(mix(5), ga_ref[...])
    k_raw = _dot(mix(2), wk_ref[...])
    lw_ref[...] = -math.exp(-0.5) / (1.0 + jnp.exp(-wl))
    a = 1.0 / (1.0 + jnp.exp(-al))
    g_ref[...] = _dot((1.0 / (1.0 + jnp.exp(-gl))).astype(BF16), gb_ref[...]).astype(g_ref.dtype)
    v_ref[...] = _dot(mix(3), wv_ref[...]).astype(v_ref.dtype)
    kk = k_raw * k_k
    bd = bd_ref[...]
    for s in range(D_MODEL // LANES):
        sl = slice(s * LANES, (s + 1) * LANES)
        t = kk[:, sl]
        t = t * lax.rsqrt(jnp.maximum(_head_sum(t * t, bd), 1e-24))
        kk_ref[:, sl] = t.astype(kk_ref.dtype)
        b_ref[:, sl] = (t * a[:, sl]).astype(b_ref.dtype)
    k_ref[...] = (k_raw * (1.0 + (a - 1.0) * k_a)).astype(k_ref.dtype)
    r_ref[...] = _dot(mix(0), wr_ref[...]).astype(r_ref.dtype)


def _head_ones():
    idx = jnp.arange(LANES) // RWKV_HEAD
    return (idx[:, None] == idx[None, :]).astype(BF16)


def _rwkv_prep(x, mu, w_rkv, vec, lora_a, lora_b, gate_a, gate_b, seq, *, tm=512):
    m, d = x.shape
    gpad = RWKV_GATE_PAD - gate_a.shape[1]
    ga = jnp.pad(gate_a, ((0, 0), (0, gpad))).astype(BF16)
    gb = jnp.pad(gate_b, ((0, gpad), (0, 0))).astype(BF16)
    wts = [w_rkv[0].astype(BF16), w_rkv[1].astype(BF16), w_rkv[2].astype(BF16),
           lora_a[0].astype(BF16), lora_b[0].astype(BF16), lora_a[1].astype(BF16),
           lora_b[1].astype(BF16), ga, gb]
    vec8 = jnp.pad(vec, ((0, SUBLANES - vec.shape[0]), (0, 0)))
    mu8 = jnp.pad(mu, ((0, SUBLANES - mu.shape[0]), (0, 0)))
    row = lambda i: (i, 0)
    sub = tm // SUBLANES
    out = jax.ShapeDtypeStruct((m, d), BF16)
    return pl.pallas_call(
        functools.partial(_rwkv_prep_body, tiles_per_seq=seq // tm),
        grid=(m // tm,),
        in_specs=[pl.BlockSpec((tm, d), row),
                  pl.BlockSpec((SUBLANES, d), lambda i: (jnp.maximum(i * sub - 1, 0), 0)),
                  _resident(mu8.shape)] + [_resident(w.shape) for w in wts]
        + [_resident(vec8.shape), _resident((LANES, LANES))],
        out_specs=[pl.BlockSpec((tm, d), row)] * 7,
        out_shape=[out, jax.ShapeDtypeStruct((m, d), F32), out, out, out, out, out],
        compiler_params=_cparams(("parallel",)),
        name="rwkv_prep",
    )(x, x, mu8, *wts, vec8, _head_ones())


def _rwkv_wkv_body(r_ref, lw_ref, k_ref, v_ref, kk_ref, b_ref, tri_ref, y_ref, state_ref):
    c = RWKV_CHUNK
    two = 2 * c

    @pl.when(pl.program_id(1) == 0)
    def _():
        state_ref[...] = jnp.zeros(state_ref.shape, F32)

    tri = tri_ref[...]

    def decayed(bi):
        lw = lw_ref[bi]
        h1 = lw.astype(BF16)
        r1 = lw - h1.astype(F32)
        h2 = r1.astype(BF16)
        h3 = (r1 - h2.astype(F32)).astype(BF16)
        cum = _dot(tri, h1) + _dot(tri, h2) + _dot(tri, h3)
        gam = jnp.exp(cum)
        gam_inv = jnp.exp(-cum)
        return (r_ref[bi].astype(F32) * gam, kk_ref[bi].astype(F32) * jnp.exp(cum - lw),
                b_ref[bi].astype(F32) * gam_inv, k_ref[bi].astype(F32) * gam_inv, gam[c - 1:c, :])

    lane_lo = lax.broadcasted_iota(jnp.int32, (c, LANES), 1) < RWKV_HEAD
    row2 = lax.broadcasted_iota(jnp.int32, (two, two), 0)
    col2 = lax.broadcasted_iota(jnp.int32, (two, two), 1)
    same = (row2 // c) == (col2 // c)
    strict = jnp.logical_and(same, row2 > col2)
    incl = jnp.logical_and(same, row2 >= col2)
    eye = (row2 == col2).astype(F32)

    def stack_masked(t):
        return jnp.concatenate([jnp.where(lane_lo, t, 0.0), jnp.where(lane_lo, 0.0, t)], axis=0)

    nb = r_ref.shape[0]
    items = [(bi, p) for bi in range(nb) for p in range(D_MODEL // LANES)]
    idx = range(len(items))
    sls = [slice(p * LANES, (p + 1) * LANES) for _, p in items]
    dec = [decayed(bi) for bi in range(nb)]
    xs, xu, bds, kds, vss, gend = [], [], [], [], [], []
    for (bi, _), sl in zip(items, sls):
        rt, kkt, bt, kt, gam_end = dec[bi]
        xs.append(jnp.concatenate([stack_masked(kkt[:, sl]), stack_masked(rt[:, sl])], axis=0).astype(BF16))
        xu.append(jnp.concatenate([kkt[:, sl], rt[:, sl]], axis=0).astype(BF16))
        bds.append(jnp.concatenate([bt[:, sl], bt[:, sl]], axis=0).astype(BF16))
        kds.append(jnp.concatenate([kt[:, sl], kt[:, sl]], axis=0).astype(BF16))
        v2 = v_ref[bi, :, sl].astype(F32)
        vss.append(jnp.where(same, jnp.concatenate([v2, v2], axis=0), 0.0).astype(BF16))
        gend.append(gam_end[:, sl])
    s2s = [state_ref[bi, p] for bi, p in items]
    a_all = [_dot_nt(xs[i], jnp.concatenate([bds[i][:c], kds[i][:c]], axis=0)) for i in idx]
    x_state = [_dot_nt(xu[i], s2s[i].astype(BF16)) for i in idx]
    head0 = row2 < c
    nmat, lk, arb, ark = [], [], [], []
    for a in a_all:
        top, bot = a[:two], a[two:]
        top_r, bot_r = pltpu.roll(top, c, 1), pltpu.roll(bot, c, 1)
        nmat.append(jnp.where(strict, -jnp.where(head0, top, top_r), 0.0))
        lk.append(jnp.where(strict, jnp.where(head0, top_r, top), 0.0).astype(BF16))
        arb.append(jnp.where(incl, jnp.where(head0, bot, bot_r), 0.0).astype(BF16))
        ark.append(jnp.where(incl, jnp.where(head0, bot_r, bot), 0.0).astype(BF16))
    rhs = [stack_masked(x_state[i][:c]) + _dot(lk[i], vss[i]) for i in idx]
    pw = [n_.astype(BF16) for n_ in nmat]
    inv = [eye + n_ for n_ in nmat]
    pw = [_dot(t, t).astype(BF16) for t in pw]
    for _ in range(int(math.log2(c)) - 2):
        both = [_dot(pw[i], jnp.concatenate([pw[i], inv[i].astype(BF16)], axis=1)) for i in idx]
        inv = [inv[i] + both[i][:, two:] for i in idx]
        pw = [t[:, :two].astype(BF16) for t in both]
    inv = [inv[i] + _dot(pw[i], inv[i].astype(BF16)) for i in idx]
    ub = [(-_dot(inv[i].astype(BF16), rhs[i].astype(BF16))).astype(BF16) for i in idx]
    for i, (bi, _) in enumerate(items):
        ys = _dot(arb[i], ub[i]) + _dot(ark[i], vss[i])
        y_ref[bi, :, sls[i]] = x_state[i][c:] + ys[:c] + ys[c:]
    for i, (bi, p) in enumerate(items):
        ds = _dot_tn(ub[i], bds[i]) + _dot_tn(vss[i], kds[i])
        state_ref[bi, p] = jnp.where(same, (s2s[i] + ds) * gend[i], 0.0)


def _rwkv_wkv(r, lw, k, v, kk, b, batch, seq):
    c = RWKV_CHUNK
    n = seq // c
    d = D_MODEL
    tri = (jnp.arange(c)[:, None] >= jnp.arange(c)[None, :]).astype(BF16)
    nb = RWKV_BATCH_ROWS if batch % RWKV_BATCH_ROWS == 0 else 1
    blk = pl.BlockSpec((nb, c, d), lambda bi, i: (bi, i, 0))
    as3d = lambda t: t.reshape(batch, seq, d)
    y = pl.pallas_call(
        _rwkv_wkv_body,
        grid=(batch // nb, n),
        in_specs=[blk] * 6 + [_resident((c, c))],
        out_specs=blk,
        out_shape=jax.ShapeDtypeStruct((batch, seq, d), F32),
        scratch_shapes=[pltpu.VMEM((nb, d // LANES, LANES, LANES), F32)],
        compiler_params=_cparams(("parallel", "arbitrary")),
        name="rwkv_wkv",
    )(as3d(r), as3d(lw), as3d(k), as3d(v), as3d(kk), as3d(b), tri)
    return y.reshape(batch * seq, d)


def _rwkv_out_body(y_ref, r_ref, k_ref, v_ref, g_ref, vec_ref, bd_ref, w_ref, res_ref, lg_ref,
                   lb_ref, o_ref, a_ref):
    bd = bd_ref[...]
    inv_n = 1.0 / RWKV_HEAD
    for s in range(D_MODEL // LANES):
        sl = slice(s * LANES, (s + 1) * LANES)
        y = y_ref[:, sl]
        mu = _head_sum(y, bd) * inv_n
        dlt = y - mu
        var = _head_sum(dlt * dlt, bd) * inv_n
        yn = dlt * lax.rsqrt(var + RWKV_GN_EPS) * vec_ref[0:1, sl] + vec_ref[1:2, sl]
        rk = r_ref[:, sl] * k_ref[:, sl] * vec_ref[2:3, sl].astype(BF16)
        bonus = _head_sum(rk, bd) * v_ref[:, sl].astype(F32)
        a_ref[:, sl] = ((yn + bonus) * g_ref[:, sl].astype(F32)).astype(BF16)
    acc = _dot(a_ref[...], w_ref[...])
    o_ref[...] = _layer_norm(DN_ALPHA * res_ref[...] + acc, lg_ref[...], lb_ref[...])


def _rwkv_out(y, r, k, v, g, vec, w, res, lg, lb, *, tm=512):
    m, d = res.shape
    row = lambda i: (i, 0)
    act = pl.BlockSpec((tm, d), row)
    return pl.pallas_call(
        _rwkv_out_body,
        grid=(m // tm,),
        in_specs=[act] * 5 + [_resident(vec.shape), _resident((LANES, LANES)), _resident((d, d)),
                              act, _resident((1, d)), _resident((1, d))],
        out_specs=act,
        out_shape=jax.ShapeDtypeStruct((m, d), F32),
        scratch_shapes=[pltpu.VMEM((tm, d), BF16)],
        compiler_params=_cparams(("parallel",)),
        name="rwkv_out",
    )(y, r, k, v, g, vec, _head_ones(), w, res, lg, lb)


def kernel(x, ret_w_in, ret_gn, ret_w_out, dil_w_in, dil_w_out, mla_w_down, mla_norm_q,
           mla_norm_kv, mla_w_uq, mla_w_ukv, mla_w_out, rwkv_mu, rwkv_w_rkv, rwkv_w_out,
           rwkv_vec, rwkv_lora_a, rwkv_lora_b, rwkv_gate_a, rwkv_gate_b, rwkv_ln_x,
           mlp_w1, mlp_w2, ln_g, ln_b):
    batch, seq, d = x.shape
    xf = x.reshape(batch * seq, d)
    w1_all, w2_all = mlp_w1.astype(BF16), mlp_w2.astype(BF16)
    n_mixers = 4
    for i in range(DEPTH):
        mixer, j = i % n_mixers, i // n_mixers
        lg, lb = ln_g[i, 0][None, :], ln_b[i, 0][None, :]
        if mixer == 0:
            proj = _mm(xf, ret_w_in[j].astype(BF16), tm=512, name="ret_proj")
            gated = _retention(proj, ret_gn[j], batch, seq)
            xf = _mm_res_ln(gated, ret_w_out[j].astype(BF16), xf, lg, lb, name="ret_out")
        elif mixer == 1:
            outs, lses = _dilated(xf, dil_w_in[j].astype(BF16), batch, seq)
            xf = _dil_out(outs, lses, dil_w_out[j].astype(BF16), xf, lg, lb, seq)
        elif mixer == 2:
            o = _mla(xf, mla_w_down[j], mla_norm_q[j], mla_norm_kv[j], mla_w_uq[j], mla_w_ukv[j],
                     batch, seq)
            xf = _mm_res_ln(o, mla_w_out[j].astype(BF16), xf, lg, lb, name="mla_out")
        else:
            r, lw, k, v, kk, b, g = _rwkv_prep(xf, rwkv_mu[j], rwkv_w_rkv[j], rwkv_vec[j],
                                               rwkv_lora_a[j], rwkv_lora_b[j], rwkv_gate_a[j],
                                               rwkv_gate_b[j], seq)
            y = _rwkv_wkv(r, lw, k, v, kk, b, batch, seq)
            vec = jnp.concatenate([rwkv_ln_x[j], rwkv_vec[j][4:5],
                                   jnp.zeros((SUBLANES - 3, d), F32)], axis=0)
            xf = _rwkv_out(y, r, k, v, g, vec, rwkv_w_out[j].astype(BF16), xf, lg, lb)
        xf = _mlp(xf, w1_all, w2_all, i, ln_g[i, 1][None, :], ln_b[i, 1][None, :])
    return xf.reshape(batch, seq, d)
```

```python
import functools
import math

import jax
import jax.numpy as jnp
import numpy as np
from jax import lax
from jax.experimental import pallas as pl
from jax.experimental.pallas import tpu as pltpu

F32 = jnp.float32
BF16 = jnp.bfloat16

D_MODEL = 1024
DEPTH = 4
D_FF = 4 * D_MODEL
LN_EPS = 1e-5
RMS_EPS = 1e-6
GN_EPS = 1e-5
DN_ALPHA = (2.0 * DEPTH) ** 0.25
NEG = -1e30
LANES = 128
SUBLANES = 8
ROPE_PARTNER = 64
MM_SUB = 256

RET_HEADS = 4
RET_QK_DIM = 256
RET_V_DIM = 512
RET_CHUNK = 512
RET_THETA = 10000.0

DIL_PAIRS = ((128, 1), (512, 4), (2048, 16))
DIL_HEADS = 8
DIL_HEAD_DIM = 128
DIL_ROT = 32
DIL_BLOCK = 128
DIL_TQ = 512
DIL_WAVE = 4
ROPE_THETA = 500000.0

MLA_HEADS = 16
MLA_NOPE = 128
MLA_ROPE = 64
MLA_V = 128
MLA_Q_RANK = 256
MLA_KV_RANK = 128
MLA_THETA = 10000.0
MLA_TQ = 1024
MLA_TK = 1024
MLA_ROW_SPLIT = 4
assert MLA_TQ == MLA_TK

RWKV_HEAD = 64
RWKV_HEADS = D_MODEL // RWKV_HEAD
RWKV_GN_EPS = 64e-5
RWKV_CHUNK = 64
RWKV_GATE_PAD = 256
RWKV_BATCH_ROWS = 4

V7X_VMEM_BYTES = 64 * 1024 * 1024
VMEM_LIMIT = V7X_VMEM_BYTES // 8 * 7


def _cparams(sem):
    return pltpu.CompilerParams(dimension_semantics=sem, vmem_limit_bytes=VMEM_LIMIT)


def _resident(shape):
    nd = len(shape)
    return pl.BlockSpec(shape, lambda *_: (0,) * nd, pipeline_mode=pl.Buffered(1))


def _layer_norm(z, g, b):
    mu = jnp.mean(z, axis=-1, keepdims=True)
    d = z - mu
    var = jnp.mean(d * d, axis=-1, keepdims=True)
    return d * lax.rsqrt(var + LN_EPS) * g + b


def _dot(a, b):
    return jnp.dot(a, b, preferred_element_type=F32)


def _dot_nt(a, b):
    return lax.dot_general(a, b, (((1,), (1,)), ((), ())), preferred_element_type=F32)


def _dot_tn(a, b):
    return lax.dot_general(a, b, (((0,), (0,)), ((), ())), preferred_element_type=F32)


def _rope_tile(a, c, s):
    return a * c + pltpu.roll(a, ROPE_PARTNER, 1) * s


def _project_columns(xb, w_ref, store, modes, tabs, scale):
    n = w_ref.shape[1]
    for c0 in range(0, n, MM_SUB):
        acc = _dot(xb, w_ref[:, c0:c0 + MM_SUB])
        for t in range(MM_SUB // LANES):
            a = acc[:, t * LANES:(t + 1) * LANES]
            mode = modes[c0 // LANES + t] if modes is not None else None
            if mode == "scale":
                a = a * scale
            elif mode is not None:
                c_ref, s_ref = tabs
                a = _rope_tile(a, c_ref[mode[1]], s_ref[mode[1]])
            store(c0 + t * LANES, a)


def _mm_body(x_ref, w_ref, *rest, modes, scale):
    tabs, o_ref = rest[:-1], rest[-1]

    def store(c0, a):
        o_ref[:, c0:c0 + LANES] = a.astype(o_ref.dtype)

    _project_columns(x_ref[...].astype(BF16), w_ref, store, modes, tabs, scale)


def _mm(x, w, *, tm, out_dtype=BF16, tabs=None, tab_map=None, modes=None, scale=1.0, name="mm"):
    m = x.shape[0]
    k, n = w.shape
    row = lambda i: (i, 0)
    in_specs = [pl.BlockSpec((tm, k), row), _resident((k, n))]
    args = [x, w]
    if tabs is not None:
        for t in tabs:
            in_specs.append(pl.BlockSpec((t.shape[0], tm, LANES), tab_map))
            args.append(t)
    return pl.pallas_call(
        functools.partial(_mm_body, modes=modes, scale=scale),
        grid=(m // tm,),
        in_specs=in_specs,
        out_specs=pl.BlockSpec((tm, n), row),
        out_shape=jax.ShapeDtypeStruct((m, n), out_dtype),
        compiler_params=_cparams(("parallel",)),
        name=name,
    )(*args)


def _mm_res_ln_body(a_ref, w_ref, res_ref, g_ref, b_ref, o_ref):
    half = a_ref.shape[0] // 2
    for r0 in (0, half):
        acc = _dot(a_ref[r0:r0 + half, :], w_ref[...])
        o_ref[r0:r0 + half, :] = _layer_norm(DN_ALPHA * res_ref[r0:r0 + half, :] + acc,
                                             g_ref[...], b_ref[...])


def _mm_res_ln(a, w, res, g, b, *, tm=512, name="mm_res_ln"):
    m, k = a.shape
    d = w.shape[1]
    row = lambda i: (i, 0)
    return pl.pallas_call(
        _mm_res_ln_body,
        grid=(m // tm,),
        in_specs=[pl.BlockSpec((tm, k), row), _resident((k, d)), pl.BlockSpec((tm, d), row),
                  _resident((1, d)), _resident((1, d))],
        out_specs=pl.BlockSpec((tm, d), row),
        out_shape=jax.ShapeDtypeStruct((m, d), F32),
        compiler_params=_cparams(("parallel",)),
        name=name,
    )(a, w, res, g, b)


def _mlp_body(x_ref, w1_ref, w2_ref, g_ref, b_ref, o_ref, *, fchunk):
    half = x_ref.shape[0] // 2
    for r0 in (0, half):
        x = x_ref[r0:r0 + half, :]
        xb = x.astype(BF16)
        acc = jnp.zeros(x.shape, F32)
        for c in range(D_FF // fchunk):
            h = _dot(xb, w1_ref[:, c * fchunk:(c + 1) * fchunk])
            h = jnp.maximum(h, 0.0)
            h = (h * h).astype(BF16)
            acc = acc + _dot(h, w2_ref[c * fchunk:(c + 1) * fchunk, :])
        o_ref[r0:r0 + half, :] = _layer_norm(DN_ALPHA * x + acc, g_ref[...], b_ref[...])


def _mlp(x, w1_all, w2_all, layer, g, b, *, tm=512, fchunk=1024):
    m, d = x.shape
    row = lambda i: (i, 0)
    slab = lambda shape: pl.BlockSpec((None,) + shape, lambda i: (layer, 0, 0),
                                      pipeline_mode=pl.Buffered(1))
    return pl.pallas_call(
        functools.partial(_mlp_body, fchunk=fchunk),
        grid=(m // tm,),
        in_specs=[pl.BlockSpec((tm, d), row), slab((d, D_FF)), slab((D_FF, d)),
                  _resident((1, d)), _resident((1, d))],
        out_specs=pl.BlockSpec((tm, d), row),
        out_shape=jax.ShapeDtypeStruct((m, d), F32),
        compiler_params=_cparams(("parallel",)),
        name="mlp",
    )(x, w1_all, w2_all, g, b)


def _ret_body(q_ref, k_ref, v_ref, g_ref, cos_ref, sin_ref, intra_ref, qdec_ref, kdec_ref,
              cdec_ref, gn_ref, o_ref, state_ref):
    dk, dv, half = RET_QK_DIM, RET_V_DIM, RET_QK_DIM // 2

    @pl.when(pl.program_id(1) == 0)
    def _():
        state_ref[...] = jnp.zeros(state_ref.shape, F32)

    cos = cos_ref[...]
    sin = sin_ref[...]

    def rope(t):
        t1 = t[:, :half].astype(F32)
        t2 = t[:, half:].astype(F32)
        return jnp.concatenate([t1 * cos - t2 * sin, t2 * cos + t1 * sin], axis=-1)

    heads = range(RET_HEADS)
    q = [rope(q_ref[:, h * dk:(h + 1) * dk]) for h in heads]
    k = [rope(k_ref[:, h * dk:(h + 1) * dk]) * (dk ** -0.5) for h in heads]
    v = [v_ref[:, h * dv:(h + 1) * dv] for h in heads]
    qb = [t.astype(BF16) for t in q]
    state = [state_ref[h] for h in heads]
    scores = [(_dot_nt(qb[h], k[h].astype(BF16)) * intra_ref[h]).astype(BF16) for h in heads]
    cross = [_dot(qb[h], state[h].astype(BF16)) * qdec_ref[h] for h in heads]
    o = [_dot(scores[h], v[h]) + cross[h] for h in heads]
    for h in heads:
        state_ref[h] = (state[h] * cdec_ref[h, 0:1, :]
                        + _dot_tn((k[h] * kdec_ref[h]).astype(BF16), v[h]))
    for h in heads:
        sl = slice(h * dv, (h + 1) * dv)
        mu = jnp.mean(o[h], axis=-1, keepdims=True)
        d = o[h] - mu
        var = jnp.mean(d * d, axis=-1, keepdims=True)
        on = d * lax.rsqrt(var + GN_EPS) * gn_ref[0:1, sl] + gn_ref[1:2, sl]
        gate = g_ref[:, sl].astype(F32)
        gate = gate * (1.0 / (1.0 + jnp.exp(-gate)))
        o_ref[:, sl] = (gate * on).astype(o_ref.dtype)


def _retention(proj, gn, batch, seq):
    h_, dk, dv, c = RET_HEADS, RET_QK_DIM, RET_V_DIM, RET_CHUNK
    n = seq // c
    half = dk // 2
    f32 = np.float32
    pos = np.arange(seq, dtype=f32)
    inv_freq = f32(RET_THETA) ** (-np.arange(half, dtype=f32) / f32(half))
    ang = pos[:, None] * inv_freq[None, :]
    cos, sin = np.cos(ang), np.sin(ang)
    log_gamma = np.log(f32(1.0) - f32(2.0) ** (f32(-5.0) - np.arange(h_, dtype=f32)))
    idx = np.arange(c, dtype=f32)
    diff = idx[:, None] - idx[None, :]
    intra = np.where(diff >= 0, np.exp(log_gamma[:, None, None] * np.maximum(diff, f32(0.0))), f32(0.0))
    intra = intra.astype(f32)
    qdec = np.broadcast_to(np.exp(log_gamma[:, None] * (idx + f32(1.0)))[:, :, None], (h_, c, dv))
    kdec = np.broadcast_to(np.exp(log_gamma[:, None] * (f32(c - 1.0) - idx))[:, :, None], (h_, c, dk))
    cdec = np.broadcast_to(np.exp(log_gamma * f32(c))[:, None, None], (h_, SUBLANES, dv))
    qdec, kdec, cdec = (np.ascontiguousarray(t, dtype=f32) for t in (qdec, kdec, cdec))
    qk_w, vg_w = h_ * dk, h_ * dv
    return pl.pallas_call(
        _ret_body,
        grid=(batch, n),
        in_specs=[
            pl.BlockSpec((c, qk_w), lambda b, i: (b * n + i, 0)),
            pl.BlockSpec((c, qk_w), lambda b, i: (b * n + i, 1)),
            pl.BlockSpec((c, vg_w), lambda b, i: (b * n + i, 2 * qk_w // vg_w)),
            pl.BlockSpec((c, vg_w), lambda b, i: (b * n + i, 2 * qk_w // vg_w + 1)),
            pl.BlockSpec((c, half), lambda b, i: (i, 0)),
            pl.BlockSpec((c, half), lambda b, i: (i, 0)),
            _resident((h_, c, c)), _resident((h_, c, dv)), _resident((h_, c, dk)),
            _resident((h_, SUBLANES, dv)), _resident((2, vg_w)),
        ],
        out_specs=pl.BlockSpec((c, vg_w), lambda b, i: (b * n + i, 0)),
        out_shape=jax.ShapeDtypeStruct((batch * seq, vg_w), BF16),
        scratch_shapes=[pltpu.VMEM((h_, dk, dv), F32)],
        compiler_params=_cparams(("parallel", "arbitrary")),
        name="retention",
    )(proj, proj, proj, proj, cos, sin, intra, qdec, kdec, cdec, gn)


def _dil_attn_body(q_ref, kp_ref, kc_ref, vp_ref, vc_ref, o_ref, lse_ref, *, seq_blocks):
    blk = DIL_BLOCK
    n_sub = q_ref.shape[0] // blk
    has_prev = pl.program_id(1) > 0
    qi = lax.broadcasted_iota(jnp.int32, (blk, 2 * blk), 0)
    ki = lax.broadcasted_iota(jnp.int32, (blk, 2 * blk), 1)
    band = jnp.logical_and(ki >= qi, ki <= qi + blk)
    band_first = jnp.logical_and(band, jnp.logical_or(ki >= blk, has_prev))
    band_start = jnp.logical_and(band, ki >= blk)

    def mask(j):
        if j == 0:
            return band_first
        return band_start if j % seq_blocks == 0 else band
    lane = lax.broadcasted_iota(jnp.int32, (blk, LANES), 1)
    ones = jnp.ones((2 * blk, LANES), BF16)
    def keys(prev_ref, cur_ref, j, sl):
        if j == 0:
            return jnp.concatenate([prev_ref[:, sl], cur_ref[:blk, sl]], axis=0)
        return cur_ref[(j - 1) * blk:(j + 1) * blk, sl]

    for j0 in range(0, n_sub, DIL_WAVE):
        items = [(j, h) for j in range(j0, min(j0 + DIL_WAVE, n_sub)) for h in range(DIL_HEADS)]
        sls = [slice(h * DIL_HEAD_DIM, (h + 1) * DIL_HEAD_DIM) for _, h in items]
        rows = [slice(j * blk, (j + 1) * blk) for j, _ in items]
        s = [jnp.where(mask(j),
                       _dot_nt(q_ref[rows[i], sls[i]], keys(kp_ref, kc_ref, j, sls[i])), NEG)
             for i, (j, _) in enumerate(items)]
        m = [jnp.max(t, axis=-1, keepdims=True) for t in s]
        p = [jnp.exp((s[i] - m[i]).astype(BF16)) for i in range(len(items))]
        pv = [_dot(p[i], jnp.concatenate([keys(vp_ref, vc_ref, j, sls[i]), ones], axis=1))
              for i, (j, _) in enumerate(items)]
        lse_tiles = {j: jnp.zeros((blk, LANES), F32) for j, _ in items}
        for i, (j, h) in enumerate(items):
            l = pv[i][:, DIL_HEAD_DIM:]
            o_ref[rows[i], sls[i]] = (pv[i][:, :DIL_HEAD_DIM] / l).astype(o_ref.dtype)
            lse_tiles[j] = jnp.where(lane == h, m[i] + jnp.log(l), lse_tiles[j])
        for j, tile in lse_tiles.items():
            lse_ref[j * blk:(j + 1) * blk, :] = tile


def _dil_attn(q, k, v, batch, seq, dil):
    blk = DIL_BLOCK
    hd = DIL_HEADS * DIL_HEAD_DIM
    sub = seq // dil
    tq = DIL_TQ
    nb = max(sub // tq, 1)
    n_seq = batch * dil * sub // (nb * tq)
    per = tq // blk
    cur = lambda z, i: (z * nb + i, 0)
    prev = lambda z, i: (jnp.maximum((z * nb + i) * per - 1, 0), 0)
    out_map = cur
    return pl.pallas_call(
        functools.partial(_dil_attn_body, seq_blocks=sub // blk),
        grid=(n_seq, nb),
        in_specs=[pl.BlockSpec((tq, hd), cur), pl.BlockSpec((blk, hd), prev),
                  pl.BlockSpec((tq, hd), cur), pl.BlockSpec((blk, hd), prev),
                  pl.BlockSpec((tq, hd), cur)],
        out_specs=[pl.BlockSpec((tq, hd), out_map), pl.BlockSpec((tq, LANES), out_map)],
        out_shape=[jax.ShapeDtypeStruct((batch * seq, hd), BF16),
                   jax.ShapeDtypeStruct((batch * seq, LANES), F32)],
        compiler_params=_cparams(("parallel", "arbitrary")),
        name=f"dil_attn_{dil}",
    )(q, k, k, v, v)


def _dil_out_body(o0_ref, o1_ref, o2_ref, l0_ref, l1_ref, l2_ref, e_ref, w_ref, res_ref,
                  g_ref, b_ref, o_ref, osc_ref, lsc_ref, mix_ref):
    tm = o_ref.shape[0]
    nh = DIL_HEADS
    for gi, (og, lg) in enumerate(((o0_ref, l0_ref), (o1_ref, l1_ref), (o2_ref, l2_ref))):
        dil = og.shape[1]
        n = tm // dil
        for r in range(dil):
            rows = pl.ds(r, n, stride=dil) if dil > 1 else slice(None)
            for h in range(nh):
                osc_ref[gi * nh + h, rows, :] = og[0, r, :, h * LANES:(h + 1) * LANES].astype(F32)
            lsc_ref[gi, rows, :] = lg[0, r]
    l0, l1, l2 = lsc_ref[0], lsc_ref[1], lsc_ref[2]
    m = jnp.maximum(jnp.maximum(l0, l1), l2)
    e0, e1, e2 = jnp.exp(l0 - m), jnp.exp(l1 - m), jnp.exp(l2 - m)
    den = e0 + e1 + e2
    e = e_ref[...]
    wfull = []
    for eg in (e0, e1, e2):
        wfull.append(_dot((eg / den).astype(BF16), e))
    for h in range(nh):
        sl = slice(h * LANES, (h + 1) * LANES)
        mixed = sum(wfull[gi][:, sl] * osc_ref[gi * nh + h] for gi in range(3))
        mix_ref[:, sl] = mixed.astype(BF16)
    acc = _dot(mix_ref[...], w_ref[...])
    o_ref[...] = _layer_norm(DN_ALPHA * res_ref[...] + acc, g_ref[...], b_ref[...])


def _dil_out(outs, lses, w, res, g, b, seq, *, tm=512):
    m, d = res.shape
    hd = DIL_HEADS * DIL_HEAD_DIM
    expand = (jnp.arange(LANES)[:, None] == (jnp.arange(hd) // DIL_HEAD_DIM)[None, :]).astype(BF16)
    row = lambda i: (i, 0)
    nt = seq // tm
    grp = lambda i: (i // nt, 0, i % nt, 0)
    dils = [dil for _, dil in DIL_PAIRS]
    batch = m // seq
    o4 = [o.reshape(batch, dil, seq // dil, hd) for o, dil in zip(outs, dils)]
    l4 = [l.reshape(batch, dil, seq // dil, LANES) for l, dil in zip(lses, dils)]
    return pl.pallas_call(
        _dil_out_body,
        grid=(m // tm,),
        in_specs=[pl.BlockSpec((1, dil, tm // dil, hd), grp) for dil in dils]
        + [pl.BlockSpec((1, dil, tm // dil, LANES), grp) for dil in dils]
        + [_resident((LANES, hd)), _resident((hd, d)), pl.BlockSpec((tm, d), row),
           _resident((1, d)), _resident((1, d))],
        out_specs=pl.BlockSpec((tm, d), row),
        out_shape=jax.ShapeDtypeStruct((m, d), F32),
        scratch_shapes=[pltpu.VMEM((len(dils) * DIL_HEADS, tm, LANES), F32),
                        pltpu.VMEM((len(dils), tm, LANES), F32), pltpu.VMEM((tm, hd), BF16)],
        compiler_params=_cparams(("parallel",)),
        name="dil_out",
    )(*o4, *l4, expand, w, res, g, b)


def _rope_tables(seq, rot, theta, scale, passthrough):
    half = rot // 2
    inv_freq = np.float32(theta) ** (-np.arange(half, dtype=np.float32) / np.float32(half))
    ang = np.arange(seq, dtype=np.float32)[:, None] * inv_freq[None, :]
    cos, sin = np.cos(ang), np.sin(ang)
    fill = np.full((seq, ROPE_PARTNER - half), passthrough, np.float32)
    zero = np.zeros((seq, ROPE_PARTNER - half), np.float32)
    c = np.concatenate([cos, fill, cos, fill], axis=1)
    s = np.concatenate([-sin, zero, sin, zero], axis=1)
    return (c * np.float32(scale)).astype(np.float32), (s * np.float32(scale)).astype(np.float32)


def _rope_lane_order(rot, width):
    half = rot // 2
    rest = list(range(rot, width))
    cut = ROPE_PARTNER - half
    return jnp.array(list(range(half)) + rest[:cut] + list(range(half, rot)) + rest[cut:])


def _dil_proj_body(x_ref, w_ref, c_ref, s_ref, q_ref, k_ref, v_ref, xb_ref, xs_ref, *, dil):
    tm = x_ref.shape[0]
    n = tm // dil
    if dil == 1:
        xb = x_ref[...].astype(BF16)
    else:
        for c in range(D_MODEL // LANES):
            xs_ref[c] = x_ref[:, c * LANES:(c + 1) * LANES]
        for r in range(dil):
            for c in range(D_MODEL // LANES):
                xb_ref[r * n:(r + 1) * n, c * LANES:(c + 1) * LANES] = (
                    xs_ref[c, pl.ds(r, n, stride=dil), :].astype(BF16))
        xb = xb_ref[...]

    hd = DIL_HEADS * DIL_HEAD_DIM

    def store(c0, a):
        o_ref = (q_ref, k_ref, v_ref)[c0 // hd]
        o_ref[0, :, :, c0 % hd:c0 % hd + LANES] = a.astype(o_ref.dtype).reshape(dil, n, LANES)

    modes = [("rope", 0)] * DIL_HEADS + [("rope", 1)] * DIL_HEADS + [None] * DIL_HEADS
    _project_columns(xb, w_ref, store, modes, (c_ref, s_ref), 1.0)


def _dil_proj(x, w, tabs, batch, seq, dil, *, tm=512):
    hd = DIL_HEADS * DIL_HEAD_DIM
    nt = seq // tm
    n = tm // dil
    tab_spec = pl.BlockSpec((2, tm, LANES), lambda i: (0, i % nt, 0))
    return pl.pallas_call(
        functools.partial(_dil_proj_body, dil=dil),
        grid=(batch * nt,),
        in_specs=[pl.BlockSpec((tm, D_MODEL), lambda i: (i, 0)), _resident((D_MODEL, 3 * hd)),
                  tab_spec, tab_spec],
        out_specs=[pl.BlockSpec((1, dil, n, hd), lambda i: (i // nt, 0, i % nt, 0))] * 3,
        out_shape=[jax.ShapeDtypeStruct((batch, dil, seq // dil, hd), BF16)] * 3,
        scratch_shapes=[pltpu.VMEM((tm, D_MODEL), BF16),
                        pltpu.VMEM((D_MODEL // LANES, tm, LANES), F32)],
        compiler_params=_cparams(("parallel",)),
        name=f"dil_proj_{dil}",
    )(x, w, *tabs)


def _dilated(x, w_in, batch, seq, *, tm=512):
    hd = DIL_HEADS * DIL_HEAD_DIM
    cq, sq = _rope_tables(seq, DIL_ROT, ROPE_THETA, DIL_HEAD_DIM ** -0.5, 1.0)
    ck, sk = _rope_tables(seq, DIL_ROT, ROPE_THETA, 1.0, 1.0)
    order = _rope_lane_order(DIL_ROT, DIL_HEAD_DIM)
    outs, lses = [], []
    for gi, (_, dil) in enumerate(DIL_PAIRS):
        wg = w_in[:, gi * 3 * hd:(gi + 1) * 3 * hd].reshape(D_MODEL, 3, DIL_HEADS, DIL_HEAD_DIM)
        wg = jnp.concatenate([wg[:, :2][..., order], wg[:, 2:]], axis=1).reshape(D_MODEL, 3 * hd)

        def by_residue(t):
            t = t.reshape(seq // tm, tm // dil, dil, LANES)
            return np.swapaxes(t, 1, 2).reshape(seq, LANES)

        tabs = [np.stack([by_residue(a), by_residue(b)]) for a, b in ((cq, ck), (sq, sk))]
        q, k, v = (t.reshape(batch * seq, hd) for t in _dil_proj(x, wg, tabs, batch, seq, dil, tm=tm))
        o, lse = _dil_attn(q, k, v, batch, seq, dil)
        outs.append(o)
        lses.append(lse)
    return outs, lses


def _mla_proj_body(x_ref, wd_ref, nq_ref, nkv_ref, wq_ref, wkv_ref, c_ref, s_ref,
                   q_ref, kv_ref, kpe_ref, *, scale):
    acc = _dot(x_ref[...].astype(BF16), wd_ref[...])
    cq = acc[:, :MLA_Q_RANK]
    ckv = acc[:, MLA_Q_RANK:MLA_Q_RANK + MLA_KV_RANK]
    kpe = acc[:, MLA_Q_RANK + MLA_KV_RANK:]
    cq = cq * lax.rsqrt(jnp.mean(cq * cq, axis=-1, keepdims=True) + RMS_EPS) * nq_ref[...]
    ckv = ckv * lax.rsqrt(jnp.mean(ckv * ckv, axis=-1, keepdims=True) + RMS_EPS) * nkv_ref[...]
    kpe_ref[...] = _rope_tile(kpe, c_ref[0], s_ref[0]).astype(kpe_ref.dtype)

    def store_q(c0, a):
        q_ref[:, c0:c0 + LANES] = a.astype(q_ref.dtype)

    def store_kv(c0, a):
        kv_ref[:, c0:c0 + LANES] = a.astype(kv_ref.dtype)

    _project_columns(cq.astype(BF16), wq_ref, store_q, ["scale", ("rope", 1)] * MLA_HEADS,
                     (c_ref, s_ref), scale)
    _project_columns(ckv.astype(BF16), wkv_ref, store_kv, None, None, 1.0)


def _mla_proj(x, wd, nq, nkv, wq, wkv, tabs, seq, scale, *, tm=512):
    m, d = x.shape
    row = lambda i: (i, 0)
    ns = seq // tm
    tab = pl.BlockSpec((2, tm, LANES), lambda i: (0, i % ns, 0))
    return pl.pallas_call(
        functools.partial(_mla_proj_body, scale=scale),
        grid=(m // tm,),
        in_specs=[pl.BlockSpec((tm, d), row), _resident(wd.shape), _resident((1, MLA_Q_RANK)),
                  _resident((1, MLA_KV_RANK)), _resident(wq.shape), _resident(wkv.shape), tab, tab],
        out_specs=[pl.BlockSpec((tm, wq.shape[1]), row), pl.BlockSpec((tm, wkv.shape[1]), row),
                   pl.BlockSpec((tm, LANES), row)],
        out_shape=[jax.ShapeDtypeStruct((m, wq.shape[1]), BF16),
                   jax.ShapeDtypeStruct((m, wkv.shape[1]), BF16),
                   jax.ShapeDtypeStruct((m, LANES), BF16)],
        compiler_params=_cparams(("parallel",)),
        name="mla_proj",
    )(x, wd, nq, nkv, wq, wkv, *tabs)


def _mla_flash_body(q_ref, kn_ref, kpe_ref, v_ref, o_ref, kcat_ref, vaug_ref):
    tq, tk, sub = MLA_TQ, MLA_TK, MLA_TQ // MLA_ROW_SPLIT
    kcat_ref[:, :MLA_NOPE] = kn_ref[...]
    kcat_ref[:, MLA_NOPE:] = kpe_ref[...]
    vaug_ref[:, :MLA_V] = v_ref[...]
    vaug_ref[:, MLA_V:] = jnp.ones((v_ref.shape[0], LANES), BF16)

    parts = range(MLA_ROW_SPLIT)
    col_minus_row = (lax.broadcasted_iota(jnp.int32, (sub, tk), 1)
                     - lax.broadcasted_iota(jnp.int32, (sub, tk), 0))

    def scores(tile, chunk, diagonal):
        out = []
        for part in parts:
            width = (part + 1) * sub if diagonal else tk
            q0 = tile * tq + part * sub
            s = _dot_nt(q_ref[q0:q0 + sub, :], kcat_ref[chunk * tk:chunk * tk + width, :])
            if diagonal:
                s = jnp.where(col_minus_row[:, :width] <= part * sub, s, NEG)
            out.append(s)
        return out

    def update(chunk, s_all, carry):
        out = []
        for s, (m, acc) in zip(s_all, carry):
            vb = vaug_ref[chunk * tk:chunk * tk + s.shape[1], :]
            m_new = jnp.maximum(m, jnp.max(s, axis=-1, keepdims=True))
            alpha = jnp.exp2(m - m_new)
            p = jnp.exp2((s - m_new).astype(BF16))
            out.append((m_new, alpha * acc + _dot(p, vb)))
        return out

    for tile in range(q_ref.shape[0] // tq):
        n_chunks = tile + 1
        carry = [(jnp.full((sub, 1), NEG, F32), jnp.zeros((sub, MLA_V + LANES), F32)) for _ in parts]
        s = scores(tile, 0, n_chunks == 1)
        for c in range(n_chunks):
            s_next = scores(tile, c + 1, c + 2 == n_chunks) if c + 1 < n_chunks else None
            carry = update(c, s, carry)
            s = s_next
        for part, (_, acc) in enumerate(carry):
            q0 = tile * tq + part * sub
            o_ref[q0:q0 + sub, :] = (acc[:, :MLA_V] / acc[:, MLA_V:]).astype(o_ref.dtype)


def _mla_flash(q, kv, kpe, batch, seq):
    h_ = MLA_HEADS
    qw = MLA_NOPE + LANES
    return pl.pallas_call(
        _mla_flash_body,
        grid=(batch, h_),
        in_specs=[pl.BlockSpec((seq, qw), lambda b, h: (b, h)),
                  pl.BlockSpec((seq, MLA_NOPE), lambda b, h: (b, h)),
                  pl.BlockSpec((seq, LANES), lambda b, h: (b, 0)),
                  pl.BlockSpec((seq, MLA_V), lambda b, h: (b, h_ + h))],
        out_specs=pl.BlockSpec((seq, MLA_V), lambda b, h: (b, h)),
        out_shape=jax.ShapeDtypeStruct((batch * seq, h_ * MLA_V), BF16),
        scratch_shapes=[pltpu.VMEM((seq, qw), BF16), pltpu.VMEM((seq, MLA_V + LANES), BF16)],
        compiler_params=_cparams(("parallel", "parallel")),
        name="mla_flash",
    )(q, kv, kpe, kv)


def _mla(x, w_down, norm_q, norm_kv, w_uq, w_ukv, batch, seq):
    h_ = MLA_HEADS
    half = MLA_ROPE // 2

    def pe_tile(w):
        z = jnp.zeros(w.shape[:-1] + (ROPE_PARTNER - half,), w.dtype)
        return jnp.concatenate([w[..., :half], z, w[..., half:], z], axis=-1)

    n_lat = MLA_Q_RANK + MLA_KV_RANK
    wd = jnp.concatenate([w_down[:, :n_lat], pe_tile(w_down[:, n_lat:])], axis=1).astype(BF16)
    wq = w_uq.reshape(MLA_Q_RANK, h_, MLA_NOPE + MLA_ROPE)
    wq = jnp.concatenate([wq[..., :MLA_NOPE], pe_tile(wq[..., MLA_NOPE:])], axis=-1)
    wq = wq.reshape(MLA_Q_RANK, -1).astype(BF16)
    wkv = w_ukv.reshape(MLA_KV_RANK, h_, MLA_NOPE + MLA_V)
    wkv = jnp.concatenate([wkv[:, :, :MLA_NOPE].reshape(MLA_KV_RANK, -1),
                           wkv[:, :, MLA_NOPE:].reshape(MLA_KV_RANK, -1)], axis=1).astype(BF16)
    scale = (MLA_NOPE + MLA_ROPE) ** -0.5 * math.log2(math.e)
    tk = _rope_tables(seq, MLA_ROPE, MLA_THETA, 1.0, 0.0)
    tq = _rope_tables(seq, MLA_ROPE, MLA_THETA, scale, 0.0)
    tabs = [np.stack([a, b]) for a, b in zip(tk, tq)]
    q, kv, kpe = _mla_proj(x, wd, norm_q[None, :], norm_kv[None, :], wq, wkv, tabs, seq, scale)
    return _mla_flash(q, kv, kpe, batch, seq)


def _head_sum(z, ones_bd):
    return _dot(z.astype(BF16), ones_bd)


def _rwkv_prep_body(x_ref, xp_ref, mu_ref, wr_ref, wk_ref, wv_ref, la0_ref, lb0_ref, la1_ref,
                    lb1_ref, ga_ref, gb_ref, vec_ref, bd_ref, r_ref, lw_ref, k_ref, v_ref,
                    kk_ref, b_ref, g_ref, *, tiles_per_seq):
    x = x_ref[...]
    tm = x.shape[0]
    first = pl.program_id(0) % tiles_per_seq == 0
    prev_row = jnp.where(first, 0.0, xp_ref[SUBLANES - 1:SUBLANES, :])
    rows = lax.broadcasted_iota(jnp.int32, x.shape, 0)
    shifted = jnp.where(rows == 0, prev_row, pltpu.roll(x, 1, 0))
    xb = x.astype(BF16)
    xxb = (shifted - x).astype(BF16)
    mix = lambda i: xb + xxb * mu_ref[i:i + 1, :].astype(BF16)
    w0, a0, k_k, k_a = (vec_ref[i:i + 1, :] for i in range(4))
    wl = w0 + _dot(jnp.tanh(_dot(mix(1), la0_ref[...])).astype(BF16), lb0_ref[...])
    al = a0 + _dot(_dot(mix(4), la1_ref[...]).astype(BF16), lb1_ref[...])
    gl = _dot(mix(5), ga_ref[...])
    k_raw = _dot(mix(2), wk_ref[...])
    lw_ref[...] = -math.exp(-0.5) / (1.0 + jnp.exp(-wl))
    a = 1.0 / (1.0 + jnp.exp(-al))
    g_ref[...] = _dot((1.0 / (1.0 + jnp.exp(-gl))).astype(BF16), gb_ref[...]).astype(g_ref.dtype)
    v_ref[...] = _dot(mix(3), wv_ref[...]).astype(v_ref.dtype)
    kk = k_raw * k_k
    bd = bd_ref[...]
    for s in range(D_MODEL // LANES):
        sl = slice(s * LANES, (s + 1) * LANES)
        t = kk[:, sl]
        t = t * lax.rsqrt(jnp.maximum(_head_sum(t * t, bd), 1e-24))
        kk_ref[:, sl] = t.astype(kk_ref.dtype)
        b_ref[:, sl] = (t * a[:, sl]).astype(b_ref.dtype)
    k_ref[...] = (k_raw * (1.0 + (a - 1.0) * k_a)).astype(k_ref.dtype)
    r_ref[...] = _dot(mix(0), wr_ref[...]).astype(r_ref.dtype)


def _head_ones():
    idx = jnp.arange(LANES) // RWKV_HEAD
    return (idx[:, None] == idx[None, :]).astype(BF16)


def _rwkv_prep(x, mu, w_rkv, vec, lora_a, lora_b, gate_a, gate_b, seq, *, tm=512):
    m, d = x.shape
    gpad = RWKV_GATE_PAD - gate_a.shape[1]
    ga = jnp.pad(gate_a, ((0, 0), (0, gpad))).astype(BF16)
    gb = jnp.pad(gate_b, ((0, gpad), (0, 0))).astype(BF16)
    wts = [w_rkv[0].astype(BF16), w_rkv[1].astype(BF16), w_rkv[2].astype(BF16),
           lora_a[0].astype(BF16), lora_b[0].astype(BF16), lora_a[1].astype(BF16),
           lora_b[1].astype(BF16), ga, gb]
    vec8 = jnp.pad(vec, ((0, SUBLANES - vec.shape[0]), (0, 0)))
    mu8 = jnp.pad(mu, ((0, SUBLANES - mu.shape[0]), (0, 0)))
    row = lambda i: (i, 0)
    sub = tm // SUBLANES
    out = jax.ShapeDtypeStruct((m, d), BF16)
    return pl.pallas_call(
        functools.partial(_rwkv_prep_body, tiles_per_seq=seq // tm),
        grid=(m // tm,),
        in_specs=[pl.BlockSpec((tm, d), row),
                  pl.BlockSpec((SUBLANES, d), lambda i: (jnp.maximum(i * sub - 1, 0), 0)),
                  _resident(mu8.shape)] + [_resident(w.shape) for w in wts]
        + [_resident(vec8.shape), _resident((LANES, LANES))],
        out_specs=[pl.BlockSpec((tm, d), row)] * 7,
        out_shape=[out, jax.ShapeDtypeStruct((m, d), F32), out, out, out, out, out],
        compiler_params=_cparams(("parallel",)),
        name="rwkv_prep",
    )(x, x, mu8, *wts, vec8, _head_ones())


def _rwkv_wkv_body(r_ref, lw_ref, k_ref, v_ref, kk_ref, b_ref, tri_ref, y_ref, state_ref):
    c = RWKV_CHUNK
    two = 2 * c

    @pl.when(pl.program_id(1) == 0)
    def _():
        state_ref[...] = jnp.zeros(state_ref.shape, F32)

    tri = tri_ref[...]

    def decayed(bi):
        lw = lw_ref[bi]
        h1 = lw.astype(BF16)
        r1 = lw - h1.astype(F32)
        h2 = r1.astype(BF16)
        h3 = (r1 - h2.astype(F32)).astype(BF16)
        cum = _dot(tri, h1) + _dot(tri, h2) + _dot(tri, h3)
        gam = jnp.exp(cum)
        gam_inv = jnp.exp(-cum)
        return (r_ref[bi].astype(F32) * gam, kk_ref[bi].astype(F32) * jnp.exp(cum - lw),
                b_ref[bi].astype(F32) * gam_inv, k_ref[bi].astype(F32) * gam_inv, gam[c - 1:c, :])

    lane_lo = lax.broadcasted_iota(jnp.int32, (c, LANES), 1) < RWKV_HEAD
    row2 = lax.broadcasted_iota(jnp.int32, (two, two), 0)
    col2 = lax.broadcasted_iota(jnp.int32, (two, two), 1)
    same = (row2 // c) == (col2 // c)
    strict = jnp.logical_and(same, row2 > col2)
    incl = jnp.logical_and(same, row2 >= col2)
    eye = (row2 == col2).astype(F32)

    def stack_masked(t):
        return jnp.concatenate([jnp.where(lane_lo, t, 0.0), jnp.where(lane_lo, 0.0, t)], axis=0)

    nb = r_ref.shape[0]
    items = [(bi, p) for bi in range(nb) for p in range(D_MODEL // LANES)]
    idx = range(len(items))
    sls = [slice(p * LANES, (p + 1) * LANES) for _, p in items]
    dec = [decayed(bi) for bi in range(nb)]
    xs, xu, bds, kds, vss, gend = [], [], [], [], [], []
    for (bi, _), sl in zip(items, sls):
        rt, kkt, bt, kt, gam_end = dec[bi]
        xs.append(jnp.concatenate([stack_masked(kkt[:, sl]), stack_masked(rt[:, sl])], axis=0).astype(BF16))
        xu.append(jnp.concatenate([kkt[:, sl], rt[:, sl]], axis=0).astype(BF16))
        bds.append(jnp.concatenate([bt[:, sl], bt[:, sl]], axis=0).astype(BF16))
        kds.append(jnp.concatenate([kt[:, sl], kt[:, sl]], axis=0).astype(BF16))
        v2 = v_ref[bi, :, sl].astype(F32)
        vss.append(jnp.where(same, jnp.concatenate([v2, v2], axis=0), 0.0).astype(BF16))
        gend.append(gam_end[:, sl])
    s2s = [state_ref[bi, p] for bi, p in items]
    a_all = [_dot_nt(xs[i], jnp.concatenate([bds[i][:c], kds[i][:c]], axis=0)) for i in idx]
    x_state = [_dot_nt(xu[i], s2s[i].astype(BF16)) for i in idx]
    head0 = row2 < c
    nmat, lk, arb, ark = [], [], [], []
    for a in a_all:
        top, bot = a[:two], a[two:]
        top_r, bot_r = pltpu.roll(top, c, 1), pltpu.roll(bot, c, 1)
        nmat.append(jnp.where(strict, -jnp.where(head0, top, top_r), 0.0))
        lk.append(jnp.where(strict, jnp.where(head0, top_r, top), 0.0).astype(BF16))
        arb.append(jnp.where(incl, jnp.where(head0, bot, bot_r), 0.0).astype(BF16))
        ark.append(jnp.where(incl, jnp.where(head0, bot_r, bot), 0.0).astype(BF16))
    rhs = [stack_masked(x_state[i][:c]) + _dot(lk[i], vss[i]) for i in idx]
    pw = [n_.astype(BF16) for n_ in nmat]
    inv = [eye + n_ for n_ in nmat]
    pw = [_dot(t, t).astype(BF16) for t in pw]
    for _ in range(int(math.log2(c)) - 2):
        both = [_dot(pw[i], jnp.concatenate([pw[i], inv[i].astype(BF16)], axis=1)) for i in idx]
        inv = [inv[i] + both[i][:, two:] for i in idx]
        pw = [t[:, :two].astype(BF16) for t in both]
    inv = [inv[i] + _dot(pw[i], inv[i].astype(BF16)) for i in idx]
    ub = [(-_dot(inv[i].astype(BF16), rhs[i].astype(BF16))).astype(BF16) for i in idx]
    for i, (bi, _) in enumerate(items):
        ys = _dot(arb[i], ub[i]) + _dot(ark[i], vss[i])
        y_ref[bi, :, sls[i]] = x_state[i][c:] + ys[:c] + ys[c:]
    for i, (bi, p) in enumerate(items):
        ds = _dot_tn(ub[i], bds[i]) + _dot_tn(vss[i], kds[i])
        state_ref[bi, p] = jnp.where(same, (s2s[i] + ds) * gend[i], 0.0)


def _rwkv_wkv(r, lw, k, v, kk, b, batch, seq):
    c = RWKV_CHUNK
    n = seq // c
    d = D_MODEL
    tri = (jnp.arange(c)[:, None] >= jnp.arange(c)[None, :]).astype(BF16)
    nb = RWKV_BATCH_ROWS if batch % RWKV_BATCH_ROWS == 0 else 1
    blk = pl.BlockSpec((nb, c, d), lambda bi, i: (bi, i, 0))
    as3d = lambda t: t.reshape(batch, seq, d)
    y = pl.pallas_call(
        _rwkv_wkv_body,
        grid=(batch // nb, n),
        in_specs=[blk] * 6 + [_resident((c, c))],
        out_specs=blk,
        out_shape=jax.ShapeDtypeStruct((batch, seq, d), F32),
        scratch_shapes=[pltpu.VMEM((nb, d // LANES, LANES, LANES), F32)],
        compiler_params=_cparams(("parallel", "arbitrary")),
        name="rwkv_wkv",
    )(as3d(r), as3d(lw), as3d(k), as3d(v), as3d(kk), as3d(b), tri)
    return y.reshape(batch * seq, d)


def _rwkv_out_body(y_ref, r_ref, k_ref, v_ref, g_ref, vec_ref, bd_ref, w_ref, res_ref, lg_ref,
                   lb_ref, o_ref, a_ref):
    bd = bd_ref[...]
    inv_n = 1.0 / RWKV_HEAD
    for s in range(D_MODEL // LANES):
        sl = slice(s * LANES, (s + 1) * LANES)
        y = y_ref[:, sl]
        mu = _head_sum(y, bd) * inv_n
        dlt = y - mu
        var = _head_sum(dlt * dlt, bd) * inv_n
        yn = dlt * lax.rsqrt(var + RWKV_GN_EPS) * vec_ref[0:1, sl] + vec_ref[1:2, sl]
        rk = r_ref[:, sl] * k_ref[:, sl] * vec_ref[2:3, sl].astype(BF16)
        bonus = _head_sum(rk, bd) * v_ref[:, sl].astype(F32)
        a_ref[:, sl] = ((yn + bonus) * g_ref[:, sl].astype(F32)).astype(BF16)
    acc = _dot(a_ref[...], w_ref[...])
    o_ref[...] = _layer_norm(DN_ALPHA * res_ref[...] + acc, lg_ref[...], lb_ref[...])


def _rwkv_out(y, r, k, v, g, vec, w, res, lg, lb, *, tm=512):
    m, d = res.shape
    row = lambda i: (i, 0)
    act = pl.BlockSpec((tm, d), row)
    return pl.pallas_call(
        _rwkv_out_body,
        grid=(m // tm,),
        in_specs=[act] * 5 + [_resident(vec.shape), _resident((LANES, LANES)), _resident((d, d)),
                              act, _resident((1, d)), _resident((1, d))],
        out_specs=act,
        out_shape=jax.ShapeDtypeStruct((m, d), F32),
        scratch_shapes=[pltpu.VMEM((tm, d), BF16)],
        compiler_params=_cparams(("parallel",)),
        name="rwkv_out",
    )(y, r, k, v, g, vec, _head_ones(), w, res, lg, lb)


def kernel(x, ret_w_in, ret_gn, ret_w_out, dil_w_in, dil_w_out, mla_w_down, mla_norm_q,
           mla_norm_kv, mla_w_uq, mla_w_ukv, mla_w_out, rwkv_mu, rwkv_w_rkv, rwkv_w_out,
           rwkv_vec, rwkv_lora_a, rwkv_lora_b, rwkv_gate_a, rwkv_gate_b, rwkv_ln_x,
           mlp_w1, mlp_w2, ln_g, ln_b):
    batch, seq, d = x.shape
    xf = x.reshape(batch * seq, d)
    w1_all, w2_all = mlp_w1.astype(BF16), mlp_w2.astype(BF16)
    n_mixers = 4
    for i in range(DEPTH):
        mixer, j = i % n_mixers, i // n_mixers
        lg, lb = ln_g[i, 0][None, :], ln_b[i, 0][None, :]
        if mixer == 0:
            proj = _mm(xf, ret_w_in[j].astype(BF16), tm=512, name="ret_proj")
            gated = _retention(proj, ret_gn[j], batch, seq)
            xf = _mm_res_ln(gated, ret_w_out[j].astype(BF16), xf, lg, lb, name="ret_out")
        elif mixer == 1:
            outs, lses = _dilated(xf, dil_w_in[j].astype(BF16), batch, seq)
            xf = _dil_out(outs, lses, dil_w_out[j].astype(BF16), xf, lg, lb, seq)
        elif mixer == 2:
            o = _mla(xf, mla_w_down[j], mla_norm_q[j], mla_norm_kv[j], mla_w_uq[j], mla_w_ukv[j],
                     batch, seq)
            xf = _mm_res_ln(o, mla_w_out[j].astype(BF16), xf, lg, lb, name="mla_out")
        else:
            r, lw, k, v, kk, b, g = _rwkv_prep(xf, rwkv_mu[j], rwkv_w_rkv[j], rwkv_vec[j],
                                               rwkv_lora_a[j], rwkv_lora_b[j], rwkv_gate_a[j],
                                               rwkv_gate_b[j], seq)
            y = _rwkv_wkv(r, lw, k, v, kk, b, batch, seq)
            vec = jnp.concatenate([rwkv_ln_x[j], rwkv_vec[j][4:5],
                                   jnp.zeros((SUBLANES - 3, d), F32)], axis=0)
            xf = _rwkv_out(y, r, k, v, g, vec, rwkv_w_out[j].astype(BF16), xf, lg, lb)
        xf = _mlp(xf, w1_all, w2_all, i, ln_g[i, 1][None, :], ln_b[i, 1][None, :])
    return xf.reshape(batch, seq, d)
```

```python
import functools
import math

import jax
import jax.numpy as jnp
import numpy as np
from jax import lax
from jax.experimental import pallas as pl
from jax.experimental.pallas import tpu as pltpu

F32 = jnp.float32
BF16 = jnp.bfloat16

D_MODEL = 1024
DEPTH = 4
D_FF = 4 * D_MODEL
LN_EPS = 1e-5
RMS_EPS = 1e-6
GN_EPS = 1e-5
DN_ALPHA = (2.0 * DEPTH) ** 0.25
NEG = -1e30
LANES = 128
SUBLANES = 8
ROPE_PARTNER = 64
MM_SUB = 256

RET_HEADS = 4
RET_QK_DIM = 256
RET_V_DIM = 512
RET_CHUNK = 512
RET_THETA = 10000.0

DIL_PAIRS = ((128, 1), (512, 4), (2048, 16))
DIL_HEADS = 8
DIL_HEAD_DIM = 128
DIL_ROT = 32
DIL_BLOCK = 128
DIL_TQ = 1024
DIL_WAVE = 4
ROPE_THETA = 500000.0

MLA_HEADS = 16
MLA_NOPE = 128
MLA_ROPE = 64
MLA_V = 128
MLA_Q_RANK = 256
MLA_KV_RANK = 128
MLA_THETA = 10000.0
MLA_TQ = 1024
MLA_TK = 1024
MLA_ROW_SPLIT = 4
assert MLA_TQ == MLA_TK

RWKV_HEAD = 64
RWKV_HEADS = D_MODEL // RWKV_HEAD
RWKV_GN_EPS = 64e-5
RWKV_CHUNK = 64
RWKV_GATE_PAD = 256
RWKV_BATCH_ROWS = 4

V7X_VMEM_BYTES = 64 * 1024 * 1024
VMEM_LIMIT = V7X_VMEM_BYTES // 8 * 7


def _cparams(sem):
    return pltpu.CompilerParams(dimension_semantics=sem, vmem_limit_bytes=VMEM_LIMIT)


def _resident(shape):
    nd = len(shape)
    return pl.BlockSpec(shape, lambda *_: (0,) * nd, pipeline_mode=pl.Buffered(1))


def _layer_norm(z, g, b):
    mu = jnp.mean(z, axis=-1, keepdims=True)
    d = z - mu
    var = jnp.mean(d * d, axis=-1, keepdims=True)
    return d * lax.rsqrt(var + LN_EPS) * g + b


def _dot(a, b):
    return jnp.dot(a, b, preferred_element_type=F32)


def _dot_nt(a, b):
    return lax.dot_general(a, b, (((1,), (1,)), ((), ())), preferred_element_type=F32)


def _dot_tn(a, b):
    return lax.dot_general(a, b, (((0,), (0,)), ((), ())), preferred_element_type=F32)


def _rope_tile(a, c, s):
    return a * c + pltpu.roll(a, ROPE_PARTNER, 1) * s


def _project_columns(xb, w_ref, store, modes, tabs, scale):
    n = w_ref.shape[1]
    for c0 in range(0, n, MM_SUB):
        acc = _dot(xb, w_ref[:, c0:c0 + MM_SUB])
        for t in range(MM_SUB // LANES):
            a = acc[:, t * LANES:(t + 1) * LANES]
            mode = modes[c0 // LANES + t] if modes is not None else None
            if mode == "scale":
                a = a * scale
            elif mode is not None:
                c_ref, s_ref = tabs
                a = _rope_tile(a, c_ref[mode[1]], s_ref[mode[1]])
            store(c0 + t * LANES, a)


def _mm_body(x_ref, w_ref, *rest, modes, scale):
    tabs, o_ref = rest[:-1], rest[-1]

    def store(c0, a):
        o_ref[:, c0:c0 + LANES] = a.astype(o_ref.dtype)

    _project_columns(x_ref[...].astype(BF16), w_ref, store, modes, tabs, scale)


def _mm(x, w, *, tm, out_dtype=BF16, tabs=None, tab_map=None, modes=None, scale=1.0, name="mm"):
    m = x.shape[0]
    k, n = w.shape
    row = lambda i: (i, 0)
    in_specs = [pl.BlockSpec((tm, k), row), _resident((k, n))]
    args = [x, w]
    if tabs is not None:
        for t in tabs:
            in_specs.append(pl.BlockSpec((t.shape[0], tm, LANES), tab_map))
            args.append(t)
    return pl.pallas_call(
        functools.partial(_mm_body, modes=modes, scale=scale),
        grid=(m // tm,),
        in_specs=in_specs,
        out_specs=pl.BlockSpec((tm, n), row),
        out_shape=jax.ShapeDtypeStruct((m, n), out_dtype),
        compiler_params=_cparams(("parallel",)),
        name=name,
    )(*args)


def _mm_res_ln_body(a_ref, w_ref, res_ref, g_ref, b_ref, o_ref):
    half = a_ref.shape[0] // 2
    for r0 in (0, half):
        acc = _dot(a_ref[r0:r0 + half, :], w_ref[...])
        o_ref[r0:r0 + half, :] = _layer_norm(DN_ALPHA * res_ref[r0:r0 + half, :] + acc,
                                             g_ref[...], b_ref[...])


def _mm_res_ln(a, w, res, g, b, *, tm=1024, name="mm_res_ln"):
    m, k = a.shape
    d = w.shape[1]
    row = lambda i: (i, 0)
    return pl.pallas_call(
        _mm_res_ln_body,
        grid=(m // tm,),
        in_specs=[pl.BlockSpec((tm, k), row), _resident((k, d)), pl.BlockSpec((tm, d), row),
                  _resident((1, d)), _resident((1, d))],
        out_specs=pl.BlockSpec((tm, d), row),
        out_shape=jax.ShapeDtypeStruct((m, d), F32),
        compiler_params=_cparams(("parallel",)),
        name=name,
    )(a, w, res, g, b)


def _mlp_body(x_ref, w1_ref, w2_ref, g_ref, b_ref, o_ref, *, fchunk):
    half = x_ref.shape[0] // 2
    for r0 in (0, half):
        x = x_ref[r0:r0 + half, :]
        xb = x.astype(BF16)
        acc = jnp.zeros(x.shape, F32)
        for c in range(D_FF // fchunk):
            h = _dot(xb, w1_ref[:, c * fchunk:(c + 1) * fchunk])
            h = jnp.maximum(h, 0.0)
            h = (h * h).astype(BF16)
            acc = acc + _dot(h, w2_ref[c * fchunk:(c + 1) * fchunk, :])
        o_ref[r0:r0 + half, :] = _layer_norm(DN_ALPHA * x + acc, g_ref[...], b_ref[...])


def _mlp(x, w1_all, w2_all, layer, g, b, *, tm=512, fchunk=1024):
    m, d = x.shape
    row = lambda i: (i, 0)
    slab = lambda shape: pl.BlockSpec((None,) + shape, lambda i: (layer, 0, 0),
                                      pipeline_mode=pl.Buffered(1))
    return pl.pallas_call(
        functools.partial(_mlp_body, fchunk=fchunk),
        grid=(m // tm,),
        in_specs=[pl.BlockSpec((tm, d), row), slab((d, D_FF)), slab((D_FF, d)),
                  _resident((1, d)), _resident((1, d))],
        out_specs=pl.BlockSpec((tm, d), row),
        out_shape=jax.ShapeDtypeStruct((m, d), F32),
        compiler_params=_cparams(("parallel",)),
        name="mlp",
    )(x, w1_all, w2_all, g, b)


def _ret_body(q_ref, k_ref, v_ref, g_ref, cos_ref, sin_ref, intra_ref, qdec_ref, kdec_ref,
              cdec_ref, gn_ref, o_ref, state_ref):
    dk, dv, half = RET_QK_DIM, RET_V_DIM, RET_QK_DIM // 2

    @pl.when(pl.program_id(1) == 0)
    def _():
        state_ref[...] = jnp.zeros(state_ref.shape, F32)

    cos = cos_ref[...]
    sin = sin_ref[...]

    def rope(t):
        t1 = t[:, :half].astype(F32)
        t2 = t[:, half:].astype(F32)
        return jnp.concatenate([t1 * cos - t2 * sin, t2 * cos + t1 * sin], axis=-1)

    heads = range(RET_HEADS)
    q = [rope(q_ref[:, h * dk:(h + 1) * dk]) for h in heads]
    k = [rope(k_ref[:, h * dk:(h + 1) * dk]) * (dk ** -0.5) for h in heads]
    v = [v_ref[:, h * dv:(h + 1) * dv] for h in heads]
    qb = [t.astype(BF16) for t in q]
    state = [state_ref[h] for h in heads]
    scores = [(_dot_nt(qb[h], k[h].astype(BF16)) * intra_ref[h]).astype(BF16) for h in heads]
    cross = [_dot(qb[h], state[h].astype(BF16)) * qdec_ref[h] for h in heads]
    o = [_dot(scores[h], v[h]) + cross[h] for h in heads]
    for h in heads:
        state_ref[h] = (state[h] * cdec_ref[h, 0:1, :]
                        + _dot_tn((k[h] * kdec_ref[h]).astype(BF16), v[h]))
    for h in heads:
        sl = slice(h * dv, (h + 1) * dv)
        mu = jnp.mean(o[h], axis=-1, keepdims=True)
        d = o[h] - mu
        var = jnp.mean(d * d, axis=-1, keepdims=True)
        on = d * lax.rsqrt(var + GN_EPS) * gn_ref[0:1, sl] + gn_ref[1:2, sl]
        gate = g_ref[:, sl].astype(F32)
        gate = gate * (1.0 / (1.0 + jnp.exp(-gate)))
        o_ref[:, sl] = (gate * on).astype(o_ref.dtype)


def _retention(proj, gn, batch, seq):
    h_, dk, dv, c = RET_HEADS, RET_QK_DIM, RET_V_DIM, RET_CHUNK
    n = seq // c
    half = dk // 2
    f32 = np.float32
    pos = np.arange(seq, dtype=f32)
    inv_freq = f32(RET_THETA) ** (-np.arange(half, dtype=f32) / f32(half))
    ang = pos[:, None] * inv_freq[None, :]
    cos, sin = np.cos(ang), np.sin(ang)
    log_gamma = np.log(f32(1.0) - f32(2.0) ** (f32(-5.0) - np.arange(h_, dtype=f32)))
    idx = np.arange(c, dtype=f32)
    diff = idx[:, None] - idx[None, :]
    intra = np.where(diff >= 0, np.exp(log_gamma[:, None, None] * np.maximum(diff, f32(0.0))), f32(0.0))
    intra = intra.astype(f32)
    qdec = np.broadcast_to(np.exp(log_gamma[:, None] * (idx + f32(1.0)))[:, :, None], (h_, c, dv))
    kdec = np.broadcast_to(np.exp(log_gamma[:, None] * (f32(c - 1.0) - idx))[:, :, None], (h_, c, dk))
    cdec = np.broadcast_to(np.exp(log_gamma * f32(c))[:, None, None], (h_, SUBLANES, dv))
    qdec, kdec, cdec = (np.ascontiguousarray(t, dtype=f32) for t in (qdec, kdec, cdec))
    qk_w, vg_w = h_ * dk, h_ * dv
    return pl.pallas_call(
        _ret_body,
        grid=(batch, n),
        in_specs=[
            pl.BlockSpec((c, qk_w), lambda b, i: (b * n + i, 0)),
            pl.BlockSpec((c, qk_w), lambda b, i: (b * n + i, 1)),
            pl.BlockSpec((c, vg_w), lambda b, i: (b * n + i, 2 * qk_w // vg_w)),
            pl.BlockSpec((c, vg_w), lambda b, i: (b * n + i, 2 * qk_w // vg_w + 1)),
            pl.BlockSpec((c, half), lambda b, i: (i, 0)),
            pl.BlockSpec((c, half), lambda b, i: (i, 0)),
            _resident((h_, c, c)), _resident((h_, c, dv)), _resident((h_, c, dk)),
            _resident((h_, SUBLANES, dv)), _resident((2, vg_w)),
        ],
        out_specs=pl.BlockSpec((c, vg_w), lambda b, i: (b * n + i, 0)),
        out_shape=jax.ShapeDtypeStruct((batch * seq, vg_w), BF16),
        scratch_shapes=[pltpu.VMEM((h_, dk, dv), F32)],
        compiler_params=_cparams(("parallel", "arbitrary")),
        name="retention",
    )(proj, proj, proj, proj, cos, sin, intra, qdec, kdec, cdec, gn)


def _dil_attn_body(q_ref, kp_ref, kc_ref, vp_ref, vc_ref, o_ref, lse_ref, *, seq_blocks):
    blk = DIL_BLOCK
    n_sub = q_ref.shape[0] // blk
    has_prev = pl.program_id(1) > 0
    qi = lax.broadcasted_iota(jnp.int32, (blk, 2 * blk), 0)
    ki = lax.broadcasted_iota(jnp.int32, (blk, 2 * blk), 1)
    band = jnp.logical_and(ki >= qi, ki <= qi + blk)
    band_first = jnp.logical_and(band, jnp.logical_or(ki >= blk, has_prev))
    band_start = jnp.logical_and(band, ki >= blk)

    def mask(j):
        if j == 0:
            return band_first
        return band_start if j % seq_blocks == 0 else band
    lane = lax.broadcasted_iota(jnp.int32, (blk, LANES), 1)
    ones = jnp.ones((2 * blk, LANES), BF16)
    def keys(prev_ref, cur_ref, j, sl):
        if j == 0:
            return jnp.concatenate([prev_ref[:, sl], cur_ref[:blk, sl]], axis=0)
        return cur_ref[(j - 1) * blk:(j + 1) * blk, sl]

    for j0 in range(0, n_sub, DIL_WAVE):
        items = [(j, h) for j in range(j0, min(j0 + DIL_WAVE, n_sub)) for h in range(DIL_HEADS)]
        sls = [slice(h * DIL_HEAD_DIM, (h + 1) * DIL_HEAD_DIM) for _, h in items]
        rows = [slice(j * blk, (j + 1) * blk) for j, _ in items]
        s = [jnp.where(mask(j),
                       _dot_nt(q_ref[rows[i], sls[i]], keys(kp_ref, kc_ref, j, sls[i])), NEG)
             for i, (j, _) in enumerate(items)]
        m = [jnp.max(t, axis=-1, keepdims=True) for t in s]
        p = [jnp.exp((s[i] - m[i]).astype(BF16)) for i in range(len(items))]
        pv = [_dot(p[i], jnp.concatenate([keys(vp_ref, vc_ref, j, sls[i]), ones], axis=1))
              for i, (j, _) in enumerate(items)]
        lse_tiles = {j: jnp.zeros((blk, LANES), F32) for j, _ in items}
        for i, (j, h) in enumerate(items):
            l = pv[i][:, DIL_HEAD_DIM:]
            o_ref[rows[i], sls[i]] = (pv[i][:, :DIL_HEAD_DIM] / l).astype(o_ref.dtype)
            lse_tiles[j] = jnp.where(lane == h, m[i] + jnp.log(l), lse_tiles[j])
        for j, tile in lse_tiles.items():
            lse_ref[j * blk:(j + 1) * blk, :] = tile


def _dil_attn(q, k, v, batch, seq, dil):
    blk = DIL_BLOCK
    hd = DIL_HEADS * DIL_HEAD_DIM
    sub = seq // dil
    tq = DIL_TQ
    nb = max(sub // tq, 1)
    n_seq = batch * dil * sub // (nb * tq)
    per = tq // blk
    cur = lambda z, i: (z * nb + i, 0)
    prev = lambda z, i: (jnp.maximum((z * nb + i) * per - 1, 0), 0)
    out_map = cur
    return pl.pallas_call(
        functools.partial(_dil_attn_body, seq_blocks=sub // blk),
        grid=(n_seq, nb),
        in_specs=[pl.BlockSpec((tq, hd), cur), pl.BlockSpec((blk, hd), prev),
                  pl.BlockSpec((tq, hd), cur), pl.BlockSpec((blk, hd), prev),
                  pl.BlockSpec((tq, hd), cur)],
        out_specs=[pl.BlockSpec((tq, hd), out_map), pl.BlockSpec((tq, LANES), out_map)],
        out_shape=[jax.ShapeDtypeStruct((batch * seq, hd), BF16),
                   jax.ShapeDtypeStruct((batch * seq, LANES), F32)],
        compiler_params=_cparams(("parallel", "arbitrary")),
        name=f"dil_attn_{dil}",
    )(q, k, k, v, v)


def _dil_out_body(o0_ref, o1_ref, o2_ref, l0_ref, l1_ref, l2_ref, e_ref, w_ref, res_ref,
                  g_ref, b_ref, o_ref, osc_ref, lsc_ref, mix_ref):
    tm = o_ref.shape[0]
    nh = DIL_HEADS
    for gi, (og, lg) in enumerate(((o0_ref, l0_ref), (o1_ref, l1_ref), (o2_ref, l2_ref))):
        dil = og.shape[1]
        n = tm // dil
        for r in range(dil):
            rows = pl.ds(r, n, stride=dil) if dil > 1 else slice(None)
            for h in range(nh):
                osc_ref[gi * nh + h, rows, :] = og[0, r, :, h * LANES:(h + 1) * LANES].astype(F32)
            lsc_ref[gi, rows, :] = lg[0, r]
    l0, l1, l2 = lsc_ref[0], lsc_ref[1], lsc_ref[2]
    m = jnp.maximum(jnp.maximum(l0, l1), l2)
    e0, e1, e2 = jnp.exp(l0 - m), jnp.exp(l1 - m), jnp.exp(l2 - m)
    den = e0 + e1 + e2
    e = e_ref[...]
    wfull = []
    for eg in (e0, e1, e2):
        wfull.append(_dot((eg / den).astype(BF16), e))
    for h in range(nh):
        sl = slice(h * LANES, (h + 1) * LANES)
        mixed = sum(wfull[gi][:, sl] * osc_ref[gi * nh + h] for gi in range(3))
        mix_ref[:, sl] = mixed.astype(BF16)
    acc = _dot(mix_ref[...], w_ref[...])
    o_ref[...] = _layer_norm(DN_ALPHA * res_ref[...] + acc, g_ref[...], b_ref[...])


def _dil_out(outs, lses, w, res, g, b, seq, *, tm=512):
    m, d = res.shape
    hd = DIL_HEADS * DIL_HEAD_DIM
    expand = (jnp.arange(LANES)[:, None] == (jnp.arange(hd) // DIL_HEAD_DIM)[None, :]).astype(BF16)
    row = lambda i: (i, 0)
    nt = seq // tm
    grp = lambda i: (i // nt, 0, i % nt, 0)
    dils = [dil for _, dil in DIL_PAIRS]
    batch = m // seq
    o4 = [o.reshape(batch, dil, seq // dil, hd) for o, dil in zip(outs, dils)]
    l4 = [l.reshape(batch, dil, seq // dil, LANES) for l, dil in zip(lses, dils)]
    return pl.pallas_call(
        _dil_out_body,
        grid=(m // tm,),
        in_specs=[pl.BlockSpec((1, dil, tm // dil, hd), grp) for dil in dils]
        + [pl.BlockSpec((1, dil, tm // dil, LANES), grp) for dil in dils]
        + [_resident((LANES, hd)), _resident((hd, d)), pl.BlockSpec((tm, d), row),
           _resident((1, d)), _resident((1, d))],
        out_specs=pl.BlockSpec((tm, d), row),
        out_shape=jax.ShapeDtypeStruct((m, d), F32),
        scratch_shapes=[pltpu.VMEM((len(dils) * DIL_HEADS, tm, LANES), F32),
                        pltpu.VMEM((len(dils), tm, LANES), F32), pltpu.VMEM((tm, hd), BF16)],
        compiler_params=_cparams(("parallel",)),
        name="dil_out",
    )(*o4, *l4, expand, w, res, g, b)


def _rope_tables(seq, rot, theta, scale, passthrough):
    half = rot // 2
    inv_freq = np.float32(theta) ** (-np.arange(half, dtype=np.float32) / np.float32(half))
    ang = np.arange(seq, dtype=np.float32)[:, None] * inv_freq[None, :]
    cos, sin = np.cos(ang), np.sin(ang)
    fill = np.full((seq, ROPE_PARTNER - half), passthrough, np.float32)
    zero = np.zeros((seq, ROPE_PARTNER - half), np.float32)
    c = np.concatenate([cos, fill, cos, fill], axis=1)
    s = np.concatenate([-sin, zero, sin, zero], axis=1)
    return (c * np.float32(scale)).astype(np.float32), (s * np.float32(scale)).astype(np.float32)


def _rope_lane_order(rot, width):
    half = rot // 2
    rest = list(range(rot, width))
    cut = ROPE_PARTNER - half
    return jnp.array(list(range(half)) + rest[:cut] + list(range(half, rot)) + rest[cut:])


def _dil_proj_body(x_ref, w_ref, c_ref, s_ref, q_ref, k_ref, v_ref, xb_ref, xs_ref, *, dil):
    tm = x_ref.shape[0]
    n = tm // dil
    if dil == 1:
        xb = x_ref[...].astype(BF16)
    else:
        for c in range(D_MODEL // LANES):
            xs_ref[c] = x_ref[:, c * LANES:(c + 1) * LANES]
        for r in range(dil):
            for c in range(D_MODEL // LANES):
                xb_ref[r * n:(r + 1) * n, c * LANES:(c + 1) * LANES] = (
                    xs_ref[c, pl.ds(r, n, stride=dil), :].astype(BF16))
        xb = xb_ref[...]

    hd = DIL_HEADS * DIL_HEAD_DIM

    def store(c0, a):
        o_ref = (q_ref, k_ref, v_ref)[c0 // hd]
        o_ref[0, :, :, c0 % hd:c0 % hd + LANES] = a.astype(o_ref.dtype).reshape(dil, n, LANES)

    modes = [("rope", 0)] * DIL_HEADS + [("rope", 1)] * DIL_HEADS + [None] * DIL_HEADS
    _project_columns(xb, w_ref, store, modes, (c_ref, s_ref), 1.0)


def _dil_proj(x, w, tabs, batch, seq, dil, *, tm=512):
    hd = DIL_HEADS * DIL_HEAD_DIM
    nt = seq // tm
    n = tm // dil
    tab_spec = pl.BlockSpec((2, tm, LANES), lambda i: (0, i % nt, 0))
    return pl.pallas_call(
        functools.partial(_dil_proj_body, dil=dil),
        grid=(batch * nt,),
        in_specs=[pl.BlockSpec((tm, D_MODEL), lambda i: (i, 0)), _resident((D_MODEL, 3 * hd)),
                  tab_spec, tab_spec],
        out_specs=[pl.BlockSpec((1, dil, n, hd), lambda i: (i // nt, 0, i % nt, 0))] * 3,
        out_shape=[jax.ShapeDtypeStruct((batch, dil, seq // dil, hd), BF16)] * 3,
        scratch_shapes=[pltpu.VMEM((tm, D_MODEL), BF16),
                        pltpu.VMEM((D_MODEL // LANES, tm, LANES), F32)],
        compiler_params=_cparams(("parallel",)),
        name=f"dil_proj_{dil}",
    )(x, w, *tabs)


def _dilated(x, w_in, batch, seq, *, tm=512):
    hd = DIL_HEADS * DIL_HEAD_DIM
    cq, sq = _rope_tables(seq, DIL_ROT, ROPE_THETA, DIL_HEAD_DIM ** -0.5, 1.0)
    ck, sk = _rope_tables(seq, DIL_ROT, ROPE_THETA, 1.0, 1.0)
    order = _rope_lane_order(DIL_ROT, DIL_HEAD_DIM)
    outs, lses = [], []
    for gi, (_, dil) in enumerate(DIL_PAIRS):
        wg = w_in[:, gi * 3 * hd:(gi + 1) * 3 * hd].reshape(D_MODEL, 3, DIL_HEADS, DIL_HEAD_DIM)
        wg = jnp.concatenate([wg[:, :2][..., order], wg[:, 2:]], axis=1).reshape(D_MODEL, 3 * hd)

        def by_residue(t):
            t = t.reshape(seq // tm, tm // dil, dil, LANES)
            return np.swapaxes(t, 1, 2).reshape(seq, LANES)

        tabs = [np.stack([by_residue(a), by_residue(b)]) for a, b in ((cq, ck), (sq, sk))]
        q, k, v = (t.reshape(batch * seq, hd) for t in _dil_proj(x, wg, tabs, batch, seq, dil, tm=tm))
        o, lse = _dil_attn(q, k, v, batch, seq, dil)
        outs.append(o)
        lses.append(lse)
    return outs, lses


def _mla_proj_body(x_ref, wd_ref, nq_ref, nkv_ref, wq_ref, wkv_ref, c_ref, s_ref,
                   q_ref, kv_ref, kpe_ref, *, scale):
    acc = _dot(x_ref[...].astype(BF16), wd_ref[...])
    cq = acc[:, :MLA_Q_RANK]
    ckv = acc[:, MLA_Q_RANK:MLA_Q_RANK + MLA_KV_RANK]
    kpe = acc[:, MLA_Q_RANK + MLA_KV_RANK:]
    cq = cq * lax.rsqrt(jnp.mean(cq * cq, axis=-1, keepdims=True) + RMS_EPS) * nq_ref[...]
    ckv = ckv * lax.rsqrt(jnp.mean(ckv * ckv, axis=-1, keepdims=True) + RMS_EPS) * nkv_ref[...]
    kpe_ref[...] = _rope_tile(kpe, c_ref[0], s_ref[0]).astype(kpe_ref.dtype)

    def store_q(c0, a):
        q_ref[:, c0:c0 + LANES] = a.astype(q_ref.dtype)

    def store_kv(c0, a):
        kv_ref[:, c0:c0 + LANES] = a.astype(kv_ref.dtype)

    _project_columns(cq.astype(BF16), wq_ref, store_q, ["scale", ("rope", 1)] * MLA_HEADS,
                     (c_ref, s_ref), scale)
    _project_columns(ckv.astype(BF16), wkv_ref, store_kv, None, None, 1.0)


def _mla_proj(x, wd, nq, nkv, wq, wkv, tabs, seq, scale, *, tm=512):
    m, d = x.shape
    row = lambda i: (i, 0)
    ns = seq // tm
    tab = pl.BlockSpec((2, tm, LANES), lambda i: (0, i % ns, 0))
    return pl.pallas_call(
        functools.partial(_mla_proj_body, scale=scale),
        grid=(m // tm,),
        in_specs=[pl.BlockSpec((tm, d), row), _resident(wd.shape), _resident((1, MLA_Q_RANK)),
                  _resident((1, MLA_KV_RANK)), _resident(wq.shape), _resident(wkv.shape), tab, tab],
        out_specs=[pl.BlockSpec((tm, wq.shape[1]), row), pl.BlockSpec((tm, wkv.shape[1]), row),
                   pl.BlockSpec((tm, LANES), row)],
        out_shape=[jax.ShapeDtypeStruct((m, wq.shape[1]), BF16),
                   jax.ShapeDtypeStruct((m, wkv.shape[1]), BF16),
                   jax.ShapeDtypeStruct((m, LANES), BF16)],
        compiler_params=_cparams(("parallel",)),
        name="mla_proj",
    )(x, wd, nq, nkv, wq, wkv, *tabs)


def _mla_flash_body(q_ref, kn_ref, kpe_ref, v_ref, o_ref, kcat_ref, vaug_ref):
    tq, tk, sub = MLA_TQ, MLA_TK, MLA_TQ // MLA_ROW_SPLIT
    kcat_ref[:, :MLA_NOPE] = kn_ref[...]
    kcat_ref[:, MLA_NOPE:] = kpe_ref[...]
    vaug_ref[:, :MLA_V] = v_ref[...]
    vaug_ref[:, MLA_V:] = jnp.ones((v_ref.shape[0], LANES), BF16)

    parts = range(MLA_ROW_SPLIT)
    col_minus_row = (lax.broadcasted_iota(jnp.int32, (sub, tk), 1)
                     - lax.broadcasted_iota(jnp.int32, (sub, tk), 0))

    def scores(tile, chunk, diagonal):
        out = []
        for part in parts:
            width = (part + 1) * sub if diagonal else tk
            q0 = tile * tq + part * sub
            s = _dot_nt(q_ref[q0:q0 + sub, :], kcat_ref[chunk * tk:chunk * tk + width, :])
            if diagonal:
                s = jnp.where(col_minus_row[:, :width] <= part * sub, s, NEG)
            out.append(s)
        return out

    def update(chunk, s_all, carry):
        out = []
        for s, (m, acc) in zip(s_all, carry):
            vb = vaug_ref[chunk * tk:chunk * tk + s.shape[1], :]
            m_new = jnp.maximum(m, jnp.max(s, axis=-1, keepdims=True))
            alpha = jnp.exp2(m - m_new)
            p = jnp.exp2((s - m_new).astype(BF16))
            out.append((m_new, alpha * acc + _dot(p, vb)))
        return out

    for tile in range(q_ref.shape[0] // tq):
        n_chunks = tile + 1
        carry = [(jnp.full((sub, 1), NEG, F32), jnp.zeros((sub, MLA_V + LANES), F32)) for _ in parts]
        s = scores(tile, 0, n_chunks == 1)
        for c in range(n_chunks):
            s_next = scores(tile, c + 1, c + 2 == n_chunks) if c + 1 < n_chunks else None
            carry = update(c, s, carry)
            s = s_next
        for part, (_, acc) in enumerate(carry):
            q0 = tile * tq + part * sub
            o_ref[q0:q0 + sub, :] = (acc[:, :MLA_V] / acc[:, MLA_V:]).astype(o_ref.dtype)


def _mla_flash(q, kv, kpe, batch, seq):
    h_ = MLA_HEADS
    qw = MLA_NOPE + LANES
    return pl.pallas_call(
        _mla_flash_body,
        grid=(batch, h_),
        in_specs=[pl.BlockSpec((seq, qw), lambda b, h: (b, h)),
                  pl.BlockSpec((seq, MLA_NOPE), lambda b, h: (b, h)),
                  pl.BlockSpec((seq, LANES), lambda b, h: (b, 0)),
                  pl.BlockSpec((seq, MLA_V), lambda b, h: (b, h_ + h))],
        out_specs=pl.BlockSpec((seq, MLA_V), lambda b, h: (b, h)),
        out_shape=jax.ShapeDtypeStruct((batch * seq, h_ * MLA_V), BF16),
        scratch_shapes=[pltpu.VMEM((seq, qw), BF16), pltpu.VMEM((seq, MLA_V + LANES), BF16)],
        compiler_params=_cparams(("parallel", "parallel")),
        name="mla_flash",
    )(q, kv, kpe, kv)


def _mla(x, w_down, norm_q, norm_kv, w_uq, w_ukv, batch, seq):
    h_ = MLA_HEADS
    half = MLA_ROPE // 2

    def pe_tile(w):
        z = jnp.zeros(w.shape[:-1] + (ROPE_PARTNER - half,), w.dtype)
        return jnp.concatenate([w[..., :half], z, w[..., half:], z], axis=-1)

    n_lat = MLA_Q_RANK + MLA_KV_RANK
    wd = jnp.concatenate([w_down[:, :n_lat], pe_tile(w_down[:, n_lat:])], axis=1).astype(BF16)
    wq = w_uq.reshape(MLA_Q_RANK, h_, MLA_NOPE + MLA_ROPE)
    wq = jnp.concatenate([wq[..., :MLA_NOPE], pe_tile(wq[..., MLA_NOPE:])], axis=-1)
    wq = wq.reshape(MLA_Q_RANK, -1).astype(BF16)
    wkv = w_ukv.reshape(MLA_KV_RANK, h_, MLA_NOPE + MLA_V)
    wkv = jnp.concatenate([wkv[:, :, :MLA_NOPE].reshape(MLA_KV_RANK, -1),
                           wkv[:, :, MLA_NOPE:].reshape(MLA_KV_RANK, -1)], axis=1).astype(BF16)
    scale = (MLA_NOPE + MLA_ROPE) ** -0.5 * math.log2(math.e)
    tk = _rope_tables(seq, MLA_ROPE, MLA_THETA, 1.0, 0.0)
    tq = _rope_tables(seq, MLA_ROPE, MLA_THETA, scale, 0.0)
    tabs = [np.stack([a, b]) for a, b in zip(tk, tq)]
    q, kv, kpe = _mla_proj(x, wd, norm_q[None, :], norm_kv[None, :], wq, wkv, tabs, seq, scale)
    return _mla_flash(q, kv, kpe, batch, seq)


def _head_sum(z, ones_bd):
    return _dot(z.astype(BF16), ones_bd)


def _rwkv_prep_body(x_ref, xp_ref, mu_ref, wr_ref, wk_ref, wv_ref, la0_ref, lb0_ref, la1_ref,
                    lb1_ref, ga_ref, gb_ref, vec_ref, bd_ref, r_ref, lw_ref, k_ref, v_ref,
                    kk_ref, b_ref, g_ref, *, tiles_per_seq):
    x = x_ref[...]
    tm = x.shape[0]
    first = pl.program_id(0) % tiles_per_seq == 0
    prev_row = jnp.where(first, 0.0, xp_ref[SUBLANES - 1:SUBLANES, :])
    rows = lax.broadcasted_iota(jnp.int32, x.shape, 0)
    shifted = jnp.where(rows == 0, prev_row, pltpu.roll(x, 1, 0))
    xb = x.astype(BF16)
    xxb = (shifted - x).astype(BF16)
    mix = lambda i: xb + xxb * mu_ref[i:i + 1, :].astype(BF16)
    w0, a0, k_k, k_a = (vec_ref[i:i + 1, :] for i in range(4))
    wl = w0 + _dot(jnp.tanh(_dot(mix(1), la0_ref[...])).astype(BF16), lb0_ref[...])
    al = a0 + _dot(_dot(mix(4), la1_ref[...]).astype(BF16), lb1_ref[...])
    gl = _dot(mix(5), ga_ref[...])
    k_raw = _dot(mix(2), wk_ref[...])
    lw_ref[...] = -math.exp(-0.5) / (1.0 + jnp.exp(-wl))
    a = 1.0 / (1.0 + jnp.exp(-al))
    g_ref[...] = _dot((1.0 / (1.0 + jnp.exp(-gl))).astype(BF16), gb_ref[...]).astype(g_ref.dtype)
    v_ref[...] = _dot(mix(3), wv_ref[...]).astype(v_ref.dtype)
    kk = k_raw * k_k
    bd = bd_ref[...]
    for s in range(D_MODEL // LANES):
        sl = slice(s * LANES, (s + 1) * LANES)
        t = kk[:, sl]
        t = t * lax.rsqrt(jnp.maximum(_head_sum(t * t, bd), 1e-24))
        kk_ref[:, sl] = t.astype(kk_ref.dtype)
        b_ref[:, sl] = (t * a[:, sl]).astype(b_ref.dtype)
    k_ref[...] = (k_raw * (1.0 + (a - 1.0) * k_a)).astype(k_ref.dtype)
    r_ref[...] = _dot(mix(0), wr_ref[...]).astype(r_ref.dtype)


def _head_ones():
    idx = jnp.arange(LANES) // RWKV_HEAD
    return (idx[:, None] == idx[None, :]).astype(BF16)


def _rwkv_prep(x, mu, w_rkv, vec, lora_a, lora_b, gate_a, gate_b, seq, *, tm=512):
    m, d = x.shape
    gpad = RWKV_GATE_PAD - gate_a.shape[1]
    ga = jnp.pad(gate_a, ((0, 0), (0, gpad))).astype(BF16)
    gb = jnp.pad(gate_b, ((0, gpad), (0, 0))).astype(BF16)
    wts = [w_rkv[0].astype(BF16), w_rkv[1].astype(BF16), w_rkv[2].astype(BF16),
           lora_a[0].astype(BF16), lora_b[0].astype(BF16), lora_a[1].astype(BF16),
           lora_b[1].astype(BF16), ga, gb]
    vec8 = jnp.pad(vec, ((0, SUBLANES - vec.shape[0]), (0, 0)))
    mu8 = jnp.pad(mu, ((0, SUBLANES - mu.shape[0]), (0, 0)))
    row = lambda i: (i, 0)
    sub = tm // SUBLANES
    out = jax.ShapeDtypeStruct((m, d), BF16)
    return pl.pallas_call(
        functools.partial(_rwkv_prep_body, tiles_per_seq=seq // tm),
        grid=(m // tm,),
        in_specs=[pl.BlockSpec((tm, d), row),
                  pl.BlockSpec((SUBLANES, d), lambda i: (jnp.maximum(i * sub - 1, 0), 0)),
                  _resident(mu8.shape)] + [_resident(w.shape) for w in wts]
        + [_resident(vec8.shape), _resident((LANES, LANES))],
        out_specs=[pl.BlockSpec((tm, d), row)] * 7,
        out_shape=[out, jax.ShapeDtypeStruct((m, d), F32), out, out, out, out, out],
        compiler_params=_cparams(("parallel",)),
        name="rwkv_prep",
    )(x, x, mu8, *wts, vec8, _head_ones())


def _rwkv_wkv_body(r_ref, lw_ref, k_ref, v_ref, kk_ref, b_ref, tri_ref, y_ref, state_ref):
    c = RWKV_CHUNK
    two = 2 * c

    @pl.when(pl.program_id(1) == 0)
    def _():
        state_ref[...] = jnp.zeros(state_ref.shape, F32)

    tri = tri_ref[...]

    def decayed(bi):
        lw = lw_ref[bi]
        h1 = lw.astype(BF16)
        r1 = lw - h1.astype(F32)
        h2 = r1.astype(BF16)
        h3 = (r1 - h2.astype(F32)).astype(BF16)
        cum = _dot(tri, h1) + _dot(tri, h2) + _dot(tri, h3)
        gam = jnp.exp(cum)
        gam_inv = jnp.exp(-cum)
        return (r_ref[bi].astype(F32) * gam, kk_ref[bi].astype(F32) * jnp.exp(cum - lw),
                b_ref[bi].astype(F32) * gam_inv, k_ref[bi].astype(F32) * gam_inv, gam[c - 1:c, :])

    lane_lo = lax.broadcasted_iota(jnp.int32, (c, LANES), 1) < RWKV_HEAD
    row2 = lax.broadcasted_iota(jnp.int32, (two, two), 0)
    col2 = lax.broadcasted_iota(jnp.int32, (two, two), 1)
    same = (row2 // c) == (col2 // c)
    strict = jnp.logical_and(same, row2 > col2)
    incl = jnp.logical_and(same, row2 >= col2)
    eye = (row2 == col2).astype(F32)

    def stack_masked(t):
        return jnp.concatenate([jnp.where(lane_lo, t, 0.0), jnp.where(lane_lo, 0.0, t)], axis=0)

    nb = r_ref.shape[0]
    items = [(bi, p) for bi in range(nb) for p in range(D_MODEL // LANES)]
    idx = range(len(items))
    sls = [slice(p * LANES, (p + 1) * LANES) for _, p in items]
    dec = [decayed(bi) for bi in range(nb)]
    xs, xu, bds, kds, vss, gend = [], [], [], [], [], []
    for (bi, _), sl in zip(items, sls):
        rt, kkt, bt, kt, gam_end = dec[bi]
        xs.append(jnp.concatenate([stack_masked(kkt[:, sl]), stack_masked(rt[:, sl])], axis=0).astype(BF16))
        xu.append(jnp.concatenate([kkt[:, sl], rt[:, sl]], axis=0).astype(BF16))
        bds.append(jnp.concatenate([bt[:, sl], bt[:, sl]], axis=0).astype(BF16))
        kds.append(jnp.concatenate([kt[:, sl], kt[:, sl]], axis=0).astype(BF16))
        v2 = v_ref[bi, :, sl].astype(F32)
        vss.append(jnp.where(same, jnp.concatenate([v2, v2], axis=0), 0.0).astype(BF16))
        gend.append(gam_end[:, sl])
    s2s = [state_ref[bi, p] for bi, p in items]
    a_all = [_dot_nt(xs[i], jnp.concatenate([bds[i][:c], kds[i][:c]], axis=0)) for i in idx]
    x_state = [_dot_nt(xu[i], s2s[i].astype(BF16)) for i in idx]
    head0 = row2 < c
    nmat, lk, arb, ark = [], [], [], []
    for a in a_all:
        top, bot = a[:two], a[two:]
        top_r, bot_r = pltpu.roll(top, c, 1), pltpu.roll(bot, c, 1)
        nmat.append(jnp.where(strict, -jnp.where(head0, top, top_r), 0.0))
        lk.append(jnp.where(strict, jnp.where(head0, top_r, top), 0.0).astype(BF16))
        arb.append(jnp.where(incl, jnp.where(head0, bot, bot_r), 0.0).astype(BF16))
        ark.append(jnp.where(incl, jnp.where(head0, bot_r, bot), 0.0).astype(BF16))
    rhs = [stack_masked(x_state[i][:c]) + _dot(lk[i], vss[i]) for i in idx]
    pw = [n_.astype(BF16) for n_ in nmat]
    inv = [eye + n_ for n_ in nmat]
    pw = [_dot(t, t).astype(BF16) for t in pw]
    for _ in range(int(math.log2(c)) - 2):
        both = [_dot(pw[i], jnp.concatenate([pw[i], inv[i].astype(BF16)], axis=1)) for i in idx]
        inv = [inv[i] + both[i][:, two:] for i in idx]
        pw = [t[:, :two].astype(BF16) for t in both]
    inv = [inv[i] + _dot(pw[i], inv[i].astype(BF16)) for i in idx]
    ub = [(-_dot(inv[i].astype(BF16), rhs[i].astype(BF16))).astype(BF16) for i in idx]
    for i, (bi, _) in enumerate(items):
        ys = _dot(arb[i], ub[i]) + _dot(ark[i], vss[i])
        y_ref[bi, :, sls[i]] = x_state[i][c:] + ys[:c] + ys[c:]
    for i, (bi, p) in enumerate(items):
        ds = _dot_tn(ub[i], bds[i]) + _dot_tn(vss[i], kds[i])
        state_ref[bi, p] = jnp.where(same, (s2s[i] + ds) * gend[i], 0.0)


def _rwkv_wkv(r, lw, k, v, kk, b, batch, seq):
    c = RWKV_CHUNK
    n = seq // c
    d = D_MODEL
    tri = (jnp.arange(c)[:, None] >= jnp.arange(c)[None, :]).astype(BF16)
    nb = RWKV_BATCH_ROWS if batch % RWKV_BATCH_ROWS == 0 else 1
    blk = pl.BlockSpec((nb, c, d), lambda bi, i: (bi, i, 0))
    as3d = lambda t: t.reshape(batch, seq, d)
    y = pl.pallas_call(
        _rwkv_wkv_body,
        grid=(batch // nb, n),
        in_specs=[blk] * 6 + [_resident((c, c))],
        out_specs=blk,
        out_shape=jax.ShapeDtypeStruct((batch, seq, d), F32),
        scratch_shapes=[pltpu.VMEM((nb, d // LANES, LANES, LANES), F32)],
        compiler_params=_cparams(("parallel", "arbitrary")),
        name="rwkv_wkv",
    )(as3d(r), as3d(lw), as3d(k), as3d(v), as3d(kk), as3d(b), tri)
    return y.reshape(batch * seq, d)


def _rwkv_out_body(y_ref, r_ref, k_ref, v_ref, g_ref, vec_ref, bd_ref, w_ref, res_ref, lg_ref,
                   lb_ref, o_ref, a_ref):
    bd = bd_ref[...]
    inv_n = 1.0 / RWKV_HEAD
    for s in range(D_MODEL // LANES):
        sl = slice(s * LANES, (s + 1) * LANES)
        y = y_ref[:, sl]
        mu = _head_sum(y, bd) * inv_n
        dlt = y - mu
        var = _head_sum(dlt * dlt, bd) * inv_n
        yn = dlt * lax.rsqrt(var + RWKV_GN_EPS) * vec_ref[0:1, sl] + vec_ref[1:2, sl]
        rk = r_ref[:, sl] * k_ref[:, sl] * vec_ref[2:3, sl].astype(BF16)
        bonus = _head_sum(rk, bd) * v_ref[:, sl].astype(F32)
        a_ref[:, sl] = ((yn + bonus) * g_ref[:, sl].astype(F32)).astype(BF16)
    acc = _dot(a_ref[...], w_ref[...])
    o_ref[...] = _layer_norm(DN_ALPHA * res_ref[...] + acc, lg_ref[...], lb_ref[...])


def _rwkv_out(y, r, k, v, g, vec, w, res, lg, lb, *, tm=512):
    m, d = res.shape
    row = lambda i: (i, 0)
    act = pl.BlockSpec((tm, d), row)
    return pl.pallas_call(
        _rwkv_out_body,
        grid=(m // tm,),
        in_specs=[act] * 5 + [_resident(vec.shape), _resident((LANES, LANES)), _resident((d, d)),
                              act, _resident((1, d)), _resident((1, d))],
        out_specs=act,
        out_shape=jax.ShapeDtypeStruct((m, d), F32),
        scratch_shapes=[pltpu.VMEM((tm, d), BF16)],
        compiler_params=_cparams(("parallel",)),
        name="rwkv_out",
    )(y, r, k, v, g, vec, _head_ones(), w, res, lg, lb)


def kernel(x, ret_w_in, ret_gn, ret_w_out, dil_w_in, dil_w_out, mla_w_down, mla_norm_q,
           mla_norm_kv, mla_w_uq, mla_w_ukv, mla_w_out, rwkv_mu, rwkv_w_rkv, rwkv_w_out,
           rwkv_vec, rwkv_lora_a, rwkv_lora_b, rwkv_gate_a, rwkv_gate_b, rwkv_ln_x,
           mlp_w1, mlp_w2, ln_g, ln_b):
    batch, seq, d = x.shape
    xf = x.reshape(batch * seq, d)
    w1_all, w2_all = mlp_w1.astype(BF16), mlp_w2.astype(BF16)
    n_mixers = 4
    for i in range(DEPTH):
        mixer, j = i % n_mixers, i // n_mixers
        lg, lb = ln_g[i, 0][None, :], ln_b[i, 0][None, :]
        if mixer == 0:
            proj = _mm(xf, ret_w_in[j].astype(BF16), tm=512, name="ret_proj")
            gated = _retention(proj, ret_gn[j], batch, seq)
            xf = _mm_res_ln(gated, ret_w_out[j].astype(BF16), xf, lg, lb, name="ret_out")
        elif mixer == 1:
            outs, lses = _dilated(xf, dil_w_in[j].astype(BF16), batch, seq)
            xf = _dil_out(outs, lses, dil_w_out[j].astype(BF16), xf, lg, lb, seq)
        elif mixer == 2:
            o = _mla(xf, mla_w_down[j], mla_norm_q[j], mla_norm_kv[j], mla_w_uq[j], mla_w_ukv[j],
                     batch, seq)
            xf = _mm_res_ln(o, mla_w_out[j].astype(BF16), xf, lg, lb, name="mla_out")
        else:
            r, lw, k, v, kk, b, g = _rwkv_prep(xf, rwkv_mu[j], rwkv_w_rkv[j], rwkv_vec[j],
                                               rwkv_lora_a[j], rwkv_lora_b[j], rwkv_gate_a[j],
                                               rwkv_gate_b[j], seq)
            y = _rwkv_wkv(r, lw, k, v, kk, b, batch, seq)
            vec = jnp.concatenate([rwkv_ln_x[j], rwkv_vec[j][4:5],
                                   jnp.zeros((SUBLANES - 3, d), F32)], axis=0)
            xf = _rwkv_out(y, r, k, v, g, vec, rwkv_w_out[j].astype(BF16), xf, lg, lb)
        xf = _mlp(xf, w1_all, w2_all, i, ln_g[i, 1][None, :], ln_b[i, 1][None, :])
    return xf.reshape(batch, seq, d)
```

```python
import functools
import math

import jax
import jax.numpy as jnp
import numpy as np
from jax import lax
from jax.experimental import pallas as pl
from jax.experimental.pallas import tpu as pltpu

F32 = jnp.float32
BF16 = jnp.bfloat16

D_MODEL = 1024
DEPTH = 4
D_FF = 4 * D_MODEL
LN_EPS = 1e-5
RMS_EPS = 1e-6
GN_EPS = 1e-5
DN_ALPHA = (2.0 * DEPTH) ** 0.25
NEG = -1e30
LANES = 128
SUBLANES = 8
ROPE_PARTNER = 64
MM_SUB = 256

RET_HEADS = 4
RET_QK_DIM = 256
RET_V_DIM = 512
RET_CHUNK = 512
RET_THETA = 10000.0

DIL_PAIRS = ((128, 1), (512, 4), (2048, 16))
DIL_HEADS = 8
DIL_HEAD_DIM = 128
DIL_ROT = 32
DIL_BLOCK = 128
DIL_TQ = 2048
DIL_WAVE = 4
ROPE_THETA = 500000.0

MLA_HEADS = 16
MLA_NOPE = 128
MLA_ROPE = 64
MLA_V = 128
MLA_Q_RANK = 256
MLA_KV_RANK = 128
MLA_THETA = 10000.0
MLA_TQ = 1024
MLA_TK = 1024
MLA_ROW_SPLIT = 4
assert MLA_TQ == MLA_TK

RWKV_HEAD = 64
RWKV_HEADS = D_MODEL // RWKV_HEAD
RWKV_GN_EPS = 64e-5
RWKV_CHUNK = 64
RWKV_GATE_PAD = 256
RWKV_BATCH_ROWS = 8

V7X_VMEM_BYTES = 64 * 1024 * 1024
VMEM_LIMIT = V7X_VMEM_BYTES // 8 * 7


def _cparams(sem):
    return pltpu.CompilerParams(dimension_semantics=sem, vmem_limit_bytes=VMEM_LIMIT)


def _resident(shape):
    nd = len(shape)
    return pl.BlockSpec(shape, lambda *_: (0,) * nd, pipeline_mode=pl.Buffered(1))


def _layer_norm(z, g, b):
    mu = jnp.mean(z, axis=-1, keepdims=True)
    d = z - mu
    var = jnp.mean(d * d, axis=-1, keepdims=True)
    return d * lax.rsqrt(var + LN_EPS) * g + b


def _dot(a, b):
    return jnp.dot(a, b, preferred_element_type=F32)


def _dot_nt(a, b):
    return lax.dot_general(a, b, (((1,), (1,)), ((), ())), preferred_element_type=F32)


def _dot_tn(a, b):
    return lax.dot_general(a, b, (((0,), (0,)), ((), ())), preferred_element_type=F32)


def _rope_tile(a, c, s):
    return a * c + pltpu.roll(a, ROPE_PARTNER, 1) * s


def _project_columns(xb, w_ref, store, modes, tabs, scale):
    n = w_ref.shape[1]
    for c0 in range(0, n, MM_SUB):
        acc = _dot(xb, w_ref[:, c0:c0 + MM_SUB])
        for t in range(MM_SUB // LANES):
            a = acc[:, t * LANES:(t + 1) * LANES]
            mode = modes[c0 // LANES + t] if modes is not None else None
            if mode == "scale":
                a = a * scale
            elif mode is not None:
                c_ref, s_ref = tabs
                a = _rope_tile(a, c_ref[mode[1]], s_ref[mode[1]])
            store(c0 + t * LANES, a)


def _mm_body(x_ref, w_ref, *rest, modes, scale):
    tabs, o_ref = rest[:-1], rest[-1]

    def store(c0, a):
        o_ref[:, c0:c0 + LANES] = a.astype(o_ref.dtype)

    _project_columns(x_ref[...].astype(BF16), w_ref, store, modes, tabs, scale)


def _mm(x, w, *, tm, out_dtype=BF16, tabs=None, tab_map=None, modes=None, scale=1.0, name="mm"):
    m = x.shape[0]
    k, n = w.shape
    row = lambda i: (i, 0)
    in_specs = [pl.BlockSpec((tm, k), row), _resident((k, n))]
    args = [x, w]
    if tabs is not None:
        for t in tabs:
            in_specs.append(pl.BlockSpec((t.shape[0], tm, LANES), tab_map))
            args.append(t)
    return pl.pallas_call(
        functools.partial(_mm_body, modes=modes, scale=scale),
        grid=(m // tm,),
        in_specs=in_specs,
        out_specs=pl.BlockSpec((tm, n), row),
        out_shape=jax.ShapeDtypeStruct((m, n), out_dtype),
        compiler_params=_cparams(("parallel",)),
        name=name,
    )(*args)


def _mm_res_ln_body(a_ref, w_ref, res_ref, g_ref, b_ref, o_ref):
    half = a_ref.shape[0] // 2
    for r0 in (0, half):
        acc = _dot(a_ref[r0:r0 + half, :], w_ref[...])
        o_ref[r0:r0 + half, :] = _layer_norm(DN_ALPHA * res_ref[r0:r0 + half, :] + acc,
                                             g_ref[...], b_ref[...])


def _mm_res_ln(a, w, res, g, b, *, tm=1024, name="mm_res_ln"):
    m, k = a.shape
    d = w.shape[1]
    row = lambda i: (i, 0)
    return pl.pallas_call(
        _mm_res_ln_body,
        grid=(m // tm,),
        in_specs=[pl.BlockSpec((tm, k), row), _resident((k, d)), pl.BlockSpec((tm, d), row),
                  _resident((1, d)), _resident((1, d))],
        out_specs=pl.BlockSpec((tm, d), row),
        out_shape=jax.ShapeDtypeStruct((m, d), F32),
        compiler_params=_cparams(("parallel",)),
        name=name,
    )(a, w, res, g, b)


def _mlp_body(x_ref, w1_ref, w2_ref, g_ref, b_ref, o_ref, *, fchunk):
    half = x_ref.shape[0] // 2
    for r0 in (0, half):
        x = x_ref[r0:r0 + half, :]
        xb = x.astype(BF16)
        acc = jnp.zeros(x.shape, F32)
        for c in range(D_FF // fchunk):
            h = _dot(xb, w1_ref[:, c * fchunk:(c + 1) * fchunk])
            h = jnp.maximum(h, 0.0)
            h = (h * h).astype(BF16)
            acc = acc + _dot(h, w2_ref[c * fchunk:(c + 1) * fchunk, :])
        o_ref[r0:r0 + half, :] = _layer_norm(DN_ALPHA * x + acc, g_ref[...], b_ref[...])


def _mlp(x, w1_all, w2_all, layer, g, b, *, tm=512, fchunk=1024):
    m, d = x.shape
    row = lambda i: (i, 0)
    slab = lambda shape: pl.BlockSpec((None,) + shape, lambda i: (layer, 0, 0),
                                      pipeline_mode=pl.Buffered(1))
    return pl.pallas_call(
        functools.partial(_mlp_body, fchunk=fchunk),
        grid=(m // tm,),
        in_specs=[pl.BlockSpec((tm, d), row), slab((d, D_FF)), slab((D_FF, d)),
                  _resident((1, d)), _resident((1, d))],
        out_specs=pl.BlockSpec((tm, d), row),
        out_shape=jax.ShapeDtypeStruct((m, d), F32),
        compiler_params=_cparams(("parallel",)),
        name="mlp",
    )(x, w1_all, w2_all, g, b)


def _ret_body(q_ref, k_ref, v_ref, g_ref, cos_ref, sin_ref, intra_ref, qdec_ref, kdec_ref,
              cdec_ref, gn_ref, o_ref, state_ref):
    dk, dv, half = RET_QK_DIM, RET_V_DIM, RET_QK_DIM // 2

    @pl.when(pl.program_id(1) == 0)
    def _():
        state_ref[...] = jnp.zeros(state_ref.shape, F32)

    cos = cos_ref[...]
    sin = sin_ref[...]

    def rope(t):
        t1 = t[:, :half].astype(F32)
        t2 = t[:, half:].astype(F32)
        return jnp.concatenate([t1 * cos - t2 * sin, t2 * cos + t1 * sin], axis=-1)

    heads = range(RET_HEADS)
    q = [rope(q_ref[:, h * dk:(h + 1) * dk]) for h in heads]
    k = [rope(k_ref[:, h * dk:(h + 1) * dk]) * (dk ** -0.5) for h in heads]
    v = [v_ref[:, h * dv:(h + 1) * dv] for h in heads]
    qb = [t.astype(BF16) for t in q]
    state = [state_ref[h] for h in heads]
    scores = [(_dot_nt(qb[h], k[h].astype(BF16)) * intra_ref[h]).astype(BF16) for h in heads]
    cross = [_dot(qb[h], state[h].astype(BF16)) * qdec_ref[h] for h in heads]
    o = [_dot(scores[h], v[h]) + cross[h] for h in heads]
    for h in heads:
        state_ref[h] = (state[h] * cdec_ref[h, 0:1, :]
                        + _dot_tn((k[h] * kdec_ref[h]).astype(BF16), v[h]))
    for h in heads:
        sl = slice(h * dv, (h + 1) * dv)
        mu = jnp.mean(o[h], axis=-1, keepdims=True)
        d = o[h] - mu
        var = jnp.mean(d * d, axis=-1, keepdims=True)
        on = d * lax.rsqrt(var + GN_EPS) * gn_ref[0:1, sl] + gn_ref[1:2, sl]
        gate = g_ref[:, sl].astype(F32)
        gate = gate * (1.0 / (1.0 + jnp.exp(-gate)))
        o_ref[:, sl] = (gate * on).astype(o_ref.dtype)


def _retention(proj, gn, batch, seq):
    h_, dk, dv, c = RET_HEADS, RET_QK_DIM, RET_V_DIM, RET_CHUNK
    n = seq // c
    half = dk // 2
    f32 = np.float32
    pos = np.arange(seq, dtype=f32)
    inv_freq = f32(RET_THETA) ** (-np.arange(half, dtype=f32) / f32(half))
    ang = pos[:, None] * inv_freq[None, :]
    cos, sin = np.cos(ang), np.sin(ang)
    log_gamma = np.log(f32(1.0) - f32(2.0) ** (f32(-5.0) - np.arange(h_, dtype=f32)))
    idx = np.arange(c, dtype=f32)
    diff = idx[:, None] - idx[None, :]
    intra = np.where(diff >= 0, np.exp(log_gamma[:, None, None] * np.maximum(diff, f32(0.0))), f32(0.0))
    intra = intra.astype(f32)
    qdec = np.broadcast_to(np.exp(log_gamma[:, None] * (idx + f32(1.0)))[:, :, None], (h_, c, dv))
    kdec = np.broadcast_to(np.exp(log_gamma[:, None] * (f32(c - 1.0) - idx))[:, :, None], (h_, c, dk))
    cdec = np.broadcast_to(np.exp(log_gamma * f32(c))[:, None, None], (h_, SUBLANES, dv))
    qdec, kdec, cdec = (np.ascontiguousarray(t, dtype=f32) for t in (qdec, kdec, cdec))
    qk_w, vg_w = h_ * dk, h_ * dv
    return pl.pallas_call(
        _ret_body,
        grid=(batch, n),
        in_specs=[
            pl.BlockSpec((c, qk_w), lambda b, i: (b * n + i, 0)),
            pl.BlockSpec((c, qk_w), lambda b, i: (b * n + i, 1)),
            pl.BlockSpec((c, vg_w), lambda b, i: (b * n + i, 2 * qk_w // vg_w)),
            pl.BlockSpec((c, vg_w), lambda b, i: (b * n + i, 2 * qk_w // vg_w + 1)),
            pl.BlockSpec((c, half), lambda b, i: (i, 0)),
            pl.BlockSpec((c, half), lambda b, i: (i, 0)),
            _resident((h_, c, c)), _resident((h_, c, dv)), _resident((h_, c, dk)),
            _resident((h_, SUBLANES, dv)), _resident((2, vg_w)),
        ],
        out_specs=pl.BlockSpec((c, vg_w), lambda b, i: (b * n + i, 0)),
        out_shape=jax.ShapeDtypeStruct((batch * seq, vg_w), BF16),
        scratch_shapes=[pltpu.VMEM((h_, dk, dv), F32)],
        compiler_params=_cparams(("parallel", "arbitrary")),
        name="retention",
    )(proj, proj, proj, proj, cos, sin, intra, qdec, kdec, cdec, gn)


def _dil_attn_body(q_ref, kp_ref, kc_ref, vp_ref, vc_ref, o_ref, lse_ref, *, seq_blocks):
    blk = DIL_BLOCK
    n_sub = q_ref.shape[0] // blk
    has_prev = pl.program_id(1) > 0
    qi = lax.broadcasted_iota(jnp.int32, (blk, 2 * blk), 0)
    ki = lax.broadcasted_iota(jnp.int32, (blk, 2 * blk), 1)
    band = jnp.logical_and(ki >= qi, ki <= qi + blk)
    band_first = jnp.logical_and(band, jnp.logical_or(ki >= blk, has_prev))
    band_start = jnp.logical_and(band, ki >= blk)

    def mask(j):
        if j == 0:
            return band_first
        return band_start if j % seq_blocks == 0 else band
    lane = lax.broadcasted_iota(jnp.int32, (blk, LANES), 1)
    ones = jnp.ones((2 * blk, LANES), BF16)
    def keys(prev_ref, cur_ref, j, sl):
        if j == 0:
            return jnp.concatenate([prev_ref[:, sl], cur_ref[:blk, sl]], axis=0)
        return cur_ref[(j - 1) * blk:(j + 1) * blk, sl]

    for j0 in range(0, n_sub, DIL_WAVE):
        items = [(j, h) for j in range(j0, min(j0 + DIL_WAVE, n_sub)) for h in range(DIL_HEADS)]
        sls = [slice(h * DIL_HEAD_DIM, (h + 1) * DIL_HEAD_DIM) for _, h in items]
        rows = [slice(j * blk, (j + 1) * blk) for j, _ in items]
        s = [jnp.where(mask(j),
                       _dot_nt(q_ref[rows[i], sls[i]], keys(kp_ref, kc_ref, j, sls[i])), NEG)
             for i, (j, _) in enumerate(items)]
        m = [jnp.max(t, axis=-1, keepdims=True) for t in s]
        p = [jnp.exp((s[i] - m[i]).astype(BF16)) for i in range(len(items))]
        pv = [_dot(p[i], jnp.concatenate([keys(vp_ref, vc_ref, j, sls[i]), ones], axis=1))
              for i, (j, _) in enumerate(items)]
        lse_tiles = {j: jnp.zeros((blk, LANES), F32) for j, _ in items}
        for i, (j, h) in enumerate(items):
            l = pv[i][:, DIL_HEAD_DIM:]
            o_ref[rows[i], sls[i]] = (pv[i][:, :DIL_HEAD_DIM] / l).astype(o_ref.dtype)
            lse_tiles[j] = jnp.where(lane == h, m[i] + jnp.log(l), lse_tiles[j])
        for j, tile in lse_tiles.items():
            lse_ref[j * blk:(j + 1) * blk, :] = tile


def _dil_attn(q, k, v, batch, seq, dil):
    blk = DIL_BLOCK
    hd = DIL_HEADS * DIL_HEAD_DIM
    sub = seq // dil
    tq = DIL_TQ
    nb = max(sub // tq, 1)
    n_seq = batch * dil * sub // (nb * tq)
    per = tq // blk
    cur = lambda z, i: (z * nb + i, 0)
    prev = lambda z, i: (jnp.maximum((z * nb + i) * per - 1, 0), 0)
    out_map = cur
    return pl.pallas_call(
        functools.partial(_dil_attn_body, seq_blocks=sub // blk),
        grid=(n_seq, nb),
        in_specs=[pl.BlockSpec((tq, hd), cur), pl.BlockSpec((blk, hd), prev),
                  pl.BlockSpec((tq, hd), cur), pl.BlockSpec((blk, hd), prev),
                  pl.BlockSpec((tq, hd), cur)],
        out_specs=[pl.BlockSpec((tq, hd), out_map), pl.BlockSpec((tq, LANES), out_map)],
        out_shape=[jax.ShapeDtypeStruct((batch * seq, hd), BF16),
                   jax.ShapeDtypeStruct((batch * seq, LANES), F32)],
        compiler_params=_cparams(("parallel", "arbitrary")),
        name=f"dil_attn_{dil}",
    )(q, k, k, v, v)


def _dil_out_body(o0_ref, o1_ref, o2_ref, l0_ref, l1_ref, l2_ref, e_ref, w_ref, res_ref,
                  g_ref, b_ref, o_ref, osc_ref, lsc_ref, mix_ref):
    tm = o_ref.shape[0]
    nh = DIL_HEADS
    for gi, (og, lg) in enumerate(((o0_ref, l0_ref), (o1_ref, l1_ref), (o2_ref, l2_ref))):
        dil = og.shape[1]
        n = tm // dil
        for r in range(dil):
            rows = pl.ds(r, n, stride=dil) if dil > 1 else slice(None)
            for h in range(nh):
                osc_ref[gi * nh + h, rows, :] = og[0, r, :, h * LANES:(h + 1) * LANES].astype(F32)
            lsc_ref[gi, rows, :] = lg[0, r]
    l0, l1, l2 = lsc_ref[0], lsc_ref[1], lsc_ref[2]
    m = jnp.maximum(jnp.maximum(l0, l1), l2)
    e0, e1, e2 = jnp.exp(l0 - m), jnp.exp(l1 - m), jnp.exp(l2 - m)
    den = e0 + e1 + e2
    e = e_ref[...]
    wfull = []
    for eg in (e0, e1, e2):
        wfull.append(_dot((eg / den).astype(BF16), e))
    for h in range(nh):
        sl = slice(h * LANES, (h + 1) * LANES)
        mixed = sum(wfull[gi][:, sl] * osc_ref[gi * nh + h] for gi in range(3))
        mix_ref[:, sl] = mixed.astype(BF16)
    acc = _dot(mix_ref[...], w_ref[...])
    o_ref[...] = _layer_norm(DN_ALPHA * res_ref[...] + acc, g_ref[...], b_ref[...])


def _dil_out(outs, lses, w, res, g, b, seq, *, tm=512):
    m, d = res.shape
    hd = DIL_HEADS * DIL_HEAD_DIM
    expand = (jnp.arange(LANES)[:, None] == (jnp.arange(hd) // DIL_HEAD_DIM)[None, :]).astype(BF16)
    row = lambda i: (i, 0)
    nt = seq // tm
    grp = lambda i: (i // nt, 0, i % nt, 0)
    dils = [dil for _, dil in DIL_PAIRS]
    batch = m // seq
    o4 = [o.reshape(batch, dil, seq // dil, hd) for o, dil in zip(outs, dils)]
    l4 = [l.reshape(batch, dil, seq // dil, LANES) for l, dil in zip(lses, dils)]
    return pl.pallas_call(
        _dil_out_body,
        grid=(m // tm,),
        in_specs=[pl.BlockSpec((1, dil, tm // dil, hd), grp) for dil in dils]
        + [pl.BlockSpec((1, dil, tm // dil, LANES), grp) for dil in dils]
        + [_resident((LANES, hd)), _resident((hd, d)), pl.BlockSpec((tm, d), row),
           _resident((1, d)), _resident((1, d))],
        out_specs=pl.BlockSpec((tm, d), row),
        out_shape=jax.ShapeDtypeStruct((m, d), F32),
        scratch_shapes=[pltpu.VMEM((len(dils) * DIL_HEADS, tm, LANES), F32),
                        pltpu.VMEM((len(dils), tm, LANES), F32), pltpu.VMEM((tm, hd), BF16)],
        compiler_params=_cparams(("parallel",)),
        name="dil_out",
    )(*o4, *l4, expand, w, res, g, b)


def _rope_tables(seq, rot, theta, scale, passthrough):
    half = rot // 2
    inv_freq = np.float32(theta) ** (-np.arange(half, dtype=np.float32) / np.float32(half))
    ang = np.arange(seq, dtype=np.float32)[:, None] * inv_freq[None, :]
    cos, sin = np.cos(ang), np.sin(ang)
    fill = np.full((seq, ROPE_PARTNER - half), passthrough, np.float32)
    zero = np.zeros((seq, ROPE_PARTNER - half), np.float32)
    c = np.concatenate([cos, fill, cos, fill], axis=1)
    s = np.concatenate([-sin, zero, sin, zero], axis=1)
    return (c * np.float32(scale)).astype(np.float32), (s * np.float32(scale)).astype(np.float32)


def _rope_lane_order(rot, width):
    half = rot // 2
    rest = list(range(rot, width))
    cut = ROPE_PARTNER - half
    return jnp.array(list(range(half)) + rest[:cut] + list(range(half, rot)) + rest[cut:])


def _dil_proj_body(x_ref, w_ref, c_ref, s_ref, q_ref, k_ref, v_ref, xb_ref, xs_ref, *, dil):
    tm = x_ref.shape[0]
    n = tm // dil
    if dil == 1:
        xb = x_ref[...].astype(BF16)
    else:
        for c in range(D_MODEL // LANES):
            xs_ref[c] = x_ref[:, c * LANES:(c + 1) * LANES]
        for r in range(dil):
            for c in range(D_MODEL // LANES):
                xb_ref[r * n:(r + 1) * n, c * LANES:(c + 1) * LANES] = (
                    xs_ref[c, pl.ds(r, n, stride=dil), :].astype(BF16))
        xb = xb_ref[...]

    hd = DIL_HEADS * DIL_HEAD_DIM

    def store(c0, a):
        o_ref = (q_ref, k_ref, v_ref)[c0 // hd]
        o_ref[0, :, :, c0 % hd:c0 % hd + LANES] = a.astype(o_ref.dtype).reshape(dil, n, LANES)

    modes = [("rope", 0)] * DIL_HEADS + [("rope", 1)] * DIL_HEADS + [None] * DIL_HEADS
    _project_columns(xb, w_ref, store, modes, (c_ref, s_ref), 1.0)


def _dil_proj(x, w, tabs, batch, seq, dil, *, tm=512):
    hd = DIL_HEADS * DIL_HEAD_DIM
    nt = seq // tm
    n = tm // dil
    tab_spec = pl.BlockSpec((2, tm, LANES), lambda i: (0, i % nt, 0))
    return pl.pallas_call(
        functools.partial(_dil_proj_body, dil=dil),
        grid=(batch * nt,),
        in_specs=[pl.BlockSpec((tm, D_MODEL), lambda i: (i, 0)), _resident((D_MODEL, 3 * hd)),
                  tab_spec, tab_spec],
        out_specs=[pl.BlockSpec((1, dil, n, hd), lambda i: (i // nt, 0, i % nt, 0))] * 3,
        out_shape=[jax.ShapeDtypeStruct((batch, dil, seq // dil, hd), BF16)] * 3,
        scratch_shapes=[pltpu.VMEM((tm, D_MODEL), BF16),
                        pltpu.VMEM((D_MODEL // LANES, tm, LANES), F32)],
        compiler_params=_cparams(("parallel",)),
        name=f"dil_proj_{dil}",
    )(x, w, *tabs)


def _dilated(x, w_in, batch, seq, *, tm=512):
    hd = DIL_HEADS * DIL_HEAD_DIM
    cq, sq = _rope_tables(seq, DIL_ROT, ROPE_THETA, DIL_HEAD_DIM ** -0.5, 1.0)
    ck, sk = _rope_tables(seq, DIL_ROT, ROPE_THETA, 1.0, 1.0)
    order = _rope_lane_order(DIL_ROT, DIL_HEAD_DIM)
    outs, lses = [], []
    for gi, (_, dil) in enumerate(DIL_PAIRS):
        wg = w_in[:, gi * 3 * hd:(gi + 1) * 3 * hd].reshape(D_MODEL, 3, DIL_HEADS, DIL_HEAD_DIM)
        wg = jnp.concatenate([wg[:, :2][..., order], wg[:, 2:]], axis=1).reshape(D_MODEL, 3 * hd)

        def by_residue(t):
            t = t.reshape(seq // tm, tm // dil, dil, LANES)
            return np.swapaxes(t, 1, 2).reshape(seq, LANES)

        tabs = [np.stack([by_residue(a), by_residue(b)]) for a, b in ((cq, ck), (sq, sk))]
        q, k, v = (t.reshape(batch * seq, hd) for t in _dil_proj(x, wg, tabs, batch, seq, dil, tm=tm))
        o, lse = _dil_attn(q, k, v, batch, seq, dil)
        outs.append(o)
        lses.append(lse)
    return outs, lses


def _mla_proj_body(x_ref, wd_ref, nq_ref, nkv_ref, wq_ref, wkv_ref, c_ref, s_ref,
                   q_ref, kv_ref, kpe_ref, *, scale):
    acc = _dot(x_ref[...].astype(BF16), wd_ref[...])
    cq = acc[:, :MLA_Q_RANK]
    ckv = acc[:, MLA_Q_RANK:MLA_Q_RANK + MLA_KV_RANK]
    kpe = acc[:, MLA_Q_RANK + MLA_KV_RANK:]
    cq = cq * lax.rsqrt(jnp.mean(cq * cq, axis=-1, keepdims=True) + RMS_EPS) * nq_ref[...]
    ckv = ckv * lax.rsqrt(jnp.mean(ckv * ckv, axis=-1, keepdims=True) + RMS_EPS) * nkv_ref[...]
    kpe_ref[...] = _rope_tile(kpe, c_ref[0], s_ref[0]).astype(kpe_ref.dtype)

    def store_q(c0, a):
        q_ref[:, c0:c0 + LANES] = a.astype(q_ref.dtype)

    def store_kv(c0, a):
        kv_ref[:, c0:c0 + LANES] = a.astype(kv_ref.dtype)

    _project_columns(cq.astype(BF16), wq_ref, store_q, ["scale", ("rope", 1)] * MLA_HEADS,
                     (c_ref, s_ref), scale)
    _project_columns(ckv.astype(BF16), wkv_ref, store_kv, None, None, 1.0)


def _mla_proj(x, wd, nq, nkv, wq, wkv, tabs, seq, scale, *, tm=512):
    m, d = x.shape
    row = lambda i: (i, 0)
    ns = seq // tm
    tab = pl.BlockSpec((2, tm, LANES), lambda i: (0, i % ns, 0))
    return pl.pallas_call(
        functools.partial(_mla_proj_body, scale=scale),
        grid=(m // tm,),
        in_specs=[pl.BlockSpec((tm, d), row), _resident(wd.shape), _resident((1, MLA_Q_RANK)),
                  _resident((1, MLA_KV_RANK)), _resident(wq.shape), _resident(wkv.shape), tab, tab],
        out_specs=[pl.BlockSpec((tm, wq.shape[1]), row), pl.BlockSpec((tm, wkv.shape[1]), row),
                   pl.BlockSpec((tm, LANES), row)],
        out_shape=[jax.ShapeDtypeStruct((m, wq.shape[1]), BF16),
                   jax.ShapeDtypeStruct((m, wkv.shape[1]), BF16),
                   jax.ShapeDtypeStruct((m, LANES), BF16)],
        compiler_params=_cparams(("parallel",)),
        name="mla_proj",
    )(x, wd, nq, nkv, wq, wkv, *tabs)


def _mla_flash_body(q_ref, kn_ref, kpe_ref, v_ref, o_ref, kcat_ref, vaug_ref):
    tq, tk, sub = MLA_TQ, MLA_TK, MLA_TQ // MLA_ROW_SPLIT
    kcat_ref[:, :MLA_NOPE] = kn_ref[...]
    kcat_ref[:, MLA_NOPE:] = kpe_ref[...]
    vaug_ref[:, :MLA_V] = v_ref[...]
    vaug_ref[:, MLA_V:] = jnp.ones((v_ref.shape[0], LANES), BF16)

    parts = range(MLA_ROW_SPLIT)
    col_minus_row = (lax.broadcasted_iota(jnp.int32, (sub, tk), 1)
                     - lax.broadcasted_iota(jnp.int32, (sub, tk), 0))

    def scores(tile, chunk, diagonal):
        out = []
        for part in parts:
            width = (part + 1) * sub if diagonal else tk
            q0 = tile * tq + part * sub
            s = _dot_nt(q_ref[q0:q0 + sub, :], kcat_ref[chunk * tk:chunk * tk + width, :])
            if diagonal:
                s = jnp.where(col_minus_row[:, :width] <= part * sub, s, NEG)
            out.append(s)
        return out

    def update(chunk, s_all, carry):
        out = []
        for s, (m, acc) in zip(s_all, carry):
            vb = vaug_ref[chunk * tk:chunk * tk + s.shape[1], :]
            m_new = jnp.maximum(m, jnp.max(s, axis=-1, keepdims=True))
            alpha = jnp.exp2(m - m_new)
            p = jnp.exp2((s - m_new).astype(BF16))
            out.append((m_new, alpha * acc + _dot(p, vb)))
        return out

    for tile in range(q_ref.shape[0] // tq):
        n_chunks = tile + 1
        carry = [(jnp.full((sub, 1), NEG, F32), jnp.zeros((sub, MLA_V + LANES), F32)) for _ in parts]
        s = scores(tile, 0, n_chunks == 1)
        for c in range(n_chunks):
            s_next = scores(tile, c + 1, c + 2 == n_chunks) if c + 1 < n_chunks else None
            carry = update(c, s, carry)
            s = s_next
        for part, (_, acc) in enumerate(carry):
            q0 = tile * tq + part * sub
            o_ref[q0:q0 + sub, :] = (acc[:, :MLA_V] / acc[:, MLA_V:]).astype(o_ref.dtype)


def _mla_flash(q, kv, kpe, batch, seq):
    h_ = MLA_HEADS
    qw = MLA_NOPE + LANES
    return pl.pallas_call(
        _mla_flash_body,
        grid=(batch, h_),
        in_specs=[pl.BlockSpec((seq, qw), lambda b, h: (b, h)),
                  pl.BlockSpec((seq, MLA_NOPE), lambda b, h: (b, h)),
                  pl.BlockSpec((seq, LANES), lambda b, h: (b, 0)),
                  pl.BlockSpec((seq, MLA_V), lambda b, h: (b, h_ + h))],
        out_specs=pl.BlockSpec((seq, MLA_V), lambda b, h: (b, h)),
        out_shape=jax.ShapeDtypeStruct((batch * seq, h_ * MLA_V), BF16),
        scratch_shapes=[pltpu.VMEM((seq, qw), BF16), pltpu.VMEM((seq, MLA_V + LANES), BF16)],
        compiler_params=_cparams(("parallel", "parallel")),
        name="mla_flash",
    )(q, kv, kpe, kv)


def _mla(x, w_down, norm_q, norm_kv, w_uq, w_ukv, batch, seq):
    h_ = MLA_HEADS
    half = MLA_ROPE // 2

    def pe_tile(w):
        z = jnp.zeros(w.shape[:-1] + (ROPE_PARTNER - half,), w.dtype)
        return jnp.concatenate([w[..., :half], z, w[..., half:], z], axis=-1)

    n_lat = MLA_Q_RANK + MLA_KV_RANK
    wd = jnp.concatenate([w_down[:, :n_lat], pe_tile(w_down[:, n_lat:])], axis=1).astype(BF16)
    wq = w_uq.reshape(MLA_Q_RANK, h_, MLA_NOPE + MLA_ROPE)
    wq = jnp.concatenate([wq[..., :MLA_NOPE], pe_tile(wq[..., MLA_NOPE:])], axis=-1)
    wq = wq.reshape(MLA_Q_RANK, -1).astype(BF16)
    wkv = w_ukv.reshape(MLA_KV_RANK, h_, MLA_NOPE + MLA_V)
    wkv = jnp.concatenate([wkv[:, :, :MLA_NOPE].reshape(MLA_KV_RANK, -1),
                           wkv[:, :, MLA_NOPE:].reshape(MLA_KV_RANK, -1)], axis=1).astype(BF16)
    scale = (MLA_NOPE + MLA_ROPE) ** -0.5 * math.log2(math.e)
    tk = _rope_tables(seq, MLA_ROPE, MLA_THETA, 1.0, 0.0)
    tq = _rope_tables(seq, MLA_ROPE, MLA_THETA, scale, 0.0)
    tabs = [np.stack([a, b]) for a, b in zip(tk, tq)]
    q, kv, kpe = _mla_proj(x, wd, norm_q[None, :], norm_kv[None, :], wq, wkv, tabs, seq, scale)
    return _mla_flash(q, kv, kpe, batch, seq)


def _head_sum(z, ones_bd):
    return _dot(z.astype(BF16), ones_bd)


def _rwkv_prep_body(x_ref, xp_ref, mu_ref, wr_ref, wk_ref, wv_ref, la0_ref, lb0_ref, la1_ref,
                    lb1_ref, ga_ref, gb_ref, vec_ref, bd_ref, r_ref, lw_ref, k_ref, v_ref,
                    kk_ref, b_ref, g_ref, *, tiles_per_seq):
    x = x_ref[...]
    tm = x.shape[0]
    first = pl.program_id(0) % tiles_per_seq == 0
    prev_row = jnp.where(first, 0.0, xp_ref[SUBLANES - 1:SUBLANES, :])
    rows = lax.broadcasted_iota(jnp.int32, x.shape, 0)
    shifted = jnp.where(rows == 0, prev_row, pltpu.roll(x, 1, 0))
    xb = x.astype(BF16)
    xxb = (shifted - x).astype(BF16)
    mix = lambda i: xb + xxb * mu_ref[i:i + 1, :].astype(BF16)
    w0, a0, k_k, k_a = (vec_ref[i:i + 1, :] for i in range(4))
    wl = w0 + _dot(jnp.tanh(_dot(mix(1), la0_ref[...])).astype(BF16), lb0_ref[...])
    al = a0 + _dot(_dot(mix(4), la1_ref[...]).astype(BF16), lb1_ref[...])
    gl = _dot(mix(5), ga_ref[...])
    k_raw = _dot(mix(2), wk_ref[...])
    lw_ref[...] = -math.exp(-0.5) / (1.0 + jnp.exp(-wl))
    a = 1.0 / (1.0 + jnp.exp(-al))
    g_ref[...] = _dot((1.0 / (1.0 + jnp.exp(-gl))).astype(BF16), gb_ref[...]).astype(g_ref.dtype)
    v_ref[...] = _dot(mix(3), wv_ref[...]).astype(v_ref.dtype)
    kk = k_raw * k_k
    bd = bd_ref[...]
    for s in range(D_MODEL // LANES):
        sl = slice(s * LANES, (s + 1) * LANES)
        t = kk[:, sl]
        t = t * lax.rsqrt(jnp.maximum(_head_sum(t * t, bd), 1e-24))
        kk_ref[:, sl] = t.astype(kk_ref.dtype)
        b_ref[:, sl] = (t * a[:, sl]).astype(b_ref.dtype)
    k_ref[...] = (k_raw * (1.0 + (a - 1.0) * k_a)).astype(k_ref.dtype)
    r_ref[...] = _dot(mix(0), wr_ref[...]).astype(r_ref.dtype)


def _head_ones():
    idx = jnp.arange(LANES) // RWKV_HEAD
    return (idx[:, None] == idx[None, :]).astype(BF16)


def _rwkv_prep(x, mu, w_rkv, vec, lora_a, lora_b, gate_a, gate_b, seq, *, tm=512):
    m, d = x.shape
    gpad = RWKV_GATE_PAD - gate_a.shape[1]
    ga = jnp.pad(gate_a, ((0, 0), (0, gpad))).astype(BF16)
    gb = jnp.pad(gate_b, ((0, gpad), (0, 0))).astype(BF16)
    wts = [w_rkv[0].astype(BF16), w_rkv[1].astype(BF16), w_rkv[2].astype(BF16),
           lora_a[0].astype(BF16), lora_b[0].astype(BF16), lora_a[1].astype(BF16),
           lora_b[1].astype(BF16), ga, gb]
    vec8 = jnp.pad(vec, ((0, SUBLANES - vec.shape[0]), (0, 0)))
    mu8 = jnp.pad(mu, ((0, SUBLANES - mu.shape[0]), (0, 0)))
    row = lambda i: (i, 0)
    sub = tm // SUBLANES
    out = jax.ShapeDtypeStruct((m, d), BF16)
    return pl.pallas_call(
        functools.partial(_rwkv_prep_body, tiles_per_seq=seq // tm),
        grid=(m // tm,),
        in_specs=[pl.BlockSpec((tm, d), row),
                  pl.BlockSpec((SUBLANES, d), lambda i: (jnp.maximum(i * sub - 1, 0), 0)),
                  _resident(mu8.shape)] + [_resident(w.shape) for w in wts]
        + [_resident(vec8.shape), _resident((LANES, LANES))],
        out_specs=[pl.BlockSpec((tm, d), row)] * 7,
        out_shape=[out, jax.ShapeDtypeStruct((m, d), F32), out, out, out, out, out],
        compiler_params=_cparams(("parallel",)),
        name="rwkv_prep",
    )(x, x, mu8, *wts, vec8, _head_ones())


def _rwkv_wkv_body(r_ref, lw_ref, k_ref, v_ref, kk_ref, b_ref, tri_ref, y_ref, state_ref):
    c = RWKV_CHUNK
    two = 2 * c

    @pl.when(pl.program_id(1) == 0)
    def _():
        state_ref[...] = jnp.zeros(state_ref.shape, F32)

    tri = tri_ref[...]

    def decayed(bi):
        lw = lw_ref[bi]
        h1 = lw.astype(BF16)
        r1 = lw - h1.astype(F32)
        h2 = r1.astype(BF16)
        h3 = (r1 - h2.astype(F32)).astype(BF16)
        cum = _dot(tri, h1) + _dot(tri, h2) + _dot(tri, h3)
        gam = jnp.exp(cum)
        gam_inv = jnp.exp(-cum)
        return (r_ref[bi].astype(F32) * gam, kk_ref[bi].astype(F32) * jnp.exp(cum - lw),
                b_ref[bi].astype(F32) * gam_inv, k_ref[bi].astype(F32) * gam_inv, gam[c - 1:c, :])

    lane_lo = lax.broadcasted_iota(jnp.int32, (c, LANES), 1) < RWKV_HEAD
    row2 = lax.broadcasted_iota(jnp.int32, (two, two), 0)
    col2 = lax.broadcasted_iota(jnp.int32, (two, two), 1)
    same = (row2 // c) == (col2 // c)
    strict = jnp.logical_and(same, row2 > col2)
    incl = jnp.logical_and(same, row2 >= col2)
    eye = (row2 == col2).astype(F32)

    def stack_masked(t):
        return jnp.concatenate([jnp.where(lane_lo, t, 0.0), jnp.where(lane_lo, 0.0, t)], axis=0)

    nb = r_ref.shape[0]
    items = [(bi, p) for bi in range(nb) for p in range(D_MODEL // LANES)]
    idx = range(len(items))
    sls = [slice(p * LANES, (p + 1) * LANES) for _, p in items]
    dec = [decayed(bi) for bi in range(nb)]
    xs, xu, bds, kds, vss, gend = [], [], [], [], [], []
    for (bi, _), sl in zip(items, sls):
        rt, kkt, bt, kt, gam_end = dec[bi]
        xs.append(jnp.concatenate([stack_masked(kkt[:, sl]), stack_masked(rt[:, sl])], axis=0).astype(BF16))
        xu.append(jnp.concatenate([kkt[:, sl], rt[:, sl]], axis=0).astype(BF16))
        bds.append(jnp.concatenate([bt[:, sl], bt[:, sl]], axis=0).astype(BF16))
        kds.append(jnp.concatenate([kt[:, sl], kt[:, sl]], axis=0).astype(BF16))
        v2 = v_ref[bi, :, sl].astype(F32)
        vss.append(jnp.where(same, jnp.concatenate([v2, v2], axis=0), 0.0).astype(BF16))
        gend.append(gam_end[:, sl])
    s2s = [state_ref[bi, p] for bi, p in items]
    a_all = [_dot_nt(xs[i], jnp.concatenate([bds[i][:c], kds[i][:c]], axis=0)) for i in idx]
    x_state = [_dot_nt(xu[i], s2s[i].astype(BF16)) for i in idx]
    head0 = row2 < c
    nmat, lk, arb, ark = [], [], [], []
    for a in a_all:
        top, bot = a[:two], a[two:]
        top_r, bot_r = pltpu.roll(top, c, 1), pltpu.roll(bot, c, 1)
        nmat.append(jnp.where(strict, -jnp.where(head0, top, top_r), 0.0))
        lk.append(jnp.where(strict, jnp.where(head0, top_r, top), 0.0).astype(BF16))
        arb.append(jnp.where(incl, jnp.where(head0, bot, bot_r), 0.0).astype(BF16))
        ark.append(jnp.where(incl, jnp.where(head0, bot_r, bot), 0.0).astype(BF16))
    rhs = [stack_masked(x_state[i][:c]) + _dot(lk[i], vss[i]) for i in idx]
    pw = [n_.astype(BF16) for n_ in nmat]
    inv = [eye + n_ for n_ in nmat]
    pw = [_dot(t, t).astype(BF16) for t in pw]
    for _ in range(int(math.log2(c)) - 2):
        both = [_dot(pw[i], jnp.concatenate([pw[i], inv[i].astype(BF16)], axis=1)) for i in idx]
        inv = [inv[i] + both[i][:, two:] for i in idx]
        pw = [t[:, :two].astype(BF16) for t in both]
    inv = [inv[i] + _dot(pw[i], inv[i].astype(BF16)) for i in idx]
    ub = [(-_dot(inv[i].astype(BF16), rhs[i].astype(BF16))).astype(BF16) for i in idx]
    for i, (bi, _) in enumerate(items):
        ys = _dot(arb[i], ub[i]) + _dot(ark[i], vss[i])
        y_ref[bi, :, sls[i]] = x_state[i][c:] + ys[:c] + ys[c:]
    for i, (bi, p) in enumerate(items):
        ds = _dot_tn(ub[i], bds[i]) + _dot_tn(vss[i], kds[i])
        state_ref[bi, p] = jnp.where(same, (s2s[i] + ds) * gend[i], 0.0)


def _rwkv_wkv(r, lw, k, v, kk, b, batch, seq):
    c = RWKV_CHUNK
    n = seq // c
    d = D_MODEL
    tri = (jnp.arange(c)[:, None] >= jnp.arange(c)[None, :]).astype(BF16)
    nb = RWKV_BATCH_ROWS if batch % RWKV_BATCH_ROWS == 0 else 1
    blk = pl.BlockSpec((nb, c, d), lambda bi, i: (bi, i, 0))
    as3d = lambda t: t.reshape(batch, seq, d)
    y = pl.pallas_call(
        _rwkv_wkv_body,
        grid=(batch // nb, n),
        in_specs=[blk] * 6 + [_resident((c, c))],
        out_specs=blk,
        out_shape=jax.ShapeDtypeStruct((batch, seq, d), F32),
        scratch_shapes=[pltpu.VMEM((nb, d // LANES, LANES, LANES), F32)],
        compiler_params=_cparams(("parallel", "arbitrary")),
        name="rwkv_wkv",
    )(as3d(r), as3d(lw), as3d(k), as3d(v), as3d(kk), as3d(b), tri)
    return y.reshape(batch * seq, d)


def _rwkv_out_body(y_ref, r_ref, k_ref, v_ref, g_ref, vec_ref, bd_ref, w_ref, res_ref, lg_ref,
                   lb_ref, o_ref, a_ref):
    bd = bd_ref[...]
    inv_n = 1.0 / RWKV_HEAD
    for s in range(D_MODEL // LANES):
        sl = slice(s * LANES, (s + 1) * LANES)
        y = y_ref[:, sl]
        mu = _head_sum(y, bd) * inv_n
        dlt = y - mu
        var = _head_sum(dlt * dlt, bd) * inv_n
        yn = dlt * lax.rsqrt(var + RWKV_GN_EPS) * vec_ref[0:1, sl] + vec_ref[1:2, sl]
        rk = r_ref[:, sl] * k_ref[:, sl] * vec_ref[2:3, sl].astype(BF16)
        bonus = _head_sum(rk, bd) * v_ref[:, sl].astype(F32)
        a_ref[:, sl] = ((yn + bonus) * g_ref[:, sl].astype(F32)).astype(BF16)
    acc = _dot(a_ref[...], w_ref[...])
    o_ref[...] = _layer_norm(DN_ALPHA * res_ref[...] + acc, lg_ref[...], lb_ref[...])


def _rwkv_out(y, r, k, v, g, vec, w, res, lg, lb, *, tm=512):
    m, d = res.shape
    row = lambda i: (i, 0)
    act = pl.BlockSpec((tm, d), row)
    return pl.pallas_call(
        _rwkv_out_body,
        grid=(m // tm,),
        in_specs=[act] * 5 + [_resident(vec.shape), _resident((LANES, LANES)), _resident((d, d)),
                              act, _resident((1, d)), _resident((1, d))],
        out_specs=act,
        out_shape=jax.ShapeDtypeStruct((m, d), F32),
        scratch_shapes=[pltpu.VMEM((tm, d), BF16)],
        compiler_params=_cparams(("parallel",)),
        name="rwkv_out",
    )(y, r, k, v, g, vec, _head_ones(), w, res, lg, lb)


def kernel(x, ret_w_in, ret_gn, ret_w_out, dil_w_in, dil_w_out, mla_w_down, mla_norm_q,
           mla_norm_kv, mla_w_uq, mla_w_ukv, mla_w_out, rwkv_mu, rwkv_w_rkv, rwkv_w_out,
           rwkv_vec, rwkv_lora_a, rwkv_lora_b, rwkv_gate_a, rwkv_gate_b, rwkv_ln_x,
           mlp_w1, mlp_w2, ln_g, ln_b):
    batch, seq, d = x.shape
    xf = x.reshape(batch * seq, d)
    w1_all, w2_all = mlp_w1.astype(BF16), mlp_w2.astype(BF16)
    n_mixers = 4
    for i in range(DEPTH):
        mixer, j = i % n_mixers, i // n_mixers
        lg, lb = ln_g[i, 0][None, :], ln_b[i, 0][None, :]
        if mixer == 0:
            proj = _mm(xf, ret_w_in[j].astype(BF16), tm=512, name="ret_proj")
            gated = _retention(proj, ret_gn[j], batch, seq)
            xf = _mm_res_ln(gated, ret_w_out[j].astype(BF16), xf, lg, lb, name="ret_out")
        elif mixer == 1:
            outs, lses = _dilated(xf, dil_w_in[j].astype(BF16), batch, seq)
            xf = _dil_out(outs, lses, dil_w_out[j].astype(BF16), xf, lg, lb, seq)
        elif mixer == 2:
            o = _mla(xf, mla_w_down[j], mla_norm_q[j], mla_norm_kv[j], mla_w_uq[j], mla_w_ukv[j],
                     batch, seq)
            xf = _mm_res_ln(o, mla_w_out[j].astype(BF16), xf, lg, lb, name="mla_out")
        else:
            r, lw, k, v, kk, b, g = _rwkv_prep(xf, rwkv_mu[j], rwkv_w_rkv[j], rwkv_vec[j],
                                               rwkv_lora_a[j], rwkv_lora_b[j], rwkv_gate_a[j],
                                               rwkv_gate_b[j], seq)
            y = _rwkv_wkv(r, lw, k, v, kk, b, batch, seq)
            vec = jnp.concatenate([rwkv_ln_x[j], rwkv_vec[j][4:5],
                                   jnp.zeros((SUBLANES - 3, d), F32)], axis=0)
            xf = _rwkv_out(y, r, k, v, g, vec, rwkv_w_out[j].astype(BF16), xf, lg, lb)
        xf = _mlp(xf, w1_all, w2_all, i, ln_g[i, 1][None, :], ln_b[i, 1][None, :])
    return xf.reshape(batch, seq, d)
```
